```python
import jax, jax.numpy as jnp
from jax import lax
import numpy as np

D_MODEL = 1024
BATCH = 8
SEQ = 4096
DEPTH = 1

CTX_LEN = 256
GRID_W = 64
RET_HEADS = 4
RET_DK = 64
RET_DV = 128
RET_WIDTH = RET_HEADS * RET_DV
RET_CHUNK = 128
GMLP_GROUPS = 4
GMLP_DG = 128
GMLP_WIDTH = GMLP_GROUPS * GMLP_DG
GMLP_CHUNK = 128
D_MIX = RET_WIDTH + GMLP_WIDTH
Q0 = 0
K0 = Q0 + RET_HEADS * RET_DK
V0 = K0 + RET_HEADS * RET_DK
GR0 = V0 + RET_WIDTH
U0 = GR0 + RET_WIDTH
VM0 = U0 + GMLP_WIDTH
GM0 = VM0 + GMLP_WIDTH
D_IN_PROJ = GM0 + GMLP_WIDTH
ROPE_BASE = 10000.0
EPS = 1e-6

kernel_name = "hybrid_retention_gmlp_dit_block"


def rms_norm(x, w):
    xf = x.astype(jnp.float32)
    y = xf * lax.rsqrt(jnp.mean(xf * xf, axis=-1, keepdims=True) + EPS)
    return (y * w.astype(jnp.float32)).astype(x.dtype)


def layer_norm(x, w):
    xf = x.astype(jnp.float32)
    mu = jnp.mean(xf, axis=-1, keepdims=True)
    var = jnp.mean(jnp.square(xf - mu), axis=-1, keepdims=True)
    return ((xf - mu) * lax.rsqrt(var + EPS) * w.astype(jnp.float32)).astype(x.dtype)


def heads(t, d):
    b, t_len, _ = t.shape
    return t.reshape(b, t_len, -1, d).transpose(0, 2, 1, 3)


def flip_t(t):
    return jnp.flip(t, axis=2)


def group_norm_heads(o, w):
    of = o.astype(jnp.float32)
    mu = jnp.mean(of, axis=-1, keepdims=True)
    var = jnp.mean(jnp.square(of - mu), axis=-1, keepdims=True)
    y = (of - mu) * lax.rsqrt(var + EPS)
    b, h, t_len, d = y.shape
    return y.transpose(0, 2, 1, 3).reshape(b, t_len, h * d) * w.astype(jnp.float32)


def rope_2d(x, rows, cols):
    quarter = RET_DK // 4
    half = RET_DK // 2
    inv_freq = ROPE_BASE ** (-jnp.arange(quarter, dtype=jnp.float32) / quarter)

    def rot(xh, pos):
        ang = pos.astype(jnp.float32)[:, None] * inv_freq[None, :]
        cos, sin = jnp.cos(ang), jnp.sin(ang)
        x1, x2 = xh[..., :quarter], xh[..., quarter:]
        return jnp.concatenate([x1 * cos - x2 * sin, x1 * sin + x2 * cos], axis=-1)

    return jnp.concatenate([rot(x[..., :half], rows), rot(x[..., half:], cols)], axis=-1)


def retention_chunk_states(k, v, log_g, s0):
    b, h, t_len, dk = k.shape
    dv = v.shape[-1]
    nc = t_len // RET_CHUNK
    kc = k.reshape(b, h, nc, RET_CHUNK, dk)
    vc = v.reshape(b, h, nc, RET_CHUNK, dv)
    j = jnp.arange(RET_CHUNK, dtype=jnp.float32)
    to_end = jnp.exp((RET_CHUNK - 1 - j)[None, :] * log_g[:, None])
    kv = jnp.einsum('bhcmd,hm,bhcme->cbhde', kc, to_end, vc)
    g_chunk = jnp.exp(RET_CHUNK * log_g)[None, :, None, None]

    def step(s, kv_c):
        return g_chunk * s + kv_c, s

    s_final, states = lax.scan(step, s0, kv)
    return states, s_final


def retention_chunk_outputs(q, k, v, log_g, states):
    b, h, t_len, dk = q.shape
    dv = v.shape[-1]
    nc = t_len // RET_CHUNK
    qc = q.reshape(b, h, nc, RET_CHUNK, dk)
    kc = k.reshape(b, h, nc, RET_CHUNK, dk)
    vc = v.reshape(b, h, nc, RET_CHUNK, dv)
    j = jnp.arange(RET_CHUNK, dtype=jnp.float32)
    diff = j[:, None] - j[None, :]
    decay = jnp.where(diff >= 0,
                      jnp.exp(jnp.maximum(diff, 0.0)[None] * log_g[:, None, None]),
                      0.0)
    scores = jnp.einsum('bhcnd,bhcmd->bhcnm', qc, kc) * decay[None, :, None]
    intra = jnp.einsum('bhcnm,bhcme->bhcne', scores, vc)
    from_start = jnp.exp((j + 1.0)[None, :] * log_g[:, None])
    cross = jnp.einsum('bhcnd,hn,cbhde->bhcne', qc, from_start, states)
    return (intra + cross).reshape(b, h, t_len, dv)


def bidirectional_retention(q, k, v, log_g_f, log_g_b, s0_f, s0_b):
    st_f, _ = retention_chunk_states(k, v, log_g_f, s0_f)
    out_f = retention_chunk_outputs(q, k, v, log_g_f, st_f)
    qr, kr, vr = flip_t(q), flip_t(k), flip_t(v)
    st_b, _ = retention_chunk_states(kr, vr, log_g_b, s0_b)
    out_b = flip_t(retention_chunk_outputs(qr, kr, vr, log_g_b, st_b))
    return out_f + out_b


def chunk_spatial_gating(u, vg, ln_w, w_s, b_s):
    b, t_len, _ = u.shape
    nc = t_len // GMLP_CHUNK
    vn = layer_norm(vg, ln_w).reshape(b, nc, GMLP_CHUNK, GMLP_GROUPS, GMLP_DG)
    s = jnp.einsum('gnm,bcmgd->bcngd', w_s, vn) + b_s.T[None, None, :, :, None]
    return u * s.reshape(b, t_len, GMLP_WIDTH)


def mixer_output(p, ret, ret_gn_w, gmlp_ln_w, w_s, b_s, w_out):
    y_ret = group_norm_heads(ret, ret_gn_w).astype(p.dtype) * jax.nn.silu(p[..., GR0:U0])
    u = jax.nn.gelu(p[..., U0:VM0], approximate=False)
    vg = jax.nn.gelu(p[..., VM0:GM0], approximate=False)
    y_sg = chunk_spatial_gating(u, vg, gmlp_ln_w, w_s, b_s) * jax.nn.silu(p[..., GM0:D_IN_PROJ])
    return jnp.concatenate([y_ret, y_sg], axis=-1) @ w_out


def hybrid_layer(x, ctx, mod_x, mod_c, norm_w, w_in, dec_f, dec_b, ret_gn_w,
                 gmlp_ln_w, w_s, b_s, w_out, rows, cols, update_ctx):
    f32 = jnp.float32
    lg_f = -jnp.exp(dec_f.astype(f32))
    lg_b = -jnp.exp(dec_b.astype(f32))
    shift, scale, gate = jnp.split(mod_x[:, None, :], 3, axis=-1)
    shift_c, scale_c, gate_c = jnp.split(mod_c, 3, axis=-1)
    h = rms_norm(x, norm_w) * (1 + scale) + shift
    hc = rms_norm(ctx, norm_w) * (1 + scale_c) + shift_c
    b = x.shape[0]
    zeros = jnp.zeros((b, RET_HEADS, RET_DK, RET_DV), f32)

    if update_ctx:
        pc = hc @ w_in
        pc_k, pc_v = pc[..., K0:V0], pc[..., V0:GR0]
    else:
        pc_kv = hc @ w_in[:, K0:GR0]
        pc_k, pc_v = pc_kv[..., :V0 - K0], pc_kv[..., V0 - K0:]
    kc = heads(pc_k.astype(f32), RET_DK) * RET_DK ** -0.5
    vc = heads(pc_v.astype(f32), RET_DV)
    st_cf, s_cf = retention_chunk_states(kc, vc, lg_f, zeros)
    st_cb, s_cb = retention_chunk_states(flip_t(kc), flip_t(vc), lg_b, zeros)

    p = h @ w_in
    q = rope_2d(heads(p[..., Q0:K0].astype(f32), RET_DK), rows, cols)
    k = rope_2d(heads(p[..., K0:V0].astype(f32), RET_DK), rows, cols) * RET_DK ** -0.5
    v = heads(p[..., V0:GR0].astype(f32), RET_DV)
    ret = bidirectional_retention(q, k, v, lg_f, lg_b, s_cf, s_cb)
    x_new = x + gate * mixer_output(p, ret, ret_gn_w, gmlp_ln_w, w_s, b_s, w_out)

    if update_ctx:
        qc = heads(pc[..., Q0:K0].astype(f32), RET_DK)
        ret_c = retention_chunk_outputs(qc, kc, vc, lg_f, st_cf) + flip_t(
            retention_chunk_outputs(flip_t(qc), flip_t(kc), flip_t(vc), lg_b, st_cb))
        ctx = ctx + gate_c * mixer_output(pc, ret_c, ret_gn_w, gmlp_ln_w, w_s, b_s, w_out)
    return x_new, ctx


def setup_inputs(seed: int = 0) -> dict:
    key = jax.random.key(seed)
    ks = jax.random.split(key, 16)
    f32 = jnp.float32
    nrm = lambda k, s: jax.random.normal(k, s, dtype=f32)
    base = jnp.log(-jnp.log1p(-(2.0 ** (-5.0 - jnp.arange(RET_HEADS, dtype=f32)))))
    return {
        "x": nrm(ks[0], (BATCH, SEQ, D_MODEL)),
        "c": nrm(ks[1], (BATCH, D_MODEL)),
        "ctx": nrm(ks[2], (BATCH, CTX_LEN, D_MODEL)),
        "c_ctx": nrm(ks[3], (D_MODEL,)),
        "w_ada": nrm(ks[4], (DEPTH, D_MODEL, 3 * D_MODEL)) * (0.5 * D_MODEL ** -0.5),
        "b_ada": 0.01 * nrm(ks[5], (DEPTH, 3 * D_MODEL)),
        "norm_w": 1.0 + 0.01 * nrm(ks[6], (DEPTH, D_MODEL)),
        "w_in": nrm(ks[7], (DEPTH, D_MODEL, D_IN_PROJ)) * D_MODEL ** -0.5,
        "ret_decay_f": base + 0.01 * nrm(ks[8], (DEPTH, RET_HEADS)),
        "ret_decay_b": base + 0.01 * nrm(ks[9], (DEPTH, RET_HEADS)),
        "ret_gn_w": 1.0 + 0.01 * nrm(ks[10], (DEPTH, RET_WIDTH)),
        "gmlp_ln_w": 1.0 + 0.01 * nrm(ks[11], (DEPTH, GMLP_WIDTH)),
        "w_s": nrm(ks[12], (DEPTH, GMLP_GROUPS, GMLP_CHUNK, GMLP_CHUNK)) * GMLP_CHUNK ** -0.5,
        "b_s": 1.0 + 0.01 * nrm(ks[13], (DEPTH, GMLP_GROUPS, GMLP_CHUNK)),
        "w_out": nrm(ks[14], (DEPTH, D_MIX, D_MODEL)) * D_MIX ** -0.5,
        "final_norm_w": 1.0 + 0.01 * nrm(ks[15], (D_MODEL,)),
    }


def reference(x, c, ctx, c_ctx, w_ada, b_ada, norm_w, w_in, ret_decay_f, ret_decay_b,
              ret_gn_w, gmlp_ln_w, w_s, b_s, w_out, final_norm_w):
    n_tok = x.shape[1]
    ROWS = n_tok // GRID_W
    rows = jnp.repeat(jnp.arange(ROWS, dtype=jnp.int32), GRID_W)
    cols = jnp.tile(jnp.arange(GRID_W, dtype=jnp.int32), ROWS)
    sc = jax.nn.silu(c)
    sc_ctx = jax.nn.silu(c_ctx)
    for layer in range(DEPTH):
        mod_x = sc @ w_ada[layer] + b_ada[layer]
        mod_c = sc_ctx @ w_ada[layer] + b_ada[layer]
        x, ctx = hybrid_layer(x, ctx, mod_x, mod_c, norm_w[layer], w_in[layer],
                              ret_decay_f[layer], ret_decay_b[layer], ret_gn_w[layer],
                              gmlp_ln_w[layer], w_s[layer], b_s[layer], w_out[layer],
                              rows, cols, layer < DEPTH - 1)
    return rms_norm(x, final_norm_w)
```

```python
import functools

import jax
import jax.numpy as jnp
from jax import lax
from jax.experimental import pallas as pl
from jax.experimental.pallas import tpu as pltpu

F32 = jnp.float32
BF16 = jnp.bfloat16

HEADS = 4
DK = 64
DV = 128
CHUNK = 128
GROUPS = 4
DG = 128
GRID_W = 64
ROPE_BASE = 10000.0
EPS = 1e-6
LANES = 128
PAIRS = HEADS * DK // LANES
RET_W = HEADS * DV
GM_W = GROUPS * DG
QK_W = HEADS * DK

TILE_KV = 512
TILE_MAIN = 256


def _rms(x, w):
    ms = jnp.mean(x * x, axis=-1, keepdims=True)
    return x * lax.rsqrt(ms + EPS) * w


def _silu(x):
    return x * jax.nn.sigmoid(x)


def _gelu(x):
    return 0.5 * x * (1.0 + lax.erf(x * (0.5 ** 0.5)))


def _dot(a, b):
    return jnp.dot(a, b, preferred_element_type=F32)


def _rope(p, cos, sin_signed, upper16):
    partner = jnp.where(upper16, pltpu.roll(p, 16, 1), pltpu.roll(p, LANES - 16, 1))
    return p * cos + partner * sin_signed


def _kv_pairs(kd, v):
    out = []
    for p in range(PAIRS):
        r = _dot(kd[p * LANES:(p + 1) * LANES, :], v[:, p * 2 * DV:(p + 1) * 2 * DV])
        out.append(r[0:DK, 0:DV])
        out.append(r[DK:2 * DK, DV:2 * DV])
    return jnp.concatenate(out, axis=0)


def _prologue_kernel(decf_ref, decb_ref, cc_ref, w_ref, b_ref, invf_ref,
                     mod_ref, cos_ref, sin_ref, dsum_ref, af_ref, ab_ref,
                     kdft_ref, kdbt_ref, gf_ref, gb_ref):
    c = cc_ref[...]
    mod_ref[...] = _dot(_silu(c), w_ref[...]) + b_ref[...]

    @pl.when(pl.program_id(0) == 0)
    def _tables():
        pos = lax.broadcasted_iota(jnp.int32, (GRID_W, LANES), 0).astype(F32)
        lane = lax.broadcasted_iota(jnp.int32, (GRID_W, LANES), 1)
        ang = pos * invf_ref[...]
        cr = jnp.cos(ang)
        sr = jnp.sin(ang)
        sr = jnp.where((lane & 16) == 0, -sr, sr)
        by_row = (lane & 32) == 0
        n_rows = cos_ref.shape[0] // GRID_W
        for r in range(n_rows):
            sl = slice(r * GRID_W, (r + 1) * GRID_W)
            cos_ref[sl, :] = jnp.where(by_row, jnp.broadcast_to(cr[r:r + 1, :], cr.shape), cr)
            sin_ref[sl, :] = jnp.where(by_row, jnp.broadcast_to(sr[r:r + 1, :], sr.shape), sr)

        def log_decay(head_idx, dec_ref):
            d = jnp.full(head_idx.shape, dec_ref[HEADS - 1], F32)
            for h in range(HEADS - 2, -1, -1):
                d = jnp.where(head_idx == h, dec_ref[h], d)
            return -jnp.exp(d)

        n = lax.broadcasted_iota(jnp.int32, (CHUNK, QK_W), 0).astype(F32)
        hl = lax.broadcasted_iota(jnp.int32, (CHUNK, QK_W), 1) // DK
        af_ref[...] = jnp.exp((n + 1.0) * log_decay(hl, decf_ref))
        ab_ref[...] = jnp.exp((CHUNK - n) * log_decay(hl, decb_ref))

        m = lax.broadcasted_iota(jnp.int32, (QK_W, CHUNK), 1).astype(F32)
        hr = lax.broadcasted_iota(jnp.int32, (QK_W, CHUNK), 0) // DK
        lgf = log_decay(hr, decf_ref)
        lgb = log_decay(hr, decb_ref)
        kdft_ref[...] = jnp.exp((CHUNK - 1.0 - m) * lgf)
        kdbt_ref[...] = jnp.exp(m * lgb)
        gf_ref[...] = jnp.exp(float(CHUNK) * lgf)
        gb_ref[...] = jnp.exp(float(CHUNK) * lgb)

        nn = lax.broadcasted_iota(jnp.int32, (CHUNK, CHUNK), 0).astype(F32)
        mm = lax.broadcasted_iota(jnp.int32, (CHUNK, CHUNK), 1).astype(F32)
        for h in range(HEADS):
            lf = -jnp.exp(jnp.full((CHUNK, CHUNK), decf_ref[h], F32))
            lb = -jnp.exp(jnp.full((CHUNK, CHUNK), decb_ref[h], F32))
            df = nn - mm
            db = mm - nn
            fwd = jnp.where(df >= 0, jnp.exp(jnp.maximum(df, 0.0) * lf), 0.0)
            bwd = jnp.where(db >= 0, jnp.exp(jnp.maximum(db, 0.0) * lb), 0.0)
            dsum_ref[h] = fwd + bwd


def _ctx_kernel(ctx_ref, mod_ref, nw_ref, wkv_ref, kdft_ref, kdbt_ref, gf_ref, gb_ref,
                scf_ref, scb_ref):
    d = ctx_ref.shape[-1]
    x = ctx_ref[0]
    mod = mod_ref[0]
    hc = _rms(x, nw_ref[...]) * (1.0 + mod[:, d:2 * d]) + mod[:, 0:d]
    pkv = _dot(hc.astype(BF16), wkv_ref[...])
    k = pkv[:, 0:QK_W] * (DK ** -0.5)
    v = pkv[:, QK_W:].astype(BF16)
    n_chunks = x.shape[0] // CHUNK
    sf = jnp.zeros((QK_W, DV), F32)
    for c in range(n_chunks):
        sl = slice(c * CHUNK, (c + 1) * CHUNK)
        kd = (k[sl, :].T * kdft_ref[...]).astype(BF16)
        sf = gf_ref[...] * sf + _kv_pairs(kd, v[sl, :])
    sb = jnp.zeros((QK_W, DV), F32)
    for c in range(n_chunks - 1, -1, -1):
        sl = slice(c * CHUNK, (c + 1) * CHUNK)
        kd = (k[sl, :].T * kdbt_ref[...]).astype(BF16)
        sb = gb_ref[...] * sb + _kv_pairs(kd, v[sl, :])
    scf_ref[0] = sf
    scb_ref[0] = sb


def _kv_kernel(x_ref, mod_ref, nw_ref, wkv_ref, cos_ref, sin_ref, kdbt_ref, gb_ref, scb_ref,
               kt_ref, v_ref, stb_ref, sb_ref):
    d = x_ref.shape[-1]
    tile = x_ref.shape[1]

    @pl.when(pl.program_id(1) == 0)
    def _init():
        sb_ref[...] = scb_ref[0]

    x = x_ref[0]
    mod = mod_ref[0]
    h = _rms(x, nw_ref[...]) * (1.0 + mod[:, d:2 * d]) + mod[:, 0:d]
    pkv = _dot(h.astype(BF16), wkv_ref[...])
    cos = cos_ref[...]
    sin = sin_ref[...]
    upper16 = (lax.broadcasted_iota(jnp.int32, (tile, LANES), 1) & 16) != 0
    k = [_rope(pkv[:, p * LANES:(p + 1) * LANES], cos, sin, upper16) * (DK ** -0.5)
         for p in range(PAIRS)]
    v = pkv[:, QK_W:].astype(BF16)
    v_ref[0] = v
    for c in range(tile // CHUNK - 1, -1, -1):
        sl = slice(c * CHUNK, (c + 1) * CHUNK)
        stb_ref[0, c] = sb_ref[...].astype(BF16)
        kt = jnp.concatenate([kp[sl, :].T for kp in k], axis=0)
        kt_ref[0, c] = kt.astype(BF16)
        kd = (kt * kdbt_ref[...]).astype(BF16)
        sb_ref[...] = gb_ref[...] * sb_ref[...] + _kv_pairs(kd, v[sl, :])


def _main_kernel(x_ref, mod_ref, nw_ref, w_ref, cos_ref, sin_ref, kt_ref, v_ref, stb_ref,
                 scf_ref, dsum_ref, af_ref, ab_ref, kdft_ref, gf_ref, gnw_ref, lnw_ref,
                 ws_ref, bs_ref, wout_ref, fnw_ref, o_ref, sf_ref, y_ref):
    d = x_ref.shape[-1]
    tile = x_ref.shape[1]

    @pl.when(pl.program_id(1) == 0)
    def _init():
        sf_ref[...] = scf_ref[0]

    x = x_ref[0]
    mod = mod_ref[0]
    h = _rms(x, nw_ref[...]) * (1.0 + mod[:, d:2 * d]) + mod[:, 0:d]
    p_all = _dot(h.astype(BF16), w_ref[...])
    c_gr = QK_W
    c_u = c_gr + RET_W
    c_vm = c_u + GM_W
    c_gm = c_vm + GM_W

    lane = lax.broadcasted_iota(jnp.int32, (CHUNK, LANES), 1)
    upper16 = (lane & 16) != 0
    head_of_lane = lane // DK

    for c in range(tile // CHUNK):
        sl = slice(c * CHUNK, (c + 1) * CHUNK)
        cos = cos_ref[sl, :]
        sin = sin_ref[sl, :]

        for p in range(PAIRS):
            ps = slice(p * LANES, (p + 1) * LANES)
            q2 = _rope(p_all[sl, ps], cos, sin, upper16)
            kt = kt_ref[0, c, ps, :]
            vp = v_ref[0, sl, p * 2 * DV:(p + 1) * 2 * DV]
            state = jnp.concatenate([sf_ref[ps, :].astype(BF16), stb_ref[0, c, ps, :]], axis=0)
            qa = q2 * af_ref[:, ps]
            qb = q2 * ab_ref[:, ps]
            for hh in range(2):
                head = 2 * p + hh
                mine = head_of_lane == hh
                qm = jnp.where(mine, q2, 0.0).astype(BF16)
                scores = _dot(qm, kt) * dsum_ref[head]
                qx = jnp.concatenate([jnp.where(mine, qa, 0.0), jnp.where(mine, qb, 0.0)],
                                     axis=1).astype(BF16)
                o = _dot(scores.astype(BF16), vp[:, hh * DV:(hh + 1) * DV]) + _dot(qx, state)
                mu = jnp.mean(o, axis=-1, keepdims=True)
                var = jnp.mean(jnp.square(o - mu), axis=-1, keepdims=True)
                hs = slice(head * DV, (head + 1) * DV)
                y = (o - mu) * lax.rsqrt(var + EPS) * gnw_ref[:, hs]
                gate_cols = slice(c_gr + head * DV, c_gr + (head + 1) * DV)
                y_ref[sl, hs] = (y * _silu(p_all[sl, gate_cols])).astype(BF16)
            kd = (kt.astype(F32) * kdft_ref[ps, :]).astype(BF16)
            r = _dot(kd, vp)
            new = jnp.concatenate([r[0:DK, 0:DV], r[DK:2 * DK, DV:2 * DV]], axis=0)
            sf_ref[ps, :] = gf_ref[ps, :] * sf_ref[ps, :] + new

        u = _gelu(p_all[sl, c_u:c_vm])
        vg = _gelu(p_all[sl, c_vm:c_gm])
        mu = jnp.mean(vg, axis=-1, keepdims=True)
        var = jnp.mean(jnp.square(vg - mu), axis=-1, keepdims=True)
        vn = ((vg - mu) * lax.rsqrt(var + EPS) * lnw_ref[...]).astype(BF16)
        gm = _silu(p_all[sl, c_gm:c_gm + GM_W])
        for g in range(GROUPS):
            gs = slice(g * DG, (g + 1) * DG)
            s = _dot(ws_ref[g], vn[:, gs]) + bs_ref[g]
            y_ref[sl, RET_W + g * DG:RET_W + (g + 1) * DG] = (u[:, gs] * s * gm[:, gs]).astype(BF16)

    out = _dot(y_ref[...], wout_ref[...])
    xn = x + mod[:, 2 * d:3 * d] * out
    o_ref[0] = _rms(xn, fnw_ref[...])


def _full(shape):
    return pl.BlockSpec(shape, lambda *_: (0,) * len(shape))


def kernel(x, c, ctx, c_ctx, w_ada, b_ada, norm_w, w_in, ret_decay_f, ret_decay_b, ret_gn_w,
           gmlp_ln_w, w_s, b_s, w_out, final_norm_w):
    bsz, seq, d = x.shape
    ctx_len = ctx.shape[1]
    depth = w_ada.shape[0]
    assert depth == 1 and d % LANES == 0
    assert seq % TILE_KV == 0 and seq % TILE_MAIN == 0 and ctx_len % CHUNK == 0
    assert seq % GRID_W == 0 and TILE_MAIN % CHUNK == 0 and TILE_KV % CHUNK == 0
    n_chunks = seq // CHUNK
    d3 = 3 * d
    k0 = QK_W
    v0 = 2 * QK_W
    gr0 = v0 + RET_W

    w_in0 = w_in[0]
    w_kv = w_in0[:, k0:gr0].astype(BF16)
    w_rest = jnp.concatenate([w_in0[:, 0:k0], w_in0[:, gr0:]], axis=1).astype(BF16)
    w_out_b = w_out[0].astype(BF16)
    w_s_b = w_s[0].astype(BF16)
    b_s_full = jnp.broadcast_to(b_s[0][:, :, None], (GROUPS, CHUNK, DG))
    cc = jnp.concatenate([c, jnp.broadcast_to(c_ctx[None, :], (bsz, d))], axis=0)
    quarter = DK // 4
    inv_freq = ROPE_BASE ** (-jnp.arange(quarter, dtype=F32) / quarter)
    invf_lane = jnp.tile(inv_freq, LANES // quarter)[None, :]
    nw = norm_w[0][None, :]
    rows2 = 2 * bsz

    n_col_blocks = 3
    smem = pl.BlockSpec(memory_space=pltpu.SMEM)
    (mod, cos_t, sin_t, dsum, af, ab, kdft, kdbt, gf, gb) = pl.pallas_call(
        _prologue_kernel,
        grid=(n_col_blocks,),
        in_specs=[smem, smem, _full((rows2, d)),
                  pl.BlockSpec((d, d3 // n_col_blocks), lambda i: (0, i)),
                  pl.BlockSpec((1, d3 // n_col_blocks), lambda i: (0, i)),
                  _full((1, LANES))],
        out_specs=[pl.BlockSpec((rows2, d3 // n_col_blocks), lambda i: (0, i)),
                   _full((seq, LANES)), _full((seq, LANES)),
                   _full((HEADS, CHUNK, CHUNK)),
                   _full((CHUNK, QK_W)), _full((CHUNK, QK_W)),
                   _full((QK_W, CHUNK)), _full((QK_W, CHUNK)),
                   _full((QK_W, DV)), _full((QK_W, DV))],
        out_shape=[jax.ShapeDtypeStruct((rows2, d3), F32),
                   jax.ShapeDtypeStruct((seq, LANES), F32),
                   jax.ShapeDtypeStruct((seq, LANES), F32),
                   jax.ShapeDtypeStruct((HEADS, CHUNK, CHUNK), F32),
                   jax.ShapeDtypeStruct((CHUNK, QK_W), F32),
                   jax.ShapeDtypeStruct((CHUNK, QK_W), F32),
                   jax.ShapeDtypeStruct((QK_W, CHUNK), F32),
                   jax.ShapeDtypeStruct((QK_W, CHUNK), F32),
                   jax.ShapeDtypeStruct((QK_W, DV), F32),
                   jax.ShapeDtypeStruct((QK_W, DV), F32)],
        compiler_params=pltpu.CompilerParams(dimension_semantics=("arbitrary",)),
        name="prologue",
    )(ret_decay_f[0], ret_decay_b[0], cc, w_ada[0], b_ada[0][None, :], invf_lane)
    mod3 = mod.reshape(rows2, 1, d3)

    s_cf, s_cb = pl.pallas_call(
        _ctx_kernel,
        grid=(bsz,),
        in_specs=[pl.BlockSpec((1, ctx_len, d), lambda b: (b, 0, 0)),
                  pl.BlockSpec((1, 1, d3), lambda b: (bsz, 0, 0)),
                  _full((1, d)), _full((d, QK_W + RET_W)),
                  _full((QK_W, CHUNK)), _full((QK_W, CHUNK)),
                  _full((QK_W, DV)), _full((QK_W, DV))],
        out_specs=[pl.BlockSpec((1, QK_W, DV), lambda b: (b, 0, 0)),
                   pl.BlockSpec((1, QK_W, DV), lambda b: (b, 0, 0))],
        out_shape=[jax.ShapeDtypeStruct((bsz, QK_W, DV), F32),
                   jax.ShapeDtypeStruct((bsz, QK_W, DV), F32)],
        compiler_params=pltpu.CompilerParams(dimension_semantics=("arbitrary",)),
        name="ctx_states",
    )(ctx, mod3, nw, w_kv, kdft, kdbt, gf, gb)

    nt_kv = seq // TILE_KV
    cpt_kv = TILE_KV // CHUNK
    kt, v, st_b = pl.pallas_call(
        _kv_kernel,
        grid=(bsz, nt_kv),
        in_specs=[pl.BlockSpec((1, TILE_KV, d), lambda b, j: (b, nt_kv - 1 - j, 0)),
                  pl.BlockSpec((1, 1, d3), lambda b, j: (b, 0, 0)),
                  _full((1, d)), _full((d, QK_W + RET_W)),
                  pl.BlockSpec((TILE_KV, LANES), lambda b, j: (nt_kv - 1 - j, 0)),
                  pl.BlockSpec((TILE_KV, LANES), lambda b, j: (nt_kv - 1 - j, 0)),
                  _full((QK_W, CHUNK)), _full((QK_W, DV)),
                  pl.BlockSpec((1, QK_W, DV), lambda b, j: (b, 0, 0))],
        out_specs=[pl.BlockSpec((1, cpt_kv, QK_W, CHUNK), lambda b, j: (b, nt_kv - 1 - j, 0, 0)),
                   pl.BlockSpec((1, TILE_KV, RET_W), lambda b, j: (b, nt_kv - 1 - j, 0)),
                   pl.BlockSpec((1, cpt_kv, QK_W, DV), lambda b, j: (b, nt_kv - 1 - j, 0, 0))],
        out_shape=[jax.ShapeDtypeStruct((bsz, n_chunks, QK_W, CHUNK), BF16),
                   jax.ShapeDtypeStruct((bsz, seq, RET_W), BF16),
                   jax.ShapeDtypeStruct((bsz, n_chunks, QK_W, DV), BF16)],
        scratch_shapes=[pltpu.VMEM((QK_W, DV), F32)],
        compiler_params=pltpu.CompilerParams(dimension_semantics=("arbitrary", "arbitrary")),
        name="kv_sweep",
    )(x, mod3, nw, w_kv, cos_t, sin_t, kdbt, gb, s_cb)

    nt = seq // TILE_MAIN
    cpt = TILE_MAIN // CHUNK
    n_rest = w_rest.shape[1]
    out = pl.pallas_call(
        _main_kernel,
        grid=(bsz, nt),
        in_specs=[pl.BlockSpec((1, TILE_MAIN, d), lambda b, j: (b, j, 0)),
                  pl.BlockSpec((1, 1, d3), lambda b, j: (b, 0, 0)),
                  _full((1, d)), _full((d, n_rest)),
                  pl.BlockSpec((TILE_MAIN, LANES), lambda b, j: (j, 0)),
                  pl.BlockSpec((TILE_MAIN, LANES), lambda b, j: (j, 0)),
                  pl.BlockSpec((1, cpt, QK_W, CHUNK), lambda b, j: (b, j, 0, 0)),
                  pl.BlockSpec((1, TILE_MAIN, RET_W), lambda b, j: (b, j, 0)),
                  pl.BlockSpec((1, cpt, QK_W, DV), lambda b, j: (b, j, 0, 0)),
                  pl.BlockSpec((1, QK_W, DV), lambda b, j: (b, 0, 0)),
                  _full((HEADS, CHUNK, CHUNK)),
                  _full((CHUNK, QK_W)), _full((CHUNK, QK_W)),
                  _full((QK_W, CHUNK)), _full((QK_W, DV)),
                  _full((1, RET_W)), _full((1, GM_W)),
                  _full((GROUPS, CHUNK, CHUNK)), _full((GROUPS, CHUNK, DG)),
                  _full((RET_W + GM_W, d)), _full((1, d))],
        out_specs=pl.BlockSpec((1, TILE_MAIN, d), lambda b, j: (b, j, 0)),
        out_shape=jax.ShapeDtypeStruct((bsz, seq, d), x.dtype),
        scratch_shapes=[pltpu.VMEM((QK_W, DV), F32),
                        pltpu.VMEM((TILE_MAIN, RET_W + GM_W), BF16)],
        compiler_params=pltpu.CompilerParams(dimension_semantics=("arbitrary", "arbitrary")),
        name="main_sweep",
    )(x, mod3, nw, w_rest, cos_t, sin_t, kt, v, st_b, s_cf, dsum, af, ab, kdft, gf,
      ret_gn_w[0][None, :], gmlp_ln_w[0][None, :], w_s_b, b_s_full, w_out_b,
      final_norm_w[None, :])
    return out
```

```python
import functools

import jax
import jax.numpy as jnp
from jax import lax
from jax.experimental import pallas as pl
from jax.experimental.pallas import tpu as pltpu

F32 = jnp.float32
BF16 = jnp.bfloat16

HEADS = 4
DK = 64
DV = 128
CHUNK = 128
GROUPS = 4
DG = 128
GRID_W = 64
ROPE_BASE = 10000.0
EPS = 1e-6
LANES = 128
PAIRS = HEADS * DK // LANES
RET_W = HEADS * DV
GM_W = GROUPS * DG
QK_W = HEADS * DK

DOT_COLS = 512
TILE_KV = 512
TILE_MAIN = 512


def _rms(x, w):
    ms = jnp.mean(x * x, axis=-1, keepdims=True)
    return x * lax.rsqrt(ms + EPS) * w


def _silu(x):
    return x * jax.nn.sigmoid(x)


def _gelu(x):
    return 0.5 * x * (1.0 + lax.erf(x * (0.5 ** 0.5)))


_dot = functools.partial(jnp.dot, preferred_element_type=F32)


def _col_groups(n):
    first = n % DOT_COLS
    edges = ([0] if first == 0 else [0, first]) + list(range(first + DOT_COLS, n + 1, DOT_COLS))
    return list(zip(edges[:-1], edges[1:]))


def _rope(p, cos, sin_signed, upper16):
    partner = jnp.where(upper16, pltpu.roll(p, 16, 1), pltpu.roll(p, LANES - 16, 1))
    return p * cos + partner * sin_signed


def _kv_pairs(kd, v):
    out = []
    for p in range(PAIRS):
        r = _dot(kd[p * LANES:(p + 1) * LANES, :], v[:, p * 2 * DV:(p + 1) * 2 * DV])
        out.append(r[0:DK, 0:DV])
        out.append(r[DK:2 * DK, DV:2 * DV])
    return jnp.concatenate(out, axis=0)


def _prologue_kernel(decf_ref, decb_ref, cc_ref, w_ref, b_ref, invf_ref,
                     mod_ref, cos_ref, sin_ref, dsum_ref, af_ref, ab_ref,
                     kdft_ref, kdbt_ref, gf_ref, gb_ref):
    c = cc_ref[...]
    mod_ref[...] = _dot(_silu(c), w_ref[...]) + b_ref[...]

    @pl.when(pl.program_id(0) == 0)
    def _tables():
        pos = lax.broadcasted_iota(jnp.int32, (GRID_W, LANES), 0).astype(F32)
        lane = lax.broadcasted_iota(jnp.int32, (GRID_W, LANES), 1)
        ang = pos * invf_ref[...]
        cr = jnp.cos(ang)
        sr = jnp.sin(ang)
        sr = jnp.where((lane & 16) == 0, -sr, sr)
        by_row = (lane & 32) == 0
        n_rows = cos_ref.shape[0] // GRID_W
        for r in range(n_rows):
            sl = slice(r * GRID_W, (r + 1) * GRID_W)
            cos_ref[sl, :] = jnp.where(by_row, jnp.broadcast_to(cr[r:r + 1, :], cr.shape), cr)
            sin_ref[sl, :] = jnp.where(by_row, jnp.broadcast_to(sr[r:r + 1, :], sr.shape), sr)

        def log_decay(head_idx, dec_ref):
            d = jnp.full(head_idx.shape, dec_ref[HEADS - 1], F32)
            for h in range(HEADS - 2, -1, -1):
                d = jnp.where(head_idx == h, dec_ref[h], d)
            return -jnp.exp(d)

        n = lax.broadcasted_iota(jnp.int32, (CHUNK, QK_W), 0).astype(F32)
        hl = lax.broadcasted_iota(jnp.int32, (CHUNK, QK_W), 1) // DK
        af_ref[...] = jnp.exp((n + 1.0) * log_decay(hl, decf_ref))
        ab_ref[...] = jnp.exp((CHUNK - n) * log_decay(hl, decb_ref))

        m = lax.broadcasted_iota(jnp.int32, (QK_W, CHUNK), 1).astype(F32)
        hr = lax.broadcasted_iota(jnp.int32, (QK_W, CHUNK), 0) // DK
        lgf = log_decay(hr, decf_ref)
        lgb = log_decay(hr, decb_ref)
        kdft_ref[...] = jnp.exp((CHUNK - 1.0 - m) * lgf)
        kdbt_ref[...] = jnp.exp(m * lgb)
        gf_ref[...] = jnp.exp(float(CHUNK) * lgf)
        gb_ref[...] = jnp.exp(float(CHUNK) * lgb)

        nn = lax.broadcasted_iota(jnp.int32, (CHUNK, CHUNK), 0).astype(F32)
        mm = lax.broadcasted_iota(jnp.int32, (CHUNK, CHUNK), 1).astype(F32)
        for h in range(HEADS):
            lf = -jnp.exp(jnp.full((CHUNK, CHUNK), decf_ref[h], F32))
            lb = -jnp.exp(jnp.full((CHUNK, CHUNK), decb_ref[h], F32))
            df = nn - mm
            db = mm - nn
            fwd = jnp.where(df >= 0, jnp.exp(jnp.maximum(df, 0.0) * lf), 0.0)
            bwd = jnp.where(db >= 0, jnp.exp(jnp.maximum(db, 0.0) * lb), 0.0)
            dsum_ref[h] = fwd + bwd


def _ctx_kernel(ctx_ref, mod_ref, nw_ref, wkv_ref, kdft_ref, kdbt_ref, gf_ref, gb_ref,
                scf_ref, scb_ref):
    d = ctx_ref.shape[-1]
    x = ctx_ref[0]
    mod = mod_ref[0]
    hc = _rms(x, nw_ref[...]) * (1.0 + mod[:, d:2 * d]) + mod[:, 0:d]
    pkv = _dot(hc.astype(BF16), wkv_ref[...])
    k = pkv[:, 0:QK_W] * (DK ** -0.5)
    v = pkv[:, QK_W:].astype(BF16)
    n_chunks = x.shape[0] // CHUNK
    sf = jnp.zeros((QK_W, DV), F32)
    for c in range(n_chunks):
        sl = slice(c * CHUNK, (c + 1) * CHUNK)
        kd = (k[sl, :].T * kdft_ref[...]).astype(BF16)
        sf = gf_ref[...] * sf + _kv_pairs(kd, v[sl, :])
    sb = jnp.zeros((QK_W, DV), F32)
    for c in range(n_chunks - 1, -1, -1):
        sl = slice(c * CHUNK, (c + 1) * CHUNK)
        kd = (k[sl, :].T * kdbt_ref[...]).astype(BF16)
        sb = gb_ref[...] * sb + _kv_pairs(kd, v[sl, :])
    scf_ref[0] = sf
    scb_ref[0] = sb


def _kv_kernel(x_ref, mod_ref, nw_ref, wkv_ref, cos_ref, sin_ref, kdbt_ref, gb_ref, scb_ref,
               kt_ref, v_ref, stb_ref, sb_ref):
    d = x_ref.shape[-1]
    tile = x_ref.shape[1]

    @pl.when(pl.program_id(1) == 0)
    def _init():
        sb_ref[...] = scb_ref[0]

    x = x_ref[0]
    mod = mod_ref[0]
    h = _rms(x, nw_ref[...]) * (1.0 + mod[:, d:2 * d]) + mod[:, 0:d]
    pkv = _dot(h.astype(BF16), wkv_ref[...])
    cos = cos_ref[...]
    sin = sin_ref[...]
    upper16 = (lax.broadcasted_iota(jnp.int32, (tile, LANES), 1) & 16) != 0
    k = [_rope(pkv[:, p * LANES:(p + 1) * LANES], cos, sin, upper16) * (DK ** -0.5)
         for p in range(PAIRS)]
    v = pkv[:, QK_W:].astype(BF16)
    v_ref[0] = v
    for c in range(tile // CHUNK - 1, -1, -1):
        sl = slice(c * CHUNK, (c + 1) * CHUNK)
        stb_ref[0, c] = sb_ref[...].astype(BF16)
        kt = jnp.concatenate([kp[sl, :].T for kp in k], axis=0)
        kt_ref[0, c] = kt.astype(BF16)
        kd = (kt * kdbt_ref[...]).astype(BF16)
        sb_ref[...] = gb_ref[...] * sb_ref[...] + _kv_pairs(kd, v[sl, :])


def _main_kernel(tiles_per_batch, xc_ref, modc_ref, x_ref, mod_ref, nw_ref, w_ref, cos_ref,
                 sin_ref, kt_ref, v_ref, stb_ref, scf_ref, dsum_ref, af_ref, ab_ref, kdft_ref,
                 gf_ref, gnw_ref, lnw_ref, ws_ref, bs_ref, wout_ref, fnw_ref, o_ref,
                 sf_ref, y_ref, h_ref, pa_ref, pb_ref):
    s = pl.program_id(0)
    prev = jnp.maximum(s - 1, 0)

    @pl.when(s == 0)
    def _first():
        pb_ref[...] = jnp.zeros(pb_ref.shape, pb_ref.dtype)

    @pl.when(lax.rem(prev, tiles_per_batch) == 0)
    def _init():
        sf_ref[...] = scf_ref[0]

    refs = (xc_ref, modc_ref, x_ref, mod_ref, nw_ref, w_ref, cos_ref, sin_ref, kt_ref, v_ref,
            stb_ref, dsum_ref, af_ref, ab_ref, kdft_ref, gf_ref, gnw_ref, lnw_ref, ws_ref,
            bs_ref, wout_ref, fnw_ref, o_ref, sf_ref, y_ref, h_ref)

    @pl.when(lax.rem(s, 2) == 0)
    def _even():
        _main_step(*refs, pa_ref, pb_ref)

    @pl.when(lax.rem(s, 2) == 1)
    def _odd():
        _main_step(*refs, pb_ref, pa_ref)


def _main_step(xc_ref, modc_ref, x_ref, mod_ref, nw_ref, w_ref, cos_ref, sin_ref, kt_ref, v_ref,
               stb_ref, dsum_ref, af_ref, ab_ref, kdft_ref, gf_ref, gnw_ref, lnw_ref, ws_ref,
               bs_ref, wout_ref, fnw_ref, o_ref, sf_ref, y_ref, h_ref, p_new_ref, p_all):
    d = x_ref.shape[-1]
    tile = x_ref.shape[1]
    n_chunks = tile // CHUNK
    half = tile // 2
    row_halves = [slice(0, half), slice(half, tile)]
    c_gr = QK_W
    c_u = c_gr + RET_W
    c_vm = c_u + GM_W
    c_gm = c_vm + GM_W
    modc = modc_ref[0]
    mod = mod_ref[0]
    lane = lax.broadcasted_iota(jnp.int32, (CHUNK, LANES), 1)
    upper16 = (lane & 16) != 0
    head_of_lane = lane // DK


    def norm_rows(rows):
        h = _rms(xc_ref[0, rows, :], nw_ref[...]) * (1.0 + modc[:, d:2 * d]) + modc[:, 0:d]
        h_ref[rows, :] = h.astype(BF16)

    def project(rows, lo, hi):
        p_new_ref[rows, lo:hi] = _dot(h_ref[rows, :], w_ref[:, lo:hi])

    def retention(c, p):
        sl = slice(c * CHUNK, (c + 1) * CHUNK)
        ps = slice(p * LANES, (p + 1) * LANES)
        q2 = _rope(p_all[sl, ps], cos_ref[sl, :], sin_ref[sl, :], upper16)
        kt = kt_ref[0, c, ps, :]
        vp = v_ref[0, sl, p * 2 * DV:(p + 1) * 2 * DV]
        qa = q2 * af_ref[:, ps]
        qb = q2 * ab_ref[:, ps]
        scores, qx = [], []
        for hh in range(2):
            mine = head_of_lane == hh
            scores.append(_dot(jnp.where(mine, q2, 0.0).astype(BF16), kt))
            qx.append(jnp.concatenate([jnp.where(mine, qa, 0.0), jnp.where(mine, qb, 0.0)],
                                      axis=1).astype(BF16))
        kd = (kt.astype(F32) * kdft_ref[ps, :]).astype(BF16)
        yield
        state = jnp.concatenate([sf_ref[ps, :].astype(BF16), stb_ref[0, c, ps, :]], axis=0)
        o = []
        for hh in range(2):
            a = (scores[hh] * dsum_ref[2 * p + hh]).astype(BF16)
            o.append(_dot(a, vp[:, hh * DV:(hh + 1) * DV]) + _dot(qx[hh], state))
        r = _dot(kd, vp)
        yield
        new = jnp.concatenate([r[0:DK, 0:DV], r[DK:2 * DK, DV:2 * DV]], axis=0)
        sf_ref[ps, :] = gf_ref[ps, :] * sf_ref[ps, :] + new
        for hh in range(2):
            head = 2 * p + hh
            mu = jnp.mean(o[hh], axis=-1, keepdims=True)
            var = jnp.mean(jnp.square(o[hh] - mu), axis=-1, keepdims=True)
            hs = slice(head * DV, (head + 1) * DV)
            y = (o[hh] - mu) * lax.rsqrt(var + EPS) * gnw_ref[:, hs]
            gate_cols = slice(c_gr + head * DV, c_gr + (head + 1) * DV)
            y_ref[sl, hs] = (y * _silu(p_all[sl, gate_cols])).astype(BF16)

    def gating(c):
        sl = slice(c * CHUNK, (c + 1) * CHUNK)
        vg = _gelu(p_all[sl, c_vm:c_gm])
        mu = jnp.mean(vg, axis=-1, keepdims=True)
        var = jnp.mean(jnp.square(vg - mu), axis=-1, keepdims=True)
        vn = ((vg - mu) * lax.rsqrt(var + EPS) * lnw_ref[...]).astype(BF16)
        s = [_dot(ws_ref[g], vn[:, g * DG:(g + 1) * DG]) for g in range(GROUPS)]
        yield
        for g in range(GROUPS):
            gs = slice(g * DG, (g + 1) * DG)
            u = _gelu(p_all[sl, c_u + g * DG:c_u + (g + 1) * DG])
            gm = _silu(p_all[sl, c_gm + g * DG:c_gm + (g + 1) * DG])
            y_ref[sl, RET_W + g * DG:RET_W + (g + 1) * DG] = (u * (s[g] + bs_ref[g]) * gm).astype(BF16)

    out_groups = _col_groups(d)
    xn = {}

    def out_project(r, g):
        rows = row_halves[r]
        lo, hi = out_groups[g]
        out = _dot(y_ref[rows, :], wout_ref[:, lo:hi])
        xn[r, g] = x_ref[0, rows, lo:hi] + mod[:, 2 * d + lo:2 * d + hi] * out

    def finish(r):
        rows = row_halves[r]
        parts = [xn[r, g] for g in range(len(out_groups))]
        ms = sum(jnp.sum(t * t, axis=-1, keepdims=True) for t in parts) * (1.0 / d)
        rs = lax.rsqrt(ms + EPS)
        for (lo, hi), t in zip(out_groups, parts):
            o_ref[0, rows, lo:hi] = t * rs * fnw_ref[:, lo:hi]

    in_groups = _col_groups(p_new_ref.shape[1])
    quarter = [slice(i * half // 2, (i + 1) * half // 2) for i in range(4)]
    tasks = []
    for c in range(n_chunks):
        tasks += [retention(c, p) for p in range(PAIRS)] + [gating(c)]
    tasks_per_half = len(tasks) // 2
    depth = 3
    projections = [functools.partial(project, row_halves[r], lo, hi)
                   for r in range(2) for lo, hi in in_groups]
    first_out_tick = tasks_per_half + depth - 1
    for g in range(len(out_groups)):
        projections.insert(min(first_out_tick + g, len(projections)),
                           functools.partial(out_project, 0, g))
    norm_rows(quarter[0])
    norm_rows(quarter[1])
    late_norms = [quarter[2], quarter[3]]
    in_flight = []
    for tick in range(max(len(tasks) + depth - 1, len(projections))):
        if tick < len(projections):
            projections[tick]()
        if tick < len(late_norms):
            norm_rows(late_norms[tick])
        if tick < len(tasks):
            in_flight.append(tasks[tick])
        for t in list(in_flight):
            if next(t, "done") == "done":
                in_flight.remove(t)
    assert not in_flight
    finish(0)
    for g in range(len(out_groups)):
        out_project(1, g)
    finish(1)


def _main_step_old(xc_ref, modc_ref, x_ref, mod_ref, nw_ref, w_ref, cos_ref, sin_ref, kt_ref, v_ref,
               stb_ref, dsum_ref, af_ref, ab_ref, kdft_ref, gf_ref, gnw_ref, lnw_ref, ws_ref,
               bs_ref, wout_ref, fnw_ref, o_ref, sf_ref, y_ref, h_ref, p_new_ref, p_all):
    d = x_ref.shape[-1]
    tile = x_ref.shape[1]

    modc = modc_ref[0]
    hcur = _rms(xc_ref[0], nw_ref[...]) * (1.0 + modc[:, d:2 * d]) + modc[:, 0:d]
    h_ref[...] = hcur.astype(BF16)
    groups = _col_groups(p_new_ref.shape[1])

    def project(lo, hi):
        p_new_ref[:, lo:hi] = _dot(h_ref[...], w_ref[:, lo:hi])

    x = x_ref[0]
    mod = mod_ref[0]
    c_gr = QK_W
    c_u = c_gr + RET_W
    c_vm = c_u + GM_W
    c_gm = c_vm + GM_W

    lane = lax.broadcasted_iota(jnp.int32, (CHUNK, LANES), 1)
    upper16 = (lane & 16) != 0
    head_of_lane = lane // DK

    n_chunks = tile // CHUNK
    for c in range(n_chunks):
        for g in groups[c::n_chunks + 1]:
            project(*g)
        sl = slice(c * CHUNK, (c + 1) * CHUNK)
        cos = cos_ref[sl, :]
        sin = sin_ref[sl, :]

        for p in range(PAIRS):
            ps = slice(p * LANES, (p + 1) * LANES)
            q2 = _rope(p_all[sl, ps], cos, sin, upper16)
            kt = kt_ref[0, c, ps, :]
            vp = v_ref[0, sl, p * 2 * DV:(p + 1) * 2 * DV]
            state = jnp.concatenate([sf_ref[ps, :].astype(BF16), stb_ref[0, c, ps, :]], axis=0)
            qa = q2 * af_ref[:, ps]
            qb = q2 * ab_ref[:, ps]
            for hh in range(2):
                head = 2 * p + hh
                mine = head_of_lane == hh
                qm = jnp.where(mine, q2, 0.0).astype(BF16)
                scores = _dot(qm, kt) * dsum_ref[head]
                qx = jnp.concatenate([jnp.where(mine, qa, 0.0), jnp.where(mine, qb, 0.0)],
                                     axis=1).astype(BF16)
                o = _dot(scores.astype(BF16), vp[:, hh * DV:(hh + 1) * DV]) + _dot(qx, state)
                mu = jnp.mean(o, axis=-1, keepdims=True)
                var = jnp.mean(jnp.square(o - mu), axis=-1, keepdims=True)
                hs = slice(head * DV, (head + 1) * DV)
                y = (o - mu) * lax.rsqrt(var + EPS) * gnw_ref[:, hs]
                gate_cols = slice(c_gr + head * DV, c_gr + (head + 1) * DV)
                y_ref[sl, hs] = (y * _silu(p_all[sl, gate_cols])).astype(BF16)
            kd = (kt.astype(F32) * kdft_ref[ps, :]).astype(BF16)
            r = _dot(kd, vp)
            new = jnp.concatenate([r[0:DK, 0:DV], r[DK:2 * DK, DV:2 * DV]], axis=0)
            sf_ref[ps, :] = gf_ref[ps, :] * sf_ref[ps, :] + new

        u = _gelu(p_all[sl, c_u:c_vm])
        vg = _gelu(p_all[sl, c_vm:c_gm])
        mu = jnp.mean(vg, axis=-1, keepdims=True)
        var = jnp.mean(jnp.square(vg - mu), axis=-1, keepdims=True)
        vn = ((vg - mu) * lax.rsqrt(var + EPS) * lnw_ref[...]).astype(BF16)
        gm = _silu(p_all[sl, c_gm:c_gm + GM_W])
        for g in range(GROUPS):
            gs = slice(g * DG, (g + 1) * DG)
            s = _dot(ws_ref[g], vn[:, gs]) + bs_ref[g]
            y_ref[sl, RET_W + g * DG:RET_W + (g + 1) * DG] = (u[:, gs] * s * gm[:, gs]).astype(BF16)

    for g in groups[n_chunks::n_chunks + 1]:
        project(*g)
    xn = []
    for lo, hi in _col_groups(d):
        out = _dot(y_ref[...], wout_ref[:, lo:hi])
        xn.append(x[:, lo:hi] + mod[:, 2 * d + lo:2 * d + hi] * out)
    ms = sum(jnp.sum(t * t, axis=-1, keepdims=True) for t in xn) * (1.0 / d)
    rs = lax.rsqrt(ms + EPS)
    for (lo, hi), t in zip(_col_groups(d), xn):
        o_ref[0, :, lo:hi] = t * rs * fnw_ref[:, lo:hi]


def _full(shape):
    return pl.BlockSpec(shape, lambda *_: (0,) * len(shape))


def kernel(x, c, ctx, c_ctx, w_ada, b_ada, norm_w, w_in, ret_decay_f, ret_decay_b, ret_gn_w,
           gmlp_ln_w, w_s, b_s, w_out, final_norm_w):
    bsz, seq, d = x.shape
    ctx_len = ctx.shape[1]
    depth = w_ada.shape[0]
    assert depth == 1 and d % LANES == 0
    assert seq % TILE_KV == 0 and seq % TILE_MAIN == 0 and ctx_len % CHUNK == 0
    assert seq % GRID_W == 0 and TILE_MAIN % CHUNK == 0 and TILE_KV % CHUNK == 0
    n_chunks = seq // CHUNK
    d3 = 3 * d
    k0 = QK_W
    v0 = 2 * QK_W
    gr0 = v0 + RET_W

    w_in0 = w_in[0]
    w_kv = w_in0[:, k0:gr0].astype(BF16)
    w_rest = jnp.concatenate([w_in0[:, 0:k0], w_in0[:, gr0:]], axis=1).astype(BF16)
    w_out_b = w_out[0].astype(BF16)
    w_s_b = w_s[0].astype(BF16)
    b_s_full = jnp.broadcast_to(b_s[0][:, :, None], (GROUPS, CHUNK, DG))
    cc = jnp.concatenate([c, jnp.broadcast_to(c_ctx[None, :], (bsz, d))], axis=0)
    quarter = DK // 4
    inv_freq = ROPE_BASE ** (-jnp.arange(quarter, dtype=F32) / quarter)
    invf_lane = jnp.tile(inv_freq, LANES // quarter)[None, :]
    nw = norm_w[0][None, :]
    rows2 = 2 * bsz

    n_col_blocks = 3
    smem = pl.BlockSpec(memory_space=pltpu.SMEM)
    (mod, cos_t, sin_t, dsum, af, ab, kdft, kdbt, gf, gb) = pl.pallas_call(
        _prologue_kernel,
        grid=(n_col_blocks,),
        in_specs=[smem, smem, _full((rows2, d)),
                  pl.BlockSpec((d, d3 // n_col_blocks), lambda i: (0, i)),
                  pl.BlockSpec((1, d3 // n_col_blocks), lambda i: (0, i)),
                  _full((1, LANES))],
        out_specs=[pl.BlockSpec((rows2, d3 // n_col_blocks), lambda i: (0, i)),
                   _full((seq, LANES)), _full((seq, LANES)),
                   _full((HEADS, CHUNK, CHUNK)),
                   _full((CHUNK, QK_W)), _full((CHUNK, QK_W)),
                   _full((QK_W, CHUNK)), _full((QK_W, CHUNK)),
                   _full((QK_W, DV)), _full((QK_W, DV))],
        out_shape=[jax.ShapeDtypeStruct((rows2, d3), F32),
                   jax.ShapeDtypeStruct((seq, LANES), F32),
                   jax.ShapeDtypeStruct((seq, LANES), F32),
                   jax.ShapeDtypeStruct((HEADS, CHUNK, CHUNK), F32),
                   jax.ShapeDtypeStruct((CHUNK, QK_W), F32),
                   jax.ShapeDtypeStruct((CHUNK, QK_W), F32),
                   jax.ShapeDtypeStruct((QK_W, CHUNK), F32),
                   jax.ShapeDtypeStruct((QK_W, CHUNK), F32),
                   jax.ShapeDtypeStruct((QK_W, DV), F32),
                   jax.ShapeDtypeStruct((QK_W, DV), F32)],
        compiler_params=pltpu.CompilerParams(dimension_semantics=("arbitrary",)),
        name="prologue",
    )(ret_decay_f[0], ret_decay_b[0], cc, w_ada[0], b_ada[0][None, :], invf_lane)
    mod3 = mod.reshape(rows2, 1, d3)

    s_cf, s_cb = pl.pallas_call(
        _ctx_kernel,
        grid=(bsz,),
        in_specs=[pl.BlockSpec((1, ctx_len, d), lambda b: (b, 0, 0)),
                  pl.BlockSpec((1, 1, d3), lambda b: (bsz, 0, 0)),
                  _full((1, d)), _full((d, QK_W + RET_W)),
                  _full((QK_W, CHUNK)), _full((QK_W, CHUNK)),
                  _full((QK_W, DV)), _full((QK_W, DV))],
        out_specs=[pl.BlockSpec((1, QK_W, DV), lambda b: (b, 0, 0)),
                   pl.BlockSpec((1, QK_W, DV), lambda b: (b, 0, 0))],
        out_shape=[jax.ShapeDtypeStruct((bsz, QK_W, DV), F32),
                   jax.ShapeDtypeStruct((bsz, QK_W, DV), F32)],
        compiler_params=pltpu.CompilerParams(dimension_semantics=("arbitrary",)),
        name="ctx_states",
    )(ctx, mod3, nw, w_kv, kdft, kdbt, gf, gb)

    nt_kv = seq // TILE_KV
    cpt_kv = TILE_KV // CHUNK
    kt, v, st_b = pl.pallas_call(
        _kv_kernel,
        grid=(bsz, nt_kv),
        in_specs=[pl.BlockSpec((1, TILE_KV, d), lambda b, j: (b, nt_kv - 1 - j, 0)),
                  pl.BlockSpec((1, 1, d3), lambda b, j: (b, 0, 0)),
                  _full((1, d)), _full((d, QK_W + RET_W)),
                  pl.BlockSpec((TILE_KV, LANES), lambda b, j: (nt_kv - 1 - j, 0)),
                  pl.BlockSpec((TILE_KV, LANES), lambda b, j: (nt_kv - 1 - j, 0)),
                  _full((QK_W, CHUNK)), _full((QK_W, DV)),
                  pl.BlockSpec((1, QK_W, DV), lambda b, j: (b, 0, 0))],
        out_specs=[pl.BlockSpec((1, cpt_kv, QK_W, CHUNK), lambda b, j: (b, nt_kv - 1 - j, 0, 0)),
                   pl.BlockSpec((1, TILE_KV, RET_W), lambda b, j: (b, nt_kv - 1 - j, 0)),
                   pl.BlockSpec((1, cpt_kv, QK_W, DV), lambda b, j: (b, nt_kv - 1 - j, 0, 0))],
        out_shape=[jax.ShapeDtypeStruct((bsz, n_chunks, QK_W, CHUNK), BF16),
                   jax.ShapeDtypeStruct((bsz, seq, RET_W), BF16),
                   jax.ShapeDtypeStruct((bsz, n_chunks, QK_W, DV), BF16)],
        scratch_shapes=[pltpu.VMEM((QK_W, DV), F32)],
        compiler_params=pltpu.CompilerParams(dimension_semantics=("arbitrary", "arbitrary")),
        name="kv_sweep",
    )(x, mod3, nw, w_kv, cos_t, sin_t, kdbt, gb, s_cb)

    nt = seq // TILE_MAIN
    cpt = TILE_MAIN // CHUNK
    n_rest = w_rest.shape[1]
    n_tiles = bsz * nt

    def cur(s):
        t = jnp.minimum(s, n_tiles - 1)
        return t // nt, t % nt

    def prv(s):
        t = jnp.maximum(s - 1, 0)
        return t // nt, t % nt

    out = pl.pallas_call(
        functools.partial(_main_kernel, nt),
        grid=(n_tiles + 1,),
        in_specs=[pl.BlockSpec((1, TILE_MAIN, d), lambda s: (*cur(s), 0)),
                  pl.BlockSpec((1, 1, d3), lambda s: (cur(s)[0], 0, 0)),
                  pl.BlockSpec((1, TILE_MAIN, d), lambda s: (*prv(s), 0)),
                  pl.BlockSpec((1, 1, d3), lambda s: (prv(s)[0], 0, 0)),
                  _full((1, d)), _full((d, n_rest)),
                  pl.BlockSpec((TILE_MAIN, LANES), lambda s: (prv(s)[1], 0)),
                  pl.BlockSpec((TILE_MAIN, LANES), lambda s: (prv(s)[1], 0)),
                  pl.BlockSpec((1, cpt, QK_W, CHUNK), lambda s: (*prv(s), 0, 0)),
                  pl.BlockSpec((1, TILE_MAIN, RET_W), lambda s: (*prv(s), 0)),
                  pl.BlockSpec((1, cpt, QK_W, DV), lambda s: (*prv(s), 0, 0)),
                  pl.BlockSpec((1, QK_W, DV), lambda s: (prv(s)[0], 0, 0)),
                  _full((HEADS, CHUNK, CHUNK)),
                  _full((CHUNK, QK_W)), _full((CHUNK, QK_W)),
                  _full((QK_W, CHUNK)), _full((QK_W, DV)),
                  _full((1, RET_W)), _full((1, GM_W)),
                  _full((GROUPS, CHUNK, CHUNK)), _full((GROUPS, CHUNK, DG)),
                  _full((RET_W + GM_W, d)), _full((1, d))],
        out_specs=pl.BlockSpec((1, TILE_MAIN, d), lambda s: (*prv(s), 0)),
        out_shape=jax.ShapeDtypeStruct((bsz, seq, d), x.dtype),
        scratch_shapes=[pltpu.VMEM((QK_W, DV), F32),
                        pltpu.VMEM((TILE_MAIN, RET_W + GM_W), BF16),
                        pltpu.VMEM((TILE_MAIN, d), BF16),
                        pltpu.VMEM((TILE_MAIN, n_rest), F32),
                        pltpu.VMEM((TILE_MAIN, n_rest), F32)],
        compiler_params=pltpu.CompilerParams(dimension_semantics=("arbitrary",)),
        name="main_sweep",
    )(x, mod3, x, mod3, nw, w_rest, cos_t, sin_t, kt, v, st_b, s_cf, dsum, af, ab, kdft, gf,
      ret_gn_w[0][None, :], gmlp_ln_w[0][None, :], w_s_b, b_s_full, w_out_b,
      final_norm_w[None, :])
    return out
```

```python
import functools

import jax
import jax.numpy as jnp
from jax import lax
from jax.experimental import pallas as pl
from jax.experimental.pallas import tpu as pltpu

F32 = jnp.float32
BF16 = jnp.bfloat16

HEADS = 4
DK = 64
DV = 128
CHUNK = 128
GROUPS = 4
DG = 128
GRID_W = 64
ROPE_BASE = 10000.0
EPS = 1e-6
LANES = 128
PAIRS = HEADS * DK // LANES
RET_W = HEADS * DV
GM_W = GROUPS * DG
QK_W = HEADS * DK

DOT_COLS = 512
TILE_KV = 1024
TILE_MAIN = 512


def _rms(x, w):
    ms = jnp.mean(x * x, axis=-1, keepdims=True)
    return x * lax.rsqrt(ms + EPS) * w


def _silu(x):
    return x * jax.nn.sigmoid(x)


def _gelu(x):
    return 0.5 * x * (1.0 + lax.erf(x * (0.5 ** 0.5)))


_dot = functools.partial(jnp.dot, preferred_element_type=F32)


def _col_groups(n):
    first = n % DOT_COLS
    edges = ([0] if first == 0 else [0, first]) + list(range(first + DOT_COLS, n + 1, DOT_COLS))
    return list(zip(edges[:-1], edges[1:]))


def _rope(p, cos, sin_signed, upper16):
    partner = jnp.where(upper16, pltpu.roll(p, 16, 1), pltpu.roll(p, LANES - 16, 1))
    return p * cos + partner * sin_signed


def _kv_pairs(kd, v):
    out = []
    for p in range(PAIRS):
        r = _dot(kd[p * LANES:(p + 1) * LANES, :], v[:, p * 2 * DV:(p + 1) * 2 * DV])
        out.append(r[0:DK, 0:DV])
        out.append(r[DK:2 * DK, DV:2 * DV])
    return jnp.concatenate(out, axis=0)


def _emit_pipelined(big_pieces, side_pieces, tasks, depth):
    in_flight = []
    for tick in range(max(len(tasks) + depth - 1, len(big_pieces), len(side_pieces))):
        if tick < len(big_pieces):
            big_pieces[tick]()
        if tick < len(side_pieces):
            side_pieces[tick]()
        if tick < len(tasks):
            in_flight.append(tasks[tick])
        for t in list(in_flight):
            if next(t, "done") == "done":
                in_flight.remove(t)
    assert not in_flight


def _by_parity(step, body, refs, buf_a, buf_b):
    @pl.when(lax.rem(step, 2) == 0)
    def _even():
        body(*refs, buf_a, buf_b)

    @pl.when(lax.rem(step, 2) == 1)
    def _odd():
        body(*refs, buf_b, buf_a)


def _prologue_kernel(decf_ref, decb_ref, cc_ref, w_ref, b_ref, invf_ref,
                     mod_ref, cos_ref, sin_ref, dsum_ref, af_ref, ab_ref,
                     kdft_ref, kdbt_ref, gf_ref, gb_ref):
    c = cc_ref[...]
    mod_ref[...] = _dot(_silu(c), w_ref[...]) + b_ref[...]

    @pl.when(pl.program_id(0) == 0)
    def _tables():
        pos = lax.broadcasted_iota(jnp.int32, (GRID_W, LANES), 0).astype(F32)
        lane = lax.broadcasted_iota(jnp.int32, (GRID_W, LANES), 1)
        ang = pos * invf_ref[...]
        cr = jnp.cos(ang)
        sr = jnp.sin(ang)
        sr = jnp.where((lane & 16) == 0, -sr, sr)
        by_row = (lane & 32) == 0
        n_rows = cos_ref.shape[0] // GRID_W
        for r in range(n_rows):
            sl = slice(r * GRID_W, (r + 1) * GRID_W)
            cos_ref[sl, :] = jnp.where(by_row, jnp.broadcast_to(cr[r:r + 1, :], cr.shape), cr)
            sin_ref[sl, :] = jnp.where(by_row, jnp.broadcast_to(sr[r:r + 1, :], sr.shape), sr)

        def log_decay(head_idx, dec_ref):
            d = jnp.full(head_idx.shape, dec_ref[HEADS - 1], F32)
            for h in range(HEADS - 2, -1, -1):
                d = jnp.where(head_idx == h, dec_ref[h], d)
            return -jnp.exp(d)

        n = lax.broadcasted_iota(jnp.int32, (CHUNK, QK_W), 0).astype(F32)
        hl = lax.broadcasted_iota(jnp.int32, (CHUNK, QK_W), 1) // DK
        af_ref[...] = jnp.exp((n + 1.0) * log_decay(hl, decf_ref))
        ab_ref[...] = jnp.exp((CHUNK - n) * log_decay(hl, decb_ref))

        m = lax.broadcasted_iota(jnp.int32, (QK_W, CHUNK), 1).astype(F32)
        hr = lax.broadcasted_iota(jnp.int32, (QK_W, CHUNK), 0) // DK
        lgf = log_decay(hr, decf_ref)
        lgb = log_decay(hr, decb_ref)
        kdft_ref[...] = jnp.exp((CHUNK - 1.0 - m) * lgf)
        kdbt_ref[...] = jnp.exp(m * lgb)
        gf_ref[...] = jnp.exp(float(CHUNK) * lgf)
        gb_ref[...] = jnp.exp(float(CHUNK) * lgb)

        nn = lax.broadcasted_iota(jnp.int32, (CHUNK, CHUNK), 0).astype(F32)
        mm = lax.broadcasted_iota(jnp.int32, (CHUNK, CHUNK), 1).astype(F32)
        for h in range(HEADS):
            lf = -jnp.exp(jnp.full((CHUNK, CHUNK), decf_ref[h], F32))
            lb = -jnp.exp(jnp.full((CHUNK, CHUNK), decb_ref[h], F32))
            df = nn - mm
            db = mm - nn
            fwd = jnp.where(df >= 0, jnp.exp(jnp.maximum(df, 0.0) * lf), 0.0)
            bwd = jnp.where(db >= 0, jnp.exp(jnp.maximum(db, 0.0) * lb), 0.0)
            dsum_ref[h] = fwd + bwd


def _ctx_kernel(ctx_ref, mod_ref, nw_ref, wkv_ref, kdft_ref, kdbt_ref, gf_ref, gb_ref,
                scf_ref, scb_ref):
    d = ctx_ref.shape[-1]
    x = ctx_ref[0]
    mod = mod_ref[0]
    hc = _rms(x, nw_ref[...]) * (1.0 + mod[:, d:2 * d]) + mod[:, 0:d]
    pkv = _dot(hc.astype(BF16), wkv_ref[...])
    k = pkv[:, 0:QK_W] * (DK ** -0.5)
    v = pkv[:, QK_W:].astype(BF16)
    n_chunks = x.shape[0] // CHUNK
    sf = jnp.zeros((QK_W, DV), F32)
    for c in range(n_chunks):
        sl = slice(c * CHUNK, (c + 1) * CHUNK)
        kd = (k[sl, :].T * kdft_ref[...]).astype(BF16)
        sf = gf_ref[...] * sf + _kv_pairs(kd, v[sl, :])
    sb = jnp.zeros((QK_W, DV), F32)
    for c in range(n_chunks - 1, -1, -1):
        sl = slice(c * CHUNK, (c + 1) * CHUNK)
        kd = (k[sl, :].T * kdbt_ref[...]).astype(BF16)
        sb = gb_ref[...] * sb + _kv_pairs(kd, v[sl, :])
    scf_ref[0] = sf
    scb_ref[0] = sb


def _kv_kernel(tiles_per_batch, xc_ref, modc_ref, nw_ref, wkv_ref, cos_ref, sin_ref, kdbt_ref,
               gb_ref, scb_ref, kt_ref, v_ref, stb_ref, sb_ref, h_ref, pa_ref, pb_ref):
    s = pl.program_id(0)
    prev = jnp.maximum(s - 1, 0)

    @pl.when(s == 0)
    def _first():
        pb_ref[...] = jnp.zeros(pb_ref.shape, pb_ref.dtype)

    @pl.when(lax.rem(prev, tiles_per_batch) == 0)
    def _init():
        sb_ref[...] = scb_ref[0]

    refs = (xc_ref, modc_ref, nw_ref, wkv_ref, cos_ref, sin_ref, kdbt_ref, gb_ref,
            kt_ref, v_ref, stb_ref, sb_ref, h_ref)
    _by_parity(s, _kv_step, refs, pa_ref, pb_ref)


def _kv_step(xc_ref, modc_ref, nw_ref, wkv_ref, cos_ref, sin_ref, kdbt_ref, gb_ref,
             kt_ref, v_ref, stb_ref, sb_ref, h_ref, p_new_ref, p_old):
    d = xc_ref.shape[-1]
    tile = xc_ref.shape[1]
    half = tile // 2
    row_halves = [slice(0, half), slice(half, tile)]
    quarter = [slice(i * half // 2, (i + 1) * half // 2) for i in range(4)]
    modc = modc_ref[0]
    upper16 = (lax.broadcasted_iota(jnp.int32, (CHUNK, LANES), 1) & 16) != 0

    def norm_rows(rows):
        h = _rms(xc_ref[0, rows, :], nw_ref[...]) * (1.0 + modc[:, d:2 * d]) + modc[:, 0:d]
        h_ref[rows, :] = h.astype(BF16)

    def project(rows, lo, hi):
        p_new_ref[rows, lo:hi] = _dot(h_ref[rows, :], wkv_ref[:, lo:hi])

    def chunk(c):
        sl = slice(c * CHUNK, (c + 1) * CHUNK)
        k = [_rope(p_old[sl, p * LANES:(p + 1) * LANES], cos_ref[sl, :], sin_ref[sl, :], upper16)
             * (DK ** -0.5) for p in range(PAIRS)]
        v = p_old[sl, QK_W:].astype(BF16)
        v_ref[0, sl, :] = v
        kt = jnp.concatenate([kp.T for kp in k], axis=0)
        kt_ref[0, c] = kt.astype(BF16)
        new = _kv_pairs((kt * kdbt_ref[...]).astype(BF16), v)
        yield
        stb_ref[0, c] = sb_ref[...].astype(BF16)
        sb_ref[...] = gb_ref[...] * sb_ref[...] + new

    norm_rows(quarter[0])
    norm_rows(quarter[1])
    projections = [functools.partial(project, row_halves[r], lo, hi)
                   for r in range(2) for lo, hi in _col_groups(p_new_ref.shape[1])]
    late_norms = [functools.partial(norm_rows, q) for q in quarter[2:]]
    tasks = [chunk(c) for c in range(tile // CHUNK - 1, -1, -1)]
    _emit_pipelined(projections, late_norms, tasks, depth=2)


def _main_kernel(tiles_per_batch, xc_ref, modc_ref, x_ref, mod_ref, nw_ref, w_ref, cos_ref,
                 sin_ref, kt_ref, v_ref, stb_ref, scf_ref, dsum_ref, af_ref, ab_ref, kdft_ref,
                 gf_ref, gnw_ref, lnw_ref, ws_ref, bs_ref, wout_ref, fnw_ref, o_ref,
                 sf_ref, y_ref, h_ref, pa_ref, pb_ref):
    s = pl.program_id(0)
    prev = jnp.maximum(s - 1, 0)

    @pl.when(s == 0)
    def _first():
        pb_ref[...] = jnp.zeros(pb_ref.shape, pb_ref.dtype)

    @pl.when(lax.rem(prev, tiles_per_batch) == 0)
    def _init():
        sf_ref[...] = scf_ref[0]

    refs = (xc_ref, modc_ref, x_ref, mod_ref, nw_ref, w_ref, cos_ref, sin_ref, kt_ref, v_ref,
            stb_ref, dsum_ref, af_ref, ab_ref, kdft_ref, gf_ref, gnw_ref, lnw_ref, ws_ref,
            bs_ref, wout_ref, fnw_ref, o_ref, sf_ref, y_ref, h_ref)
    _by_parity(s, _main_step, refs, pa_ref, pb_ref)


def _main_step(xc_ref, modc_ref, x_ref, mod_ref, nw_ref, w_ref, cos_ref, sin_ref, kt_ref, v_ref,
               stb_ref, dsum_ref, af_ref, ab_ref, kdft_ref, gf_ref, gnw_ref, lnw_ref, ws_ref,
               bs_ref, wout_ref, fnw_ref, o_ref, sf_ref, y_ref, h_ref, p_new_ref, p_all):
    d = x_ref.shape[-1]
    tile = x_ref.shape[1]
    n_chunks = tile // CHUNK
    half = tile // 2
    row_halves = [slice(0, half), slice(half, tile)]
    quarter = [slice(i * half // 2, (i + 1) * half // 2) for i in range(4)]
    c_gr = QK_W
    c_u = c_gr + RET_W
    c_vm = c_u + GM_W
    c_gm = c_vm + GM_W
    modc = modc_ref[0]
    mod = mod_ref[0]
    lane = lax.broadcasted_iota(jnp.int32, (CHUNK, LANES), 1)
    upper16 = (lane & 16) != 0
    head_of_lane = lane // DK

    def norm_rows(rows):
        h = _rms(xc_ref[0, rows, :], nw_ref[...]) * (1.0 + modc[:, d:2 * d]) + modc[:, 0:d]
        h_ref[rows, :] = h.astype(BF16)

    def project(rows, lo, hi):
        p_new_ref[rows, lo:hi] = _dot(h_ref[rows, :], w_ref[:, lo:hi])

    def retention(c, p):
        sl = slice(c * CHUNK, (c + 1) * CHUNK)
        ps = slice(p * LANES, (p + 1) * LANES)
        q2 = _rope(p_all[sl, ps], cos_ref[sl, :], sin_ref[sl, :], upper16)
        kt = kt_ref[0, c, ps, :]
        vp = v_ref[0, sl, p * 2 * DV:(p + 1) * 2 * DV]
        qa = q2 * af_ref[:, ps]
        qb = q2 * ab_ref[:, ps]
        scores, qx = [], []
        for hh in range(2):
            mine = head_of_lane == hh
            scores.append(_dot(jnp.where(mine, q2, 0.0).astype(BF16), kt))
            qx.append(jnp.concatenate([jnp.where(mine, qa, 0.0), jnp.where(mine, qb, 0.0)],
                                      axis=1).astype(BF16))
        kd = (kt.astype(F32) * kdft_ref[ps, :]).astype(BF16)
        yield
        state = jnp.concatenate([sf_ref[ps, :].astype(BF16), stb_ref[0, c, ps, :]], axis=0)
        o = []
        for hh in range(2):
            a = (scores[hh] * dsum_ref[2 * p + hh]).astype(BF16)
            o.append(_dot(a, vp[:, hh * DV:(hh + 1) * DV]) + _dot(qx[hh], state))
        r = _dot(kd, vp)
        yield
        new = jnp.concatenate([r[0:DK, 0:DV], r[DK:2 * DK, DV:2 * DV]], axis=0)
        sf_ref[ps, :] = gf_ref[ps, :] * sf_ref[ps, :] + new
        for hh in range(2):
            head = 2 * p + hh
            mu = jnp.mean(o[hh], axis=-1, keepdims=True)
            var = jnp.mean(jnp.square(o[hh] - mu), axis=-1, keepdims=True)
            hs = slice(head * DV, (head + 1) * DV)
            y = (o[hh] - mu) * lax.rsqrt(var + EPS) * gnw_ref[:, hs]
            gate_cols = slice(c_gr + head * DV, c_gr + (head + 1) * DV)
            y_ref[sl, hs] = (y * _silu(p_all[sl, gate_cols])).astype(BF16)

    def gating(c):
        sl = slice(c * CHUNK, (c + 1) * CHUNK)
        vg = _gelu(p_all[sl, c_vm:c_gm])
        mu = jnp.mean(vg, axis=-1, keepdims=True)
        var = jnp.mean(jnp.square(vg - mu), axis=-1, keepdims=True)
        vn = ((vg - mu) * lax.rsqrt(var + EPS) * lnw_ref[...]).astype(BF16)
        s = [_dot(ws_ref[g], vn[:, g * DG:(g + 1) * DG]) for g in range(GROUPS)]
        yield
        for g in range(GROUPS):
            u = _gelu(p_all[sl, c_u + g * DG:c_u + (g + 1) * DG])
            gm = _silu(p_all[sl, c_gm + g * DG:c_gm + (g + 1) * DG])
            y_ref[sl, RET_W + g * DG:RET_W + (g + 1) * DG] = (u * (s[g] + bs_ref[g]) * gm).astype(BF16)

    out_groups = _col_groups(d)
    xn = {}

    def out_project(r, g):
        rows = row_halves[r]
        lo, hi = out_groups[g]
        out = _dot(y_ref[rows, :], wout_ref[:, lo:hi])
        xn[r, g] = x_ref[0, rows, lo:hi] + mod[:, 2 * d + lo:2 * d + hi] * out

    def finish(r):
        rows = row_halves[r]
        parts = [xn[r, g] for g in range(len(out_groups))]
        ms = sum(jnp.sum(t * t, axis=-1, keepdims=True) for t in parts) * (1.0 / d)
        rs = lax.rsqrt(ms + EPS)
        for (lo, hi), t in zip(out_groups, parts):
            o_ref[0, rows, lo:hi] = t * rs * fnw_ref[:, lo:hi]

    tasks = []
    for c in range(n_chunks):
        tasks += [retention(c, p) for p in range(PAIRS)] + [gating(c)]
    depth = 3
    projections = [functools.partial(project, row_halves[r], lo, hi)
                   for r in range(2) for lo, hi in _col_groups(p_new_ref.shape[1])]
    first_out_tick = len(tasks) // 2 + depth - 1
    for g in range(len(out_groups)):
        projections.insert(min(first_out_tick + g, len(projections)),
                           functools.partial(out_project, 0, g))
    norm_rows(quarter[0])
    norm_rows(quarter[1])
    late_norms = [functools.partial(norm_rows, q) for q in quarter[2:]]
    _emit_pipelined(projections, late_norms, tasks, depth)
    finish(0)
    for g in range(len(out_groups)):
        out_project(1, g)
    finish(1)


def _full(shape):
    return pl.BlockSpec(shape, lambda *_: (0,) * len(shape))


def kernel(x, c, ctx, c_ctx, w_ada, b_ada, norm_w, w_in, ret_decay_f, ret_decay_b, ret_gn_w,
           gmlp_ln_w, w_s, b_s, w_out, final_norm_w):
    bsz, seq, d = x.shape
    ctx_len = ctx.shape[1]
    depth = w_ada.shape[0]
    assert depth == 1 and d % LANES == 0
    assert seq % TILE_KV == 0 and seq % TILE_MAIN == 0 and ctx_len % CHUNK == 0
    assert seq % GRID_W == 0 and TILE_MAIN % (4 * CHUNK) == 0 and TILE_KV % (4 * CHUNK) == 0
    n_chunks = seq // CHUNK
    d3 = 3 * d
    k0 = QK_W
    v0 = 2 * QK_W
    gr0 = v0 + RET_W

    w_in0 = w_in[0]
    w_kv = w_in0[:, k0:gr0].astype(BF16)
    w_rest = jnp.concatenate([w_in0[:, 0:k0], w_in0[:, gr0:]], axis=1).astype(BF16)
    w_out_b = w_out[0].astype(BF16)
    w_s_b = w_s[0].astype(BF16)
    b_s_full = jnp.broadcast_to(b_s[0][:, :, None], (GROUPS, CHUNK, DG))
    cc = jnp.concatenate([c, jnp.broadcast_to(c_ctx[None, :], (bsz, d))], axis=0)
    quarter = DK // 4
    inv_freq = ROPE_BASE ** (-jnp.arange(quarter, dtype=F32) / quarter)
    invf_lane = jnp.tile(inv_freq, LANES // quarter)[None, :]
    nw = norm_w[0][None, :]
    rows2 = 2 * bsz

    n_col_blocks = 3
    smem = pl.BlockSpec(memory_space=pltpu.SMEM)
    (mod, cos_t, sin_t, dsum, af, ab, kdft, kdbt, gf, gb) = pl.pallas_call(
        _prologue_kernel,
        grid=(n_col_blocks,),
        in_specs=[smem, smem, _full((rows2, d)),
                  pl.BlockSpec((d, d3 // n_col_blocks), lambda i: (0, i)),
                  pl.BlockSpec((1, d3 // n_col_blocks), lambda i: (0, i)),
                  _full((1, LANES))],
        out_specs=[pl.BlockSpec((rows2, d3 // n_col_blocks), lambda i: (0, i)),
                   _full((seq, LANES)), _full((seq, LANES)),
                   _full((HEADS, CHUNK, CHUNK)),
                   _full((CHUNK, QK_W)), _full((CHUNK, QK_W)),
                   _full((QK_W, CHUNK)), _full((QK_W, CHUNK)),
                   _full((QK_W, DV)), _full((QK_W, DV))],
        out_shape=[jax.ShapeDtypeStruct((rows2, d3), F32),
                   jax.ShapeDtypeStruct((seq, LANES), F32),
                   jax.ShapeDtypeStruct((seq, LANES), F32),
                   jax.ShapeDtypeStruct((HEADS, CHUNK, CHUNK), F32),
                   jax.ShapeDtypeStruct((CHUNK, QK_W), F32),
                   jax.ShapeDtypeStruct((CHUNK, QK_W), F32),
                   jax.ShapeDtypeStruct((QK_W, CHUNK), F32),
                   jax.ShapeDtypeStruct((QK_W, CHUNK), F32),
                   jax.ShapeDtypeStruct((QK_W, DV), F32),
                   jax.ShapeDtypeStruct((QK_W, DV), F32)],
        compiler_params=pltpu.CompilerParams(dimension_semantics=("arbitrary",)),
        name="prologue",
    )(ret_decay_f[0], ret_decay_b[0], cc, w_ada[0], b_ada[0][None, :], invf_lane)
    mod3 = mod.reshape(rows2, 1, d3)

    s_cf, s_cb = pl.pallas_call(
        _ctx_kernel,
        grid=(bsz,),
        in_specs=[pl.BlockSpec((1, ctx_len, d), lambda b: (b, 0, 0)),
                  pl.BlockSpec((1, 1, d3), lambda b: (bsz, 0, 0)),
                  _full((1, d)), _full((d, QK_W + RET_W)),
                  _full((QK_W, CHUNK)), _full((QK_W, CHUNK)),
                  _full((QK_W, DV)), _full((QK_W, DV))],
        out_specs=[pl.BlockSpec((1, QK_W, DV), lambda b: (b, 0, 0)),
                   pl.BlockSpec((1, QK_W, DV), lambda b: (b, 0, 0))],
        out_shape=[jax.ShapeDtypeStruct((bsz, QK_W, DV), F32),
                   jax.ShapeDtypeStruct((bsz, QK_W, DV), F32)],
        compiler_params=pltpu.CompilerParams(dimension_semantics=("arbitrary",)),
        name="ctx_states",
    )(ctx, mod3, nw, w_kv, kdft, kdbt, gf, gb)

    def tile_maps(n_tiles, per_batch, reverse):
        def where(t):
            j = t % per_batch
            return t // per_batch, (per_batch - 1 - j) if reverse else j
        return (lambda s: where(jnp.minimum(s, n_tiles - 1)),
                lambda s: where(jnp.maximum(s - 1, 0)))

    nt_kv = seq // TILE_KV
    cpt_kv = TILE_KV // CHUNK
    n_kv = QK_W + RET_W
    cur, prv = tile_maps(bsz * nt_kv, nt_kv, reverse=True)
    kt, v, st_b = pl.pallas_call(
        functools.partial(_kv_kernel, nt_kv),
        grid=(bsz * nt_kv + 1,),
        in_specs=[pl.BlockSpec((1, TILE_KV, d), lambda s: (*cur(s), 0)),
                  pl.BlockSpec((1, 1, d3), lambda s: (cur(s)[0], 0, 0)),
                  _full((1, d)), _full((d, n_kv)),
                  pl.BlockSpec((TILE_KV, LANES), lambda s: (prv(s)[1], 0)),
                  pl.BlockSpec((TILE_KV, LANES), lambda s: (prv(s)[1], 0)),
                  _full((QK_W, CHUNK)), _full((QK_W, DV)),
                  pl.BlockSpec((1, QK_W, DV), lambda s: (prv(s)[0], 0, 0))],
        out_specs=[pl.BlockSpec((1, cpt_kv, QK_W, CHUNK), lambda s: (*prv(s), 0, 0)),
                   pl.BlockSpec((1, TILE_KV, RET_W), lambda s: (*prv(s), 0)),
                   pl.BlockSpec((1, cpt_kv, QK_W, DV), lambda s: (*prv(s), 0, 0))],
        out_shape=[jax.ShapeDtypeStruct((bsz, n_chunks, QK_W, CHUNK), BF16),
                   jax.ShapeDtypeStruct((bsz, seq, RET_W), BF16),
                   jax.ShapeDtypeStruct((bsz, n_chunks, QK_W, DV), BF16)],
        scratch_shapes=[pltpu.VMEM((QK_W, DV), F32),
                        pltpu.VMEM((TILE_KV, d), BF16),
                        pltpu.VMEM((TILE_KV, n_kv), F32),
                        pltpu.VMEM((TILE_KV, n_kv), F32)],
        compiler_params=pltpu.CompilerParams(dimension_semantics=("arbitrary",)),
        name="kv_sweep",
    )(x, mod3, nw, w_kv, cos_t, sin_t, kdbt, gb, s_cb)

    nt = seq // TILE_MAIN
    cpt = TILE_MAIN // CHUNK
    n_rest = w_rest.shape[1]
    cur, prv = tile_maps(bsz * nt, nt, reverse=False)
    out = pl.pallas_call(
        functools.partial(_main_kernel, nt),
        grid=(bsz * nt + 1,),
        in_specs=[pl.BlockSpec((1, TILE_MAIN, d), lambda s: (*cur(s), 0)),
                  pl.BlockSpec((1, 1, d3), lambda s: (cur(s)[0], 0, 0)),
                  pl.BlockSpec((1, TILE_MAIN, d), lambda s: (*prv(s), 0)),
                  pl.BlockSpec((1, 1, d3), lambda s: (prv(s)[0], 0, 0)),
                  _full((1, d)), _full((d, n_rest)),
                  pl.BlockSpec((TILE_MAIN, LANES), lambda s: (prv(s)[1], 0)),
                  pl.BlockSpec((TILE_MAIN, LANES), lambda s: (prv(s)[1], 0)),
                  pl.BlockSpec((1, cpt, QK_W, CHUNK), lambda s: (*prv(s), 0, 0)),
                  pl.BlockSpec((1, TILE_MAIN, RET_W), lambda s: (*prv(s), 0)),
                  pl.BlockSpec((1, cpt, QK_W, DV), lambda s: (*prv(s), 0, 0)),
                  pl.BlockSpec((1, QK_W, DV), lambda s: (prv(s)[0], 0, 0)),
                  _full((HEADS, CHUNK, CHUNK)),
                  _full((CHUNK, QK_W)), _full((CHUNK, QK_W)),
                  _full((QK_W, CHUNK)), _full((QK_W, DV)),
                  _full((1, RET_W)), _full((1, GM_W)),
                  _full((GROUPS, CHUNK, CHUNK)), _full((GROUPS, CHUNK, DG)),
                  _full((RET_W + GM_W, d)), _full((1, d))],
        out_specs=pl.BlockSpec((1, TILE_MAIN, d), lambda s: (*prv(s), 0)),
        out_shape=jax.ShapeDtypeStruct((bsz, seq, d), x.dtype),
        scratch_shapes=[pltpu.VMEM((QK_W, DV), F32),
                        pltpu.VMEM((TILE_MAIN, RET_W + GM_W), BF16),
                        pltpu.VMEM((TILE_MAIN, d), BF16),
                        pltpu.VMEM((TILE_MAIN, n_rest), F32),
                        pltpu.VMEM((TILE_MAIN, n_rest), F32)],
        compiler_params=pltpu.CompilerParams(dimension_semantics=("arbitrary",)),
        name="main_sweep",
    )(x, mod3, x, mod3, nw, w_rest, cos_t, sin_t, kt, v, st_b, s_cf, dsum, af, ab, kdft, gf,
      ret_gn_w[0][None, :], gmlp_ln_w[0][None, :], w_s_b, b_s_full, w_out_b,
      final_norm_w[None, :])
    return out
```

```python
import functools

import jax
import jax.numpy as jnp
from jax import lax
from jax.experimental import pallas as pl
from jax.experimental.pallas import tpu as pltpu

F32 = jnp.float32
BF16 = jnp.bfloat16

HEADS = 4
DK = 64
DV = 128
CHUNK = 128
GROUPS = 4
DG = 128
GRID_W = 64
ROPE_BASE = 10000.0
EPS = 1e-6
LANES = 128
PAIRS = HEADS * DK // LANES
RET_W = HEADS * DV
GM_W = GROUPS * DG
QK_W = HEADS * DK

DOT_COLS = 512
TILE_KV = 1024
TILE_MAIN = 512


def _rms(x, w):
    ms = jnp.mean(x * x, axis=-1, keepdims=True)
    return x * lax.rsqrt(ms + EPS) * w


def _silu(x):
    return x * jax.nn.sigmoid(x)


def _gelu(x):
    return 0.5 * x * (1.0 + lax.erf(x * (0.5 ** 0.5)))


_dot = functools.partial(jnp.dot, preferred_element_type=F32)


def _col_groups(n):
    first = n % DOT_COLS
    edges = ([0] if first == 0 else [0, first]) + list(range(first + DOT_COLS, n + 1, DOT_COLS))
    return list(zip(edges[:-1], edges[1:]))


def _rope(p, cos, sin_signed, upper16):
    partner = jnp.where(upper16, pltpu.roll(p, 16, 1), pltpu.roll(p, LANES - 16, 1))
    return p * cos + partner * sin_signed


def _kv_pairs(kd, v):
    out = []
    for p in range(PAIRS):
        r = _dot(kd[p * LANES:(p + 1) * LANES, :], v[:, p * 2 * DV:(p + 1) * 2 * DV])
        out.append(r[0:DK, 0:DV])
        out.append(r[DK:2 * DK, DV:2 * DV])
    return jnp.concatenate(out, axis=0)


def _emit_pipelined(big_pieces, side_pieces, tasks, depth):
    in_flight = []
    for tick in range(max(len(tasks) + depth - 1, len(big_pieces), len(side_pieces))):
        if tick < len(big_pieces):
            big_pieces[tick]()
        urgent, relaxed = side_pieces[tick] if tick < len(side_pieces) else (None, None)
        if urgent is not None:
            urgent()
        if tick < len(tasks):
            in_flight.append([tasks[tick][0], tasks[tick][1]])
        for t in [t for t in in_flight if t[1] > 1] + [t for t in in_flight if t[1] == 1]:
            next(t[0], None)
            t[1] -= 1
        in_flight = [t for t in in_flight if t[1] > 0]
        if relaxed is not None:
            relaxed()
    assert not in_flight


def _by_parity(step, body, refs, bufs_a, bufs_b):
    @pl.when(lax.rem(step, 2) == 0)
    def _even():
        body(*refs, *bufs_a, *bufs_b)

    @pl.when(lax.rem(step, 2) == 1)
    def _odd():
        body(*refs, *bufs_b, *bufs_a)


def _prologue_kernel(decf_ref, decb_ref, cc_ref, w_ref, b_ref, invf_ref,
                     mod_ref, cos_ref, sin_ref, dsum_ref, af_ref, ab_ref,
                     kdft_ref, kdbt_ref, gf_ref, gb_ref):
    c = cc_ref[...]
    mod_ref[...] = _dot(_silu(c), w_ref[...]) + b_ref[...]

    @pl.when(pl.program_id(0) == 0)
    def _tables():
        pos = lax.broadcasted_iota(jnp.int32, (GRID_W, LANES), 0).astype(F32)
        lane = lax.broadcasted_iota(jnp.int32, (GRID_W, LANES), 1)
        ang = pos * invf_ref[...]
        cr = jnp.cos(ang)
        sr = jnp.sin(ang)
        sr = jnp.where((lane & 16) == 0, -sr, sr)
        by_row = (lane & 32) == 0
        n_rows = cos_ref.shape[0] // GRID_W
        for r in range(n_rows):
            sl = slice(r * GRID_W, (r + 1) * GRID_W)
            cos_ref[sl, :] = jnp.where(by_row, jnp.broadcast_to(cr[r:r + 1, :], cr.shape), cr)
            sin_ref[sl, :] = jnp.where(by_row, jnp.broadcast_to(sr[r:r + 1, :], sr.shape), sr)

        def log_decay(head_idx, dec_ref):
            d = jnp.full(head_idx.shape, dec_ref[HEADS - 1], F32)
            for h in range(HEADS - 2, -1, -1):
                d = jnp.where(head_idx == h, dec_ref[h], d)
            return -jnp.exp(d)

        n = lax.broadcasted_iota(jnp.int32, (CHUNK, QK_W), 0).astype(F32)
        hl = lax.broadcasted_iota(jnp.int32, (CHUNK, QK_W), 1) // DK
        af_ref[...] = jnp.exp((n + 1.0) * log_decay(hl, decf_ref))
        ab_ref[...] = jnp.exp((CHUNK - n) * log_decay(hl, decb_ref))

        m = lax.broadcasted_iota(jnp.int32, (QK_W, CHUNK), 1).astype(F32)
        hr = lax.broadcasted_iota(jnp.int32, (QK_W, CHUNK), 0) // DK
        lgf = log_decay(hr, decf_ref)
        lgb = log_decay(hr, decb_ref)
        kdft_ref[...] = jnp.exp((CHUNK - 1.0 - m) * lgf)
        kdbt_ref[...] = jnp.exp(m * lgb)
        gf_ref[...] = jnp.exp(float(CHUNK) * lgf)
        gb_ref[...] = jnp.exp(float(CHUNK) * lgb)

        nn = lax.broadcasted_iota(jnp.int32, (CHUNK, CHUNK), 0).astype(F32)
        mm = lax.broadcasted_iota(jnp.int32, (CHUNK, CHUNK), 1).astype(F32)
        for h in range(HEADS):
            lf = -jnp.exp(jnp.full((CHUNK, CHUNK), decf_ref[h], F32))
            lb = -jnp.exp(jnp.full((CHUNK, CHUNK), decb_ref[h], F32))
            df = nn - mm
            db = mm - nn
            fwd = jnp.where(df >= 0, jnp.exp(jnp.maximum(df, 0.0) * lf), 0.0)
            bwd = jnp.where(db >= 0, jnp.exp(jnp.maximum(db, 0.0) * lb), 0.0)
            dsum_ref[h] = fwd + bwd


def _ctx_kernel(ctx_ref, mod_ref, nw_ref, wkv_ref, kdft_ref, kdbt_ref, gf_ref, gb_ref,
                scf_ref, scb_ref):
    d = ctx_ref.shape[-1]
    x = ctx_ref[0]
    mod = mod_ref[0]
    hc = _rms(x, nw_ref[...]) * (1.0 + mod[:, d:2 * d]) + mod[:, 0:d]
    pkv = _dot(hc.astype(BF16), wkv_ref[...])
    k = pkv[:, 0:QK_W] * (DK ** -0.5)
    v = pkv[:, QK_W:].astype(BF16)
    n_chunks = x.shape[0] // CHUNK
    sf = jnp.zeros((QK_W, DV), F32)
    for c in range(n_chunks):
        sl = slice(c * CHUNK, (c + 1) * CHUNK)
        kd = (k[sl, :].T * kdft_ref[...]).astype(BF16)
        sf = gf_ref[...] * sf + _kv_pairs(kd, v[sl, :])
    sb = jnp.zeros((QK_W, DV), F32)
    for c in range(n_chunks - 1, -1, -1):
        sl = slice(c * CHUNK, (c + 1) * CHUNK)
        kd = (k[sl, :].T * kdbt_ref[...]).astype(BF16)
        sb = gb_ref[...] * sb + _kv_pairs(kd, v[sl, :])
    scf_ref[0] = sf
    scb_ref[0] = sb


def _kv_kernel(tiles_per_batch, xc_ref, modc_ref, nw_ref, wkv_ref, cos_ref, sin_ref, kdbt_ref,
               gb_ref, scb_ref, h_ref, kt_ref, v_ref, stb_ref, sb_ref, pa_ref, pb_ref):
    s = pl.program_id(0)
    prev = jnp.maximum(s - 1, 0)

    @pl.when(s == 0)
    def _first():
        pb_ref[...] = jnp.zeros(pb_ref.shape, pb_ref.dtype)

    @pl.when(lax.rem(prev, tiles_per_batch) == 0)
    def _init():
        sb_ref[...] = scb_ref[0]

    refs = (xc_ref, modc_ref, nw_ref, wkv_ref, cos_ref, sin_ref, kdbt_ref, gb_ref,
            h_ref, kt_ref, v_ref, stb_ref, sb_ref)
    _by_parity(s, _kv_step, refs, (pa_ref,), (pb_ref,))


def _kv_step(xc_ref, modc_ref, nw_ref, wkv_ref, cos_ref, sin_ref, kdbt_ref, gb_ref,
             h_ref, kt_ref, v_ref, stb_ref, sb_ref, p_new_ref, p_old):
    d = xc_ref.shape[-1]
    tile = xc_ref.shape[1]
    half = tile // 2
    row_halves = [slice(0, half), slice(half, tile)]
    quarter = [slice(i * half // 2, (i + 1) * half // 2) for i in range(4)]
    modc = modc_ref[0]
    upper16 = (lax.broadcasted_iota(jnp.int32, (CHUNK, LANES), 1) & 16) != 0

    def norm_rows(rows):
        h = _rms(xc_ref[0, rows, :], nw_ref[...]) * (1.0 + modc[:, d:2 * d]) + modc[:, 0:d]
        h_ref[0, rows, :] = h.astype(BF16)

    def project(rows, lo, hi):
        p_new_ref[rows, lo:hi] = _dot(h_ref[0, rows, :], wkv_ref[:, lo:hi])

    def chunk(c):
        sl = slice(c * CHUNK, (c + 1) * CHUNK)
        k = [_rope(p_old[sl, p * LANES:(p + 1) * LANES], cos_ref[sl, :], sin_ref[sl, :], upper16)
             * (DK ** -0.5) for p in range(PAIRS)]
        v = p_old[sl, QK_W:].astype(BF16)
        v_ref[0, sl, :] = v
        kt = jnp.concatenate([kp.T for kp in k], axis=0)
        kt_ref[0, c] = kt.astype(BF16)
        new = _kv_pairs((kt * kdbt_ref[...]).astype(BF16), v)
        yield
        stb_ref[0, c] = sb_ref[...].astype(BF16)
        sb_ref[...] = gb_ref[...] * sb_ref[...] + new

    norm_rows(quarter[0])
    norm_rows(quarter[1])
    projections = [functools.partial(project, row_halves[r], lo, hi)
                   for r in range(2) for lo, hi in _col_groups(p_new_ref.shape[1])]
    late_norms = [(functools.partial(norm_rows, q), None) for q in quarter[2:]]
    tasks = [(chunk(c), 2) for c in range(tile // CHUNK - 1, -1, -1)]
    _emit_pipelined(projections, late_norms, tasks, depth=2)


def _main_kernel(tiles_per_batch, h_ref, x_ref, mod_ref, w_ref, cos_ref,
                 sin_ref, kt_ref, v_ref, stb_ref, scf_ref, dsum_ref, af_ref, ab_ref, kdft_ref,
                 gf_ref, gnw_ref, lnw_ref, ws_ref, bs_ref, wout_ref, fnw_ref, o_ref,
                 sf_ref, pa_ref, pb_ref, ya_ref, yb_ref):
    s = pl.program_id(0)
    prev = jnp.maximum(s - 1, 0)

    @pl.when(s == 0)
    def _first():
        pb_ref[...] = jnp.zeros(pb_ref.shape, pb_ref.dtype)
        yb_ref[...] = jnp.zeros(yb_ref.shape, yb_ref.dtype)

    @pl.when(lax.rem(prev, tiles_per_batch) == 0)
    def _init():
        sf_ref[...] = scf_ref[0]

    refs = (h_ref, x_ref, mod_ref, w_ref, cos_ref, sin_ref, kt_ref, v_ref,
            stb_ref, dsum_ref, af_ref, ab_ref, kdft_ref, gf_ref, gnw_ref, lnw_ref, ws_ref,
            bs_ref, wout_ref, fnw_ref, o_ref, sf_ref)
    _by_parity(s, _main_step, refs, (pa_ref, ya_ref), (pb_ref, yb_ref))


def _main_step(h_ref, x_ref, mod_ref, w_ref, cos_ref, sin_ref, kt_ref, v_ref,
               stb_ref, dsum_ref, af_ref, ab_ref, kdft_ref, gf_ref, gnw_ref, lnw_ref, ws_ref,
               bs_ref, wout_ref, fnw_ref, o_ref, sf_ref, p_new_ref, y_ref, p_all, y_old):
    d = x_ref.shape[-1]
    tile = x_ref.shape[1]
    n_chunks = tile // CHUNK
    half = tile // 2
    row_halves = [slice(0, half), slice(half, tile)]
    c_gr = QK_W
    c_u = c_gr + RET_W
    c_vm = c_u + GM_W
    c_gm = c_vm + GM_W
    mod = mod_ref[0]
    lane = lax.broadcasted_iota(jnp.int32, (CHUNK, LANES), 1)
    upper16 = (lane & 16) != 0
    head_of_lane = lane // DK

    def project(rows, lo, hi):
        p_new_ref[rows, lo:hi] = _dot(h_ref[0, rows, :], w_ref[:, lo:hi])

    def retention(c, p):
        sl = slice(c * CHUNK, (c + 1) * CHUNK)
        ps = slice(p * LANES, (p + 1) * LANES)
        q2 = _rope(p_all[sl, ps], cos_ref[sl, :], sin_ref[sl, :], upper16)
        kt = kt_ref[0, c, ps, :]
        vp = v_ref[0, sl, p * 2 * DV:(p + 1) * 2 * DV]
        qa = q2 * af_ref[:, ps]
        qb = q2 * ab_ref[:, ps]
        scores, qx = [], []
        for hh in range(2):
            mine = head_of_lane == hh
            scores.append(_dot(jnp.where(mine, q2, 0.0).astype(BF16), kt))
            qx.append(jnp.concatenate([jnp.where(mine, qa, 0.0), jnp.where(mine, qb, 0.0)],
                                      axis=1).astype(BF16))
        kd = (kt.astype(F32) * kdft_ref[ps, :]).astype(BF16)
        yield
        state = jnp.concatenate([sf_ref[ps, :].astype(BF16), stb_ref[0, c, ps, :]], axis=0)
        o = []
        for hh in range(2):
            a = (scores[hh] * dsum_ref[2 * p + hh]).astype(BF16)
            o.append(_dot(a, vp[:, hh * DV:(hh + 1) * DV]) + _dot(qx[hh], state))
        r = _dot(kd, vp)
        yield
        new = jnp.concatenate([r[0:DK, 0:DV], r[DK:2 * DK, DV:2 * DV]], axis=0)
        sf_ref[ps, :] = gf_ref[ps, :] * sf_ref[ps, :] + new
        for hh in range(2):
            head = 2 * p + hh
            mu = jnp.mean(o[hh], axis=-1, keepdims=True)
            var = jnp.mean(jnp.square(o[hh] - mu), axis=-1, keepdims=True)
            hs = slice(head * DV, (head + 1) * DV)
            y = (o[hh] - mu) * lax.rsqrt(var + EPS) * gnw_ref[:, hs]
            gate_cols = slice(c_gr + head * DV, c_gr + (head + 1) * DV)
            y_ref[sl, hs] = (y * _silu(p_all[sl, gate_cols])).astype(BF16)

    def gating(c):
        sl = slice(c * CHUNK, (c + 1) * CHUNK)
        vg = _gelu(p_all[sl, c_vm:c_gm])
        mu = jnp.mean(vg, axis=-1, keepdims=True)
        var = jnp.mean(jnp.square(vg - mu), axis=-1, keepdims=True)
        vn = ((vg - mu) * lax.rsqrt(var + EPS) * lnw_ref[...]).astype(BF16)
        s = [_dot(ws_ref[g], vn[:, g * DG:(g + 1) * DG]) for g in range(GROUPS)]
        yield
        for g in range(GROUPS):
            u = _gelu(p_all[sl, c_u + g * DG:c_u + (g + 1) * DG])
            gm = _silu(p_all[sl, c_gm + g * DG:c_gm + (g + 1) * DG])
            y_ref[sl, RET_W + g * DG:RET_W + (g + 1) * DG] = (u * (s[g] + bs_ref[g]) * gm).astype(BF16)

    out_groups = _col_groups(d)
    xn = {}

    def out_project(r, g):
        rows = row_halves[r]
        lo, hi = out_groups[g]
        out = _dot(y_old[rows, :], wout_ref[:, lo:hi])
        xn[r, g] = x_ref[0, rows, lo:hi] + mod[:, 2 * d + lo:2 * d + hi] * out

    def finish(r, part):
        sub = slice(part * half // 2, (part + 1) * half // 2)
        rows = slice(row_halves[r].start + sub.start, row_halves[r].start + sub.stop)
        parts = [xn[r, g][sub, :] for g in range(len(out_groups))]
        ms = sum(jnp.sum(t * t, axis=-1, keepdims=True) for t in parts) * (1.0 / d)
        rs = lax.rsqrt(ms + EPS)
        for (lo, hi), t in zip(out_groups, parts):
            o_ref[0, rows, lo:hi] = t * rs * fnw_ref[:, lo:hi]

    tasks = []
    for c in range(n_chunks):
        tasks += [(retention(c, p), 3) for p in range(PAIRS)] + [(gating(c), 2)]
    in_groups = _col_groups(p_new_ref.shape[1])
    big = []
    for r in range(2):
        big += [functools.partial(out_project, r, g) for g in range(len(out_groups))]
        big += [functools.partial(project, row_halves[r], lo, hi) for lo, hi in in_groups]
    n_out = len(out_groups)
    per_half = n_out + len(in_groups)
    relaxed = [None] * len(big)
    for r in range(2):
        relaxed[r * per_half + n_out + 1] = functools.partial(finish, r, 0)
        relaxed[r * per_half + n_out + 3] = functools.partial(finish, r, 1)
    _emit_pipelined(big, [(None, piece) for piece in relaxed], tasks, depth=3)


def _full(shape):
    return pl.BlockSpec(shape, lambda *_: (0,) * len(shape))


def kernel(x, c, ctx, c_ctx, w_ada, b_ada, norm_w, w_in, ret_decay_f, ret_decay_b, ret_gn_w,
           gmlp_ln_w, w_s, b_s, w_out, final_norm_w):
    bsz, seq, d = x.shape
    ctx_len = ctx.shape[1]
    depth = w_ada.shape[0]
    assert depth == 1 and d % LANES == 0
    assert seq % TILE_KV == 0 and seq % TILE_MAIN == 0 and ctx_len % CHUNK == 0
    assert seq % GRID_W == 0 and TILE_MAIN % (4 * CHUNK) == 0 and TILE_KV % (4 * CHUNK) == 0
    n_chunks = seq // CHUNK
    d3 = 3 * d
    k0 = QK_W
    v0 = 2 * QK_W
    gr0 = v0 + RET_W

    w_in0 = w_in[0]
    w_kv = w_in0[:, k0:gr0].astype(BF16)
    w_rest = jnp.concatenate([w_in0[:, 0:k0], w_in0[:, gr0:]], axis=1).astype(BF16)
    w_out_b = w_out[0].astype(BF16)
    w_s_b = w_s[0].astype(BF16)
    b_s_full = jnp.broadcast_to(b_s[0][:, :, None], (GROUPS, CHUNK, DG))
    cc = jnp.concatenate([c, jnp.broadcast_to(c_ctx[None, :], (bsz, d))], axis=0)
    quarter = DK // 4
    inv_freq = ROPE_BASE ** (-jnp.arange(quarter, dtype=F32) / quarter)
    invf_lane = jnp.tile(inv_freq, LANES // quarter)[None, :]
    nw = norm_w[0][None, :]
    rows2 = 2 * bsz

    n_col_blocks = 3
    smem = pl.BlockSpec(memory_space=pltpu.SMEM)
    (mod, cos_t, sin_t, dsum, af, ab, kdft, kdbt, gf, gb) = pl.pallas_call(
        _prologue_kernel,
        grid=(n_col_blocks,),
        in_specs=[smem, smem, _full((rows2, d)),
                  pl.BlockSpec((d, d3 // n_col_blocks), lambda i: (0, i)),
                  pl.BlockSpec((1, d3 // n_col_blocks), lambda i: (0, i)),
                  _full((1, LANES))],
        out_specs=[pl.BlockSpec((rows2, d3 // n_col_blocks), lambda i: (0, i)),
                   _full((seq, LANES)), _full((seq, LANES)),
                   _full((HEADS, CHUNK, CHUNK)),
                   _full((CHUNK, QK_W)), _full((CHUNK, QK_W)),
                   _full((QK_W, CHUNK)), _full((QK_W, CHUNK)),
                   _full((QK_W, DV)), _full((QK_W, DV))],
        out_shape=[jax.ShapeDtypeStruct((rows2, d3), F32),
                   jax.ShapeDtypeStruct((seq, LANES), F32),
                   jax.ShapeDtypeStruct((seq, LANES), F32),
                   jax.ShapeDtypeStruct((HEADS, CHUNK, CHUNK), F32),
                   jax.ShapeDtypeStruct((CHUNK, QK_W), F32),
                   jax.ShapeDtypeStruct((CHUNK, QK_W), F32),
                   jax.ShapeDtypeStruct((QK_W, CHUNK), F32),
                   jax.ShapeDtypeStruct((QK_W, CHUNK), F32),
                   jax.ShapeDtypeStruct((QK_W, DV), F32),
                   jax.ShapeDtypeStruct((QK_W, DV), F32)],
        compiler_params=pltpu.CompilerParams(dimension_semantics=("arbitrary",)),
        name="prologue",
    )(ret_decay_f[0], ret_decay_b[0], cc, w_ada[0], b_ada[0][None, :], invf_lane)
    mod3 = mod.reshape(rows2, 1, d3)

    s_cf, s_cb = pl.pallas_call(
        _ctx_kernel,
        grid=(bsz,),
        in_specs=[pl.BlockSpec((1, ctx_len, d), lambda b: (b, 0, 0)),
                  pl.BlockSpec((1, 1, d3), lambda b: (bsz, 0, 0)),
                  _full((1, d)), _full((d, QK_W + RET_W)),
                  _full((QK_W, CHUNK)), _full((QK_W, CHUNK)),
                  _full((QK_W, DV)), _full((QK_W, DV))],
        out_specs=[pl.BlockSpec((1, QK_W, DV), lambda b: (b, 0, 0)),
                   pl.BlockSpec((1, QK_W, DV), lambda b: (b, 0, 0))],
        out_shape=[jax.ShapeDtypeStruct((bsz, QK_W, DV), F32),
                   jax.ShapeDtypeStruct((bsz, QK_W, DV), F32)],
        compiler_params=pltpu.CompilerParams(dimension_semantics=("arbitrary",)),
        name="ctx_states",
    )(ctx, mod3, nw, w_kv, kdft, kdbt, gf, gb)

    def tile_maps(n_tiles, per_batch, reverse):
        def where(t):
            j = t % per_batch
            return t // per_batch, (per_batch - 1 - j) if reverse else j
        return [lambda s, k=k: where(jnp.clip(s - k, 0, n_tiles - 1)) for k in range(3)]

    nt_kv = seq // TILE_KV
    cpt_kv = TILE_KV // CHUNK
    n_kv = QK_W + RET_W
    cur, prv, _ = tile_maps(bsz * nt_kv, nt_kv, reverse=True)
    h, kt, v, st_b = pl.pallas_call(
        functools.partial(_kv_kernel, nt_kv),
        grid=(bsz * nt_kv + 1,),
        in_specs=[pl.BlockSpec((1, TILE_KV, d), lambda s: (*cur(s), 0)),
                  pl.BlockSpec((1, 1, d3), lambda s: (cur(s)[0], 0, 0)),
                  _full((1, d)), _full((d, n_kv)),
                  pl.BlockSpec((TILE_KV, LANES), lambda s: (prv(s)[1], 0)),
                  pl.BlockSpec((TILE_KV, LANES), lambda s: (prv(s)[1], 0)),
                  _full((QK_W, CHUNK)), _full((QK_W, DV)),
                  pl.BlockSpec((1, QK_W, DV), lambda s: (prv(s)[0], 0, 0))],
        out_specs=[pl.BlockSpec((1, TILE_KV, d), lambda s: (*cur(s), 0)),
                   pl.BlockSpec((1, cpt_kv, QK_W, CHUNK), lambda s: (*prv(s), 0, 0)),
                   pl.BlockSpec((1, TILE_KV, RET_W), lambda s: (*prv(s), 0)),
                   pl.BlockSpec((1, cpt_kv, QK_W, DV), lambda s: (*prv(s), 0, 0))],
        out_shape=[jax.ShapeDtypeStruct((bsz, seq, d), BF16),
                   jax.ShapeDtypeStruct((bsz, n_chunks, QK_W, CHUNK), BF16),
                   jax.ShapeDtypeStruct((bsz, seq, RET_W), BF16),
                   jax.ShapeDtypeStruct((bsz, n_chunks, QK_W, DV), BF16)],
        scratch_shapes=[pltpu.VMEM((QK_W, DV), F32),
                        pltpu.VMEM((TILE_KV, n_kv), F32),
                        pltpu.VMEM((TILE_KV, n_kv), F32)],
        compiler_params=pltpu.CompilerParams(dimension_semantics=("arbitrary",)),
        name="kv_sweep",
    )(x, mod3, nw, w_kv, cos_t, sin_t, kdbt, gb, s_cb)

    nt = seq // TILE_MAIN
    cpt = TILE_MAIN // CHUNK
    n_rest = w_rest.shape[1]
    cur, prv, prv2 = tile_maps(bsz * nt, nt, reverse=False)
    out = pl.pallas_call(
        functools.partial(_main_kernel, nt),
        grid=(bsz * nt + 2,),
        in_specs=[pl.BlockSpec((1, TILE_MAIN, d), lambda s: (*cur(s), 0)),
                  pl.BlockSpec((1, TILE_MAIN, d), lambda s: (*prv2(s), 0)),
                  pl.BlockSpec((1, 1, d3), lambda s: (prv2(s)[0], 0, 0)),
                  _full((d, n_rest)),
                  pl.BlockSpec((TILE_MAIN, LANES), lambda s: (prv(s)[1], 0)),
                  pl.BlockSpec((TILE_MAIN, LANES), lambda s: (prv(s)[1], 0)),
                  pl.BlockSpec((1, cpt, QK_W, CHUNK), lambda s: (*prv(s), 0, 0)),
                  pl.BlockSpec((1, TILE_MAIN, RET_W), lambda s: (*prv(s), 0)),
                  pl.BlockSpec((1, cpt, QK_W, DV), lambda s: (*prv(s), 0, 0)),
                  pl.BlockSpec((1, QK_W, DV), lambda s: (prv(s)[0], 0, 0)),
                  _full((HEADS, CHUNK, CHUNK)),
                  _full((CHUNK, QK_W)), _full((CHUNK, QK_W)),
                  _full((QK_W, CHUNK)), _full((QK_W, DV)),
                  _full((1, RET_W)), _full((1, GM_W)),
                  _full((GROUPS, CHUNK, CHUNK)), _full((GROUPS, CHUNK, DG)),
                  _full((RET_W + GM_W, d)), _full((1, d))],
        out_specs=pl.BlockSpec((1, TILE_MAIN, d), lambda s: (*prv2(s), 0)),
        out_shape=jax.ShapeDtypeStruct((bsz, seq, d), x.dtype),
        scratch_shapes=[pltpu.VMEM((QK_W, DV), F32),
                        pltpu.VMEM((TILE_MAIN, n_rest), F32),
                        pltpu.VMEM((TILE_MAIN, n_rest), F32),
                        pltpu.VMEM((TILE_MAIN, RET_W + GM_W), BF16),
                        pltpu.VMEM((TILE_MAIN, RET_W + GM_W), BF16)],
        compiler_params=pltpu.CompilerParams(dimension_semantics=("arbitrary",)),
        name="main_sweep",
    )(h, x, mod3, w_rest, cos_t, sin_t, kt, v, st_b, s_cf, dsum, af, ab, kdft, gf,
      ret_gn_w[0][None, :], gmlp_ln_w[0][None, :], w_s_b, b_s_full, w_out_b,
      final_norm_w[None, :])
    return out
```

```python
import functools

import jax
import jax.numpy as jnp
from jax import lax
from jax.experimental import pallas as pl
from jax.experimental.pallas import tpu as pltpu

F32 = jnp.float32
BF16 = jnp.bfloat16

HEADS = 4
DK = 64
DV = 128
CHUNK = 128
GROUPS = 4
DG = 128
GRID_W = 64
ROPE_BASE = 10000.0
EPS = 1e-6
LANES = 128
PAIRS = HEADS * DK // LANES
RET_W = HEADS * DV
GM_W = GROUPS * DG
QK_W = HEADS * DK

DOT_COLS = 512
TILE_KV = 1024
TILE_MAIN = 512


def _rms(x, w):
    ms = jnp.mean(x * x, axis=-1, keepdims=True)
    return x * lax.rsqrt(ms + EPS) * w


def _silu(x):
    return x * jax.nn.sigmoid(x)


def _gelu(x):
    return 0.5 * x * (1.0 + lax.erf(x * (0.5 ** 0.5)))


_dot = functools.partial(jnp.dot, preferred_element_type=F32)


def _col_groups(n):
    first = n % DOT_COLS
    edges = ([0] if first == 0 else [0, first]) + list(range(first + DOT_COLS, n + 1, DOT_COLS))
    return list(zip(edges[:-1], edges[1:]))


def _rope(p, cos, sin_signed, upper16):
    partner = jnp.where(upper16, pltpu.roll(p, 16, 1), pltpu.roll(p, LANES - 16, 1))
    return p * cos + partner * sin_signed


def _kv_pairs(kd, v):
    out = []
    for p in range(PAIRS):
        r = _dot(kd[p * LANES:(p + 1) * LANES, :], v[:, p * 2 * DV:(p + 1) * 2 * DV])
        out.append(r[0:DK, 0:DV])
        out.append(r[DK:2 * DK, DV:2 * DV])
    return jnp.concatenate(out, axis=0)


def _emit_pipelined(big_pieces, side_pieces, tasks, depth):
    in_flight = []
    for tick in range(max(len(tasks) + depth - 1, len(big_pieces), len(side_pieces))):
        if tick < len(big_pieces):
            big_pieces[tick]()
        urgent, relaxed = side_pieces[tick] if tick < len(side_pieces) else (None, None)
        if urgent is not None:
            urgent()
        if tick < len(tasks):
            in_flight.append([tasks[tick][0], tasks[tick][1]])
        for t in [t for t in in_flight if t[1] > 1] + [t for t in in_flight if t[1] == 1]:
            next(t[0], None)
            t[1] -= 1
        in_flight = [t for t in in_flight if t[1] > 0]
        if relaxed is not None:
            relaxed()
    assert not in_flight


def _run_pipeline_step(step, n_tiles, n_stages, body, refs, bufs_a, bufs_b):
    def call(s_static_parity, stages):
        new, old = (bufs_a, bufs_b) if s_static_parity == 0 else (bufs_b, bufs_a)
        body(stages, *refs, *new, *old)

    edge = n_stages - 1
    for s in list(range(edge)) + list(range(n_tiles, n_tiles + edge)):
        stages = tuple(0 <= s - j < n_tiles for j in range(n_stages))
        pl.when(step == s)(functools.partial(call, s % 2, stages))
    steady = (step >= edge) & (step < n_tiles)
    for parity in range(2):
        pl.when(steady & (lax.rem(step, 2) == parity))(
            functools.partial(call, parity, (True,) * n_stages))


def _prologue_kernel(decf_ref, decb_ref, cc_ref, w_ref, b_ref, invf_ref,
                     mod_ref, cos_ref, sin_ref, dsum_ref, af_ref, ab_ref,
                     kdft_ref, kdbt_ref, gf_ref, gb_ref):
    c = cc_ref[...]
    mod_ref[...] = _dot(_silu(c), w_ref[...]) + b_ref[...]

    @pl.when(pl.program_id(0) == 0)
    def _tables():
        pos = lax.broadcasted_iota(jnp.int32, (GRID_W, LANES), 0).astype(F32)
        lane = lax.broadcasted_iota(jnp.int32, (GRID_W, LANES), 1)
        ang = pos * invf_ref[...]
        cr = jnp.cos(ang)
        sr = jnp.sin(ang)
        sr = jnp.where((lane & 16) == 0, -sr, sr)
        by_row = (lane & 32) == 0
        n_rows = cos_ref.shape[0] // GRID_W
        for r in range(n_rows):
            sl = slice(r * GRID_W, (r + 1) * GRID_W)
            cos_ref[sl, :] = jnp.where(by_row, jnp.broadcast_to(cr[r:r + 1, :], cr.shape), cr)
            sin_ref[sl, :] = jnp.where(by_row, jnp.broadcast_to(sr[r:r + 1, :], sr.shape), sr)

        def log_decay(head_idx, dec_ref):
            d = jnp.full(head_idx.shape, dec_ref[HEADS - 1], F32)
            for h in range(HEADS - 2, -1, -1):
                d = jnp.where(head_idx == h, dec_ref[h], d)
            return -jnp.exp(d)

        n = lax.broadcasted_iota(jnp.int32, (CHUNK, QK_W), 0).astype(F32)
        hl = lax.broadcasted_iota(jnp.int32, (CHUNK, QK_W), 1) // DK
        af_ref[...] = jnp.exp((n + 1.0) * log_decay(hl, decf_ref))
        ab_ref[...] = jnp.exp((CHUNK - n) * log_decay(hl, decb_ref))

        m = lax.broadcasted_iota(jnp.int32, (QK_W, CHUNK), 1).astype(F32)
        hr = lax.broadcasted_iota(jnp.int32, (QK_W, CHUNK), 0) // DK
        lgf = log_decay(hr, decf_ref)
        lgb = log_decay(hr, decb_ref)
        kdft_ref[...] = jnp.exp((CHUNK - 1.0 - m) * lgf)
        kdbt_ref[...] = jnp.exp(m * lgb)
        gf_ref[...] = jnp.exp(float(CHUNK) * lgf)
        gb_ref[...] = jnp.exp(float(CHUNK) * lgb)

        nn = lax.broadcasted_iota(jnp.int32, (CHUNK, CHUNK), 0).astype(F32)
        mm = lax.broadcasted_iota(jnp.int32, (CHUNK, CHUNK), 1).astype(F32)
        for h in range(HEADS):
            lf = -jnp.exp(jnp.full((CHUNK, CHUNK), decf_ref[h], F32))
            lb = -jnp.exp(jnp.full((CHUNK, CHUNK), decb_ref[h], F32))
            df = nn - mm
            db = mm - nn
            fwd = jnp.where(df >= 0, jnp.exp(jnp.maximum(df, 0.0) * lf), 0.0)
            bwd = jnp.where(db >= 0, jnp.exp(jnp.maximum(db, 0.0) * lb), 0.0)
            dsum_ref[h] = fwd + bwd


def _ctx_kernel(ctx_ref, mod_ref, nw_ref, wkv_ref, kdft_ref, kdbt_ref, gf_ref, gb_ref,
                scf_ref, scb_ref):
    d = ctx_ref.shape[-1]
    x = ctx_ref[0]
    mod = mod_ref[0]
    hc = _rms(x, nw_ref[...]) * (1.0 + mod[:, d:2 * d]) + mod[:, 0:d]
    pkv = _dot(hc.astype(BF16), wkv_ref[...])
    k = pkv[:, 0:QK_W] * (DK ** -0.5)
    v = pkv[:, QK_W:].astype(BF16)
    n_chunks = x.shape[0] // CHUNK
    sf = jnp.zeros((QK_W, DV), F32)
    for c in range(n_chunks):
        sl = slice(c * CHUNK, (c + 1) * CHUNK)
        kd = (k[sl, :].T * kdft_ref[...]).astype(BF16)
        sf = gf_ref[...] * sf + _kv_pairs(kd, v[sl, :])
    sb = jnp.zeros((QK_W, DV), F32)
    for c in range(n_chunks - 1, -1, -1):
        sl = slice(c * CHUNK, (c + 1) * CHUNK)
        kd = (k[sl, :].T * kdbt_ref[...]).astype(BF16)
        sb = gb_ref[...] * sb + _kv_pairs(kd, v[sl, :])
    scf_ref[0] = sf
    scb_ref[0] = sb


def _kv_kernel(n_tiles, tiles_per_batch, xc_ref, modc_ref, nw_ref, wkv_ref, cos_ref, sin_ref,
               kdbt_ref, gb_ref, scb_ref, h_ref, kt_ref, v_ref, stb_ref, sb_ref, pa_ref, pb_ref):
    s = pl.program_id(0)

    @pl.when((s >= 1) & (lax.rem(s - 1, tiles_per_batch) == 0))
    def _init():
        sb_ref[...] = scb_ref[0]

    refs = (xc_ref, modc_ref, nw_ref, wkv_ref, cos_ref, sin_ref, kdbt_ref, gb_ref,
            h_ref, kt_ref, v_ref, stb_ref, sb_ref)
    _run_pipeline_step(s, n_tiles, 2, _kv_step, refs, (pa_ref,), (pb_ref,))


def _kv_step(stages, xc_ref, modc_ref, nw_ref, wkv_ref, cos_ref, sin_ref, kdbt_ref, gb_ref,
             h_ref, kt_ref, v_ref, stb_ref, sb_ref, p_new_ref, p_old):
    project_stage, finish_stage = stages
    d = xc_ref.shape[-1]
    tile = xc_ref.shape[1]
    half = tile // 2
    row_halves = [slice(0, half), slice(half, tile)]
    quarter = [slice(i * half // 2, (i + 1) * half // 2) for i in range(4)]
    modc = modc_ref[0]
    upper16 = (lax.broadcasted_iota(jnp.int32, (CHUNK, LANES), 1) & 16) != 0

    def norm_rows(rows):
        h = _rms(xc_ref[0, rows, :], nw_ref[...]) * (1.0 + modc[:, d:2 * d]) + modc[:, 0:d]
        h_ref[0, rows, :] = h.astype(BF16)

    def project(rows, lo, hi):
        p_new_ref[rows, lo:hi] = _dot(h_ref[0, rows, :], wkv_ref[:, lo:hi])

    def chunk(c):
        sl = slice(c * CHUNK, (c + 1) * CHUNK)
        k = [_rope(p_old[sl, p * LANES:(p + 1) * LANES], cos_ref[sl, :], sin_ref[sl, :], upper16)
             * (DK ** -0.5) for p in range(PAIRS)]
        v = p_old[sl, QK_W:].astype(BF16)
        v_ref[0, sl, :] = v
        kt = jnp.concatenate([kp.T for kp in k], axis=0)
        kt_ref[0, c] = kt.astype(BF16)
        new = _kv_pairs((kt * kdbt_ref[...]).astype(BF16), v)
        yield
        stb_ref[0, c] = sb_ref[...].astype(BF16)
        sb_ref[...] = gb_ref[...] * sb_ref[...] + new

    projections, late_norms, tasks = [], [], []
    if project_stage:
        norm_rows(quarter[0])
        norm_rows(quarter[1])
        projections = [functools.partial(project, row_halves[r], lo, hi)
                       for r in range(2) for lo, hi in _col_groups(p_new_ref.shape[1])]
        late_norms = [(functools.partial(norm_rows, q), None) for q in quarter[2:]]
    if finish_stage:
        tasks = [(chunk(c), 2) for c in range(tile // CHUNK - 1, -1, -1)]
    _emit_pipelined(projections, late_norms, tasks, depth=2)


def _main_kernel(n_tiles, tiles_per_batch, h_ref, x_ref, mod_ref, w_ref, cos_ref,
                 sin_ref, kt_ref, v_ref, stb_ref, scf_ref, dsum_ref, af_ref, ab_ref, kdft_ref,
                 gf_ref, gnw_ref, lnw_ref, ws_ref, bs_ref, wout_ref, fnw_ref, o_ref,
                 sf_ref, pa_ref, pb_ref, ya_ref, yb_ref):
    s = pl.program_id(0)

    @pl.when((s >= 1) & (lax.rem(s - 1, tiles_per_batch) == 0))
    def _init():
        sf_ref[...] = scf_ref[0]

    refs = (h_ref, x_ref, mod_ref, w_ref, cos_ref, sin_ref, kt_ref, v_ref,
            stb_ref, dsum_ref, af_ref, ab_ref, kdft_ref, gf_ref, gnw_ref, lnw_ref, ws_ref,
            bs_ref, wout_ref, fnw_ref, o_ref, sf_ref)
    _run_pipeline_step(s, n_tiles, 3, _main_step, refs, (pa_ref, ya_ref), (pb_ref, yb_ref))


def _main_step(stages, h_ref, x_ref, mod_ref, w_ref, cos_ref, sin_ref, kt_ref, v_ref,
               stb_ref, dsum_ref, af_ref, ab_ref, kdft_ref, gf_ref, gnw_ref, lnw_ref, ws_ref,
               bs_ref, wout_ref, fnw_ref, o_ref, sf_ref, p_new_ref, y_ref, p_all, y_old):
    project_stage, mix_stage, out_stage = stages
    d = x_ref.shape[-1]
    tile = x_ref.shape[1]
    n_chunks = tile // CHUNK
    half = tile // 2
    row_halves = [slice(0, half), slice(half, tile)]
    c_gr = QK_W
    c_u = c_gr + RET_W
    c_vm = c_u + GM_W
    c_gm = c_vm + GM_W
    mod = mod_ref[0]
    lane = lax.broadcasted_iota(jnp.int32, (CHUNK, LANES), 1)
    upper16 = (lane & 16) != 0
    head_of_lane = lane // DK

    def project(rows, lo, hi):
        p_new_ref[rows, lo:hi] = _dot(h_ref[0, rows, :], w_ref[:, lo:hi])

    def retention(c, p):
        sl = slice(c * CHUNK, (c + 1) * CHUNK)
        ps = slice(p * LANES, (p + 1) * LANES)
        q2 = _rope(p_all[sl, ps], cos_ref[sl, :], sin_ref[sl, :], upper16)
        kt = kt_ref[0, c, ps, :]
        vp = v_ref[0, sl, p * 2 * DV:(p + 1) * 2 * DV]
        qa = q2 * af_ref[:, ps]
        qb = q2 * ab_ref[:, ps]
        scores, qx = [], []
        for hh in range(2):
            mine = head_of_lane == hh
            scores.append(_dot(jnp.where(mine, q2, 0.0).astype(BF16), kt))
            qx.append(jnp.concatenate([jnp.where(mine, qa, 0.0), jnp.where(mine, qb, 0.0)],
                                      axis=1).astype(BF16))
        kd = (kt.astype(F32) * kdft_ref[ps, :]).astype(BF16)
        yield
        state = jnp.concatenate([sf_ref[ps, :].astype(BF16), stb_ref[0, c, ps, :]], axis=0)
        o = []
        for hh in range(2):
            a = (scores[hh] * dsum_ref[2 * p + hh]).astype(BF16)
            o.append(_dot(a, vp[:, hh * DV:(hh + 1) * DV]) + _dot(qx[hh], state))
        r = _dot(kd, vp)
        yield
        new = jnp.concatenate([r[0:DK, 0:DV], r[DK:2 * DK, DV:2 * DV]], axis=0)
        sf_ref[ps, :] = gf_ref[ps, :] * sf_ref[ps, :] + new
        for hh in range(2):
            head = 2 * p + hh
            mu = jnp.mean(o[hh], axis=-1, keepdims=True)
            var = jnp.mean(jnp.square(o[hh] - mu), axis=-1, keepdims=True)
            hs = slice(head * DV, (head + 1) * DV)
            y = (o[hh] - mu) * lax.rsqrt(var + EPS) * gnw_ref[:, hs]
            gate_cols = slice(c_gr + head * DV, c_gr + (head + 1) * DV)
            y_ref[sl, hs] = (y * _silu(p_all[sl, gate_cols])).astype(BF16)

    def gating(c):
        sl = slice(c * CHUNK, (c + 1) * CHUNK)
        vg = _gelu(p_all[sl, c_vm:c_gm])
        mu = jnp.mean(vg, axis=-1, keepdims=True)
        var = jnp.mean(jnp.square(vg - mu), axis=-1, keepdims=True)
        vn = ((vg - mu) * lax.rsqrt(var + EPS) * lnw_ref[...]).astype(BF16)
        s = [_dot(ws_ref[g], vn[:, g * DG:(g + 1) * DG]) for g in range(GROUPS)]
        yield
        for g in range(GROUPS):
            u = _gelu(p_all[sl, c_u + g * DG:c_u + (g + 1) * DG])
            gm = _silu(p_all[sl, c_gm + g * DG:c_gm + (g + 1) * DG])
            y_ref[sl, RET_W + g * DG:RET_W + (g + 1) * DG] = (u * (s[g] + bs_ref[g]) * gm).astype(BF16)

    out_groups = _col_groups(d)
    xn = {}

    def out_project(r, g):
        rows = row_halves[r]
        lo, hi = out_groups[g]
        out = _dot(y_old[rows, :], wout_ref[:, lo:hi])
        xn[r, g] = x_ref[0, rows, lo:hi] + mod[:, 2 * d + lo:2 * d + hi] * out

    def finish(r, part):
        sub = slice(part * half // 2, (part + 1) * half // 2)
        rows = slice(row_halves[r].start + sub.start, row_halves[r].start + sub.stop)
        parts = [xn[r, g][sub, :] for g in range(len(out_groups))]
        ms = sum(jnp.sum(t * t, axis=-1, keepdims=True) for t in parts) * (1.0 / d)
        rs = lax.rsqrt(ms + EPS)
        for (lo, hi), t in zip(out_groups, parts):
            o_ref[0, rows, lo:hi] = t * rs * fnw_ref[:, lo:hi]

    tasks = []
    if mix_stage:
        for c in range(n_chunks):
            tasks += [(retention(c, p), 3) for p in range(PAIRS)] + [(gating(c), 2)]
    big, relaxed = [], {}

    def place(tick, piece):
        while tick in relaxed:
            tick += 1
        relaxed[tick] = piece

    for r in range(2):
        if out_stage:
            big += [functools.partial(out_project, r, g) for g in range(len(out_groups))]
            place(len(big) + 1, functools.partial(finish, r, 0))
            place(len(big) + 3, functools.partial(finish, r, 1))
        if project_stage:
            big += [functools.partial(project, row_halves[r], lo, hi)
                    for lo, hi in _col_groups(p_new_ref.shape[1])]
    side = [(None, relaxed.get(tick)) for tick in range(max(relaxed, default=-1) + 1)]
    _emit_pipelined(big, side, tasks, depth=3)


def _full(shape):
    return pl.BlockSpec(shape, lambda *_: (0,) * len(shape))


def kernel(x, c, ctx, c_ctx, w_ada, b_ada, norm_w, w_in, ret_decay_f, ret_decay_b, ret_gn_w,
           gmlp_ln_w, w_s, b_s, w_out, final_norm_w):
    bsz, seq, d = x.shape
    ctx_len = ctx.shape[1]
    depth = w_ada.shape[0]
    assert depth == 1 and d % LANES == 0
    assert seq % TILE_KV == 0 and seq % TILE_MAIN == 0 and ctx_len % CHUNK == 0
    assert seq % GRID_W == 0 and TILE_MAIN % (4 * CHUNK) == 0 and TILE_KV % (4 * CHUNK) == 0
    n_chunks = seq // CHUNK
    d3 = 3 * d
    k0 = QK_W
    v0 = 2 * QK_W
    gr0 = v0 + RET_W

    w_in0 = w_in[0]
    w_kv = w_in0[:, k0:gr0].astype(BF16)
    w_rest = jnp.concatenate([w_in0[:, 0:k0], w_in0[:, gr0:]], axis=1).astype(BF16)
    w_out_b = w_out[0].astype(BF16)
    w_s_b = w_s[0].astype(BF16)
    b_s_full = jnp.broadcast_to(b_s[0][:, :, None], (GROUPS, CHUNK, DG))
    cc = jnp.concatenate([c, jnp.broadcast_to(c_ctx[None, :], (bsz, d))], axis=0)
    quarter = DK // 4
    inv_freq = ROPE_BASE ** (-jnp.arange(quarter, dtype=F32) / quarter)
    invf_lane = jnp.tile(inv_freq, LANES // quarter)[None, :]
    nw = norm_w[0][None, :]
    rows2 = 2 * bsz

    n_col_blocks = 3
    smem = pl.BlockSpec(memory_space=pltpu.SMEM)
    (mod, cos_t, sin_t, dsum, af, ab, kdft, kdbt, gf, gb) = pl.pallas_call(
        _prologue_kernel,
        grid=(n_col_blocks,),
        in_specs=[smem, smem, _full((rows2, d)),
                  pl.BlockSpec((d, d3 // n_col_blocks), lambda i: (0, i)),
                  pl.BlockSpec((1, d3 // n_col_blocks), lambda i: (0, i)),
                  _full((1, LANES))],
        out_specs=[pl.BlockSpec((rows2, d3 // n_col_blocks), lambda i: (0, i)),
                   _full((seq, LANES)), _full((seq, LANES)),
                   _full((HEADS, CHUNK, CHUNK)),
                   _full((CHUNK, QK_W)), _full((CHUNK, QK_W)),
                   _full((QK_W, CHUNK)), _full((QK_W, CHUNK)),
                   _full((QK_W, DV)), _full((QK_W, DV))],
        out_shape=[jax.ShapeDtypeStruct((rows2, d3), F32),
                   jax.ShapeDtypeStruct((seq, LANES), F32),
                   jax.ShapeDtypeStruct((seq, LANES), F32),
                   jax.ShapeDtypeStruct((HEADS, CHUNK, CHUNK), F32),
                   jax.ShapeDtypeStruct((CHUNK, QK_W), F32),
                   jax.ShapeDtypeStruct((CHUNK, QK_W), F32),
                   jax.ShapeDtypeStruct((QK_W, CHUNK), F32),
                   jax.ShapeDtypeStruct((QK_W, CHUNK), F32),
                   jax.ShapeDtypeStruct((QK_W, DV), F32),
                   jax.ShapeDtypeStruct((QK_W, DV), F32)],
        compiler_params=pltpu.CompilerParams(dimension_semantics=("arbitrary",)),
        name="prologue",
    )(ret_decay_f[0], ret_decay_b[0], cc, w_ada[0], b_ada[0][None, :], invf_lane)
    mod3 = mod.reshape(rows2, 1, d3)

    s_cf, s_cb = pl.pallas_call(
        _ctx_kernel,
        grid=(bsz,),
        in_specs=[pl.BlockSpec((1, ctx_len, d), lambda b: (b, 0, 0)),
                  pl.BlockSpec((1, 1, d3), lambda b: (bsz, 0, 0)),
                  _full((1, d)), _full((d, QK_W + RET_W)),
                  _full((QK_W, CHUNK)), _full((QK_W, CHUNK)),
                  _full((QK_W, DV)), _full((QK_W, DV))],
        out_specs=[pl.BlockSpec((1, QK_W, DV), lambda b: (b, 0, 0)),
                   pl.BlockSpec((1, QK_W, DV), lambda b: (b, 0, 0))],
        out_shape=[jax.ShapeDtypeStruct((bsz, QK_W, DV), F32),
                   jax.ShapeDtypeStruct((bsz, QK_W, DV), F32)],
        compiler_params=pltpu.CompilerParams(dimension_semantics=("arbitrary",)),
        name="ctx_states",
    )(ctx, mod3, nw, w_kv, kdft, kdbt, gf, gb)

    def tile_maps(n_tiles, per_batch, reverse):
        def where(t):
            j = t % per_batch
            return t // per_batch, (per_batch - 1 - j) if reverse else j
        return [lambda s, k=k: where(jnp.clip(s - k, 0, n_tiles - 1)) for k in range(3)]

    nt_kv = seq // TILE_KV
    cpt_kv = TILE_KV // CHUNK
    n_kv = QK_W + RET_W
    cur, prv, _ = tile_maps(bsz * nt_kv, nt_kv, reverse=True)
    h, kt, v, st_b = pl.pallas_call(
        functools.partial(_kv_kernel, bsz * nt_kv, nt_kv),
        grid=(bsz * nt_kv + 1,),
        in_specs=[pl.BlockSpec((1, TILE_KV, d), lambda s: (*cur(s), 0)),
                  pl.BlockSpec((1, 1, d3), lambda s: (cur(s)[0], 0, 0)),
                  _full((1, d)), _full((d, n_kv)),
                  pl.BlockSpec((TILE_KV, LANES), lambda s: (prv(s)[1], 0)),
                  pl.BlockSpec((TILE_KV, LANES), lambda s: (prv(s)[1], 0)),
                  _full((QK_W, CHUNK)), _full((QK_W, DV)),
                  pl.BlockSpec((1, QK_W, DV), lambda s: (prv(s)[0], 0, 0))],
        out_specs=[pl.BlockSpec((1, TILE_KV, d), lambda s: (*cur(s), 0)),
                   pl.BlockSpec((1, cpt_kv, QK_W, CHUNK), lambda s: (*prv(s), 0, 0)),
                   pl.BlockSpec((1, TILE_KV, RET_W), lambda s: (*prv(s), 0)),
                   pl.BlockSpec((1, cpt_kv, QK_W, DV), lambda s: (*prv(s), 0, 0))],
        out_shape=[jax.ShapeDtypeStruct((bsz, seq, d), BF16),
                   jax.ShapeDtypeStruct((bsz, n_chunks, QK_W, CHUNK), BF16),
                   jax.ShapeDtypeStruct((bsz, seq, RET_W), BF16),
                   jax.ShapeDtypeStruct((bsz, n_chunks, QK_W, DV), BF16)],
        scratch_shapes=[pltpu.VMEM((QK_W, DV), F32),
                        pltpu.VMEM((TILE_KV, n_kv), F32),
                        pltpu.VMEM((TILE_KV, n_kv), F32)],
        compiler_params=pltpu.CompilerParams(dimension_semantics=("arbitrary",)),
        name="kv_sweep",
    )(x, mod3, nw, w_kv, cos_t, sin_t, kdbt, gb, s_cb)

    nt = seq // TILE_MAIN
    cpt = TILE_MAIN // CHUNK
    n_rest = w_rest.shape[1]
    cur, prv, prv2 = tile_maps(bsz * nt, nt, reverse=False)
    out = pl.pallas_call(
        functools.partial(_main_kernel, bsz * nt, nt),
        grid=(bsz * nt + 2,),
        in_specs=[pl.BlockSpec((1, TILE_MAIN, d), lambda s: (*cur(s), 0)),
                  pl.BlockSpec((1, TILE_MAIN, d), lambda s: (*prv2(s), 0)),
                  pl.BlockSpec((1, 1, d3), lambda s: (prv2(s)[0], 0, 0)),
                  _full((d, n_rest)),
                  pl.BlockSpec((TILE_MAIN, LANES), lambda s: (prv(s)[1], 0)),
                  pl.BlockSpec((TILE_MAIN, LANES), lambda s: (prv(s)[1], 0)),
                  pl.BlockSpec((1, cpt, QK_W, CHUNK), lambda s: (*prv(s), 0, 0)),
                  pl.BlockSpec((1, TILE_MAIN, RET_W), lambda s: (*prv(s), 0)),
                  pl.BlockSpec((1, cpt, QK_W, DV), lambda s: (*prv(s), 0, 0)),
                  pl.BlockSpec((1, QK_W, DV), lambda s: (prv(s)[0], 0, 0)),
                  _full((HEADS, CHUNK, CHUNK)),
                  _full((CHUNK, QK_W)), _full((CHUNK, QK_W)),
                  _full((QK_W, CHUNK)), _full((QK_W, DV)),
                  _full((1, RET_W)), _full((1, GM_W)),
                  _full((GROUPS, CHUNK, CHUNK)), _full((GROUPS, CHUNK, DG)),
                  _full((RET_W + GM_W, d)), _full((1, d))],
        out_specs=pl.BlockSpec((1, TILE_MAIN, d), lambda s: (*prv2(s), 0)),
        out_shape=jax.ShapeDtypeStruct((bsz, seq, d), x.dtype),
        scratch_shapes=[pltpu.VMEM((QK_W, DV), F32),
                        pltpu.VMEM((TILE_MAIN, n_rest), F32),
                        pltpu.VMEM((TILE_MAIN, n_rest), F32),
                        pltpu.VMEM((TILE_MAIN, RET_W + GM_W), BF16),
                        pltpu.VMEM((TILE_MAIN, RET_W + GM_W), BF16)],
        compiler_params=pltpu.CompilerParams(dimension_semantics=("arbitrary",)),
        name="main_sweep",
    )(h, x, mod3, w_rest, cos_t, sin_t, kt, v, st_b, s_cf, dsum, af, ab, kdft, gf,
      ret_gn_w[0][None, :], gmlp_ln_w[0][None, :], w_s_b, b_s_full, w_out_b,
      final_norm_w[None, :])
    return out
```

```python
import functools

import jax
import jax.numpy as jnp
from jax import lax
from jax.experimental import pallas as pl
from jax.experimental.pallas import tpu as pltpu

F32 = jnp.float32
BF16 = jnp.bfloat16

HEADS = 4
DK = 64
DV = 128
CHUNK = 128
GROUPS = 4
DG = 128
GRID_W = 64
ROPE_BASE = 10000.0
EPS = 1e-6
LANES = 128
PAIRS = HEADS * DK // LANES
RET_W = HEADS * DV
GM_W = GROUPS * DG
QK_W = HEADS * DK

DOT_COLS = 512
TILE_KV = 1024
TILE_MAIN = 512


def _rms(x, w):
    ms = jnp.mean(x * x, axis=-1, keepdims=True)
    return x * lax.rsqrt(ms + EPS) * w


def _silu(x):
    return x * jax.nn.sigmoid(x)


def _gelu(x):
    return 0.5 * x * (1.0 + lax.erf(x * (0.5 ** 0.5)))


_dot = functools.partial(jnp.dot, preferred_element_type=F32)


def _col_groups(n):
    first = n % DOT_COLS
    edges = ([0] if first == 0 else [0, first]) + list(range(first + DOT_COLS, n + 1, DOT_COLS))
    return list(zip(edges[:-1], edges[1:]))


def _rope(p, cos, sin_signed, upper16):
    partner = jnp.where(upper16, pltpu.roll(p, 16, 1), pltpu.roll(p, LANES - 16, 1))
    return p * cos + partner * sin_signed


def _kv_pairs(kd, v):
    out = []
    for p in range(PAIRS):
        r = _dot(kd[p * LANES:(p + 1) * LANES, :], v[:, p * 2 * DV:(p + 1) * 2 * DV])
        out.append(r[0:DK, 0:DV])
        out.append(r[DK:2 * DK, DV:2 * DV])
    return jnp.concatenate(out, axis=0)


def _emit_pipelined(big_pieces, side_pieces, tasks, depth):
    in_flight = []
    for tick in range(max(len(tasks) + depth - 1, len(big_pieces), len(side_pieces))):
        if tick < len(big_pieces):
            big_pieces[tick]()
        urgent, relaxed = side_pieces[tick] if tick < len(side_pieces) else (None, None)
        if urgent is not None:
            urgent()
        if tick < len(tasks):
            in_flight.append([tasks[tick][0], tasks[tick][1]])
        for t in [t for t in in_flight if t[1] > 1] + [t for t in in_flight if t[1] == 1]:
            next(t[0], None)
            t[1] -= 1
        in_flight = [t for t in in_flight if t[1] > 0]
        if relaxed is not None:
            relaxed()
    assert not in_flight


def _run_pipeline_step(step, n_tiles, n_stages, body, refs, bufs_a, bufs_b):
    def call(s_static_parity, stages):
        new, old = (bufs_a, bufs_b) if s_static_parity == 0 else (bufs_b, bufs_a)
        body(stages, *refs, *new, *old)

    edge = n_stages - 1
    for s in list(range(edge)) + list(range(n_tiles, n_tiles + edge)):
        stages = tuple(0 <= s - j < n_tiles for j in range(n_stages))
        pl.when(step == s)(functools.partial(call, s % 2, stages))
    steady = (step >= edge) & (step < n_tiles)
    for parity in range(2):
        pl.when(steady & (lax.rem(step, 2) == parity))(
            functools.partial(call, parity, (True,) * n_stages))


def _prologue_kernel(decf_ref, decb_ref, c_ref, cctx_ref, w_ref, b_ref, invf_ref, win_ref,
                     wout_ref, ws_ref, bs_ref,
                     mod_ref, cos_ref, sin_ref, dsum_ref, af_ref, ab_ref,
                     kdft_ref, kdbt_ref, gf_ref, gb_ref,
                     wq_ref, wkv_ref, wg_ref, woutb_ref, wsb_ref, bsf_ref):
    i = pl.program_id(0)
    c = c_ref[...]
    cond = jnp.concatenate([c, jnp.broadcast_to(cctx_ref[...], c.shape)], axis=0)
    mod_ref[...] = _dot(_silu(cond), w_ref[...]) + b_ref[...]

    @pl.when(i == 0)
    def _first_block():
        wq_ref[...] = win_ref[:, 0:QK_W].astype(BF16)
        wkv_ref[...] = win_ref[:, QK_W:].astype(BF16)
        woutb_ref[...] = wout_ref[...].astype(BF16)
        wsb_ref[...] = ws_ref[...].astype(BF16)
        for g in range(GROUPS):
            bsf_ref[g] = jnp.broadcast_to(bs_ref[g:g + 1, :], (CHUNK, CHUNK)).T

    @pl.when(i > 0)
    def _later_blocks():
        wg_ref[...] = win_ref[...].astype(BF16)

    @pl.when(i == 0)
    def _tables():
        pos = lax.broadcasted_iota(jnp.int32, (GRID_W, LANES), 0).astype(F32)
        lane = lax.broadcasted_iota(jnp.int32, (GRID_W, LANES), 1)
        ang = pos * invf_ref[...]
        cr = jnp.cos(ang)
        sr = jnp.sin(ang)
        sr = jnp.where((lane & 16) == 0, -sr, sr)
        by_row = (lane & 32) == 0
        n_rows = cos_ref.shape[0] // GRID_W
        for r in range(n_rows):
            sl = slice(r * GRID_W, (r + 1) * GRID_W)
            cos_ref[sl, :] = jnp.where(by_row, jnp.broadcast_to(cr[r:r + 1, :], cr.shape), cr)
            sin_ref[sl, :] = jnp.where(by_row, jnp.broadcast_to(sr[r:r + 1, :], sr.shape), sr)

        def log_decay(head_idx, dec_ref):
            d = jnp.full(head_idx.shape, dec_ref[HEADS - 1], F32)
            for h in range(HEADS - 2, -1, -1):
                d = jnp.where(head_idx == h, dec_ref[h], d)
            return -jnp.exp(d)

        n = lax.broadcasted_iota(jnp.int32, (CHUNK, QK_W), 0).astype(F32)
        hl = lax.broadcasted_iota(jnp.int32, (CHUNK, QK_W), 1) // DK
        af_ref[...] = jnp.exp((n + 1.0) * log_decay(hl, decf_ref))
        ab_ref[...] = jnp.exp((CHUNK - n) * log_decay(hl, decb_ref))

        m = lax.broadcasted_iota(jnp.int32, (QK_W, CHUNK), 1).astype(F32)
        hr = lax.broadcasted_iota(jnp.int32, (QK_W, CHUNK), 0) // DK
        lgf = log_decay(hr, decf_ref)
        lgb = log_decay(hr, decb_ref)
        kdft_ref[...] = jnp.exp((CHUNK - 1.0 - m) * lgf)
        kdbt_ref[...] = jnp.exp(m * lgb)
        gf_ref[...] = jnp.exp(float(CHUNK) * lgf)
        gb_ref[...] = jnp.exp(float(CHUNK) * lgb)

        nn = lax.broadcasted_iota(jnp.int32, (CHUNK, CHUNK), 0).astype(F32)
        mm = lax.broadcasted_iota(jnp.int32, (CHUNK, CHUNK), 1).astype(F32)
        for h in range(HEADS):
            lf = -jnp.exp(jnp.full((CHUNK, CHUNK), decf_ref[h], F32))
            lb = -jnp.exp(jnp.full((CHUNK, CHUNK), decb_ref[h], F32))
            df = nn - mm
            db = mm - nn
            fwd = jnp.where(df >= 0, jnp.exp(jnp.maximum(df, 0.0) * lf), 0.0)
            bwd = jnp.where(db >= 0, jnp.exp(jnp.maximum(db, 0.0) * lb), 0.0)
            dsum_ref[h] = fwd + bwd


def _ctx_kernel(ctx_ref, mod_ref, nw_ref, wkv_ref, kdft_ref, kdbt_ref, gf_ref, gb_ref,
                scf_ref, scb_ref):
    d = ctx_ref.shape[-1]
    x = ctx_ref[0]
    ctx_row = mod_ref.shape[0] // 2
    mod = mod_ref[ctx_row:ctx_row + 1, :]
    hc = _rms(x, nw_ref[...]) * (1.0 + mod[:, d:2 * d]) + mod[:, 0:d]
    pkv = _dot(hc.astype(BF16), wkv_ref[...])
    k = pkv[:, 0:QK_W] * (DK ** -0.5)
    v = pkv[:, QK_W:].astype(BF16)
    n_chunks = x.shape[0] // CHUNK
    sf = jnp.zeros((QK_W, DV), F32)
    for c in range(n_chunks):
        sl = slice(c * CHUNK, (c + 1) * CHUNK)
        kd = (k[sl, :].T * kdft_ref[...]).astype(BF16)
        sf = gf_ref[...] * sf + _kv_pairs(kd, v[sl, :])
    sb = jnp.zeros((QK_W, DV), F32)
    for c in range(n_chunks - 1, -1, -1):
        sl = slice(c * CHUNK, (c + 1) * CHUNK)
        kd = (k[sl, :].T * kdbt_ref[...]).astype(BF16)
        sb = gb_ref[...] * sb + _kv_pairs(kd, v[sl, :])
    scf_ref[0] = sf
    scb_ref[0] = sb


def _kv_kernel(n_tiles, tiles_per_batch, xc_ref, modc_ref, nw_ref, wkv_ref, cos_ref, sin_ref,
               kdbt_ref, gb_ref, scb_ref, h_ref, kt_ref, v_ref, stb_ref, sb_ref, pa_ref, pb_ref):
    s = pl.program_id(0)

    @pl.when((s >= 1) & (lax.rem(s - 1, tiles_per_batch) == 0))
    def _init():
        sb_ref[...] = scb_ref[0]

    batch = jnp.minimum(s, n_tiles - 1) // tiles_per_batch
    modc = modc_ref[pl.ds(batch, 1), :]
    refs = (xc_ref, modc, nw_ref, wkv_ref, cos_ref, sin_ref, kdbt_ref, gb_ref,
            h_ref, kt_ref, v_ref, stb_ref, sb_ref)
    _run_pipeline_step(s, n_tiles, 2, _kv_step, refs, (pa_ref,), (pb_ref,))


def _kv_step(stages, xc_ref, modc, nw_ref, wkv_ref, cos_ref, sin_ref, kdbt_ref, gb_ref,
             h_ref, kt_ref, v_ref, stb_ref, sb_ref, p_new_ref, p_old):
    project_stage, finish_stage = stages
    d = xc_ref.shape[-1]
    tile = xc_ref.shape[1]
    half = tile // 2
    row_halves = [slice(0, half), slice(half, tile)]
    quarter = [slice(i * half // 2, (i + 1) * half // 2) for i in range(4)]
    upper16 = (lax.broadcasted_iota(jnp.int32, (CHUNK, LANES), 1) & 16) != 0

    def norm_rows(rows):
        h = _rms(xc_ref[0, rows, :], nw_ref[...]) * (1.0 + modc[:, d:2 * d]) + modc[:, 0:d]
        h_ref[0, rows, :] = h.astype(BF16)

    def project(rows, lo, hi):
        p_new_ref[rows, lo:hi] = _dot(h_ref[0, rows, :], wkv_ref[:, lo:hi])

    def chunk(c):
        sl = slice(c * CHUNK, (c + 1) * CHUNK)
        k = [_rope(p_old[sl, p * LANES:(p + 1) * LANES], cos_ref[sl, :], sin_ref[sl, :], upper16)
             * (DK ** -0.5) for p in range(PAIRS)]
        v = p_old[sl, QK_W:].astype(BF16)
        v_ref[0, sl, :] = v
        kt = jnp.concatenate([kp.T for kp in k], axis=0)
        kt_ref[0, c] = kt.astype(BF16)
        new = _kv_pairs((kt * kdbt_ref[...]).astype(BF16), v)
        yield
        stb_ref[0, c] = sb_ref[...].astype(BF16)
        sb_ref[...] = gb_ref[...] * sb_ref[...] + new

    projections, late_norms, tasks = [], [], []
    if project_stage:
        norm_rows(quarter[0])
        norm_rows(quarter[1])
        projections = [functools.partial(project, row_halves[r], lo, hi)
                       for r in range(2) for lo, hi in _col_groups(p_new_ref.shape[1])]
        late_norms = [(functools.partial(norm_rows, q), None) for q in quarter[2:]]
    if finish_stage:
        tasks = [(chunk(c), 2) for c in range(tile // CHUNK - 1, -1, -1)]
    _emit_pipelined(projections, late_norms, tasks, depth=2)


def _main_kernel(n_tiles, tiles_per_batch, h_ref, x_ref, mod_ref, wq_ref, wg_ref, cos_ref,
                 sin_ref, kt_ref, v_ref, stb_ref, scf_ref, dsum_ref, af_ref, ab_ref, kdft_ref,
                 gf_ref, gnw_ref, lnw_ref, ws_ref, bs_ref, wout_ref, fnw_ref, o_ref,
                 sf_ref, pa_ref, pb_ref, ya_ref, yb_ref):
    s = pl.program_id(0)

    @pl.when((s >= 1) & (lax.rem(s - 1, tiles_per_batch) == 0))
    def _init():
        sf_ref[...] = scf_ref[0]

    batch = jnp.clip(s - 2, 0, n_tiles - 1) // tiles_per_batch
    mod = mod_ref[pl.ds(batch, 1), :]
    refs = (h_ref, x_ref, mod, wq_ref, wg_ref, cos_ref, sin_ref, kt_ref, v_ref,
            stb_ref, dsum_ref, af_ref, ab_ref, kdft_ref, gf_ref, gnw_ref, lnw_ref, ws_ref,
            bs_ref, wout_ref, fnw_ref, o_ref, sf_ref)
    _run_pipeline_step(s, n_tiles, 3, _main_step, refs, (pa_ref, ya_ref), (pb_ref, yb_ref))


def _main_step(stages, h_ref, x_ref, mod, wq_ref, wg_ref, cos_ref, sin_ref, kt_ref, v_ref,
               stb_ref, dsum_ref, af_ref, ab_ref, kdft_ref, gf_ref, gnw_ref, lnw_ref, ws_ref,
               bs_ref, wout_ref, fnw_ref, o_ref, sf_ref, p_new_ref, y_ref, p_all, y_old):
    project_stage, mix_stage, out_stage = stages
    d = x_ref.shape[-1]
    tile = x_ref.shape[1]
    n_chunks = tile // CHUNK
    half = tile // 2
    row_halves = [slice(0, half), slice(half, tile)]
    c_gr = QK_W
    c_u = c_gr + RET_W
    c_vm = c_u + GM_W
    c_gm = c_vm + GM_W
    lane = lax.broadcasted_iota(jnp.int32, (CHUNK, LANES), 1)
    upper16 = (lane & 16) != 0
    head_of_lane = lane // DK

    operand = {}

    def loaded_once(key, load):
        if key not in operand:
            operand[key] = load()
        return operand[key]

    def project(r, lo, hi):
        rows = row_halves[r]
        lhs = loaded_once(("h", r), lambda: h_ref[0, rows, :])
        w = wq_ref[...] if hi <= QK_W else wg_ref[:, lo - QK_W:hi - QK_W]
        p_new_ref[rows, lo:hi] = _dot(lhs, w)

    def retention(c, p):
        sl = slice(c * CHUNK, (c + 1) * CHUNK)
        ps = slice(p * LANES, (p + 1) * LANES)
        q2 = _rope(p_all[sl, ps], cos_ref[sl, :], sin_ref[sl, :], upper16)
        kt = kt_ref[0, c, ps, :]
        vp = v_ref[0, sl, p * 2 * DV:(p + 1) * 2 * DV]
        qa = q2 * af_ref[:, ps]
        qb = q2 * ab_ref[:, ps]
        scores, qx = [], []
        for hh in range(2):
            mine = head_of_lane == hh
            scores.append(_dot(jnp.where(mine, q2, 0.0).astype(BF16), kt))
            qx.append(jnp.concatenate([jnp.where(mine, qa, 0.0), jnp.where(mine, qb, 0.0)],
                                      axis=1).astype(BF16))
        kd = (kt.astype(F32) * kdft_ref[ps, :]).astype(BF16)
        yield
        state = jnp.concatenate([sf_ref[ps, :].astype(BF16), stb_ref[0, c, ps, :]], axis=0)
        o = []
        for hh in range(2):
            a = (scores[hh] * dsum_ref[2 * p + hh]).astype(BF16)
            o.append(_dot(a, vp[:, hh * DV:(hh + 1) * DV]) + _dot(qx[hh], state))
        r = _dot(kd, vp)
        yield
        new = jnp.concatenate([r[0:DK, 0:DV], r[DK:2 * DK, DV:2 * DV]], axis=0)
        sf_ref[ps, :] = gf_ref[ps, :] * sf_ref[ps, :] + new
        for hh in range(2):
            head = 2 * p + hh
            mu = jnp.mean(o[hh], axis=-1, keepdims=True)
            var = jnp.mean(jnp.square(o[hh] - mu), axis=-1, keepdims=True)
            hs = slice(head * DV, (head + 1) * DV)
            y = (o[hh] - mu) * lax.rsqrt(var + EPS) * gnw_ref[:, hs]
            gate_cols = slice(c_gr + head * DV, c_gr + (head + 1) * DV)
            y_ref[sl, hs] = (y * _silu(p_all[sl, gate_cols])).astype(BF16)

    def gating(c):
        sl = slice(c * CHUNK, (c + 1) * CHUNK)
        vg = _gelu(p_all[sl, c_vm:c_gm])
        mu = jnp.mean(vg, axis=-1, keepdims=True)
        var = jnp.mean(jnp.square(vg - mu), axis=-1, keepdims=True)
        vn = ((vg - mu) * lax.rsqrt(var + EPS) * lnw_ref[...]).astype(BF16)
        s = [_dot(ws_ref[g], vn[:, g * DG:(g + 1) * DG]) for g in range(GROUPS)]
        yield
        for g in range(GROUPS):
            u = _gelu(p_all[sl, c_u + g * DG:c_u + (g + 1) * DG])
            gm = _silu(p_all[sl, c_gm + g * DG:c_gm + (g + 1) * DG])
            y_ref[sl, RET_W + g * DG:RET_W + (g + 1) * DG] = (u * (s[g] + bs_ref[g]) * gm).astype(BF16)

    out_groups = _col_groups(d)
    xn = {}

    def out_project(r, g):
        rows = row_halves[r]
        lo, hi = out_groups[g]
        lhs = loaded_once(("y", r), lambda: y_old[rows, :])
        out = _dot(lhs, wout_ref[:, lo:hi])
        xn[r, g] = x_ref[0, rows, lo:hi] + mod[:, 2 * d + lo:2 * d + hi] * out

    def finish(r, part):
        sub = slice(part * half // 2, (part + 1) * half // 2)
        rows = slice(row_halves[r].start + sub.start, row_halves[r].start + sub.stop)
        parts = [xn[r, g][sub, :] for g in range(len(out_groups))]
        ms = sum(jnp.sum(t * t, axis=-1, keepdims=True) for t in parts) * (1.0 / d)
        rs = lax.rsqrt(ms + EPS)
        for (lo, hi), t in zip(out_groups, parts):
            o_ref[0, rows, lo:hi] = t * rs * fnw_ref[:, lo:hi]

    tasks = []
    if mix_stage:
        for c in range(n_chunks):
            tasks += [(retention(c, p), 3) for p in range(PAIRS)] + [(gating(c), 2)]
    big, relaxed = [], {}

    def place(tick, piece):
        while tick in relaxed:
            tick += 1
        relaxed[tick] = piece

    for r in range(2):
        if out_stage:
            big += [functools.partial(out_project, r, g) for g in range(len(out_groups))]
            place(len(big) + 1, functools.partial(finish, r, 0))
            place(len(big) + 3, functools.partial(finish, r, 1))
        if project_stage:
            big += [functools.partial(project, r, lo, hi)
                    for lo, hi in _col_groups(p_new_ref.shape[1])]
    side = [(None, relaxed.get(tick)) for tick in range(max(relaxed, default=-1) + 1)]
    _emit_pipelined(big, side, tasks, depth=3)


def _full(shape):
    return pl.BlockSpec(shape, lambda *_: (0,) * len(shape))


def kernel(x, c, ctx, c_ctx, w_ada, b_ada, norm_w, w_in, ret_decay_f, ret_decay_b, ret_gn_w,
           gmlp_ln_w, w_s, b_s, w_out, final_norm_w):
    bsz, seq, d = x.shape
    ctx_len = ctx.shape[1]
    depth = w_ada.shape[0]
    assert depth == 1 and d % LANES == 0
    assert seq % TILE_KV == 0 and seq % TILE_MAIN == 0 and ctx_len % CHUNK == 0
    assert seq % GRID_W == 0 and TILE_MAIN % (4 * CHUNK) == 0 and TILE_KV % (4 * CHUNK) == 0
    n_chunks = seq // CHUNK
    d3 = 3 * d
    n_kv = QK_W + RET_W
    n_rest = w_in.shape[2] - n_kv
    n_gates = n_rest - QK_W
    assert QK_W + n_kv == d and n_gates % d == 0 and d3 // d == 1 + n_gates // d
    assert _col_groups(n_rest)[0] == (0, QK_W)

    quarter = DK // 4
    inv_freq = ROPE_BASE ** (-jnp.arange(quarter, dtype=F32) / quarter)
    invf_lane = jnp.tile(inv_freq, LANES // quarter)[None, :]
    nw = norm_w[0][None, :]
    rows2 = 2 * bsz

    n_col_blocks = d3 // d
    smem = pl.BlockSpec(memory_space=pltpu.SMEM)
    (mod, cos_t, sin_t, dsum, af, ab, kdft, kdbt, gf, gb,
     w_q, w_kv, w_g, w_out_b, w_s_b, b_s_full) = pl.pallas_call(
        _prologue_kernel,
        grid=(n_col_blocks,),
        in_specs=[smem, smem, _full((bsz, d)), _full((1, d)),
                  pl.BlockSpec((d, d), lambda i: (0, i)),
                  pl.BlockSpec((1, d), lambda i: (0, i)),
                  _full((1, LANES)),
                  pl.BlockSpec((d, d), lambda i: (0, i)),
                  _full((RET_W + GM_W, d)),
                  _full((GROUPS, CHUNK, CHUNK)), _full((GROUPS, CHUNK))],
        out_specs=[pl.BlockSpec((rows2, d), lambda i: (0, i)),
                   _full((seq, LANES)), _full((seq, LANES)),
                   _full((HEADS, CHUNK, CHUNK)),
                   _full((CHUNK, QK_W)), _full((CHUNK, QK_W)),
                   _full((QK_W, CHUNK)), _full((QK_W, CHUNK)),
                   _full((QK_W, DV)), _full((QK_W, DV)),
                   _full((d, QK_W)), _full((d, n_kv)),
                   pl.BlockSpec((d, d), lambda i: (0, jnp.maximum(i - 1, 0))),
                   _full((RET_W + GM_W, d)),
                   _full((GROUPS, CHUNK, CHUNK)), _full((GROUPS, CHUNK, DG))],
        out_shape=[jax.ShapeDtypeStruct((rows2, d3), F32),
                   jax.ShapeDtypeStruct((seq, LANES), F32),
                   jax.ShapeDtypeStruct((seq, LANES), F32),
                   jax.ShapeDtypeStruct((HEADS, CHUNK, CHUNK), F32),
                   jax.ShapeDtypeStruct((CHUNK, QK_W), F32),
                   jax.ShapeDtypeStruct((CHUNK, QK_W), F32),
                   jax.ShapeDtypeStruct((QK_W, CHUNK), F32),
                   jax.ShapeDtypeStruct((QK_W, CHUNK), F32),
                   jax.ShapeDtypeStruct((QK_W, DV), F32),
                   jax.ShapeDtypeStruct((QK_W, DV), F32),
                   jax.ShapeDtypeStruct((d, QK_W), BF16),
                   jax.ShapeDtypeStruct((d, n_kv), BF16),
                   jax.ShapeDtypeStruct((d, n_gates), BF16),
                   jax.ShapeDtypeStruct((RET_W + GM_W, d), BF16),
                   jax.ShapeDtypeStruct((GROUPS, CHUNK, CHUNK), BF16),
                   jax.ShapeDtypeStruct((GROUPS, CHUNK, DG), F32)],
        compiler_params=pltpu.CompilerParams(dimension_semantics=("arbitrary",)),
        name="prologue",
    )(ret_decay_f[0], ret_decay_b[0], c, c_ctx[None, :], w_ada[0], b_ada[0][None, :], invf_lane,
      w_in[0], w_out[0], w_s[0], b_s[0])

    s_cf, s_cb = pl.pallas_call(
        _ctx_kernel,
        grid=(bsz,),
        in_specs=[pl.BlockSpec((1, ctx_len, d), lambda b: (b, 0, 0)),
                  _full((rows2, d3)),
                  _full((1, d)), _full((d, n_kv)),
                  _full((QK_W, CHUNK)), _full((QK_W, CHUNK)),
                  _full((QK_W, DV)), _full((QK_W, DV))],
        out_specs=[pl.BlockSpec((1, QK_W, DV), lambda b: (b, 0, 0)),
                   pl.BlockSpec((1, QK_W, DV), lambda b: (b, 0, 0))],
        out_shape=[jax.ShapeDtypeStruct((bsz, QK_W, DV), F32),
                   jax.ShapeDtypeStruct((bsz, QK_W, DV), F32)],
        compiler_params=pltpu.CompilerParams(dimension_semantics=("arbitrary",)),
        name="ctx_states",
    )(ctx, mod, nw, w_kv, kdft, kdbt, gf, gb)

    def tile_maps(n_tiles, per_batch, reverse):
        def where(t):
            j = t % per_batch
            return t // per_batch, (per_batch - 1 - j) if reverse else j
        return [lambda s, k=k: where(jnp.clip(s - k, 0, n_tiles - 1)) for k in range(3)]

    nt_kv = seq // TILE_KV
    cpt_kv = TILE_KV // CHUNK
    cur, prv, _ = tile_maps(bsz * nt_kv, nt_kv, reverse=True)
    h, kt, v, st_b = pl.pallas_call(
        functools.partial(_kv_kernel, bsz * nt_kv, nt_kv),
        grid=(bsz * nt_kv + 1,),
        in_specs=[pl.BlockSpec((1, TILE_KV, d), lambda s: (*cur(s), 0)),
                  _full((rows2, d3)),
                  _full((1, d)), _full((d, n_kv)),
                  pl.BlockSpec((TILE_KV, LANES), lambda s: (prv(s)[1], 0)),
                  pl.BlockSpec((TILE_KV, LANES), lambda s: (prv(s)[1], 0)),
                  _full((QK_W, CHUNK)), _full((QK_W, DV)),
                  pl.BlockSpec((1, QK_W, DV), lambda s: (prv(s)[0], 0, 0))],
        out_specs=[pl.BlockSpec((1, TILE_KV, d), lambda s: (*cur(s), 0)),
                   pl.BlockSpec((1, cpt_kv, QK_W, CHUNK), lambda s: (*prv(s), 0, 0)),
                   pl.BlockSpec((1, TILE_KV, RET_W), lambda s: (*prv(s), 0)),
                   pl.BlockSpec((1, cpt_kv, QK_W, DV), lambda s: (*prv(s), 0, 0))],
        out_shape=[jax.ShapeDtypeStruct((bsz, seq, d), BF16),
                   jax.ShapeDtypeStruct((bsz, n_chunks, QK_W, CHUNK), BF16),
                   jax.ShapeDtypeStruct((bsz, seq, RET_W), BF16),
                   jax.ShapeDtypeStruct((bsz, n_chunks, QK_W, DV), BF16)],
        scratch_shapes=[pltpu.VMEM((QK_W, DV), F32),
                        pltpu.VMEM((TILE_KV, n_kv), F32),
                        pltpu.VMEM((TILE_KV, n_kv), F32)],
        compiler_params=pltpu.CompilerParams(dimension_semantics=("arbitrary",)),
        name="kv_sweep",
    )(x, mod, nw, w_kv, cos_t, sin_t, kdbt, gb, s_cb)

    nt = seq // TILE_MAIN
    cpt = TILE_MAIN // CHUNK
    cur, prv, prv2 = tile_maps(bsz * nt, nt, reverse=False)
    out = pl.pallas_call(
        functools.partial(_main_kernel, bsz * nt, nt),
        grid=(bsz * nt + 2,),
        in_specs=[pl.BlockSpec((1, TILE_MAIN, d), lambda s: (*cur(s), 0)),
                  pl.BlockSpec((1, TILE_MAIN, d), lambda s: (*prv2(s), 0)),
                  _full((rows2, d3)),
                  _full((d, QK_W)), _full((d, n_gates)),
                  pl.BlockSpec((TILE_MAIN, LANES), lambda s: (prv(s)[1], 0)),
                  pl.BlockSpec((TILE_MAIN, LANES), lambda s: (prv(s)[1], 0)),
                  pl.BlockSpec((1, cpt, QK_W, CHUNK), lambda s: (*prv(s), 0, 0)),
                  pl.BlockSpec((1, TILE_MAIN, RET_W), lambda s: (*prv(s), 0)),
                  pl.BlockSpec((1, cpt, QK_W, DV), lambda s: (*prv(s), 0, 0)),
                  pl.BlockSpec((1, QK_W, DV), lambda s: (prv(s)[0], 0, 0)),
                  _full((HEADS, CHUNK, CHUNK)),
                  _full((CHUNK, QK_W)), _full((CHUNK, QK_W)),
                  _full((QK_W, CHUNK)), _full((QK_W, DV)),
                  _full((1, RET_W)), _full((1, GM_W)),
                  _full((GROUPS, CHUNK, CHUNK)), _full((GROUPS, CHUNK, DG)),
                  _full((RET_W + GM_W, d)), _full((1, d))],
        out_specs=pl.BlockSpec((1, TILE_MAIN, d), lambda s: (*prv2(s), 0)),
        out_shape=jax.ShapeDtypeStruct((bsz, seq, d), x.dtype),
        scratch_shapes=[pltpu.VMEM((QK_W, DV), F32),
                        pltpu.VMEM((TILE_MAIN, n_rest), F32),
                        pltpu.VMEM((TILE_MAIN, n_rest), F32),
                        pltpu.VMEM((TILE_MAIN, RET_W + GM_W), BF16),
                        pltpu.VMEM((TILE_MAIN, RET_W + GM_W), BF16)],
        compiler_params=pltpu.CompilerParams(dimension_semantics=("arbitrary",)),
        name="main_sweep",
    )(h, x, mod, w_q, w_g, cos_t, sin_t, kt, v, st_b, s_cf, dsum, af, ab, kdft, gf,
      ret_gn_w[0][None, :], gmlp_ln_w[0][None, :], w_s_b, b_s_full, w_out_b,
      final_norm_w[None, :])
    return out
```

```python
import functools

import jax
import jax.numpy as jnp
from jax import lax
from jax.experimental import pallas as pl
from jax.experimental.pallas import tpu as pltpu

F32 = jnp.float32
BF16 = jnp.bfloat16

HEADS = 4
DK = 64
DV = 128
CHUNK = 128
GROUPS = 4
DG = 128
GRID_W = 64
ROPE_BASE = 10000.0
EPS = 1e-6
LANES = 128
PAIRS = HEADS * DK // LANES
RET_W = HEADS * DV
GM_W = GROUPS * DG
QK_W = HEADS * DK

DOT_COLS = 512
TILE_KV = 2048
TILE_MAIN = 512


def _rms(x, w):
    ms = jnp.mean(x * x, axis=-1, keepdims=True)
    return x * lax.rsqrt(ms + EPS) * w


def _silu(x):
    return x * jax.nn.sigmoid(x)


def _gelu(x):
    return 0.5 * x * (1.0 + lax.erf(x * (0.5 ** 0.5)))


_dot = functools.partial(jnp.dot, preferred_element_type=F32)


def _col_groups(n):
    first = n % DOT_COLS
    edges = ([0] if first == 0 else [0, first]) + list(range(first + DOT_COLS, n + 1, DOT_COLS))
    return list(zip(edges[:-1], edges[1:]))


def _rope(p, cos, sin_signed, upper16):
    partner = jnp.where(upper16, pltpu.roll(p, 16, 1), pltpu.roll(p, LANES - 16, 1))
    return p * cos + partner * sin_signed


def _kv_pairs(kd, v):
    out = []
    for p in range(PAIRS):
        r = _dot(kd[p * LANES:(p + 1) * LANES, :], v[:, p * 2 * DV:(p + 1) * 2 * DV])
        out.append(r[0:DK, 0:DV])
        out.append(r[DK:2 * DK, DV:2 * DV])
    return jnp.concatenate(out, axis=0)


def _emit_pipelined(big_pieces, side_pieces, tasks, depth):
    in_flight = []
    for tick in range(max(len(tasks) + depth - 1, len(big_pieces), len(side_pieces))):
        if tick < len(big_pieces):
            big_pieces[tick]()
        urgent, relaxed = side_pieces[tick] if tick < len(side_pieces) else (None, None)
        if urgent is not None:
            urgent()
        if tick < len(tasks):
            in_flight.append([tasks[tick][0], tasks[tick][1]])
        for t in [t for t in in_flight if t[1] > 1] + [t for t in in_flight if t[1] == 1]:
            next(t[0], None)
            t[1] -= 1
        in_flight = [t for t in in_flight if t[1] > 0]
        if relaxed is not None:
            relaxed()
    assert not in_flight


def _run_pipeline_step(step, n_tiles, n_stages, body, refs, bufs_a, bufs_b):
    def call(s_static_parity, stages):
        new, old = (bufs_a, bufs_b) if s_static_parity == 0 else (bufs_b, bufs_a)
        body(stages, *refs, *new, *old)

    edge = n_stages - 1
    for s in list(range(edge)) + list(range(n_tiles, n_tiles + edge)):
        stages = tuple(0 <= s - j < n_tiles for j in range(n_stages))
        pl.when(step == s)(functools.partial(call, s % 2, stages))
    steady = (step >= edge) & (step < n_tiles)
    for parity in range(2):
        pl.when(steady & (lax.rem(step, 2) == parity))(
            functools.partial(call, parity, (True,) * n_stages))


def _prologue_kernel(decf_ref, decb_ref, c_ref, cctx_ref, w_ref, b_ref, invf_ref, win_ref,
                     wout_ref, ws_ref, bs_ref,
                     mod_ref, cos_ref, sin_ref, dsum_ref, af_ref, ab_ref,
                     kdft_ref, kdbt_ref, gf_ref, gb_ref,
                     wq_ref, wkv_ref, wg_ref, woutb_ref, wsb_ref, bsf_ref):
    i = pl.program_id(0)
    c = c_ref[...]
    cond = jnp.concatenate([c, jnp.broadcast_to(cctx_ref[...], c.shape)], axis=0)
    mod_ref[...] = _dot(_silu(cond), w_ref[...]) + b_ref[...]

    @pl.when(i == 0)
    def _first_block():
        wq_ref[...] = win_ref[:, 0:QK_W].astype(BF16)
        wkv_ref[...] = win_ref[:, QK_W:].astype(BF16)
        woutb_ref[...] = wout_ref[...].astype(BF16)
        wsb_ref[...] = ws_ref[...].astype(BF16)
        for g in range(GROUPS):
            bsf_ref[g] = jnp.broadcast_to(bs_ref[g:g + 1, :], (CHUNK, CHUNK)).T

    @pl.when(i > 0)
    def _later_blocks():
        wg_ref[...] = win_ref[...].astype(BF16)

    @pl.when(i == 0)
    def _tables():
        pos = lax.broadcasted_iota(jnp.int32, (GRID_W, LANES), 0).astype(F32)
        lane = lax.broadcasted_iota(jnp.int32, (GRID_W, LANES), 1)
        ang = pos * invf_ref[...]
        cr = jnp.cos(ang)
        sr = jnp.sin(ang)
        sr = jnp.where((lane & 16) == 0, -sr, sr)
        by_row = (lane & 32) == 0
        n_rows = cos_ref.shape[0] // GRID_W
        for r in range(n_rows):
            sl = slice(r * GRID_W, (r + 1) * GRID_W)
            cos_ref[sl, :] = jnp.where(by_row, jnp.broadcast_to(cr[r:r + 1, :], cr.shape), cr)
            sin_ref[sl, :] = jnp.where(by_row, jnp.broadcast_to(sr[r:r + 1, :], sr.shape), sr)

        def log_decay(head_idx, dec_ref):
            d = jnp.full(head_idx.shape, dec_ref[HEADS - 1], F32)
            for h in range(HEADS - 2, -1, -1):
                d = jnp.where(head_idx == h, dec_ref[h], d)
            return -jnp.exp(d)

        n = lax.broadcasted_iota(jnp.int32, (CHUNK, QK_W), 0).astype(F32)
        hl = lax.broadcasted_iota(jnp.int32, (CHUNK, QK_W), 1) // DK
        af_ref[...] = jnp.exp((n + 1.0) * log_decay(hl, decf_ref))
        ab_ref[...] = jnp.exp((CHUNK - n) * log_decay(hl, decb_ref))

        m = lax.broadcasted_iota(jnp.int32, (QK_W, CHUNK), 1).astype(F32)
        hr = lax.broadcasted_iota(jnp.int32, (QK_W, CHUNK), 0) // DK
        lgf = log_decay(hr, decf_ref)
        lgb = log_decay(hr, decb_ref)
        kdft_ref[...] = jnp.exp((CHUNK - 1.0 - m) * lgf)
        kdbt_ref[...] = jnp.exp(m * lgb)
        gf_ref[...] = jnp.exp(float(CHUNK) * lgf)
        gb_ref[...] = jnp.exp(float(CHUNK) * lgb)

        nn = lax.broadcasted_iota(jnp.int32, (CHUNK, CHUNK), 0).astype(F32)
        mm = lax.broadcasted_iota(jnp.int32, (CHUNK, CHUNK), 1).astype(F32)
        for h in range(HEADS):
            lf = -jnp.exp(jnp.full((CHUNK, CHUNK), decf_ref[h], F32))
            lb = -jnp.exp(jnp.full((CHUNK, CHUNK), decb_ref[h], F32))
            df = nn - mm
            db = mm - nn
            fwd = jnp.where(df >= 0, jnp.exp(jnp.maximum(df, 0.0) * lf), 0.0)
            bwd = jnp.where(db >= 0, jnp.exp(jnp.maximum(db, 0.0) * lb), 0.0)
            dsum_ref[h] = fwd + bwd


def _ctx_kernel(ctx_ref, mod_ref, nw_ref, wkv_ref, kdft_ref, kdbt_ref, gf_ref, gb_ref,
                scf_ref, scb_ref):
    d = ctx_ref.shape[-1]
    x = ctx_ref[0]
    ctx_row = mod_ref.shape[0] // 2
    mod = mod_ref[ctx_row:ctx_row + 1, :]
    hc = _rms(x, nw_ref[...]) * (1.0 + mod[:, d:2 * d]) + mod[:, 0:d]
    pkv = _dot(hc.astype(BF16), wkv_ref[...])
    k = pkv[:, 0:QK_W] * (DK ** -0.5)
    v = pkv[:, QK_W:].astype(BF16)
    n_chunks = x.shape[0] // CHUNK
    sf = jnp.zeros((QK_W, DV), F32)
    for c in range(n_chunks):
        sl = slice(c * CHUNK, (c + 1) * CHUNK)
        kd = (k[sl, :].T * kdft_ref[...]).astype(BF16)
        sf = gf_ref[...] * sf + _kv_pairs(kd, v[sl, :])
    sb = jnp.zeros((QK_W, DV), F32)
    for c in range(n_chunks - 1, -1, -1):
        sl = slice(c * CHUNK, (c + 1) * CHUNK)
        kd = (k[sl, :].T * kdbt_ref[...]).astype(BF16)
        sb = gb_ref[...] * sb + _kv_pairs(kd, v[sl, :])
    scf_ref[0] = sf
    scb_ref[0] = sb


def _kv_kernel(n_tiles, tiles_per_batch, xc_ref, modc_ref, nw_ref, wkv_ref, cos_ref, sin_ref,
               kdbt_ref, gb_ref, scb_ref, h_ref, kt_ref, v_ref, stb_ref, sb_ref, pa_ref, pb_ref):
    s = pl.program_id(0)

    @pl.when((s >= 1) & (lax.rem(s - 1, tiles_per_batch) == 0))
    def _init():
        sb_ref[...] = scb_ref[0]

    batch = jnp.minimum(s, n_tiles - 1) // tiles_per_batch
    modc = modc_ref[pl.ds(batch, 1), :]
    refs = (xc_ref, modc, nw_ref, wkv_ref, cos_ref, sin_ref, kdbt_ref, gb_ref,
            h_ref, kt_ref, v_ref, stb_ref, sb_ref)
    _run_pipeline_step(s, n_tiles, 2, _kv_step, refs, (pa_ref,), (pb_ref,))


def _kv_step(stages, xc_ref, modc, nw_ref, wkv_ref, cos_ref, sin_ref, kdbt_ref, gb_ref,
             h_ref, kt_ref, v_ref, stb_ref, sb_ref, p_new_ref, p_old):
    project_stage, finish_stage = stages
    d = xc_ref.shape[-1]
    tile = xc_ref.shape[1]
    half = tile // 2
    row_halves = [slice(0, half), slice(half, tile)]
    quarter = [slice(i * half // 2, (i + 1) * half // 2) for i in range(4)]
    upper16 = (lax.broadcasted_iota(jnp.int32, (CHUNK, LANES), 1) & 16) != 0

    def norm_rows(rows):
        h = _rms(xc_ref[0, rows, :], nw_ref[...]) * (1.0 + modc[:, d:2 * d]) + modc[:, 0:d]
        h_ref[0, rows, :] = h.astype(BF16)

    def project(rows, lo, hi):
        p_new_ref[rows, lo:hi] = _dot(h_ref[0, rows, :], wkv_ref[:, lo:hi])

    def chunk(c):
        sl = slice(c * CHUNK, (c + 1) * CHUNK)
        k = [_rope(p_old[sl, p * LANES:(p + 1) * LANES], cos_ref[sl, :], sin_ref[sl, :], upper16)
             * (DK ** -0.5) for p in range(PAIRS)]
        v = p_old[sl, QK_W:].astype(BF16)
        v_ref[0, sl, :] = v
        kt = jnp.concatenate([kp.T for kp in k], axis=0)
        kt_ref[0, c] = kt.astype(BF16)
        new = _kv_pairs((kt * kdbt_ref[...]).astype(BF16), v)
        yield
        stb_ref[0, c] = sb_ref[...].astype(BF16)
        sb_ref[...] = gb_ref[...] * sb_ref[...] + new

    projections, late_norms, tasks = [], [], []
    if project_stage:
        norm_rows(quarter[0])
        norm_rows(quarter[1])
        projections = [functools.partial(project, row_halves[r], lo, hi)
                       for r in range(2) for lo, hi in _col_groups(p_new_ref.shape[1])]
        late_norms = [(functools.partial(norm_rows, q), None) for q in quarter[2:]]
    if finish_stage:
        tasks = [(chunk(c), 2) for c in range(tile // CHUNK - 1, -1, -1)]
    _emit_pipelined(projections, late_norms, tasks, depth=2)


def _main_kernel(n_tiles, tiles_per_batch, h_ref, x_ref, mod_ref, wq_ref, wg_ref, cos_ref,
                 sin_ref, kt_ref, v_ref, stb_ref, scf_ref, dsum_ref, af_ref, ab_ref, kdft_ref,
                 gf_ref, gnw_ref, lnw_ref, ws_ref, bs_ref, wout_ref, fnw_ref, o_ref,
                 sf_ref, pa_ref, pb_ref, ya_ref, yb_ref):
    s = pl.program_id(0)

    @pl.when((s >= 1) & (lax.rem(s - 1, tiles_per_batch) == 0))
    def _init():
        sf_ref[...] = scf_ref[0]

    batch = jnp.clip(s - 2, 0, n_tiles - 1) // tiles_per_batch
    mod = mod_ref[pl.ds(batch, 1), :]
    refs = (h_ref, x_ref, mod, wq_ref, wg_ref, cos_ref, sin_ref, kt_ref, v_ref,
            stb_ref, dsum_ref, af_ref, ab_ref, kdft_ref, gf_ref, gnw_ref, lnw_ref, ws_ref,
            bs_ref, wout_ref, fnw_ref, o_ref, sf_ref)
    _run_pipeline_step(s, n_tiles, 3, _main_step, refs, (pa_ref, ya_ref), (pb_ref, yb_ref))


def _main_step(stages, h_ref, x_ref, mod, wq_ref, wg_ref, cos_ref, sin_ref, kt_ref, v_ref,
               stb_ref, dsum_ref, af_ref, ab_ref, kdft_ref, gf_ref, gnw_ref, lnw_ref, ws_ref,
               bs_ref, wout_ref, fnw_ref, o_ref, sf_ref, p_new_ref, y_ref, p_all, y_old):
    project_stage, mix_stage, out_stage = stages
    d = x_ref.shape[-1]
    tile = x_ref.shape[1]
    n_chunks = tile // CHUNK
    half = tile // 2
    row_halves = [slice(0, half), slice(half, tile)]
    c_gr = QK_W
    c_u = c_gr + RET_W
    c_vm = c_u + GM_W
    c_gm = c_vm + GM_W
    lane = lax.broadcasted_iota(jnp.int32, (CHUNK, LANES), 1)
    upper16 = (lane & 16) != 0
    head_of_lane = lane // DK

    operand = {}

    def loaded_once(key, load):
        if key not in operand:
            operand[key] = load()
        return operand[key]

    def project(r, lo, hi):
        rows = row_halves[r]
        lhs = loaded_once(("h", r), lambda: h_ref[0, rows, :])
        w = wq_ref[...] if hi <= QK_W else wg_ref[:, lo - QK_W:hi - QK_W]
        p_new_ref[rows, lo:hi] = _dot(lhs, w)

    def retention(c, p):
        sl = slice(c * CHUNK, (c + 1) * CHUNK)
        ps = slice(p * LANES, (p + 1) * LANES)
        q2 = _rope(p_all[sl, ps], cos_ref[sl, :], sin_ref[sl, :], upper16)
        kt = kt_ref[0, c, ps, :]
        vp = v_ref[0, sl, p * 2 * DV:(p + 1) * 2 * DV]
        qa = q2 * af_ref[:, ps]
        qb = q2 * ab_ref[:, ps]
        scores, qx = [], []
        for hh in range(2):
            mine = head_of_lane == hh
            scores.append(_dot(jnp.where(mine, q2, 0.0).astype(BF16), kt))
            qx.append(jnp.concatenate([jnp.where(mine, qa, 0.0), jnp.where(mine, qb, 0.0)],
                                      axis=1).astype(BF16))
        kd = (kt.astype(F32) * kdft_ref[ps, :]).astype(BF16)
        yield
        state = jnp.concatenate([sf_ref[ps, :].astype(BF16), stb_ref[0, c, ps, :]], axis=0)
        o = []
        for hh in range(2):
            a = (scores[hh] * dsum_ref[2 * p + hh]).astype(BF16)
            o.append(_dot(a, vp[:, hh * DV:(hh + 1) * DV]) + _dot(qx[hh], state))
        r = _dot(kd, vp)
        yield
        new = jnp.concatenate([r[0:DK, 0:DV], r[DK:2 * DK, DV:2 * DV]], axis=0)
        sf_ref[ps, :] = gf_ref[ps, :] * sf_ref[ps, :] + new
        for hh in range(2):
            head = 2 * p + hh
            mu = jnp.mean(o[hh], axis=-1, keepdims=True)
            var = jnp.mean(jnp.square(o[hh] - mu), axis=-1, keepdims=True)
            hs = slice(head * DV, (head + 1) * DV)
            y = (o[hh] - mu) * lax.rsqrt(var + EPS) * gnw_ref[:, hs]
            gate_cols = slice(c_gr + head * DV, c_gr + (head + 1) * DV)
            y_ref[sl, hs] = (y * _silu(p_all[sl, gate_cols])).astype(BF16)

    def gating(c):
        sl = slice(c * CHUNK, (c + 1) * CHUNK)
        vg = _gelu(p_all[sl, c_vm:c_gm])
        mu = jnp.mean(vg, axis=-1, keepdims=True)
        var = jnp.mean(jnp.square(vg - mu), axis=-1, keepdims=True)
        vn = ((vg - mu) * lax.rsqrt(var + EPS) * lnw_ref[...]).astype(BF16)
        s = [_dot(ws_ref[g], vn[:, g * DG:(g + 1) * DG]) for g in range(GROUPS)]
        yield
        for g in range(GROUPS):
            u = _gelu(p_all[sl, c_u + g * DG:c_u + (g + 1) * DG])
            gm = _silu(p_all[sl, c_gm + g * DG:c_gm + (g + 1) * DG])
            y_ref[sl, RET_W + g * DG:RET_W + (g + 1) * DG] = (u * (s[g] + bs_ref[g]) * gm).astype(BF16)

    out_groups = _col_groups(d)
    xn = {}

    def out_project(r, g):
        rows = row_halves[r]
        lo, hi = out_groups[g]
        lhs = loaded_once(("y", r), lambda: y_old[rows, :])
        out = _dot(lhs, wout_ref[:, lo:hi])
        xn[r, g] = x_ref[0, rows, lo:hi] + mod[:, 2 * d + lo:2 * d + hi] * out

    def finish(r, part):
        sub = slice(part * half // 2, (part + 1) * half // 2)
        rows = slice(row_halves[r].start + sub.start, row_halves[r].start + sub.stop)
        parts = [xn[r, g][sub, :] for g in range(len(out_groups))]
        ms = sum(jnp.sum(t * t, axis=-1, keepdims=True) for t in parts) * (1.0 / d)
        rs = lax.rsqrt(ms + EPS)
        for (lo, hi), t in zip(out_groups, parts):
            o_ref[0, rows, lo:hi] = t * rs * fnw_ref[:, lo:hi]

    tasks = []
    if mix_stage:
        for c in range(n_chunks):
            tasks += [(retention(c, p), 3) for p in range(PAIRS)] + [(gating(c), 2)]
    big, relaxed = [], {}

    def place(tick, piece):
        while tick in relaxed:
            tick += 1
        relaxed[tick] = piece

    for r in range(2):
        if out_stage:
            big += [functools.partial(out_project, r, g) for g in range(len(out_groups))]
            place(len(big) + 1, functools.partial(finish, r, 0))
            place(len(big) + 3, functools.partial(finish, r, 1))
        if project_stage:
            big += [functools.partial(project, r, lo, hi)
                    for lo, hi in _col_groups(p_new_ref.shape[1])]
    side = [(None, relaxed.get(tick)) for tick in range(max(relaxed, default=-1) + 1)]
    _emit_pipelined(big, side, tasks, depth=3)


def _full(shape):
    return pl.BlockSpec(shape, lambda *_: (0,) * len(shape))


def kernel(x, c, ctx, c_ctx, w_ada, b_ada, norm_w, w_in, ret_decay_f, ret_decay_b, ret_gn_w,
           gmlp_ln_w, w_s, b_s, w_out, final_norm_w):
    bsz, seq, d = x.shape
    ctx_len = ctx.shape[1]
    depth = w_ada.shape[0]
    assert depth == 1 and d % LANES == 0
    assert seq % TILE_KV == 0 and seq % TILE_MAIN == 0 and ctx_len % CHUNK == 0
    assert seq % GRID_W == 0 and TILE_MAIN % (4 * CHUNK) == 0 and TILE_KV % (4 * CHUNK) == 0
    n_chunks = seq // CHUNK
    d3 = 3 * d
    n_kv = QK_W + RET_W
    n_rest = w_in.shape[2] - n_kv
    n_gates = n_rest - QK_W
    assert QK_W + n_kv == d and n_gates % d == 0 and d3 // d == 1 + n_gates // d
    assert _col_groups(n_rest)[0] == (0, QK_W)

    quarter = DK // 4
    inv_freq = ROPE_BASE ** (-jnp.arange(quarter, dtype=F32) / quarter)
    invf_lane = jnp.tile(inv_freq, LANES // quarter)[None, :]
    nw = norm_w[0][None, :]
    rows2 = 2 * bsz

    n_col_blocks = d3 // d
    smem = pl.BlockSpec(memory_space=pltpu.SMEM)
    (mod, cos_t, sin_t, dsum, af, ab, kdft, kdbt, gf, gb,
     w_q, w_kv, w_g, w_out_b, w_s_b, b_s_full) = pl.pallas_call(
        _prologue_kernel,
        grid=(n_col_blocks,),
        in_specs=[smem, smem, _full((bsz, d)), _full((1, d)),
                  pl.BlockSpec((d, d), lambda i: (0, i)),
                  pl.BlockSpec((1, d), lambda i: (0, i)),
                  _full((1, LANES)),
                  pl.BlockSpec((d, d), lambda i: (0, i)),
                  _full((RET_W + GM_W, d)),
                  _full((GROUPS, CHUNK, CHUNK)), _full((GROUPS, CHUNK))],
        out_specs=[pl.BlockSpec((rows2, d), lambda i: (0, i)),
                   _full((seq, LANES)), _full((seq, LANES)),
                   _full((HEADS, CHUNK, CHUNK)),
                   _full((CHUNK, QK_W)), _full((CHUNK, QK_W)),
                   _full((QK_W, CHUNK)), _full((QK_W, CHUNK)),
                   _full((QK_W, DV)), _full((QK_W, DV)),
                   _full((d, QK_W)), _full((d, n_kv)),
                   pl.BlockSpec((d, d), lambda i: (0, jnp.maximum(i - 1, 0))),
                   _full((RET_W + GM_W, d)),
                   _full((GROUPS, CHUNK, CHUNK)), _full((GROUPS, CHUNK, DG))],
        out_shape=[jax.ShapeDtypeStruct((rows2, d3), F32),
                   jax.ShapeDtypeStruct((seq, LANES), F32),
                   jax.ShapeDtypeStruct((seq, LANES), F32),
                   jax.ShapeDtypeStruct((HEADS, CHUNK, CHUNK), F32),
                   jax.ShapeDtypeStruct((CHUNK, QK_W), F32),
                   jax.ShapeDtypeStruct((CHUNK, QK_W), F32),
                   jax.ShapeDtypeStruct((QK_W, CHUNK), F32),
                   jax.ShapeDtypeStruct((QK_W, CHUNK), F32),
                   jax.ShapeDtypeStruct((QK_W, DV), F32),
                   jax.ShapeDtypeStruct((QK_W, DV), F32),
                   jax.ShapeDtypeStruct((d, QK_W), BF16),
                   jax.ShapeDtypeStruct((d, n_kv), BF16),
                   jax.ShapeDtypeStruct((d, n_gates), BF16),
                   jax.ShapeDtypeStruct((RET_W + GM_W, d), BF16),
                   jax.ShapeDtypeStruct((GROUPS, CHUNK, CHUNK), BF16),
                   jax.ShapeDtypeStruct((GROUPS, CHUNK, DG), F32)],
        compiler_params=pltpu.CompilerParams(dimension_semantics=("arbitrary",)),
        name="prologue",
    )(ret_decay_f[0], ret_decay_b[0], c, c_ctx[None, :], w_ada[0], b_ada[0][None, :], invf_lane,
      w_in[0], w_out[0], w_s[0], b_s[0])

    s_cf, s_cb = pl.pallas_call(
        _ctx_kernel,
        grid=(bsz,),
        in_specs=[pl.BlockSpec((1, ctx_len, d), lambda b: (b, 0, 0)),
                  _full((rows2, d3)),
                  _full((1, d)), _full((d, n_kv)),
                  _full((QK_W, CHUNK)), _full((QK_W, CHUNK)),
                  _full((QK_W, DV)), _full((QK_W, DV))],
        out_specs=[pl.BlockSpec((1, QK_W, DV), lambda b: (b, 0, 0)),
                   pl.BlockSpec((1, QK_W, DV), lambda b: (b, 0, 0))],
        out_shape=[jax.ShapeDtypeStruct((bsz, QK_W, DV), F32),
                   jax.ShapeDtypeStruct((bsz, QK_W, DV), F32)],
        compiler_params=pltpu.CompilerParams(dimension_semantics=("arbitrary",)),
        name="ctx_states",
    )(ctx, mod, nw, w_kv, kdft, kdbt, gf, gb)

    def tile_maps(n_tiles, per_batch, reverse):
        def where(t):
            j = t % per_batch
            return t // per_batch, (per_batch - 1 - j) if reverse else j
        return [lambda s, k=k: where(jnp.clip(s - k, 0, n_tiles - 1)) for k in range(3)]

    nt_kv = seq // TILE_KV
    cpt_kv = TILE_KV // CHUNK
    cur, prv, _ = tile_maps(bsz * nt_kv, nt_kv, reverse=True)
    h, kt, v, st_b = pl.pallas_call(
        functools.partial(_kv_kernel, bsz * nt_kv, nt_kv),
        grid=(bsz * nt_kv + 1,),
        in_specs=[pl.BlockSpec((1, TILE_KV, d), lambda s: (*cur(s), 0)),
                  _full((rows2, d3)),
                  _full((1, d)), _full((d, n_kv)),
                  pl.BlockSpec((TILE_KV, LANES), lambda s: (prv(s)[1], 0)),
                  pl.BlockSpec((TILE_KV, LANES), lambda s: (prv(s)[1], 0)),
                  _full((QK_W, CHUNK)), _full((QK_W, DV)),
                  pl.BlockSpec((1, QK_W, DV), lambda s: (prv(s)[0], 0, 0))],
        out_specs=[pl.BlockSpec((1, TILE_KV, d), lambda s: (*cur(s), 0)),
                   pl.BlockSpec((1, cpt_kv, QK_W, CHUNK), lambda s: (*prv(s), 0, 0)),
                   pl.BlockSpec((1, TILE_KV, RET_W), lambda s: (*prv(s), 0)),
                   pl.BlockSpec((1, cpt_kv, QK_W, DV), lambda s: (*prv(s), 0, 0))],
        out_shape=[jax.ShapeDtypeStruct((bsz, seq, d), BF16),
                   jax.ShapeDtypeStruct((bsz, n_chunks, QK_W, CHUNK), BF16),
                   jax.ShapeDtypeStruct((bsz, seq, RET_W), BF16),
                   jax.ShapeDtypeStruct((bsz, n_chunks, QK_W, DV), BF16)],
        scratch_shapes=[pltpu.VMEM((QK_W, DV), F32),
                        pltpu.VMEM((TILE_KV, n_kv), F32),
                        pltpu.VMEM((TILE_KV, n_kv), F32)],
        compiler_params=pltpu.CompilerParams(dimension_semantics=("arbitrary",)),
        name="kv_sweep",
    )(x, mod, nw, w_kv, cos_t, sin_t, kdbt, gb, s_cb)

    nt = seq // TILE_MAIN
    cpt = TILE_MAIN // CHUNK
    cur, prv, prv2 = tile_maps(bsz * nt, nt, reverse=False)
    out = pl.pallas_call(
        functools.partial(_main_kernel, bsz * nt, nt),
        grid=(bsz * nt + 2,),
        in_specs=[pl.BlockSpec((1, TILE_MAIN, d), lambda s: (*cur(s), 0)),
                  pl.BlockSpec((1, TILE_MAIN, d), lambda s: (*prv2(s), 0)),
                  _full((rows2, d3)),
                  _full((d, QK_W)), _full((d, n_gates)),
                  pl.BlockSpec((TILE_MAIN, LANES), lambda s: (prv(s)[1], 0)),
                  pl.BlockSpec((TILE_MAIN, LANES), lambda s: (prv(s)[1], 0)),
                  pl.BlockSpec((1, cpt, QK_W, CHUNK), lambda s: (*prv(s), 0, 0)),
                  pl.BlockSpec((1, TILE_MAIN, RET_W), lambda s: (*prv(s), 0)),
                  pl.BlockSpec((1, cpt, QK_W, DV), lambda s: (*prv(s), 0, 0)),
                  pl.BlockSpec((1, QK_W, DV), lambda s: (prv(s)[0], 0, 0)),
                  _full((HEADS, CHUNK, CHUNK)),
                  _full((CHUNK, QK_W)), _full((CHUNK, QK_W)),
                  _full((QK_W, CHUNK)), _full((QK_W, DV)),
                  _full((1, RET_W)), _full((1, GM_W)),
                  _full((GROUPS, CHUNK, CHUNK)), _full((GROUPS, CHUNK, DG)),
                  _full((RET_W + GM_W, d)), _full((1, d))],
        out_specs=pl.BlockSpec((1, TILE_MAIN, d), lambda s: (*prv2(s), 0)),
        out_shape=jax.ShapeDtypeStruct((bsz, seq, d), x.dtype),
        scratch_shapes=[pltpu.VMEM((QK_W, DV), F32),
                        pltpu.VMEM((TILE_MAIN, n_rest), F32),
                        pltpu.VMEM((TILE_MAIN, n_rest), F32),
                        pltpu.VMEM((TILE_MAIN, RET_W + GM_W), BF16),
                        pltpu.VMEM((TILE_MAIN, RET_W + GM_W), BF16)],
        compiler_params=pltpu.CompilerParams(dimension_semantics=("arbitrary",)),
        name="main_sweep",
    )(h, x, mod, w_q, w_g, cos_t, sin_t, kt, v, st_b, s_cf, dsum, af, ab, kdft, gf,
      ret_gn_w[0][None, :], gmlp_ln_w[0][None, :], w_s_b, b_s_full, w_out_b,
      final_norm_w[None, :])
    return out
```

```python
import functools

import jax
import jax.numpy as jnp
from jax import lax
from jax.experimental import pallas as pl
from jax.experimental.pallas import tpu as pltpu

F32 = jnp.float32
BF16 = jnp.bfloat16

HEADS = 4
DK = 64
DV = 128
CHUNK = 128
GROUPS = 4
DG = 128
GRID_W = 64
ROPE_BASE = 10000.0
EPS = 1e-6
LANES = 128
PAIRS = HEADS * DK // LANES
RET_W = HEADS * DV
GM_W = GROUPS * DG
QK_W = HEADS * DK

DOT_COLS = 512
TILE_KV = 2048
TILE_MAIN = 512


def _rms(x, w):
    ms = jnp.mean(x * x, axis=-1, keepdims=True)
    return x * lax.rsqrt(ms + EPS) * w


def _silu(x):
    return x * jax.nn.sigmoid(x)


def _gelu(x):
    return 0.5 * x * (1.0 + lax.erf(x * (0.5 ** 0.5)))


_dot = functools.partial(jnp.dot, preferred_element_type=F32)


def _col_groups(n):
    first = n % DOT_COLS
    edges = ([0] if first == 0 else [0, first]) + list(range(first + DOT_COLS, n + 1, DOT_COLS))
    return list(zip(edges[:-1], edges[1:]))


def _rope(p, cos, sin_signed, upper16):
    partner = jnp.where(upper16, pltpu.roll(p, 16, 1), pltpu.roll(p, LANES - 16, 1))
    return p * cos + partner * sin_signed


def _kv_pairs(kd, v):
    out = []
    for p in range(PAIRS):
        r = _dot(kd[p * LANES:(p + 1) * LANES, :], v[:, p * 2 * DV:(p + 1) * 2 * DV])
        out.append(r[0:DK, 0:DV])
        out.append(r[DK:2 * DK, DV:2 * DV])
    return jnp.concatenate(out, axis=0)


def _emit_pipelined(big_pieces, side_pieces, tasks):
    in_flight = []
    last_tick = max([start + parts for start, _, parts in tasks] +
                    [len(big_pieces), len(side_pieces)])
    for tick in range(last_tick):
        if tick < len(big_pieces):
            big_pieces[tick]()
        urgent, relaxed = side_pieces[tick] if tick < len(side_pieces) else (None, None)
        if urgent is not None:
            urgent()
        in_flight += [[gen, parts] for start, gen, parts in tasks if start == tick]
        for t in [t for t in in_flight if t[1] > 1] + [t for t in in_flight if t[1] == 1]:
            next(t[0], None)
            t[1] -= 1
        in_flight = [t for t in in_flight if t[1] > 0]
        if relaxed is not None:
            relaxed()
    assert not in_flight


def _run_pipeline_step(step, n_tiles, n_stages, body, refs, bufs_a, bufs_b):
    def call(s_static_parity, stages):
        new, old = (bufs_a, bufs_b) if s_static_parity == 0 else (bufs_b, bufs_a)
        body(stages, *refs, *new, *old)

    edge = n_stages - 1
    for s in list(range(edge)) + list(range(n_tiles, n_tiles + edge)):
        stages = tuple(0 <= s - j < n_tiles for j in range(n_stages))
        pl.when(step == s)(functools.partial(call, s % 2, stages))
    steady = (step >= edge) & (step < n_tiles)
    for parity in range(2):
        pl.when(steady & (lax.rem(step, 2) == parity))(
            functools.partial(call, parity, (True,) * n_stages))


def _prologue_kernel(decf_ref, decb_ref, c_ref, cctx_ref, w_ref, b_ref, invf_ref, win_ref,
                     wout_ref, ws_ref, bs_ref,
                     mod_ref, cos_ref, sin_ref, dsum_ref, af_ref, ab_ref,
                     kdft_ref, kdbt_ref, gf_ref, gb_ref,
                     wq_ref, wkv_ref, wg_ref, woutb_ref, wsb_ref, bsf_ref):
    i = pl.program_id(0)
    c = c_ref[...]
    cond = jnp.concatenate([c, jnp.broadcast_to(cctx_ref[...], c.shape)], axis=0)
    mod_ref[...] = _dot(_silu(cond), w_ref[...]) + b_ref[...]

    @pl.when(i == 0)
    def _first_block():
        wq_ref[...] = win_ref[:, 0:QK_W].astype(BF16)
        wkv_ref[...] = win_ref[:, QK_W:].astype(BF16)
        woutb_ref[...] = wout_ref[...].astype(BF16)
        wsb_ref[...] = ws_ref[...].astype(BF16)
        for g in range(GROUPS):
            bsf_ref[g] = jnp.broadcast_to(bs_ref[g:g + 1, :], (CHUNK, CHUNK)).T

    @pl.when(i > 0)
    def _later_blocks():
        wg_ref[...] = win_ref[...].astype(BF16)

    @pl.when(i == 0)
    def _tables():
        pos = lax.broadcasted_iota(jnp.int32, (GRID_W, LANES), 0).astype(F32)
        lane = lax.broadcasted_iota(jnp.int32, (GRID_W, LANES), 1)
        ang = pos * invf_ref[...]
        cr = jnp.cos(ang)
        sr = jnp.sin(ang)
        sr = jnp.where((lane & 16) == 0, -sr, sr)
        by_row = (lane & 32) == 0
        n_rows = cos_ref.shape[0] // GRID_W
        for r in range(n_rows):
            sl = slice(r * GRID_W, (r + 1) * GRID_W)
            cos_ref[sl, :] = jnp.where(by_row, jnp.broadcast_to(cr[r:r + 1, :], cr.shape), cr)
            sin_ref[sl, :] = jnp.where(by_row, jnp.broadcast_to(sr[r:r + 1, :], sr.shape), sr)

        def log_decay(head_idx, dec_ref):
            d = jnp.full(head_idx.shape, dec_ref[HEADS - 1], F32)
            for h in range(HEADS - 2, -1, -1):
                d = jnp.where(head_idx == h, dec_ref[h], d)
            return -jnp.exp(d)

        n = lax.broadcasted_iota(jnp.int32, (CHUNK, QK_W), 0).astype(F32)
        hl = lax.broadcasted_iota(jnp.int32, (CHUNK, QK_W), 1) // DK
        af_ref[...] = jnp.exp((n + 1.0) * log_decay(hl, decf_ref))
        ab_ref[...] = jnp.exp((CHUNK - n) * log_decay(hl, decb_ref))

        m = lax.broadcasted_iota(jnp.int32, (QK_W, CHUNK), 1).astype(F32)
        hr = lax.broadcasted_iota(jnp.int32, (QK_W, CHUNK), 0) // DK
        lgf = log_decay(hr, decf_ref)
        lgb = log_decay(hr, decb_ref)
        kdft_ref[...] = jnp.exp((CHUNK - 1.0 - m) * lgf)
        kdbt_ref[...] = jnp.exp(m * lgb)
        gf_ref[...] = jnp.exp(float(CHUNK) * lgf)
        gb_ref[...] = jnp.exp(float(CHUNK) * lgb)

        nn = lax.broadcasted_iota(jnp.int32, (CHUNK, CHUNK), 0).astype(F32)
        mm = lax.broadcasted_iota(jnp.int32, (CHUNK, CHUNK), 1).astype(F32)
        for h in range(HEADS):
            lf = -jnp.exp(jnp.full((CHUNK, CHUNK), decf_ref[h], F32))
            lb = -jnp.exp(jnp.full((CHUNK, CHUNK), decb_ref[h], F32))
            df = nn - mm
            db = mm - nn
            fwd = jnp.where(df >= 0, jnp.exp(jnp.maximum(df, 0.0) * lf), 0.0)
            bwd = jnp.where(db >= 0, jnp.exp(jnp.maximum(db, 0.0) * lb), 0.0)
            dsum_ref[h] = fwd + bwd


def _ctx_kernel(ctx_ref, mod_ref, nw_ref, wkv_ref, kdft_ref, kdbt_ref, gf_ref, gb_ref,
                scf_ref, scb_ref):
    d = ctx_ref.shape[-1]
    x = ctx_ref[0]
    ctx_row = mod_ref.shape[0] // 2
    mod = mod_ref[ctx_row:ctx_row + 1, :]
    hc = _rms(x, nw_ref[...]) * (1.0 + mod[:, d:2 * d]) + mod[:, 0:d]
    pkv = _dot(hc.astype(BF16), wkv_ref[...])
    k = pkv[:, 0:QK_W] * (DK ** -0.5)
    v = pkv[:, QK_W:].astype(BF16)
    n_chunks = x.shape[0] // CHUNK
    sf = jnp.zeros((QK_W, DV), F32)
    for c in range(n_chunks):
        sl = slice(c * CHUNK, (c + 1) * CHUNK)
        kd = (k[sl, :].T * kdft_ref[...]).astype(BF16)
        sf = gf_ref[...] * sf + _kv_pairs(kd, v[sl, :])
    sb = jnp.zeros((QK_W, DV), F32)
    for c in range(n_chunks - 1, -1, -1):
        sl = slice(c * CHUNK, (c + 1) * CHUNK)
        kd = (k[sl, :].T * kdbt_ref[...]).astype(BF16)
        sb = gb_ref[...] * sb + _kv_pairs(kd, v[sl, :])
    scf_ref[0] = sf
    scb_ref[0] = sb


def _kv_kernel(n_tiles, tiles_per_batch, xc_ref, modc_ref, nw_ref, wkv_ref, cos_ref, sin_ref,
               kdbt_ref, gb_ref, scb_ref, h_ref, kt_ref, v_ref, stb_ref, sb_ref, pa_ref, pb_ref):
    s = pl.program_id(0)

    @pl.when((s >= 1) & (lax.rem(s - 1, tiles_per_batch) == 0))
    def _init():
        sb_ref[...] = scb_ref[0]

    batch = jnp.minimum(s, n_tiles - 1) // tiles_per_batch
    modc = modc_ref[pl.ds(batch, 1), :]
    refs = (xc_ref, modc, nw_ref, wkv_ref, cos_ref, sin_ref, kdbt_ref, gb_ref,
            h_ref, kt_ref, v_ref, stb_ref, sb_ref)
    _run_pipeline_step(s, n_tiles, 2, _kv_step, refs, (pa_ref,), (pb_ref,))


def _kv_step(stages, xc_ref, modc, nw_ref, wkv_ref, cos_ref, sin_ref, kdbt_ref, gb_ref,
             h_ref, kt_ref, v_ref, stb_ref, sb_ref, p_new_ref, p_old):
    project_stage, finish_stage = stages
    d = xc_ref.shape[-1]
    tile = xc_ref.shape[1]
    half = tile // 2
    row_halves = [slice(0, half), slice(half, tile)]
    quarter = [slice(i * half // 2, (i + 1) * half // 2) for i in range(4)]
    upper16 = (lax.broadcasted_iota(jnp.int32, (CHUNK, LANES), 1) & 16) != 0

    def norm_rows(rows):
        h = _rms(xc_ref[0, rows, :], nw_ref[...]) * (1.0 + modc[:, d:2 * d]) + modc[:, 0:d]
        h_ref[0, rows, :] = h.astype(BF16)

    def project(rows, lo, hi):
        p_new_ref[rows, lo:hi] = _dot(h_ref[0, rows, :], wkv_ref[:, lo:hi])

    def chunk(c):
        sl = slice(c * CHUNK, (c + 1) * CHUNK)
        k = [_rope(p_old[sl, p * LANES:(p + 1) * LANES], cos_ref[sl, :], sin_ref[sl, :], upper16)
             * (DK ** -0.5) for p in range(PAIRS)]
        v = p_old[sl, QK_W:].astype(BF16)
        v_ref[0, sl, :] = v
        kt = jnp.concatenate([kp.T for kp in k], axis=0)
        kt_ref[0, c] = kt.astype(BF16)
        new = _kv_pairs((kt * kdbt_ref[...]).astype(BF16), v)
        yield
        stb_ref[0, c] = sb_ref[...].astype(BF16)
        sb_ref[...] = gb_ref[...] * sb_ref[...] + new

    projections, late_norms, tasks = [], [], []
    if project_stage:
        norm_rows(quarter[0])
        norm_rows(quarter[1])
        projections = [functools.partial(project, row_halves[r], lo, hi)
                       for r in range(2) for lo, hi in _col_groups(p_new_ref.shape[1])]
        late_norms = [(functools.partial(norm_rows, q), None) for q in quarter[2:]]
    if finish_stage:
        tasks = [(tick, chunk(c), 2) for tick, c in enumerate(range(tile // CHUNK - 1, -1, -1))]
    _emit_pipelined(projections, late_norms, tasks)


def _main_kernel(n_tiles, tiles_per_batch, h_ref, x_ref, mod_ref, wq_ref, wg_ref, cos_ref,
                 sin_ref, kt_ref, v_ref, stb_ref, scf_ref, dsum_ref, af_ref, ab_ref, kdft_ref,
                 gf_ref, gnw_ref, lnw_ref, ws_ref, bs_ref, wout_ref, fnw_ref, o_ref,
                 sf_ref, pa_ref, pb_ref, ya_ref, yb_ref):
    s = pl.program_id(0)

    @pl.when((s >= 1) & (lax.rem(s - 1, tiles_per_batch) == 0))
    def _init():
        sf_ref[...] = scf_ref[0]

    batch = jnp.clip(s - 2, 0, n_tiles - 1) // tiles_per_batch
    mod = mod_ref[pl.ds(batch, 1), :]
    refs = (h_ref, x_ref, mod, wq_ref, wg_ref, cos_ref, sin_ref, kt_ref, v_ref,
            stb_ref, dsum_ref, af_ref, ab_ref, kdft_ref, gf_ref, gnw_ref, lnw_ref, ws_ref,
            bs_ref, wout_ref, fnw_ref, o_ref, sf_ref)
    _run_pipeline_step(s, n_tiles, 3, _main_step, refs, (pa_ref, ya_ref), (pb_ref, yb_ref))


def _main_step(stages, h_ref, x_ref, mod, wq_ref, wg_ref, cos_ref, sin_ref, kt_ref, v_ref,
               stb_ref, dsum_ref, af_ref, ab_ref, kdft_ref, gf_ref, gnw_ref, lnw_ref, ws_ref,
               bs_ref, wout_ref, fnw_ref, o_ref, sf_ref, p_new_ref, y_ref, p_all, y_old):
    project_stage, mix_stage, out_stage = stages
    d = x_ref.shape[-1]
    tile = x_ref.shape[1]
    n_chunks = tile // CHUNK
    half = tile // 2
    row_halves = [slice(0, half), slice(half, tile)]
    c_gr = QK_W
    c_u = c_gr + RET_W
    c_vm = c_u + GM_W
    c_gm = c_vm + GM_W
    lane = lax.broadcasted_iota(jnp.int32, (CHUNK, LANES), 1)
    upper16 = (lane & 16) != 0
    head_of_lane = lane // DK

    operand = {}

    def loaded_once(key, load):
        if key not in operand:
            operand[key] = load()
        return operand[key]

    def project(r, lo, hi):
        rows = row_halves[r]
        lhs = loaded_once(("h", r), lambda: h_ref[0, rows, :])
        w = wq_ref[...] if hi <= QK_W else wg_ref[:, lo - QK_W:hi - QK_W]
        p_new_ref[rows, lo:hi] = _dot(lhs, w)

    def retention(c, p):
        sl = slice(c * CHUNK, (c + 1) * CHUNK)
        ps = slice(p * LANES, (p + 1) * LANES)
        q2 = _rope(p_all[sl, ps], cos_ref[sl, :], sin_ref[sl, :], upper16)
        kt = kt_ref[0, c, ps, :]
        vp = v_ref[0, sl, p * 2 * DV:(p + 1) * 2 * DV]
        qa = q2 * af_ref[:, ps]
        qb = q2 * ab_ref[:, ps]
        scores, qx = [], []
        for hh in range(2):
            mine = head_of_lane == hh
            scores.append(_dot(jnp.where(mine, q2, 0.0).astype(BF16), kt))
            qx.append(jnp.concatenate([jnp.where(mine, qa, 0.0), jnp.where(mine, qb, 0.0)],
                                      axis=1).astype(BF16))
        kd = (kt.astype(F32) * kdft_ref[ps, :]).astype(BF16)
        yield
        state = jnp.concatenate([sf_ref[ps, :].astype(BF16), stb_ref[0, c, ps, :]], axis=0)
        o = []
        for hh in range(2):
            a = (scores[hh] * dsum_ref[2 * p + hh]).astype(BF16)
            o.append(_dot(a, vp[:, hh * DV:(hh + 1) * DV]) + _dot(qx[hh], state))
        r = _dot(kd, vp)
        yield
        new = jnp.concatenate([r[0:DK, 0:DV], r[DK:2 * DK, DV:2 * DV]], axis=0)
        sf_ref[ps, :] = gf_ref[ps, :] * sf_ref[ps, :] + new
        for hh in range(2):
            head = 2 * p + hh
            mu = jnp.mean(o[hh], axis=-1, keepdims=True)
            var = jnp.mean(jnp.square(o[hh] - mu), axis=-1, keepdims=True)
            hs = slice(head * DV, (head + 1) * DV)
            y = (o[hh] - mu) * lax.rsqrt(var + EPS) * gnw_ref[:, hs]
            gate_cols = slice(c_gr + head * DV, c_gr + (head + 1) * DV)
            y_ref[sl, hs] = (y * _silu(p_all[sl, gate_cols])).astype(BF16)

    def gating(c):
        sl = slice(c * CHUNK, (c + 1) * CHUNK)
        vg = _gelu(p_all[sl, c_vm:c_gm])
        mu = jnp.mean(vg, axis=-1, keepdims=True)
        var = jnp.mean(jnp.square(vg - mu), axis=-1, keepdims=True)
        yield
        vn = ((vg - mu) * lax.rsqrt(var + EPS) * lnw_ref[...]).astype(BF16)
        s = [_dot(ws_ref[g], vn[:, g * DG:(g + 1) * DG]) for g in range(GROUPS)]
        yield
        for g in range(GROUPS):
            if g == GROUPS // 2:
                yield
            u = _gelu(p_all[sl, c_u + g * DG:c_u + (g + 1) * DG])
            gm = _silu(p_all[sl, c_gm + g * DG:c_gm + (g + 1) * DG])
            y_ref[sl, RET_W + g * DG:RET_W + (g + 1) * DG] = (u * (s[g] + bs_ref[g]) * gm).astype(BF16)

    out_groups = _col_groups(d)
    xn = {}

    def out_project(r, g):
        rows = row_halves[r]
        lo, hi = out_groups[g]
        lhs = loaded_once(("y", r), lambda: y_old[rows, :])
        out = _dot(lhs, wout_ref[:, lo:hi])
        xn[r, g] = x_ref[0, rows, lo:hi] + mod[:, 2 * d + lo:2 * d + hi] * out

    def finish(r, part):
        sub = slice(part * half // 2, (part + 1) * half // 2)
        rows = slice(row_halves[r].start + sub.start, row_halves[r].start + sub.stop)
        parts = [xn[r, g][sub, :] for g in range(len(out_groups))]
        ms = sum(jnp.sum(t * t, axis=-1, keepdims=True) for t in parts) * (1.0 / d)
        rs = lax.rsqrt(ms + EPS)
        for (lo, hi), t in zip(out_groups, parts):
            o_ref[0, rows, lo:hi] = t * rs * fnw_ref[:, lo:hi]

    tasks = []
    if mix_stage:
        for c in range(n_chunks):
            base = 3 * c
            tasks += [(base + min(p, 1), retention(c, p), 3) for p in range(PAIRS)]
            tasks.append((base + 1, gating(c), 4))
    big, relaxed = [], {}

    def place(tick, piece):
        while tick in relaxed:
            tick += 1
        relaxed[tick] = piece

    for r in range(2):
        outs = [functools.partial(out_project, r, g) for g in range(len(out_groups))] if out_stage else []
        projs = ([functools.partial(project, r, lo, hi) for lo, hi in _col_groups(p_new_ref.shape[1])]
                 if project_stage else [])
        for i in range(max(len(outs), len(projs))):
            big += outs[i:i + 1] + projs[i:i + 1]
            if i == len(outs) - 1:
                place(len(big) + 1, functools.partial(finish, r, 0))
                place(len(big) + 3, functools.partial(finish, r, 1))
    side = [(None, relaxed.get(tick)) for tick in range(max(relaxed, default=-1) + 1)]
    _emit_pipelined(big, side, tasks)


def _full(shape):
    return pl.BlockSpec(shape, lambda *_: (0,) * len(shape))


def kernel(x, c, ctx, c_ctx, w_ada, b_ada, norm_w, w_in, ret_decay_f, ret_decay_b, ret_gn_w,
           gmlp_ln_w, w_s, b_s, w_out, final_norm_w):
    bsz, seq, d = x.shape
    ctx_len = ctx.shape[1]
    depth = w_ada.shape[0]
    assert depth == 1 and d % LANES == 0
    assert seq % TILE_KV == 0 and seq % TILE_MAIN == 0 and ctx_len % CHUNK == 0
    assert seq % GRID_W == 0 and TILE_MAIN % (4 * CHUNK) == 0 and TILE_KV % (4 * CHUNK) == 0
    n_chunks = seq // CHUNK
    d3 = 3 * d
    n_kv = QK_W + RET_W
    n_rest = w_in.shape[2] - n_kv
    n_gates = n_rest - QK_W
    assert QK_W + n_kv == d and n_gates % d == 0 and d3 // d == 1 + n_gates // d
    assert _col_groups(n_rest)[0] == (0, QK_W)

    quarter = DK // 4
    inv_freq = ROPE_BASE ** (-jnp.arange(quarter, dtype=F32) / quarter)
    invf_lane = jnp.tile(inv_freq, LANES // quarter)[None, :]
    nw = norm_w[0][None, :]
    rows2 = 2 * bsz

    n_col_blocks = d3 // d
    smem = pl.BlockSpec(memory_space=pltpu.SMEM)
    (mod, cos_t, sin_t, dsum, af, ab, kdft, kdbt, gf, gb,
     w_q, w_kv, w_g, w_out_b, w_s_b, b_s_full) = pl.pallas_call(
        _prologue_kernel,
        grid=(n_col_blocks,),
        in_specs=[smem, smem, _full((bsz, d)), _full((1, d)),
                  pl.BlockSpec((d, d), lambda i: (0, i)),
                  pl.BlockSpec((1, d), lambda i: (0, i)),
                  _full((1, LANES)),
                  pl.BlockSpec((d, d), lambda i: (0, i)),
                  _full((RET_W + GM_W, d)),
                  _full((GROUPS, CHUNK, CHUNK)), _full((GROUPS, CHUNK))],
        out_specs=[pl.BlockSpec((rows2, d), lambda i: (0, i)),
                   _full((seq, LANES)), _full((seq, LANES)),
                   _full((HEADS, CHUNK, CHUNK)),
                   _full((CHUNK, QK_W)), _full((CHUNK, QK_W)),
                   _full((QK_W, CHUNK)), _full((QK_W, CHUNK)),
                   _full((QK_W, DV)), _full((QK_W, DV)),
                   _full((d, QK_W)), _full((d, n_kv)),
                   pl.BlockSpec((d, d), lambda i: (0, jnp.maximum(i - 1, 0))),
                   _full((RET_W + GM_W, d)),
                   _full((GROUPS, CHUNK, CHUNK)), _full((GROUPS, CHUNK, DG))],
        out_shape=[jax.ShapeDtypeStruct((rows2, d3), F32),
                   jax.ShapeDtypeStruct((seq, LANES), F32),
                   jax.ShapeDtypeStruct((seq, LANES), F32),
                   jax.ShapeDtypeStruct((HEADS, CHUNK, CHUNK), F32),
                   jax.ShapeDtypeStruct((CHUNK, QK_W), F32),
                   jax.ShapeDtypeStruct((CHUNK, QK_W), F32),
                   jax.ShapeDtypeStruct((QK_W, CHUNK), F32),
                   jax.ShapeDtypeStruct((QK_W, CHUNK), F32),
                   jax.ShapeDtypeStruct((QK_W, DV), F32),
                   jax.ShapeDtypeStruct((QK_W, DV), F32),
                   jax.ShapeDtypeStruct((d, QK_W), BF16),
                   jax.ShapeDtypeStruct((d, n_kv), BF16),
                   jax.ShapeDtypeStruct((d, n_gates), BF16),
                   jax.ShapeDtypeStruct((RET_W + GM_W, d), BF16),
                   jax.ShapeDtypeStruct((GROUPS, CHUNK, CHUNK), BF16),
                   jax.ShapeDtypeStruct((GROUPS, CHUNK, DG), F32)],
        compiler_params=pltpu.CompilerParams(dimension_semantics=("arbitrary",)),
        name="prologue",
    )(ret_decay_f[0], ret_decay_b[0], c, c_ctx[None, :], w_ada[0], b_ada[0][None, :], invf_lane,
      w_in[0], w_out[0], w_s[0], b_s[0])

    s_cf, s_cb = pl.pallas_call(
        _ctx_kernel,
        grid=(bsz,),
        in_specs=[pl.BlockSpec((1, ctx_len, d), lambda b: (b, 0, 0)),
                  _full((rows2, d3)),
                  _full((1, d)), _full((d, n_kv)),
                  _full((QK_W, CHUNK)), _full((QK_W, CHUNK)),
                  _full((QK_W, DV)), _full((QK_W, DV))],
        out_specs=[pl.BlockSpec((1, QK_W, DV), lambda b: (b, 0, 0)),
                   pl.BlockSpec((1, QK_W, DV), lambda b: (b, 0, 0))],
        out_shape=[jax.ShapeDtypeStruct((bsz, QK_W, DV), F32),
                   jax.ShapeDtypeStruct((bsz, QK_W, DV), F32)],
        compiler_params=pltpu.CompilerParams(dimension_semantics=("arbitrary",)),
        name="ctx_states",
    )(ctx, mod, nw, w_kv, kdft, kdbt, gf, gb)

    def tile_maps(n_tiles, per_batch, reverse):
        def where(t):
            j = t % per_batch
            return t // per_batch, (per_batch - 1 - j) if reverse else j
        return [lambda s, k=k: where(jnp.clip(s - k, 0, n_tiles - 1)) for k in range(3)]

    nt_kv = seq // TILE_KV
    cpt_kv = TILE_KV // CHUNK
    cur, prv, _ = tile_maps(bsz * nt_kv, nt_kv, reverse=True)
    h, kt, v, st_b = pl.pallas_call(
        functools.partial(_kv_kernel, bsz * nt_kv, nt_kv),
        grid=(bsz * nt_kv + 1,),
        in_specs=[pl.BlockSpec((1, TILE_KV, d), lambda s: (*cur(s), 0)),
                  _full((rows2, d3)),
                  _full((1, d)), _full((d, n_kv)),
                  pl.BlockSpec((TILE_KV, LANES), lambda s: (prv(s)[1], 0)),
                  pl.BlockSpec((TILE_KV, LANES), lambda s: (prv(s)[1], 0)),
                  _full((QK_W, CHUNK)), _full((QK_W, DV)),
                  pl.BlockSpec((1, QK_W, DV), lambda s: (prv(s)[0], 0, 0))],
        out_specs=[pl.BlockSpec((1, TILE_KV, d), lambda s: (*cur(s), 0)),
                   pl.BlockSpec((1, cpt_kv, QK_W, CHUNK), lambda s: (*prv(s), 0, 0)),
                   pl.BlockSpec((1, TILE_KV, RET_W), lambda s: (*prv(s), 0)),
                   pl.BlockSpec((1, cpt_kv, QK_W, DV), lambda s: (*prv(s), 0, 0))],
        out_shape=[jax.ShapeDtypeStruct((bsz, seq, d), BF16),
                   jax.ShapeDtypeStruct((bsz, n_chunks, QK_W, CHUNK), BF16),
                   jax.ShapeDtypeStruct((bsz, seq, RET_W), BF16),
                   jax.ShapeDtypeStruct((bsz, n_chunks, QK_W, DV), BF16)],
        scratch_shapes=[pltpu.VMEM((QK_W, DV), F32),
                        pltpu.VMEM((TILE_KV, n_kv), F32),
                        pltpu.VMEM((TILE_KV, n_kv), F32)],
        compiler_params=pltpu.CompilerParams(dimension_semantics=("arbitrary",)),
        name="kv_sweep",
    )(x, mod, nw, w_kv, cos_t, sin_t, kdbt, gb, s_cb)

    nt = seq // TILE_MAIN
    cpt = TILE_MAIN // CHUNK
    cur, prv, prv2 = tile_maps(bsz * nt, nt, reverse=False)
    out = pl.pallas_call(
        functools.partial(_main_kernel, bsz * nt, nt),
        grid=(bsz * nt + 2,),
        in_specs=[pl.BlockSpec((1, TILE_MAIN, d), lambda s: (*cur(s), 0)),
                  pl.BlockSpec((1, TILE_MAIN, d), lambda s: (*prv2(s), 0)),
                  _full((rows2, d3)),
                  _full((d, QK_W)), _full((d, n_gates)),
                  pl.BlockSpec((TILE_MAIN, LANES), lambda s: (prv(s)[1], 0)),
                  pl.BlockSpec((TILE_MAIN, LANES), lambda s: (prv(s)[1], 0)),
                  pl.BlockSpec((1, cpt, QK_W, CHUNK), lambda s: (*prv(s), 0, 0)),
                  pl.BlockSpec((1, TILE_MAIN, RET_W), lambda s: (*prv(s), 0)),
                  pl.BlockSpec((1, cpt, QK_W, DV), lambda s: (*prv(s), 0, 0)),
                  pl.BlockSpec((1, QK_W, DV), lambda s: (prv(s)[0], 0, 0)),
                  _full((HEADS, CHUNK, CHUNK)),
                  _full((CHUNK, QK_W)), _full((CHUNK, QK_W)),
                  _full((QK_W, CHUNK)), _full((QK_W, DV)),
                  _full((1, RET_W)), _full((1, GM_W)),
                  _full((GROUPS, CHUNK, CHUNK)), _full((GROUPS, CHUNK, DG)),
                  _full((RET_W + GM_W, d)), _full((1, d))],
        out_specs=pl.BlockSpec((1, TILE_MAIN, d), lambda s: (*prv2(s), 0)),
        out_shape=jax.ShapeDtypeStruct((bsz, seq, d), x.dtype),
        scratch_shapes=[pltpu.VMEM((QK_W, DV), F32),
                        pltpu.VMEM((TILE_MAIN, n_rest), F32),
                        pltpu.VMEM((TILE_MAIN, n_rest), F32),
                        pltpu.VMEM((TILE_MAIN, RET_W + GM_W), BF16),
                        pltpu.VMEM((TILE_MAIN, RET_W + GM_W), BF16)],
        compiler_params=pltpu.CompilerParams(dimension_semantics=("arbitrary",)),
        name="main_sweep",
    )(h, x, mod, w_q, w_g, cos_t, sin_t, kt, v, st_b, s_cf, dsum, af, ab, kdft, gf,
      ret_gn_w[0][None, :], gmlp_ln_w[0][None, :], w_s_b, b_s_full, w_out_b,
      final_norm_w[None, :])
    return out
```

```python
import functools

import jax
import jax.numpy as jnp
from jax import lax
from jax.experimental import pallas as pl
from jax.experimental.pallas import tpu as pltpu

F32 = jnp.float32
BF16 = jnp.bfloat16

HEADS = 4
DK = 64
DV = 128
CHUNK = 128
GROUPS = 4
DG = 128
GRID_W = 64
ROPE_BASE = 10000.0
EPS = 1e-6
LANES = 128
PAIRS = HEADS * DK // LANES
RET_W = HEADS * DV
GM_W = GROUPS * DG
QK_W = HEADS * DK

DOT_COLS = 512
TILE_KV = 2048
TILE_MAIN = 512


def _rms(x, w):
    ms = jnp.mean(x * x, axis=-1, keepdims=True)
    return x * lax.rsqrt(ms + EPS) * w


def _silu(x):
    return x * jax.nn.sigmoid(x)


def _gelu(x):
    return 0.5 * x * (1.0 + lax.erf(x * (0.5 ** 0.5)))


_dot = functools.partial(jnp.dot, preferred_element_type=F32)


def _col_groups(n):
    first = n % DOT_COLS
    edges = ([0] if first == 0 else [0, first]) + list(range(first + DOT_COLS, n + 1, DOT_COLS))
    return list(zip(edges[:-1], edges[1:]))


def _rope(p, cos, sin_signed, upper16):
    partner = jnp.where(upper16, pltpu.roll(p, 16, 1), pltpu.roll(p, LANES - 16, 1))
    return p * cos + partner * sin_signed


def _kv_pairs(kd, v):
    out = []
    for p in range(PAIRS):
        r = _dot(kd[p * LANES:(p + 1) * LANES, :], v[:, p * 2 * DV:(p + 1) * 2 * DV])
        out.append(r[0:DK, 0:DV])
        out.append(r[DK:2 * DK, DV:2 * DV])
    return jnp.concatenate(out, axis=0)


def _emit_pipelined(big_pieces, side_pieces, tasks):
    in_flight = []
    last_tick = max([start + parts for start, _, parts in tasks] +
                    [len(big_pieces), len(side_pieces)])
    for tick in range(last_tick):
        if tick < len(big_pieces):
            big_pieces[tick]()
        urgent, relaxed = side_pieces[tick] if tick < len(side_pieces) else (None, None)
        if urgent is not None:
            urgent()
        in_flight += [[gen, parts] for start, gen, parts in tasks if start == tick]
        for t in [t for t in in_flight if t[1] > 1] + [t for t in in_flight if t[1] == 1]:
            next(t[0], None)
            t[1] -= 1
        in_flight = [t for t in in_flight if t[1] > 0]
        if relaxed is not None:
            relaxed()
    assert not in_flight


def _run_pipeline_step(step, n_tiles, n_stages, body, refs, bufs_a, bufs_b):
    def call(s_static_parity, stages):
        new, old = (bufs_a, bufs_b) if s_static_parity == 0 else (bufs_b, bufs_a)
        body(stages, *refs, *new, *old)

    edge = n_stages - 1
    for s in list(range(edge)) + list(range(n_tiles, n_tiles + edge)):
        stages = tuple(0 <= s - j < n_tiles for j in range(n_stages))
        pl.when(step == s)(functools.partial(call, s % 2, stages))
    steady = (step >= edge) & (step < n_tiles)
    for parity in range(2):
        pl.when(steady & (lax.rem(step, 2) == parity))(
            functools.partial(call, parity, (True,) * n_stages))


def _prologue_kernel(decf_ref, decb_ref, c_ref, cctx_ref, w_ref, b_ref, invf_ref, win_ref,
                     wout_ref, ws_ref, bs_ref,
                     mod_ref, cos_ref, sin_ref, dsum_ref, af_ref, ab_ref,
                     kdft_ref, kdbt_ref, gf_ref, gb_ref,
                     wq_ref, wkv_ref, wg_ref, woutb_ref, wsb_ref, bsf_ref):
    i = pl.program_id(0)
    c = c_ref[...]
    cond = jnp.concatenate([c, jnp.broadcast_to(cctx_ref[...], c.shape)], axis=0)
    mod_ref[...] = _dot(_silu(cond), w_ref[...]) + b_ref[...]

    @pl.when(i == 0)
    def _first_block():
        wq_ref[...] = win_ref[:, 0:QK_W].astype(BF16)
        wkv_ref[...] = win_ref[:, QK_W:].astype(BF16)
        woutb_ref[...] = wout_ref[...].astype(BF16)
        wsb_ref[...] = ws_ref[...].astype(BF16)
        for g in range(GROUPS):
            bsf_ref[g] = jnp.broadcast_to(bs_ref[g:g + 1, :], (CHUNK, CHUNK)).T

    @pl.when(i > 0)
    def _later_blocks():
        wg_ref[...] = win_ref[...].astype(BF16)

    @pl.when(i == 0)
    def _tables():
        pos = lax.broadcasted_iota(jnp.int32, (GRID_W, LANES), 0).astype(F32)
        lane = lax.broadcasted_iota(jnp.int32, (GRID_W, LANES), 1)
        ang = pos * invf_ref[...]
        cr = jnp.cos(ang)
        sr = jnp.sin(ang)
        sr = jnp.where((lane & 16) == 0, -sr, sr)
        by_row = (lane & 32) == 0
        n_rows = cos_ref.shape[0] // GRID_W
        for r in range(n_rows):
            sl = slice(r * GRID_W, (r + 1) * GRID_W)
            cos_ref[sl, :] = jnp.where(by_row, jnp.broadcast_to(cr[r:r + 1, :], cr.shape), cr)
            sin_ref[sl, :] = jnp.where(by_row, jnp.broadcast_to(sr[r:r + 1, :], sr.shape), sr)

        def log_decay(head_idx, dec_ref):
            d = jnp.full(head_idx.shape, dec_ref[HEADS - 1], F32)
            for h in range(HEADS - 2, -1, -1):
                d = jnp.where(head_idx == h, dec_ref[h], d)
            return -jnp.exp(d)

        n = lax.broadcasted_iota(jnp.int32, (CHUNK, QK_W), 0).astype(F32)
        hl = lax.broadcasted_iota(jnp.int32, (CHUNK, QK_W), 1) // DK
        af_ref[...] = jnp.exp((n + 1.0) * log_decay(hl, decf_ref))
        ab_ref[...] = jnp.exp((CHUNK - n) * log_decay(hl, decb_ref))

        m = lax.broadcasted_iota(jnp.int32, (QK_W, CHUNK), 1).astype(F32)
        hr = lax.broadcasted_iota(jnp.int32, (QK_W, CHUNK), 0) // DK
        lgf = log_decay(hr, decf_ref)
        lgb = log_decay(hr, decb_ref)
        kdft_ref[...] = jnp.exp((CHUNK - 1.0 - m) * lgf)
        kdbt_ref[...] = jnp.exp(m * lgb)
        gf_ref[...] = jnp.exp(float(CHUNK) * lgf)
        gb_ref[...] = jnp.exp(float(CHUNK) * lgb)

        nn = lax.broadcasted_iota(jnp.int32, (CHUNK, CHUNK), 0).astype(F32)
        mm = lax.broadcasted_iota(jnp.int32, (CHUNK, CHUNK), 1).astype(F32)
        for h in range(HEADS):
            lf = -jnp.exp(jnp.full((CHUNK, CHUNK), decf_ref[h], F32))
            lb = -jnp.exp(jnp.full((CHUNK, CHUNK), decb_ref[h], F32))
            df = nn - mm
            db = mm - nn
            fwd = jnp.where(df >= 0, jnp.exp(jnp.maximum(df, 0.0) * lf), 0.0)
            bwd = jnp.where(db >= 0, jnp.exp(jnp.maximum(db, 0.0) * lb), 0.0)
            dsum_ref[h] = fwd + bwd


def _ctx_kernel(n_tiles, ctx_ref, mod_ref, nw_ref, wkv_ref, kdft_ref, kdbt_ref, gf_ref, gb_ref,
                scf_ref, scb_ref, h_ref, pa_ref, pb_ref):
    refs = (ctx_ref, mod_ref, nw_ref, wkv_ref, kdft_ref, kdbt_ref, gf_ref, gb_ref,
            scf_ref, scb_ref, h_ref)
    _run_pipeline_step(pl.program_id(0), n_tiles, 2, _ctx_step, refs, (pa_ref,), (pb_ref,))


def _ctx_step(stages, ctx_ref, mod_ref, nw_ref, wkv_ref, kdft_ref, kdbt_ref, gf_ref, gb_ref,
              scf_ref, scb_ref, h_ref, p_new_ref, p_old):
    project_stage, state_stage = stages
    d = ctx_ref.shape[-1]
    n_chunks = ctx_ref.shape[1] // CHUNK
    ctx_row = mod_ref.shape[0] // 2
    mod = mod_ref[ctx_row:ctx_row + 1, :]

    def norm_rows(rows):
        h = _rms(ctx_ref[0, rows, :], nw_ref[...]) * (1.0 + mod[:, d:2 * d]) + mod[:, 0:d]
        h_ref[rows, :] = h.astype(BF16)

    def project(lo, hi):
        p_new_ref[:, lo:hi] = _dot(h_ref[...], wkv_ref[:, lo:hi])

    new_f, new_b = {}, {}

    def chunk(c):
        sl = slice(c * CHUNK, (c + 1) * CHUNK)
        kt = (p_old[sl, 0:QK_W] * (DK ** -0.5)).T
        v = p_old[sl, QK_W:].astype(BF16)
        new_f[c] = _kv_pairs((kt * kdft_ref[...]).astype(BF16), v)
        new_b[c] = _kv_pairs((kt * kdbt_ref[...]).astype(BF16), v)
        yield

    def combine():
        yield
        sf = jnp.zeros((QK_W, DV), F32)
        for c in range(n_chunks):
            sf = gf_ref[...] * sf + new_f[c]
        sb = jnp.zeros((QK_W, DV), F32)
        for c in range(n_chunks - 1, -1, -1):
            sb = gb_ref[...] * sb + new_b[c]
        scf_ref[0] = sf
        scb_ref[0] = sb

    big, tasks = [], []
    if project_stage:
        for c in range(n_chunks):
            norm_rows(slice(c * CHUNK, (c + 1) * CHUNK))
        big = [functools.partial(project, lo, hi) for lo, hi in _col_groups(p_new_ref.shape[1])]
    if state_stage:
        tasks = [(c, chunk(c), 1) for c in range(n_chunks)] + [(n_chunks - 1, combine(), 2)]
    _emit_pipelined(big, [], tasks)


def _kv_kernel(n_tiles, tiles_per_batch, xc_ref, modc_ref, nw_ref, wkv_ref, cos_ref, sin_ref,
               kdbt_ref, gb_ref, scb_ref, h_ref, kt_ref, v_ref, stb_ref, sb_ref, pa_ref, pb_ref):
    s = pl.program_id(0)

    @pl.when((s >= 1) & (lax.rem(s - 1, tiles_per_batch) == 0))
    def _init():
        sb_ref[...] = scb_ref[0]

    batch = jnp.minimum(s, n_tiles - 1) // tiles_per_batch
    modc = modc_ref[pl.ds(batch, 1), :]
    refs = (xc_ref, modc, nw_ref, wkv_ref, cos_ref, sin_ref, kdbt_ref, gb_ref,
            h_ref, kt_ref, v_ref, stb_ref, sb_ref)
    _run_pipeline_step(s, n_tiles, 2, _kv_step, refs, (pa_ref,), (pb_ref,))


def _kv_step(stages, xc_ref, modc, nw_ref, wkv_ref, cos_ref, sin_ref, kdbt_ref, gb_ref,
             h_ref, kt_ref, v_ref, stb_ref, sb_ref, p_new_ref, p_old):
    project_stage, finish_stage = stages
    d = xc_ref.shape[-1]
    tile = xc_ref.shape[1]
    half = tile // 2
    row_halves = [slice(0, half), slice(half, tile)]
    quarter = [slice(i * half // 2, (i + 1) * half // 2) for i in range(4)]
    upper16 = (lax.broadcasted_iota(jnp.int32, (CHUNK, LANES), 1) & 16) != 0

    def norm_rows(rows):
        h = _rms(xc_ref[0, rows, :], nw_ref[...]) * (1.0 + modc[:, d:2 * d]) + modc[:, 0:d]
        h_ref[0, rows, :] = h.astype(BF16)

    def project(rows, lo, hi):
        p_new_ref[rows, lo:hi] = _dot(h_ref[0, rows, :], wkv_ref[:, lo:hi])

    def chunk(c):
        sl = slice(c * CHUNK, (c + 1) * CHUNK)
        k = [_rope(p_old[sl, p * LANES:(p + 1) * LANES], cos_ref[sl, :], sin_ref[sl, :], upper16)
             * (DK ** -0.5) for p in range(PAIRS)]
        v = p_old[sl, QK_W:].astype(BF16)
        v_ref[0, sl, :] = v
        kt = jnp.concatenate([kp.T for kp in k], axis=0)
        kt_ref[0, c] = kt.astype(BF16)
        new = _kv_pairs((kt * kdbt_ref[...]).astype(BF16), v)
        yield
        stb_ref[0, c] = sb_ref[...].astype(BF16)
        sb_ref[...] = gb_ref[...] * sb_ref[...] + new

    projections, late_norms, tasks = [], [], []
    if project_stage:
        norm_rows(quarter[0])
        norm_rows(quarter[1])
        projections = [functools.partial(project, row_halves[r], lo, hi)
                       for r in range(2) for lo, hi in _col_groups(p_new_ref.shape[1])]
        late_norms = [(functools.partial(norm_rows, q), None) for q in quarter[2:]]
    if finish_stage:
        tasks = [(tick, chunk(c), 2) for tick, c in enumerate(range(tile // CHUNK - 1, -1, -1))]
    _emit_pipelined(projections, late_norms, tasks)


def _main_kernel(n_tiles, tiles_per_batch, h_ref, x_ref, mod_ref, wq_ref, wg_ref, cos_ref,
                 sin_ref, kt_ref, v_ref, stb_ref, scf_ref, dsum_ref, af_ref, ab_ref, kdft_ref,
                 gf_ref, gnw_ref, lnw_ref, ws_ref, bs_ref, wout_ref, fnw_ref, o_ref,
                 sf_ref, pa_ref, pb_ref, ya_ref, yb_ref):
    s = pl.program_id(0)

    @pl.when((s >= 1) & (lax.rem(s - 1, tiles_per_batch) == 0))
    def _init():
        sf_ref[...] = scf_ref[0]

    batch = jnp.clip(s - 2, 0, n_tiles - 1) // tiles_per_batch
    mod = mod_ref[pl.ds(batch, 1), :]
    refs = (h_ref, x_ref, mod, wq_ref, wg_ref, cos_ref, sin_ref, kt_ref, v_ref,
            stb_ref, dsum_ref, af_ref, ab_ref, kdft_ref, gf_ref, gnw_ref, lnw_ref, ws_ref,
            bs_ref, wout_ref, fnw_ref, o_ref, sf_ref)
    _run_pipeline_step(s, n_tiles, 3, _main_step, refs, (pa_ref, ya_ref), (pb_ref, yb_ref))


def _main_step(stages, h_ref, x_ref, mod, wq_ref, wg_ref, cos_ref, sin_ref, kt_ref, v_ref,
               stb_ref, dsum_ref, af_ref, ab_ref, kdft_ref, gf_ref, gnw_ref, lnw_ref, ws_ref,
               bs_ref, wout_ref, fnw_ref, o_ref, sf_ref, p_new_ref, y_ref, p_all, y_old):
    project_stage, mix_stage, out_stage = stages
    d = x_ref.shape[-1]
    tile = x_ref.shape[1]
    n_chunks = tile // CHUNK
    half = tile // 2
    row_halves = [slice(0, half), slice(half, tile)]
    c_gr = QK_W
    c_u = c_gr + RET_W
    c_vm = c_u + GM_W
    c_gm = c_vm + GM_W
    lane = lax.broadcasted_iota(jnp.int32, (CHUNK, LANES), 1)
    upper16 = (lane & 16) != 0
    head_of_lane = lane // DK

    operand = {}

    def loaded_once(key, load):
        if key not in operand:
            operand[key] = load()
        return operand[key]

    def project(r, lo, hi):
        rows = row_halves[r]
        lhs = loaded_once(("h", r), lambda: h_ref[0, rows, :])
        w = wq_ref[...] if hi <= QK_W else wg_ref[:, lo - QK_W:hi - QK_W]
        p_new_ref[rows, lo:hi] = _dot(lhs, w)

    def retention(c, p):
        sl = slice(c * CHUNK, (c + 1) * CHUNK)
        ps = slice(p * LANES, (p + 1) * LANES)
        q2 = _rope(p_all[sl, ps], cos_ref[sl, :], sin_ref[sl, :], upper16)
        kt = kt_ref[0, c, ps, :]
        vp = v_ref[0, sl, p * 2 * DV:(p + 1) * 2 * DV]
        qa = q2 * af_ref[:, ps]
        qb = q2 * ab_ref[:, ps]
        scores, qx = [], []
        for hh in range(2):
            mine = head_of_lane == hh
            scores.append(_dot(jnp.where(mine, q2, 0.0).astype(BF16), kt))
            qx.append(jnp.concatenate([jnp.where(mine, qa, 0.0), jnp.where(mine, qb, 0.0)],
                                      axis=1).astype(BF16))
        kd = (kt.astype(F32) * kdft_ref[ps, :]).astype(BF16)
        yield
        state = jnp.concatenate([sf_ref[ps, :].astype(BF16), stb_ref[0, c, ps, :]], axis=0)
        o = []
        for hh in range(2):
            a = (scores[hh] * dsum_ref[2 * p + hh]).astype(BF16)
            o.append(_dot(a, vp[:, hh * DV:(hh + 1) * DV]) + _dot(qx[hh], state))
        r = _dot(kd, vp)
        yield
        new = jnp.concatenate([r[0:DK, 0:DV], r[DK:2 * DK, DV:2 * DV]], axis=0)
        sf_ref[ps, :] = gf_ref[ps, :] * sf_ref[ps, :] + new
        for hh in range(2):
            head = 2 * p + hh
            mu = jnp.mean(o[hh], axis=-1, keepdims=True)
            var = jnp.mean(jnp.square(o[hh] - mu), axis=-1, keepdims=True)
            hs = slice(head * DV, (head + 1) * DV)
            y = (o[hh] - mu) * lax.rsqrt(var + EPS) * gnw_ref[:, hs]
            gate_cols = slice(c_gr + head * DV, c_gr + (head + 1) * DV)
            y_ref[sl, hs] = (y * _silu(p_all[sl, gate_cols])).astype(BF16)

    def gating(c):
        sl = slice(c * CHUNK, (c + 1) * CHUNK)
        vg = _gelu(p_all[sl, c_vm:c_gm])
        mu = jnp.mean(vg, axis=-1, keepdims=True)
        var = jnp.mean(jnp.square(vg - mu), axis=-1, keepdims=True)
        yield
        vn = ((vg - mu) * lax.rsqrt(var + EPS) * lnw_ref[...]).astype(BF16)
        s = [_dot(ws_ref[g], vn[:, g * DG:(g + 1) * DG]) for g in range(GROUPS)]
        yield
        for g in range(GROUPS):
            if g == GROUPS // 2:
                yield
            u = _gelu(p_all[sl, c_u + g * DG:c_u + (g + 1) * DG])
            gm = _silu(p_all[sl, c_gm + g * DG:c_gm + (g + 1) * DG])
            y_ref[sl, RET_W + g * DG:RET_W + (g + 1) * DG] = (u * (s[g] + bs_ref[g]) * gm).astype(BF16)

    out_groups = _col_groups(d)
    xn = {}

    def out_project(r, g):
        rows = row_halves[r]
        lo, hi = out_groups[g]
        lhs = loaded_once(("y", r), lambda: y_old[rows, :])
        out = _dot(lhs, wout_ref[:, lo:hi])
        xn[r, g] = x_ref[0, rows, lo:hi] + mod[:, 2 * d + lo:2 * d + hi] * out

    def finish(r, part):
        sub = slice(part * half // 2, (part + 1) * half // 2)
        rows = slice(row_halves[r].start + sub.start, row_halves[r].start + sub.stop)
        parts = [xn[r, g][sub, :] for g in range(len(out_groups))]
        ms = sum(jnp.sum(t * t, axis=-1, keepdims=True) for t in parts) * (1.0 / d)
        rs = lax.rsqrt(ms + EPS)
        for (lo, hi), t in zip(out_groups, parts):
            o_ref[0, rows, lo:hi] = t * rs * fnw_ref[:, lo:hi]

    tasks = []
    if mix_stage:
        for c in range(n_chunks):
            base = 3 * c
            tasks += [(base + min(p, 1), retention(c, p), 3) for p in range(PAIRS)]
            tasks.append((base + 1, gating(c), 4))
    big, relaxed = [], {}

    def place(tick, piece):
        while tick in relaxed:
            tick += 1
        relaxed[tick] = piece

    for r in range(2):
        outs = [functools.partial(out_project, r, g) for g in range(len(out_groups))] if out_stage else []
        projs = ([functools.partial(project, r, lo, hi) for lo, hi in _col_groups(p_new_ref.shape[1])]
                 if project_stage else [])
        for i in range(max(len(outs), len(projs))):
            big += outs[i:i + 1] + projs[i:i + 1]
            if i == len(outs) - 1:
                place(len(big) + 1, functools.partial(finish, r, 0))
                place(len(big) + 3, functools.partial(finish, r, 1))
    side = [(None, relaxed.get(tick)) for tick in range(max(relaxed, default=-1) + 1)]
    _emit_pipelined(big, side, tasks)


def _full(shape):
    return pl.BlockSpec(shape, lambda *_: (0,) * len(shape))


def kernel(x, c, ctx, c_ctx, w_ada, b_ada, norm_w, w_in, ret_decay_f, ret_decay_b, ret_gn_w,
           gmlp_ln_w, w_s, b_s, w_out, final_norm_w):
    bsz, seq, d = x.shape
    ctx_len = ctx.shape[1]
    depth = w_ada.shape[0]
    assert depth == 1 and d % LANES == 0
    assert seq % TILE_KV == 0 and seq % TILE_MAIN == 0 and ctx_len % CHUNK == 0
    assert seq % GRID_W == 0 and TILE_MAIN % (4 * CHUNK) == 0 and TILE_KV % (4 * CHUNK) == 0
    n_chunks = seq // CHUNK
    d3 = 3 * d
    n_kv = QK_W + RET_W
    n_rest = w_in.shape[2] - n_kv
    n_gates = n_rest - QK_W
    assert QK_W + n_kv == d and n_gates % d == 0 and d3 // d == 1 + n_gates // d
    assert _col_groups(n_rest)[0] == (0, QK_W)

    quarter = DK // 4
    inv_freq = ROPE_BASE ** (-jnp.arange(quarter, dtype=F32) / quarter)
    invf_lane = jnp.tile(inv_freq, LANES // quarter)[None, :]
    nw = norm_w[0][None, :]
    rows2 = 2 * bsz

    n_col_blocks = d3 // d
    smem = pl.BlockSpec(memory_space=pltpu.SMEM)
    (mod, cos_t, sin_t, dsum, af, ab, kdft, kdbt, gf, gb,
     w_q, w_kv, w_g, w_out_b, w_s_b, b_s_full) = pl.pallas_call(
        _prologue_kernel,
        grid=(n_col_blocks,),
        in_specs=[smem, smem, _full((bsz, d)), _full((1, d)),
                  pl.BlockSpec((d, d), lambda i: (0, i)),
                  pl.BlockSpec((1, d), lambda i: (0, i)),
                  _full((1, LANES)),
                  pl.BlockSpec((d, d), lambda i: (0, i)),
                  _full((RET_W + GM_W, d)),
                  _full((GROUPS, CHUNK, CHUNK)), _full((GROUPS, CHUNK))],
        out_specs=[pl.BlockSpec((rows2, d), lambda i: (0, i)),
                   _full((seq, LANES)), _full((seq, LANES)),
                   _full((HEADS, CHUNK, CHUNK)),
                   _full((CHUNK, QK_W)), _full((CHUNK, QK_W)),
                   _full((QK_W, CHUNK)), _full((QK_W, CHUNK)),
                   _full((QK_W, DV)), _full((QK_W, DV)),
                   _full((d, QK_W)), _full((d, n_kv)),
                   pl.BlockSpec((d, d), lambda i: (0, jnp.maximum(i - 1, 0))),
                   _full((RET_W + GM_W, d)),
                   _full((GROUPS, CHUNK, CHUNK)), _full((GROUPS, CHUNK, DG))],
        out_shape=[jax.ShapeDtypeStruct((rows2, d3), F32),
                   jax.ShapeDtypeStruct((seq, LANES), F32),
                   jax.ShapeDtypeStruct((seq, LANES), F32),
                   jax.ShapeDtypeStruct((HEADS, CHUNK, CHUNK), F32),
                   jax.ShapeDtypeStruct((CHUNK, QK_W), F32),
                   jax.ShapeDtypeStruct((CHUNK, QK_W), F32),
                   jax.ShapeDtypeStruct((QK_W, CHUNK), F32),
                   jax.ShapeDtypeStruct((QK_W, CHUNK), F32),
                   jax.ShapeDtypeStruct((QK_W, DV), F32),
                   jax.ShapeDtypeStruct((QK_W, DV), F32),
                   jax.ShapeDtypeStruct((d, QK_W), BF16),
                   jax.ShapeDtypeStruct((d, n_kv), BF16),
                   jax.ShapeDtypeStruct((d, n_gates), BF16),
                   jax.ShapeDtypeStruct((RET_W + GM_W, d), BF16),
                   jax.ShapeDtypeStruct((GROUPS, CHUNK, CHUNK), BF16),
                   jax.ShapeDtypeStruct((GROUPS, CHUNK, DG), F32)],
        compiler_params=pltpu.CompilerParams(dimension_semantics=("arbitrary",)),
        name="prologue",
    )(ret_decay_f[0], ret_decay_b[0], c, c_ctx[None, :], w_ada[0], b_ada[0][None, :], invf_lane,
      w_in[0], w_out[0], w_s[0], b_s[0])

    s_cf, s_cb = pl.pallas_call(
        functools.partial(_ctx_kernel, bsz),
        grid=(bsz + 1,),
        in_specs=[pl.BlockSpec((1, ctx_len, d), lambda s: (jnp.minimum(s, bsz - 1), 0, 0)),
                  _full((rows2, d3)),
                  _full((1, d)), _full((d, n_kv)),
                  _full((QK_W, CHUNK)), _full((QK_W, CHUNK)),
                  _full((QK_W, DV)), _full((QK_W, DV))],
        out_specs=[pl.BlockSpec((1, QK_W, DV), lambda s: (jnp.maximum(s - 1, 0), 0, 0)),
                   pl.BlockSpec((1, QK_W, DV), lambda s: (jnp.maximum(s - 1, 0), 0, 0))],
        out_shape=[jax.ShapeDtypeStruct((bsz, QK_W, DV), F32),
                   jax.ShapeDtypeStruct((bsz, QK_W, DV), F32)],
        scratch_shapes=[pltpu.VMEM((ctx_len, d), BF16),
                        pltpu.VMEM((ctx_len, n_kv), F32),
                        pltpu.VMEM((ctx_len, n_kv), F32)],
        compiler_params=pltpu.CompilerParams(dimension_semantics=("arbitrary",)),
        name="ctx_states",
    )(ctx, mod, nw, w_kv, kdft, kdbt, gf, gb)

    def tile_maps(n_tiles, per_batch, reverse):
        def where(t):
            j = t % per_batch
            return t // per_batch, (per_batch - 1 - j) if reverse else j
        return [lambda s, k=k: where(jnp.clip(s - k, 0, n_tiles - 1)) for k in range(3)]

    nt_kv = seq // TILE_KV
    cpt_kv = TILE_KV // CHUNK
    cur, prv, _ = tile_maps(bsz * nt_kv, nt_kv, reverse=True)
    h, kt, v, st_b = pl.pallas_call(
        functools.partial(_kv_kernel, bsz * nt_kv, nt_kv),
        grid=(bsz * nt_kv + 1,),
        in_specs=[pl.BlockSpec((1, TILE_KV, d), lambda s: (*cur(s), 0)),
                  _full((rows2, d3)),
                  _full((1, d)), _full((d, n_kv)),
                  pl.BlockSpec((TILE_KV, LANES), lambda s: (prv(s)[1], 0)),
                  pl.BlockSpec((TILE_KV, LANES), lambda s: (prv(s)[1], 0)),
                  _full((QK_W, CHUNK)), _full((QK_W, DV)),
                  pl.BlockSpec((1, QK_W, DV), lambda s: (prv(s)[0], 0, 0))],
        out_specs=[pl.BlockSpec((1, TILE_KV, d), lambda s: (*cur(s), 0)),
                   pl.BlockSpec((1, cpt_kv, QK_W, CHUNK), lambda s: (*prv(s), 0, 0)),
                   pl.BlockSpec((1, TILE_KV, RET_W), lambda s: (*prv(s), 0)),
                   pl.BlockSpec((1, cpt_kv, QK_W, DV), lambda s: (*prv(s), 0, 0))],
        out_shape=[jax.ShapeDtypeStruct((bsz, seq, d), BF16),
                   jax.ShapeDtypeStruct((bsz, n_chunks, QK_W, CHUNK), BF16),
                   jax.ShapeDtypeStruct((bsz, seq, RET_W), BF16),
                   jax.ShapeDtypeStruct((bsz, n_chunks, QK_W, DV), BF16)],
        scratch_shapes=[pltpu.VMEM((QK_W, DV), F32),
                        pltpu.VMEM((TILE_KV, n_kv), F32),
                        pltpu.VMEM((TILE_KV, n_kv), F32)],
        compiler_params=pltpu.CompilerParams(dimension_semantics=("arbitrary",)),
        name="kv_sweep",
    )(x, mod, nw, w_kv, cos_t, sin_t, kdbt, gb, s_cb)

    nt = seq // TILE_MAIN
    cpt = TILE_MAIN // CHUNK
    cur, prv, prv2 = tile_maps(bsz * nt, nt, reverse=False)
    out = pl.pallas_call(
        functools.partial(_main_kernel, bsz * nt, nt),
        grid=(bsz * nt + 2,),
        in_specs=[pl.BlockSpec((1, TILE_MAIN, d), lambda s: (*cur(s), 0)),
                  pl.BlockSpec((1, TILE_MAIN, d), lambda s: (*prv2(s), 0)),
                  _full((rows2, d3)),
                  _full((d, QK_W)), _full((d, n_gates)),
                  pl.BlockSpec((TILE_MAIN, LANES), lambda s: (prv(s)[1], 0)),
                  pl.BlockSpec((TILE_MAIN, LANES), lambda s: (prv(s)[1], 0)),
                  pl.BlockSpec((1, cpt, QK_W, CHUNK), lambda s: (*prv(s), 0, 0)),
                  pl.BlockSpec((1, TILE_MAIN, RET_W), lambda s: (*prv(s), 0)),
                  pl.BlockSpec((1, cpt, QK_W, DV), lambda s: (*prv(s), 0, 0)),
                  pl.BlockSpec((1, QK_W, DV), lambda s: (prv(s)[0], 0, 0)),
                  _full((HEADS, CHUNK, CHUNK)),
                  _full((CHUNK, QK_W)), _full((CHUNK, QK_W)),
                  _full((QK_W, CHUNK)), _full((QK_W, DV)),
                  _full((1, RET_W)), _full((1, GM_W)),
                  _full((GROUPS, CHUNK, CHUNK)), _full((GROUPS, CHUNK, DG)),
                  _full((RET_W + GM_W, d)), _full((1, d))],
        out_specs=pl.BlockSpec((1, TILE_MAIN, d), lambda s: (*prv2(s), 0)),
        out_shape=jax.ShapeDtypeStruct((bsz, seq, d), x.dtype),
        scratch_shapes=[pltpu.VMEM((QK_W, DV), F32),
                        pltpu.VMEM((TILE_MAIN, n_rest), F32),
                        pltpu.VMEM((TILE_MAIN, n_rest), F32),
                        pltpu.VMEM((TILE_MAIN, RET_W + GM_W), BF16),
                        pltpu.VMEM((TILE_MAIN, RET_W + GM_W), BF16)],
        compiler_params=pltpu.CompilerParams(dimension_semantics=("arbitrary",)),
        name="main_sweep",
    )(h, x, mod, w_q, w_g, cos_t, sin_t, kt, v, st_b, s_cf, dsum, af, ab, kdft, gf,
      ret_gn_w[0][None, :], gmlp_ln_w[0][None, :], w_s_b, b_s_full, w_out_b,
      final_norm_w[None, :])
    return out
```

```python
import functools

import jax
import jax.numpy as jnp
from jax import lax
from jax.experimental import pallas as pl
from jax.experimental.pallas import tpu as pltpu

F32 = jnp.float32
BF16 = jnp.bfloat16

HEADS = 4
DK = 64
DV = 128
CHUNK = 128
GROUPS = 4
DG = 128
GRID_W = 64
ROPE_BASE = 10000.0
EPS = 1e-6
LANES = 128
PAIRS = HEADS * DK // LANES
RET_W = HEADS * DV
GM_W = GROUPS * DG
QK_W = HEADS * DK

DOT_COLS = 512
TILE_KV = 1024
X_RING = 3
TILE_MAIN = 512


def _rms(x, w):
    ms = jnp.mean(x * x, axis=-1, keepdims=True)
    return x * lax.rsqrt(ms + EPS) * w


def _silu(x):
    return x * jax.nn.sigmoid(x)


def _gelu(x):
    return 0.5 * x * (1.0 + lax.erf(x * (0.5 ** 0.5)))


_dot = functools.partial(jnp.dot, preferred_element_type=F32)


def _col_groups(n):
    first = n % DOT_COLS
    edges = ([0] if first == 0 else [0, first]) + list(range(first + DOT_COLS, n + 1, DOT_COLS))
    return list(zip(edges[:-1], edges[1:]))


def _rope(p, cos, sin_signed, upper16):
    partner = jnp.where(upper16, pltpu.roll(p, 16, 1), pltpu.roll(p, LANES - 16, 1))
    return p * cos + partner * sin_signed


def _kv_pairs(kd, v):
    out = []
    for p in range(PAIRS):
        r = _dot(kd[p * LANES:(p + 1) * LANES, :], v[:, p * 2 * DV:(p + 1) * 2 * DV])
        out.append(r[0:DK, 0:DV])
        out.append(r[DK:2 * DK, DV:2 * DV])
    return jnp.concatenate(out, axis=0)


def _emit_pipelined(big_pieces, side_pieces, tasks):
    in_flight = []
    last_tick = max([start + parts for start, _, parts in tasks] +
                    [len(big_pieces), len(side_pieces)])
    for tick in range(last_tick):
        if tick < len(big_pieces):
            big_pieces[tick]()
        urgent, relaxed = side_pieces[tick] if tick < len(side_pieces) else (None, None)
        if urgent is not None:
            urgent()
        in_flight += [[gen, parts] for start, gen, parts in tasks if start == tick]
        for t in [t for t in in_flight if t[1] > 1] + [t for t in in_flight if t[1] == 1]:
            next(t[0], None)
            t[1] -= 1
        in_flight = [t for t in in_flight if t[1] > 0]
        if relaxed is not None:
            relaxed()
    assert not in_flight


def _run_pipeline_step(step, n_tiles, n_stages, body, refs, bufs_a, bufs_b):
    def call(s_static_parity, stages):
        new, old = (bufs_a, bufs_b) if s_static_parity == 0 else (bufs_b, bufs_a)
        body(stages, *refs, *new, *old)

    edge = n_stages - 1
    for s in list(range(edge)) + list(range(n_tiles, n_tiles + edge)):
        stages = tuple(0 <= s - j < n_tiles for j in range(n_stages))
        pl.when(step == s)(functools.partial(call, s % 2, stages))
    steady = (step >= edge) & (step < n_tiles)
    for parity in range(2):
        pl.when(steady & (lax.rem(step, 2) == parity))(
            functools.partial(call, parity, (True,) * n_stages))


def _prologue_kernel(decf_ref, decb_ref, c_ref, cctx_ref, w_ref, b_ref, invf_ref, win_ref,
                     wout_ref, ws_ref, bs_ref,
                     mod_ref, cos_ref, sin_ref, dsum_ref, af_ref, ab_ref,
                     kdft_ref, kdbt_ref, gf_ref, gb_ref,
                     wq_ref, wkv_ref, wg_ref, woutb_ref, wsb_ref, bsf_ref):
    i = pl.program_id(0)
    c = c_ref[...]
    cond = jnp.concatenate([c, jnp.broadcast_to(cctx_ref[...], c.shape)], axis=0)
    mod_ref[...] = _dot(_silu(cond), w_ref[...]) + b_ref[...]

    @pl.when(i == 0)
    def _first_block():
        wq_ref[...] = win_ref[:, 0:QK_W].astype(BF16)
        wkv_ref[...] = win_ref[:, QK_W:].astype(BF16)
        woutb_ref[...] = wout_ref[...].astype(BF16)
        wsb_ref[...] = ws_ref[...].astype(BF16)
        for g in range(GROUPS):
            bsf_ref[g] = jnp.broadcast_to(bs_ref[g:g + 1, :], (CHUNK, CHUNK)).T

    @pl.when(i > 0)
    def _later_blocks():
        wg_ref[...] = win_ref[...].astype(BF16)

    @pl.when(i == 0)
    def _tables():
        pos = lax.broadcasted_iota(jnp.int32, (GRID_W, LANES), 0).astype(F32)
        lane = lax.broadcasted_iota(jnp.int32, (GRID_W, LANES), 1)
        ang = pos * invf_ref[...]
        cr = jnp.cos(ang)
        sr = jnp.sin(ang)
        sr = jnp.where((lane & 16) == 0, -sr, sr)
        by_row = (lane & 32) == 0
        n_rows = cos_ref.shape[0] // GRID_W
        for r in range(n_rows):
            sl = slice(r * GRID_W, (r + 1) * GRID_W)
            cos_ref[sl, :] = jnp.where(by_row, jnp.broadcast_to(cr[r:r + 1, :], cr.shape), cr)
            sin_ref[sl, :] = jnp.where(by_row, jnp.broadcast_to(sr[r:r + 1, :], sr.shape), sr)

        def log_decay(head_idx, dec_ref):
            d = jnp.full(head_idx.shape, dec_ref[HEADS - 1], F32)
            for h in range(HEADS - 2, -1, -1):
                d = jnp.where(head_idx == h, dec_ref[h], d)
            return -jnp.exp(d)

        n = lax.broadcasted_iota(jnp.int32, (CHUNK, QK_W), 0).astype(F32)
        hl = lax.broadcasted_iota(jnp.int32, (CHUNK, QK_W), 1) // DK
        af_ref[...] = jnp.exp((n + 1.0) * log_decay(hl, decf_ref))
        ab_ref[...] = jnp.exp((CHUNK - n) * log_decay(hl, decb_ref))

        m = lax.broadcasted_iota(jnp.int32, (QK_W, CHUNK), 1).astype(F32)
        hr = lax.broadcasted_iota(jnp.int32, (QK_W, CHUNK), 0) // DK
        lgf = log_decay(hr, decf_ref)
        lgb = log_decay(hr, decb_ref)
        kdft_ref[...] = jnp.exp((CHUNK - 1.0 - m) * lgf)
        kdbt_ref[...] = jnp.exp(m * lgb)
        gf_ref[...] = jnp.exp(float(CHUNK) * lgf)
        gb_ref[...] = jnp.exp(float(CHUNK) * lgb)

        nn = lax.broadcasted_iota(jnp.int32, (CHUNK, CHUNK), 0).astype(F32)
        mm = lax.broadcasted_iota(jnp.int32, (CHUNK, CHUNK), 1).astype(F32)
        for h in range(HEADS):
            lf = -jnp.exp(jnp.full((CHUNK, CHUNK), decf_ref[h], F32))
            lb = -jnp.exp(jnp.full((CHUNK, CHUNK), decb_ref[h], F32))
            df = nn - mm
            db = mm - nn
            fwd = jnp.where(df >= 0, jnp.exp(jnp.maximum(df, 0.0) * lf), 0.0)
            bwd = jnp.where(db >= 0, jnp.exp(jnp.maximum(db, 0.0) * lb), 0.0)
            dsum_ref[h] = fwd + bwd


def _ctx_kernel(n_tiles, ctx_ref, mod_ref, nw_ref, wkv_ref, kdft_ref, kdbt_ref, gf_ref, gb_ref,
                scf_ref, scb_ref, h_ref, pa_ref, pb_ref):
    refs = (ctx_ref, mod_ref, nw_ref, wkv_ref, kdft_ref, kdbt_ref, gf_ref, gb_ref,
            scf_ref, scb_ref, h_ref)
    _run_pipeline_step(pl.program_id(0), n_tiles, 2, _ctx_step, refs, (pa_ref,), (pb_ref,))


def _ctx_step(stages, ctx_ref, mod_ref, nw_ref, wkv_ref, kdft_ref, kdbt_ref, gf_ref, gb_ref,
              scf_ref, scb_ref, h_ref, p_new_ref, p_old):
    project_stage, state_stage = stages
    d = ctx_ref.shape[-1]
    n_chunks = ctx_ref.shape[1] // CHUNK
    ctx_row = mod_ref.shape[0] // 2
    mod = mod_ref[ctx_row:ctx_row + 1, :]

    def norm_rows(rows):
        h = _rms(ctx_ref[0, rows, :], nw_ref[...]) * (1.0 + mod[:, d:2 * d]) + mod[:, 0:d]
        h_ref[rows, :] = h.astype(BF16)

    def project(lo, hi):
        p_new_ref[:, lo:hi] = _dot(h_ref[...], wkv_ref[:, lo:hi])

    new_f, new_b = {}, {}

    def chunk(c):
        sl = slice(c * CHUNK, (c + 1) * CHUNK)
        kt = (p_old[sl, 0:QK_W] * (DK ** -0.5)).T
        v = p_old[sl, QK_W:].astype(BF16)
        new_f[c] = _kv_pairs((kt * kdft_ref[...]).astype(BF16), v)
        new_b[c] = _kv_pairs((kt * kdbt_ref[...]).astype(BF16), v)
        yield

    def combine():
        yield
        sf = jnp.zeros((QK_W, DV), F32)
        for c in range(n_chunks):
            sf = gf_ref[...] * sf + new_f[c]
        sb = jnp.zeros((QK_W, DV), F32)
        for c in range(n_chunks - 1, -1, -1):
            sb = gb_ref[...] * sb + new_b[c]
        scf_ref[0] = sf
        scb_ref[0] = sb

    big, tasks = [], []
    if project_stage:
        for c in range(n_chunks):
            norm_rows(slice(c * CHUNK, (c + 1) * CHUNK))
        big = [functools.partial(project, lo, hi) for lo, hi in _col_groups(p_new_ref.shape[1])]
    if state_stage:
        tasks = [(c, chunk(c), 1) for c in range(n_chunks)] + [(n_chunks - 1, combine(), 2)]
    _emit_pipelined(big, [], tasks)


def _kv_kernel(n_tiles, tiles_per_batch, x_hbm, modc_ref, nw_ref, wkv_ref, cos_ref, sin_ref,
               kdbt_ref, gb_ref, scb_ref, h_ref, kt_ref, v_ref, stb_ref, sb_ref, pa_ref, pb_ref,
               xbuf_ref, xsem_ref):
    s = pl.program_id(0)
    tile = xbuf_ref.shape[1]

    def x_copy(t):
        b = t // tiles_per_batch
        j = tiles_per_batch - 1 - lax.rem(t, tiles_per_batch)
        slot = lax.rem(t, X_RING)
        return pltpu.make_async_copy(x_hbm.at[b, pl.ds(j * tile, tile), :],
                                     xbuf_ref.at[slot], xsem_ref.at[slot])

    @pl.when(s == 0)
    def _prime():
        for t in range(min(X_RING - 1, n_tiles)):
            x_copy(t).start()

    @pl.when(s < n_tiles)
    def _arrived():
        x_copy(s).wait()

    @pl.when(s + X_RING - 1 < n_tiles)
    def _prefetch():
        x_copy(s + X_RING - 1).start()

    @pl.when((s >= 1) & (lax.rem(s - 1, tiles_per_batch) == 0))
    def _init():
        sb_ref[...] = scb_ref[0]

    batch = jnp.minimum(s, n_tiles - 1) // tiles_per_batch
    modc = modc_ref[pl.ds(batch, 1), :]
    xc_ref = xbuf_ref.at[lax.rem(s, X_RING)]
    refs = (xc_ref, modc, nw_ref, wkv_ref, cos_ref, sin_ref, kdbt_ref, gb_ref,
            h_ref, kt_ref, v_ref, stb_ref, sb_ref)
    _run_pipeline_step(s, n_tiles, 2, _kv_step, refs, (pa_ref,), (pb_ref,))


def _kv_step(stages, xc_ref, modc, nw_ref, wkv_ref, cos_ref, sin_ref, kdbt_ref, gb_ref,
             h_ref, kt_ref, v_ref, stb_ref, sb_ref, p_new_ref, p_old):
    project_stage, finish_stage = stages
    d = xc_ref.shape[-1]
    tile = xc_ref.shape[0]
    half = tile // 2
    row_halves = [slice(0, half), slice(half, tile)]
    quarter = [slice(i * half // 2, (i + 1) * half // 2) for i in range(4)]
    upper16 = (lax.broadcasted_iota(jnp.int32, (CHUNK, LANES), 1) & 16) != 0

    def norm_rows(rows):
        h = _rms(xc_ref[rows, :], nw_ref[...]) * (1.0 + modc[:, d:2 * d]) + modc[:, 0:d]
        h_ref[0, rows, :] = h.astype(BF16)

    def project(rows, lo, hi):
        p_new_ref[rows, lo:hi] = _dot(h_ref[0, rows, :], wkv_ref[:, lo:hi])

    def chunk(c):
        sl = slice(c * CHUNK, (c + 1) * CHUNK)
        k = [_rope(p_old[sl, p * LANES:(p + 1) * LANES], cos_ref[sl, :], sin_ref[sl, :], upper16)
             * (DK ** -0.5) for p in range(PAIRS)]
        v = p_old[sl, QK_W:].astype(BF16)
        v_ref[0, sl, :] = v
        kt = jnp.concatenate([kp.T for kp in k], axis=0)
        kt_ref[0, c] = kt.astype(BF16)
        new = _kv_pairs((kt * kdbt_ref[...]).astype(BF16), v)
        yield
        stb_ref[0, c] = sb_ref[...].astype(BF16)
        sb_ref[...] = gb_ref[...] * sb_ref[...] + new

    projections, late_norms, tasks = [], [], []
    if project_stage:
        norm_rows(quarter[0])
        norm_rows(quarter[1])
        projections = [functools.partial(project, row_halves[r], lo, hi)
                       for r in range(2) for lo, hi in _col_groups(p_new_ref.shape[1])]
        late_norms = [(functools.partial(norm_rows, q), None) for q in quarter[2:]]
    if finish_stage:
        tasks = [(tick, chunk(c), 2) for tick, c in enumerate(range(tile // CHUNK - 1, -1, -1))]
    _emit_pipelined(projections, late_norms, tasks)


def _main_kernel(n_tiles, tiles_per_batch, h_ref, x_ref, mod_ref, wq_ref, wg_ref, cos_ref,
                 sin_ref, kt_ref, v_ref, stb_ref, scf_ref, dsum_ref, af_ref, ab_ref, kdft_ref,
                 gf_ref, gnw_ref, lnw_ref, ws_ref, bs_ref, wout_ref, fnw_ref, o_ref,
                 sf_ref, pa_ref, pb_ref, ya_ref, yb_ref):
    s = pl.program_id(0)

    @pl.when((s >= 1) & (lax.rem(s - 1, tiles_per_batch) == 0))
    def _init():
        sf_ref[...] = scf_ref[0]

    batch = jnp.clip(s - 2, 0, n_tiles - 1) // tiles_per_batch
    mod = mod_ref[pl.ds(batch, 1), :]
    refs = (h_ref, x_ref, mod, wq_ref, wg_ref, cos_ref, sin_ref, kt_ref, v_ref,
            stb_ref, dsum_ref, af_ref, ab_ref, kdft_ref, gf_ref, gnw_ref, lnw_ref, ws_ref,
            bs_ref, wout_ref, fnw_ref, o_ref, sf_ref)
    _run_pipeline_step(s, n_tiles, 3, _main_step, refs, (pa_ref, ya_ref), (pb_ref, yb_ref))


def _main_step(stages, h_ref, x_ref, mod, wq_ref, wg_ref, cos_ref, sin_ref, kt_ref, v_ref,
               stb_ref, dsum_ref, af_ref, ab_ref, kdft_ref, gf_ref, gnw_ref, lnw_ref, ws_ref,
               bs_ref, wout_ref, fnw_ref, o_ref, sf_ref, p_new_ref, y_ref, p_all, y_old):
    project_stage, mix_stage, out_stage = stages
    d = x_ref.shape[-1]
    tile = x_ref.shape[1]
    n_chunks = tile // CHUNK
    half = tile // 2
    row_halves = [slice(0, half), slice(half, tile)]
    c_gr = QK_W
    c_u = c_gr + RET_W
    c_vm = c_u + GM_W
    c_gm = c_vm + GM_W
    lane = lax.broadcasted_iota(jnp.int32, (CHUNK, LANES), 1)
    upper16 = (lane & 16) != 0
    head_of_lane = lane // DK

    operand = {}

    def loaded_once(key, load):
        if key not in operand:
            operand[key] = load()
        return operand[key]

    def project(r, lo, hi):
        rows = row_halves[r]
        lhs = loaded_once(("h", r), lambda: h_ref[0, rows, :])
        w = wq_ref[...] if hi <= QK_W else wg_ref[:, lo - QK_W:hi - QK_W]
        p_new_ref[rows, lo:hi] = _dot(lhs, w)

    def retention(c, p):
        sl = slice(c * CHUNK, (c + 1) * CHUNK)
        ps = slice(p * LANES, (p + 1) * LANES)
        q2 = _rope(p_all[sl, ps], cos_ref[sl, :], sin_ref[sl, :], upper16)
        kt = kt_ref[0, c, ps, :]
        vp = v_ref[0, sl, p * 2 * DV:(p + 1) * 2 * DV]
        qa = q2 * af_ref[:, ps]
        qb = q2 * ab_ref[:, ps]
        scores, qx = [], []
        for hh in range(2):
            mine = head_of_lane == hh
            scores.append(_dot(jnp.where(mine, q2, 0.0).astype(BF16), kt))
            qx.append(jnp.concatenate([jnp.where(mine, qa, 0.0), jnp.where(mine, qb, 0.0)],
                                      axis=1).astype(BF16))
        kd = (kt.astype(F32) * kdft_ref[ps, :]).astype(BF16)
        yield
        state = jnp.concatenate([sf_ref[ps, :].astype(BF16), stb_ref[0, c, ps, :]], axis=0)
        o = []
        for hh in range(2):
            a = (scores[hh] * dsum_ref[2 * p + hh]).astype(BF16)
            o.append(_dot(a, vp[:, hh * DV:(hh + 1) * DV]) + _dot(qx[hh], state))
        r = _dot(kd, vp)
        yield
        new = jnp.concatenate([r[0:DK, 0:DV], r[DK:2 * DK, DV:2 * DV]], axis=0)
        sf_ref[ps, :] = gf_ref[ps, :] * sf_ref[ps, :] + new
        for hh in range(2):
            head = 2 * p + hh
            mu = jnp.mean(o[hh], axis=-1, keepdims=True)
            var = jnp.mean(jnp.square(o[hh] - mu), axis=-1, keepdims=True)
            hs = slice(head * DV, (head + 1) * DV)
            y = (o[hh] - mu) * lax.rsqrt(var + EPS) * gnw_ref[:, hs]
            gate_cols = slice(c_gr + head * DV, c_gr + (head + 1) * DV)
            y_ref[sl, hs] = (y * _silu(p_all[sl, gate_cols])).astype(BF16)

    def gating(c):
        sl = slice(c * CHUNK, (c + 1) * CHUNK)
        vg = _gelu(p_all[sl, c_vm:c_gm])
        mu = jnp.mean(vg, axis=-1, keepdims=True)
        var = jnp.mean(jnp.square(vg - mu), axis=-1, keepdims=True)
        yield
        vn = ((vg - mu) * lax.rsqrt(var + EPS) * lnw_ref[...]).astype(BF16)
        s = [_dot(ws_ref[g], vn[:, g * DG:(g + 1) * DG]) for g in range(GROUPS)]
        yield
        for g in range(GROUPS):
            if g == GROUPS // 2:
                yield
            u = _gelu(p_all[sl, c_u + g * DG:c_u + (g + 1) * DG])
            gm = _silu(p_all[sl, c_gm + g * DG:c_gm + (g + 1) * DG])
            y_ref[sl, RET_W + g * DG:RET_W + (g + 1) * DG] = (u * (s[g] + bs_ref[g]) * gm).astype(BF16)

    out_groups = _col_groups(d)
    xn = {}

    def out_project(r, g):
        rows = row_halves[r]
        lo, hi = out_groups[g]
        lhs = loaded_once(("y", r), lambda: y_old[rows, :])
        out = _dot(lhs, wout_ref[:, lo:hi])
        xn[r, g] = x_ref[0, rows, lo:hi] + mod[:, 2 * d + lo:2 * d + hi] * out

    def finish(r, part):
        sub = slice(part * half // 2, (part + 1) * half // 2)
        rows = slice(row_halves[r].start + sub.start, row_halves[r].start + sub.stop)
        parts = [xn[r, g][sub, :] for g in range(len(out_groups))]
        ms = sum(jnp.sum(t * t, axis=-1, keepdims=True) for t in parts) * (1.0 / d)
        rs = lax.rsqrt(ms + EPS)
        for (lo, hi), t in zip(out_groups, parts):
            o_ref[0, rows, lo:hi] = t * rs * fnw_ref[:, lo:hi]

    tasks = []
    if mix_stage:
        for c in range(n_chunks):
            base = 3 * c
            tasks += [(base + min(p, 1), retention(c, p), 3) for p in range(PAIRS)]
            tasks.append((base + 1, gating(c), 4))
    big, relaxed = [], {}

    def place(tick, piece):
        while tick in relaxed:
            tick += 1
        relaxed[tick] = piece

    for r in range(2):
        outs = [functools.partial(out_project, r, g) for g in range(len(out_groups))] if out_stage else []
        projs = ([functools.partial(project, r, lo, hi) for lo, hi in _col_groups(p_new_ref.shape[1])]
                 if project_stage else [])
        for i in range(max(len(outs), len(projs))):
            big += outs[i:i + 1] + projs[i:i + 1]
            if i == len(outs) - 1:
                place(len(big) + 1, functools.partial(finish, r, 0))
                place(len(big) + 3, functools.partial(finish, r, 1))
    side = [(None, relaxed.get(tick)) for tick in range(max(relaxed, default=-1) + 1)]
    _emit_pipelined(big, side, tasks)


def _full(shape):
    return pl.BlockSpec(shape, lambda *_: (0,) * len(shape))


def kernel(x, c, ctx, c_ctx, w_ada, b_ada, norm_w, w_in, ret_decay_f, ret_decay_b, ret_gn_w,
           gmlp_ln_w, w_s, b_s, w_out, final_norm_w):
    bsz, seq, d = x.shape
    ctx_len = ctx.shape[1]
    depth = w_ada.shape[0]
    assert depth == 1 and d % LANES == 0
    assert seq % TILE_KV == 0 and seq % TILE_MAIN == 0 and ctx_len % CHUNK == 0
    assert seq % GRID_W == 0 and TILE_MAIN % (4 * CHUNK) == 0 and TILE_KV % (4 * CHUNK) == 0
    n_chunks = seq // CHUNK
    d3 = 3 * d
    n_kv = QK_W + RET_W
    n_rest = w_in.shape[2] - n_kv
    n_gates = n_rest - QK_W
    assert QK_W + n_kv == d and n_gates % d == 0 and d3 // d == 1 + n_gates // d
    assert _col_groups(n_rest)[0] == (0, QK_W)

    quarter = DK // 4
    inv_freq = ROPE_BASE ** (-jnp.arange(quarter, dtype=F32) / quarter)
    invf_lane = jnp.tile(inv_freq, LANES // quarter)[None, :]
    nw = norm_w[0][None, :]
    rows2 = 2 * bsz

    n_col_blocks = d3 // d
    smem = pl.BlockSpec(memory_space=pltpu.SMEM)
    (mod, cos_t, sin_t, dsum, af, ab, kdft, kdbt, gf, gb,
     w_q, w_kv, w_g, w_out_b, w_s_b, b_s_full) = pl.pallas_call(
        _prologue_kernel,
        grid=(n_col_blocks,),
        in_specs=[smem, smem, _full((bsz, d)), _full((1, d)),
                  pl.BlockSpec((d, d), lambda i: (0, i)),
                  pl.BlockSpec((1, d), lambda i: (0, i)),
                  _full((1, LANES)),
                  pl.BlockSpec((d, d), lambda i: (0, i)),
                  _full((RET_W + GM_W, d)),
                  _full((GROUPS, CHUNK, CHUNK)), _full((GROUPS, CHUNK))],
        out_specs=[pl.BlockSpec((rows2, d), lambda i: (0, i)),
                   _full((seq, LANES)), _full((seq, LANES)),
                   _full((HEADS, CHUNK, CHUNK)),
                   _full((CHUNK, QK_W)), _full((CHUNK, QK_W)),
                   _full((QK_W, CHUNK)), _full((QK_W, CHUNK)),
                   _full((QK_W, DV)), _full((QK_W, DV)),
                   _full((d, QK_W)), _full((d, n_kv)),
                   pl.BlockSpec((d, d), lambda i: (0, jnp.maximum(i - 1, 0))),
                   _full((RET_W + GM_W, d)),
                   _full((GROUPS, CHUNK, CHUNK)), _full((GROUPS, CHUNK, DG))],
        out_shape=[jax.ShapeDtypeStruct((rows2, d3), F32),
                   jax.ShapeDtypeStruct((seq, LANES), F32),
                   jax.ShapeDtypeStruct((seq, LANES), F32),
                   jax.ShapeDtypeStruct((HEADS, CHUNK, CHUNK), F32),
                   jax.ShapeDtypeStruct((CHUNK, QK_W), F32),
                   jax.ShapeDtypeStruct((CHUNK, QK_W), F32),
                   jax.ShapeDtypeStruct((QK_W, CHUNK), F32),
                   jax.ShapeDtypeStruct((QK_W, CHUNK), F32),
                   jax.ShapeDtypeStruct((QK_W, DV), F32),
                   jax.ShapeDtypeStruct((QK_W, DV), F32),
                   jax.ShapeDtypeStruct((d, QK_W), BF16),
                   jax.ShapeDtypeStruct((d, n_kv), BF16),
                   jax.ShapeDtypeStruct((d, n_gates), BF16),
                   jax.ShapeDtypeStruct((RET_W + GM_W, d), BF16),
                   jax.ShapeDtypeStruct((GROUPS, CHUNK, CHUNK), BF16),
                   jax.ShapeDtypeStruct((GROUPS, CHUNK, DG), F32)],
        compiler_params=pltpu.CompilerParams(dimension_semantics=("arbitrary",)),
        name="prologue",
    )(ret_decay_f[0], ret_decay_b[0], c, c_ctx[None, :], w_ada[0], b_ada[0][None, :], invf_lane,
      w_in[0], w_out[0], w_s[0], b_s[0])

    s_cf, s_cb = pl.pallas_call(
        functools.partial(_ctx_kernel, bsz),
        grid=(bsz + 1,),
        in_specs=[pl.BlockSpec((1, ctx_len, d), lambda s: (jnp.minimum(s, bsz - 1), 0, 0)),
                  _full((rows2, d3)),
                  _full((1, d)), _full((d, n_kv)),
                  _full((QK_W, CHUNK)), _full((QK_W, CHUNK)),
                  _full((QK_W, DV)), _full((QK_W, DV))],
        out_specs=[pl.BlockSpec((1, QK_W, DV), lambda s: (jnp.maximum(s - 1, 0), 0, 0)),
                   pl.BlockSpec((1, QK_W, DV), lambda s: (jnp.maximum(s - 1, 0), 0, 0))],
        out_shape=[jax.ShapeDtypeStruct((bsz, QK_W, DV), F32),
                   jax.ShapeDtypeStruct((bsz, QK_W, DV), F32)],
        scratch_shapes=[pltpu.VMEM((ctx_len, d), BF16),
                        pltpu.VMEM((ctx_len, n_kv), F32),
                        pltpu.VMEM((ctx_len, n_kv), F32)],
        compiler_params=pltpu.CompilerParams(dimension_semantics=("arbitrary",)),
        name="ctx_states",
    )(ctx, mod, nw, w_kv, kdft, kdbt, gf, gb)

    def tile_maps(n_tiles, per_batch, reverse):
        def where(t):
            j = t % per_batch
            return t // per_batch, (per_batch - 1 - j) if reverse else j
        return [lambda s, k=k: where(jnp.clip(s - k, 0, n_tiles - 1)) for k in range(3)]

    nt_kv = seq // TILE_KV
    cpt_kv = TILE_KV // CHUNK
    cur, prv, _ = tile_maps(bsz * nt_kv, nt_kv, reverse=True)
    h, kt, v, st_b = pl.pallas_call(
        functools.partial(_kv_kernel, bsz * nt_kv, nt_kv),
        grid=(bsz * nt_kv + 1,),
        in_specs=[pl.BlockSpec(memory_space=pl.ANY),
                  _full((rows2, d3)),
                  _full((1, d)), _full((d, n_kv)),
                  pl.BlockSpec((TILE_KV, LANES), lambda s: (prv(s)[1], 0)),
                  pl.BlockSpec((TILE_KV, LANES), lambda s: (prv(s)[1], 0)),
                  _full((QK_W, CHUNK)), _full((QK_W, DV)),
                  pl.BlockSpec((1, QK_W, DV), lambda s: (prv(s)[0], 0, 0))],
        out_specs=[pl.BlockSpec((1, TILE_KV, d), lambda s: (*cur(s), 0)),
                   pl.BlockSpec((1, cpt_kv, QK_W, CHUNK), lambda s: (*prv(s), 0, 0)),
                   pl.BlockSpec((1, TILE_KV, RET_W), lambda s: (*prv(s), 0)),
                   pl.BlockSpec((1, cpt_kv, QK_W, DV), lambda s: (*prv(s), 0, 0))],
        out_shape=[jax.ShapeDtypeStruct((bsz, seq, d), BF16),
                   jax.ShapeDtypeStruct((bsz, n_chunks, QK_W, CHUNK), BF16),
                   jax.ShapeDtypeStruct((bsz, seq, RET_W), BF16),
                   jax.ShapeDtypeStruct((bsz, n_chunks, QK_W, DV), BF16)],
        scratch_shapes=[pltpu.VMEM((QK_W, DV), F32),
                        pltpu.VMEM((TILE_KV, n_kv), F32),
                        pltpu.VMEM((TILE_KV, n_kv), F32),
                        pltpu.VMEM((X_RING, TILE_KV, d), F32),
                        pltpu.SemaphoreType.DMA((X_RING,))],
        compiler_params=pltpu.CompilerParams(dimension_semantics=("arbitrary",)),
        name="kv_sweep",
    )(x, mod, nw, w_kv, cos_t, sin_t, kdbt, gb, s_cb)

    nt = seq // TILE_MAIN
    cpt = TILE_MAIN // CHUNK
    cur, prv, prv2 = tile_maps(bsz * nt, nt, reverse=False)
    out = pl.pallas_call(
        functools.partial(_main_kernel, bsz * nt, nt),
        grid=(bsz * nt + 2,),
        in_specs=[pl.BlockSpec((1, TILE_MAIN, d), lambda s: (*cur(s), 0)),
                  pl.BlockSpec((1, TILE_MAIN, d), lambda s: (*prv2(s), 0)),
                  _full((rows2, d3)),
                  _full((d, QK_W)), _full((d, n_gates)),
                  pl.BlockSpec((TILE_MAIN, LANES), lambda s: (prv(s)[1], 0)),
                  pl.BlockSpec((TILE_MAIN, LANES), lambda s: (prv(s)[1], 0)),
                  pl.BlockSpec((1, cpt, QK_W, CHUNK), lambda s: (*prv(s), 0, 0)),
                  pl.BlockSpec((1, TILE_MAIN, RET_W), lambda s: (*prv(s), 0)),
                  pl.BlockSpec((1, cpt, QK_W, DV), lambda s: (*prv(s), 0, 0)),
                  pl.BlockSpec((1, QK_W, DV), lambda s: (prv(s)[0], 0, 0)),
                  _full((HEADS, CHUNK, CHUNK)),
                  _full((CHUNK, QK_W)), _full((CHUNK, QK_W)),
                  _full((QK_W, CHUNK)), _full((QK_W, DV)),
                  _full((1, RET_W)), _full((1, GM_W)),
                  _full((GROUPS, CHUNK, CHUNK)), _full((GROUPS, CHUNK, DG)),
                  _full((RET_W + GM_W, d)), _full((1, d))],
        out_specs=pl.BlockSpec((1, TILE_MAIN, d), lambda s: (*prv2(s), 0)),
        out_shape=jax.ShapeDtypeStruct((bsz, seq, d), x.dtype),
        scratch_shapes=[pltpu.VMEM((QK_W, DV), F32),
                        pltpu.VMEM((TILE_MAIN, n_rest), F32),
                        pltpu.VMEM((TILE_MAIN, n_rest), F32),
                        pltpu.VMEM((TILE_MAIN, RET_W + GM_W), BF16),
                        pltpu.VMEM((TILE_MAIN, RET_W + GM_W), BF16)],
        compiler_params=pltpu.CompilerParams(dimension_semantics=("arbitrary",)),
        name="main_sweep",
    )(h, x, mod, w_q, w_g, cos_t, sin_t, kt, v, st_b, s_cf, dsum, af, ab, kdft, gf,
      ret_gn_w[0][None, :], gmlp_ln_w[0][None, :], w_s_b, b_s_full, w_out_b,
      final_norm_w[None, :])
    return out
```

```python
import functools

import jax
import jax.numpy as jnp
from jax import lax
from jax.experimental import pallas as pl
from jax.experimental.pallas import tpu as pltpu

F32 = jnp.float32
BF16 = jnp.bfloat16

HEADS = 4
DK = 64
DV = 128
CHUNK = 128
GROUPS = 4
DG = 128
GRID_W = 64
ROPE_BASE = 10000.0
EPS = 1e-6
LANES = 128
PAIRS = HEADS * DK // LANES
RET_W = HEADS * DV
GM_W = GROUPS * DG
QK_W = HEADS * DK

DOT_COLS = 512
TILE_KV = 1024
X_RING = 3
TILE_MAIN = 512


def _rms(x, w):
    ms = jnp.mean(x * x, axis=-1, keepdims=True)
    return x * lax.rsqrt(ms + EPS) * w


def _silu(x):
    return x * jax.nn.sigmoid(x)


def _gelu(x):
    return 0.5 * x * (1.0 + lax.erf(x * (0.5 ** 0.5)))


_dot = functools.partial(jnp.dot, preferred_element_type=F32)


def _col_groups(n):
    first = n % DOT_COLS
    edges = ([0] if first == 0 else [0, first]) + list(range(first + DOT_COLS, n + 1, DOT_COLS))
    return list(zip(edges[:-1], edges[1:]))


def _rope(p, cos, sin_signed, upper16):
    partner = jnp.where(upper16, pltpu.roll(p, 16, 1), pltpu.roll(p, LANES - 16, 1))
    return p * cos + partner * sin_signed


def _kv_pairs(kd, v):
    out = []
    for p in range(PAIRS):
        r = _dot(kd[p * LANES:(p + 1) * LANES, :], v[:, p * 2 * DV:(p + 1) * 2 * DV])
        out.append(r[0:DK, 0:DV])
        out.append(r[DK:2 * DK, DV:2 * DV])
    return jnp.concatenate(out, axis=0)


def _emit_pipelined(big_pieces, side_pieces, tasks):
    in_flight = []
    last_tick = max([start + parts for start, _, parts in tasks] +
                    [len(big_pieces), len(side_pieces)])
    for tick in range(last_tick):
        if tick < len(big_pieces):
            big_pieces[tick]()
        urgent, relaxed = side_pieces[tick] if tick < len(side_pieces) else (None, None)
        if urgent is not None:
            urgent()
        in_flight += [[gen, parts] for start, gen, parts in tasks if start == tick]
        for t in [t for t in in_flight if t[1] > 1] + [t for t in in_flight if t[1] == 1]:
            next(t[0], None)
            t[1] -= 1
        in_flight = [t for t in in_flight if t[1] > 0]
        if relaxed is not None:
            relaxed()
    assert not in_flight


def _run_pipeline_step(step, n_tiles, n_stages, body, refs, bufs_a, bufs_b):
    def call(s_static_parity, stages):
        new, old = (bufs_a, bufs_b) if s_static_parity == 0 else (bufs_b, bufs_a)
        body(stages, *refs, *new, *old)

    edge = n_stages - 1
    for s in list(range(edge)) + list(range(n_tiles, n_tiles + edge)):
        stages = tuple(0 <= s - j < n_tiles for j in range(n_stages))
        pl.when(step == s)(functools.partial(call, s % 2, stages))
    steady = (step >= edge) & (step < n_tiles)
    for parity in range(2):
        pl.when(steady & (lax.rem(step, 2) == parity))(
            functools.partial(call, parity, (True,) * n_stages))


def _prologue_kernel(decf_ref, decb_ref, c_ref, cctx_ref, w_ref, b_ref, invf_ref, win_ref,
                     wout_ref, ws_ref, bs_ref,
                     mod_ref, cos_ref, sin_ref, dsum_ref, af_ref, ab_ref,
                     kdft_ref, kdbt_ref, gf_ref, gb_ref,
                     wrest_ref, wkv_ref, woutb_ref, wsb_ref, bsf_ref):
    i = pl.program_id(0)
    d = win_ref.shape[0]
    c = c_ref[...]
    cond = jnp.concatenate([c, jnp.broadcast_to(cctx_ref[...], c.shape)], axis=0)
    mod_ref[...] = _dot(_silu(cond), w_ref[...]) + b_ref[...]

    @pl.when(i == 0)
    def _first_block():
        wrest_ref[:, 0:QK_W] = win_ref[:, 0:QK_W].astype(BF16)
        wkv_ref[...] = win_ref[:, QK_W:].astype(BF16)
        woutb_ref[:, 0:d] = wout_ref[...].astype(BF16)
        woutb_ref[:, d:] = jnp.zeros((woutb_ref.shape[0], woutb_ref.shape[1] - d), BF16)
        wsb_ref[...] = ws_ref[...].astype(BF16)
        for g in range(GROUPS):
            bsf_ref[g] = jnp.broadcast_to(bs_ref[g:g + 1, :], (CHUNK, CHUNK)).T

    for blk in range(1, (wrest_ref.shape[1] - QK_W) // d + 1):
        @pl.when(i == blk)
        def _later_block(blk=blk):
            wrest_ref[:, QK_W + (blk - 1) * d:QK_W + blk * d] = win_ref[...].astype(BF16)

    @pl.when(i == 0)
    def _tables():
        pos = lax.broadcasted_iota(jnp.int32, (GRID_W, LANES), 0).astype(F32)
        lane = lax.broadcasted_iota(jnp.int32, (GRID_W, LANES), 1)
        ang = pos * invf_ref[...]
        cr = jnp.cos(ang)
        sr = jnp.sin(ang)
        sr = jnp.where((lane & 16) == 0, -sr, sr)
        by_row = (lane & 32) == 0
        n_rows = cos_ref.shape[0] // GRID_W
        for r in range(n_rows):
            sl = slice(r * GRID_W, (r + 1) * GRID_W)
            cos_ref[sl, :] = jnp.where(by_row, jnp.broadcast_to(cr[r:r + 1, :], cr.shape), cr)
            sin_ref[sl, :] = jnp.where(by_row, jnp.broadcast_to(sr[r:r + 1, :], sr.shape), sr)

        def log_decay(head_idx, dec_ref):
            d = jnp.full(head_idx.shape, dec_ref[HEADS - 1], F32)
            for h in range(HEADS - 2, -1, -1):
                d = jnp.where(head_idx == h, dec_ref[h], d)
            return -jnp.exp(d)

        n = lax.broadcasted_iota(jnp.int32, (CHUNK, QK_W), 0).astype(F32)
        hl = lax.broadcasted_iota(jnp.int32, (CHUNK, QK_W), 1) // DK
        af_ref[...] = jnp.exp((n + 1.0) * log_decay(hl, decf_ref))
        ab_ref[...] = jnp.exp((CHUNK - n) * log_decay(hl, decb_ref))

        m = lax.broadcasted_iota(jnp.int32, (QK_W, CHUNK), 1).astype(F32)
        hr = lax.broadcasted_iota(jnp.int32, (QK_W, CHUNK), 0) // DK
        lgf = log_decay(hr, decf_ref)
        lgb = log_decay(hr, decb_ref)
        kdft_ref[...] = jnp.exp((CHUNK - 1.0 - m) * lgf)
        kdbt_ref[...] = jnp.exp(m * lgb)
        gf_ref[...] = jnp.exp(float(CHUNK) * lgf)
        gb_ref[...] = jnp.exp(float(CHUNK) * lgb)

        nn = lax.broadcasted_iota(jnp.int32, (CHUNK, CHUNK), 0).astype(F32)
        mm = lax.broadcasted_iota(jnp.int32, (CHUNK, CHUNK), 1).astype(F32)
        for h in range(HEADS):
            lf = -jnp.exp(jnp.full((CHUNK, CHUNK), decf_ref[h], F32))
            lb = -jnp.exp(jnp.full((CHUNK, CHUNK), decb_ref[h], F32))
            df = nn - mm
            db = mm - nn
            fwd = jnp.where(df >= 0, jnp.exp(jnp.maximum(df, 0.0) * lf), 0.0)
            bwd = jnp.where(db >= 0, jnp.exp(jnp.maximum(db, 0.0) * lb), 0.0)
            dsum_ref[h] = fwd + bwd


def _ctx_kernel(n_tiles, ctx_ref, mod_ref, nw_ref, wkv_ref, kdft_ref, kdbt_ref, gf_ref, gb_ref,
                scf_ref, scb_ref, h_ref, pa_ref, pb_ref):
    refs = (ctx_ref, mod_ref, nw_ref, wkv_ref, kdft_ref, kdbt_ref, gf_ref, gb_ref,
            scf_ref, scb_ref, h_ref)
    _run_pipeline_step(pl.program_id(0), n_tiles, 2, _ctx_step, refs, (pa_ref,), (pb_ref,))


def _ctx_step(stages, ctx_ref, mod_ref, nw_ref, wkv_ref, kdft_ref, kdbt_ref, gf_ref, gb_ref,
              scf_ref, scb_ref, h_ref, p_new_ref, p_old):
    project_stage, state_stage = stages
    d = ctx_ref.shape[-1]
    n_chunks = ctx_ref.shape[1] // CHUNK
    ctx_row = mod_ref.shape[0] // 2
    mod = mod_ref[ctx_row:ctx_row + 1, :]

    def norm_rows(rows):
        h = _rms(ctx_ref[0, rows, :], nw_ref[...]) * (1.0 + mod[:, d:2 * d]) + mod[:, 0:d]
        h_ref[rows, :] = h.astype(BF16)

    def project(lo, hi):
        p_new_ref[:, lo:hi] = _dot(h_ref[...], wkv_ref[:, lo:hi])

    new_f, new_b = {}, {}

    def chunk(c):
        sl = slice(c * CHUNK, (c + 1) * CHUNK)
        kt = (p_old[sl, 0:QK_W] * (DK ** -0.5)).T
        v = p_old[sl, QK_W:].astype(BF16)
        new_f[c] = _kv_pairs((kt * kdft_ref[...]).astype(BF16), v)
        new_b[c] = _kv_pairs((kt * kdbt_ref[...]).astype(BF16), v)
        yield

    def combine():
        yield
        sf = jnp.zeros((QK_W, DV), F32)
        for c in range(n_chunks):
            sf = gf_ref[...] * sf + new_f[c]
        sb = jnp.zeros((QK_W, DV), F32)
        for c in range(n_chunks - 1, -1, -1):
            sb = gb_ref[...] * sb + new_b[c]
        scf_ref[0] = sf
        scb_ref[0] = sb

    big, tasks = [], []
    if project_stage:
        for c in range(n_chunks):
            norm_rows(slice(c * CHUNK, (c + 1) * CHUNK))
        big = [functools.partial(project, lo, hi) for lo, hi in _col_groups(p_new_ref.shape[1])]
    if state_stage:
        tasks = [(c, chunk(c), 1) for c in range(n_chunks)] + [(n_chunks - 1, combine(), 2)]
    _emit_pipelined(big, [], tasks)


def _kv_kernel(n_tiles, tiles_per_batch, x_hbm, modc_ref, nw_ref, wkv_ref, cos_ref, sin_ref,
               kdbt_ref, gb_ref, scb_ref, h_ref, kt_ref, v_ref, stb_ref, sb_ref, pa_ref, pb_ref,
               xbuf_ref, xsem_ref):
    s = pl.program_id(0)
    tile = xbuf_ref.shape[1]

    def x_copy(t):
        b = t // tiles_per_batch
        j = tiles_per_batch - 1 - lax.rem(t, tiles_per_batch)
        slot = lax.rem(t, X_RING)
        return pltpu.make_async_copy(x_hbm.at[b, pl.ds(j * tile, tile), :],
                                     xbuf_ref.at[slot], xsem_ref.at[slot])

    @pl.when(s == 0)
    def _prime():
        for t in range(min(X_RING - 1, n_tiles)):
            x_copy(t).start()

    @pl.when(s < n_tiles)
    def _arrived():
        x_copy(s).wait()

    @pl.when(s + X_RING - 1 < n_tiles)
    def _prefetch():
        x_copy(s + X_RING - 1).start()

    @pl.when((s >= 1) & (lax.rem(s - 1, tiles_per_batch) == 0))
    def _init():
        sb_ref[...] = scb_ref[0]

    batch = jnp.minimum(s, n_tiles - 1) // tiles_per_batch
    modc = modc_ref[pl.ds(batch, 1), :]
    xc_ref = xbuf_ref.at[lax.rem(s, X_RING)]
    refs = (xc_ref, modc, nw_ref, wkv_ref, cos_ref, sin_ref, kdbt_ref, gb_ref,
            h_ref, kt_ref, v_ref, stb_ref, sb_ref)
    _run_pipeline_step(s, n_tiles, 2, _kv_step, refs, (pa_ref,), (pb_ref,))


def _kv_step(stages, xc_ref, modc, nw_ref, wkv_ref, cos_ref, sin_ref, kdbt_ref, gb_ref,
             h_ref, kt_ref, v_ref, stb_ref, sb_ref, p_new_ref, p_old):
    project_stage, finish_stage = stages
    d = xc_ref.shape[-1]
    tile = xc_ref.shape[0]
    half = tile // 2
    row_halves = [slice(0, half), slice(half, tile)]
    quarter = [slice(i * half // 2, (i + 1) * half // 2) for i in range(4)]
    upper16 = (lax.broadcasted_iota(jnp.int32, (CHUNK, LANES), 1) & 16) != 0

    def norm_rows(rows):
        h = _rms(xc_ref[rows, :], nw_ref[...]) * (1.0 + modc[:, d:2 * d]) + modc[:, 0:d]
        h_ref[0, rows, :] = h.astype(BF16)

    def project(rows, lo, hi):
        p_new_ref[rows, lo:hi] = _dot(h_ref[0, rows, :], wkv_ref[:, lo:hi])

    def chunk(c):
        sl = slice(c * CHUNK, (c + 1) * CHUNK)
        k = [_rope(p_old[sl, p * LANES:(p + 1) * LANES], cos_ref[sl, :], sin_ref[sl, :], upper16)
             * (DK ** -0.5) for p in range(PAIRS)]
        v = p_old[sl, QK_W:].astype(BF16)
        v_ref[0, sl, :] = v
        kt = jnp.concatenate([kp.T for kp in k], axis=0)
        kt_ref[0, c] = kt.astype(BF16)
        new = _kv_pairs((kt * kdbt_ref[...]).astype(BF16), v)
        yield
        stb_ref[0, c] = sb_ref[...].astype(BF16)
        sb_ref[...] = gb_ref[...] * sb_ref[...] + new

    projections, late_norms, tasks = [], [], []
    if project_stage:
        norm_rows(quarter[0])
        norm_rows(quarter[1])
        projections = [functools.partial(project, row_halves[r], lo, hi)
                       for r in range(2) for lo, hi in _col_groups(p_new_ref.shape[1])]
        late_norms = [(functools.partial(norm_rows, q), None) for q in quarter[2:]]
    if finish_stage:
        tasks = [(tick, chunk(c), 2) for tick, c in enumerate(range(tile // CHUNK - 1, -1, -1))]
    _emit_pipelined(projections, late_norms, tasks)


def _main_kernel(n_tiles, tiles_per_batch, h_ref, x_ref, mod_ref, w_ref, cos_ref,
                 sin_ref, kt_ref, v_ref, stb_ref, scf_ref, dsum_ref, af_ref, ab_ref, kdft_ref,
                 gf_ref, gnw_ref, lnw_ref, ws_ref, bs_ref, wout_ref, fnw_ref, o_ref,
                 sf_ref, pa_ref, pb_ref, ya_ref, yb_ref):
    s = pl.program_id(0)

    @pl.when((s >= 1) & (lax.rem(s - 1, tiles_per_batch) == 0))
    def _init():
        sf_ref[...] = scf_ref[0]

    batch = jnp.clip(s - 2, 0, n_tiles - 1) // tiles_per_batch
    mod = mod_ref[pl.ds(batch, 1), :]
    refs = (h_ref, x_ref, mod, w_ref, cos_ref, sin_ref, kt_ref, v_ref,
            stb_ref, dsum_ref, af_ref, ab_ref, kdft_ref, gf_ref, gnw_ref, lnw_ref, ws_ref,
            bs_ref, wout_ref, fnw_ref, o_ref, sf_ref)
    _run_pipeline_step(s, n_tiles, 3, _main_step, refs, (pa_ref, ya_ref), (pb_ref, yb_ref))


def _main_step(stages, h_ref, x_ref, mod, w_ref, cos_ref, sin_ref, kt_ref, v_ref,
               stb_ref, dsum_ref, af_ref, ab_ref, kdft_ref, gf_ref, gnw_ref, lnw_ref, ws_ref,
               bs_ref, wout_ref, fnw_ref, o_ref, sf_ref, p_new_ref, y_ref, p_all, y_old):
    project_stage, mix_stage, out_stage = stages
    d = x_ref.shape[-1]
    tile = x_ref.shape[1]
    n_chunks = tile // CHUNK
    half = tile // 2
    row_halves = [slice(0, half), slice(half, tile)]
    c_gr = QK_W
    c_u = c_gr + RET_W
    c_vm = c_u + GM_W
    c_gm = c_vm + GM_W
    lane = lax.broadcasted_iota(jnp.int32, (CHUNK, LANES), 1)
    upper16 = (lane & 16) != 0
    head_of_lane = lane // DK

    operand = {}

    def loaded_once(key, load):
        if key not in operand:
            operand[key] = load()
        return operand[key]

    def project(r, lo, hi):
        rows = row_halves[r]
        lhs = loaded_once(("h", r), lambda: h_ref[0, rows, :])
        p_new_ref[rows, lo:hi] = _dot(lhs, w_ref[:, lo:hi])

    def retention(c, p):
        sl = slice(c * CHUNK, (c + 1) * CHUNK)
        ps = slice(p * LANES, (p + 1) * LANES)
        q2 = _rope(p_all[sl, ps], cos_ref[sl, :], sin_ref[sl, :], upper16)
        kt = kt_ref[0, c, ps, :]
        vp = v_ref[0, sl, p * 2 * DV:(p + 1) * 2 * DV]
        qa = q2 * af_ref[:, ps]
        qb = q2 * ab_ref[:, ps]
        scores, qx = [], []
        for hh in range(2):
            mine = head_of_lane == hh
            scores.append(_dot(jnp.where(mine, q2, 0.0).astype(BF16), kt))
            qx.append(jnp.concatenate([jnp.where(mine, qa, 0.0), jnp.where(mine, qb, 0.0)],
                                      axis=1).astype(BF16))
        kd = (kt.astype(F32) * kdft_ref[ps, :]).astype(BF16)
        yield
        state = jnp.concatenate([sf_ref[ps, :].astype(BF16), stb_ref[0, c, ps, :]], axis=0)
        o = []
        for hh in range(2):
            a = (scores[hh] * dsum_ref[2 * p + hh]).astype(BF16)
            o.append(_dot(a, vp[:, hh * DV:(hh + 1) * DV]) + _dot(qx[hh], state))
        r = _dot(kd, vp)
        yield
        new = jnp.concatenate([r[0:DK, 0:DV], r[DK:2 * DK, DV:2 * DV]], axis=0)
        sf_ref[ps, :] = gf_ref[ps, :] * sf_ref[ps, :] + new
        for hh in range(2):
            head = 2 * p + hh
            mu = jnp.mean(o[hh], axis=-1, keepdims=True)
            var = jnp.mean(jnp.square(o[hh] - mu), axis=-1, keepdims=True)
            hs = slice(head * DV, (head + 1) * DV)
            y = (o[hh] - mu) * lax.rsqrt(var + EPS) * gnw_ref[:, hs]
            gate_cols = slice(c_gr + head * DV, c_gr + (head + 1) * DV)
            y_ref[sl, hs] = (y * _silu(p_all[sl, gate_cols])).astype(BF16)

    def gating(c):
        sl = slice(c * CHUNK, (c + 1) * CHUNK)
        vg = _gelu(p_all[sl, c_vm:c_gm])
        mu = jnp.mean(vg, axis=-1, keepdims=True)
        var = jnp.mean(jnp.square(vg - mu), axis=-1, keepdims=True)
        yield
        vn = ((vg - mu) * lax.rsqrt(var + EPS) * lnw_ref[...]).astype(BF16)
        s = [_dot(ws_ref[g], vn[:, g * DG:(g + 1) * DG]) for g in range(GROUPS)]
        yield
        for g in range(GROUPS):
            if g == GROUPS // 2:
                yield
            u = _gelu(p_all[sl, c_u + g * DG:c_u + (g + 1) * DG])
            gm = _silu(p_all[sl, c_gm + g * DG:c_gm + (g + 1) * DG])
            y_ref[sl, RET_W + g * DG:RET_W + (g + 1) * DG] = (u * (s[g] + bs_ref[g]) * gm).astype(BF16)

    out_groups = _col_groups(d)
    xn = {}

    def out_project(r, g):
        rows = row_halves[r]
        lo, hi = out_groups[g]
        lhs = loaded_once(("y", r), lambda: y_old[rows, :])
        out = _dot(lhs, wout_ref[:, lo:hi])
        xn[r, g] = x_ref[0, rows, lo:hi] + mod[:, 2 * d + lo:2 * d + hi] * out

    def finish(r, part):
        sub = slice(part * half // 2, (part + 1) * half // 2)
        rows = slice(row_halves[r].start + sub.start, row_halves[r].start + sub.stop)
        parts = [xn[r, g][sub, :] for g in range(len(out_groups))]
        ms = sum(jnp.sum(t * t, axis=-1, keepdims=True) for t in parts) * (1.0 / d)
        rs = lax.rsqrt(ms + EPS)
        for (lo, hi), t in zip(out_groups, parts):
            o_ref[0, rows, lo:hi] = t * rs * fnw_ref[:, lo:hi]

    tasks = []
    if mix_stage:
        for c in range(n_chunks):
            base = 3 * c
            tasks += [(base + min(p, 1), retention(c, p), 3) for p in range(PAIRS)]
            tasks.append((base + 1, gating(c), 4))
    big, relaxed = [], {}

    def place(tick, piece):
        while tick in relaxed:
            tick += 1
        relaxed[tick] = piece

    for r in range(2):
        outs = [functools.partial(out_project, r, g) for g in range(len(out_groups))] if out_stage else []
        projs = ([functools.partial(project, r, lo, hi) for lo, hi in _col_groups(p_new_ref.shape[1])]
                 if project_stage else [])
        for i in range(max(len(outs), len(projs))):
            big += outs[i:i + 1] + projs[i:i + 1]
            if i == len(outs) - 1:
                place(len(big) + 1, functools.partial(finish, r, 0))
                place(len(big) + 3, functools.partial(finish, r, 1))
    side = [(None, relaxed.get(tick)) for tick in range(max(relaxed, default=-1) + 1)]
    _emit_pipelined(big, side, tasks)


def _full(shape):
    return pl.BlockSpec(shape, lambda *_: (0,) * len(shape))


def kernel(x, c, ctx, c_ctx, w_ada, b_ada, norm_w, w_in, ret_decay_f, ret_decay_b, ret_gn_w,
           gmlp_ln_w, w_s, b_s, w_out, final_norm_w):
    bsz, seq, d = x.shape
    ctx_len = ctx.shape[1]
    depth = w_ada.shape[0]
    assert depth == 1 and d % LANES == 0
    assert seq % TILE_KV == 0 and seq % TILE_MAIN == 0 and ctx_len % CHUNK == 0
    assert seq % GRID_W == 0 and TILE_MAIN % (4 * CHUNK) == 0 and TILE_KV % (4 * CHUNK) == 0
    n_chunks = seq // CHUNK
    d3 = 3 * d
    n_kv = QK_W + RET_W
    n_rest = w_in.shape[2] - n_kv
    n_gates = n_rest - QK_W
    assert QK_W + n_kv == d and n_gates % d == 0 and d3 // d == 1 + n_gates // d
    assert _col_groups(n_rest)[0] == (0, QK_W)

    quarter = DK // 4
    inv_freq = ROPE_BASE ** (-jnp.arange(quarter, dtype=F32) / quarter)
    invf_lane = jnp.tile(inv_freq, LANES // quarter)[None, :]
    nw = norm_w[0][None, :]
    rows2 = 2 * bsz

    n_col_blocks = d3 // d
    smem = pl.BlockSpec(memory_space=pltpu.SMEM)
    n_out_padded = d + QK_W
    (mod, cos_t, sin_t, dsum, af, ab, kdft, kdbt, gf, gb,
     w_rest, w_kv, w_out_b, w_s_b, b_s_full) = pl.pallas_call(
        _prologue_kernel,
        grid=(n_col_blocks,),
        in_specs=[smem, smem, _full((bsz, d)), _full((1, d)),
                  pl.BlockSpec((d, d), lambda i: (0, i)),
                  pl.BlockSpec((1, d), lambda i: (0, i)),
                  _full((1, LANES)),
                  pl.BlockSpec((d, d), lambda i: (0, i)),
                  _full((RET_W + GM_W, d)),
                  _full((GROUPS, CHUNK, CHUNK)), _full((GROUPS, CHUNK))],
        out_specs=[pl.BlockSpec((rows2, d), lambda i: (0, i)),
                   _full((seq, LANES)), _full((seq, LANES)),
                   _full((HEADS, CHUNK, CHUNK)),
                   _full((CHUNK, QK_W)), _full((CHUNK, QK_W)),
                   _full((QK_W, CHUNK)), _full((QK_W, CHUNK)),
                   _full((QK_W, DV)), _full((QK_W, DV)),
                   _full((d, n_rest)), _full((d, n_kv)),
                   _full((RET_W + GM_W, n_out_padded)),
                   _full((GROUPS, CHUNK, CHUNK)), _full((GROUPS, CHUNK, DG))],
        out_shape=[jax.ShapeDtypeStruct((rows2, d3), F32),
                   jax.ShapeDtypeStruct((seq, LANES), F32),
                   jax.ShapeDtypeStruct((seq, LANES), F32),
                   jax.ShapeDtypeStruct((HEADS, CHUNK, CHUNK), F32),
                   jax.ShapeDtypeStruct((CHUNK, QK_W), F32),
                   jax.ShapeDtypeStruct((CHUNK, QK_W), F32),
                   jax.ShapeDtypeStruct((QK_W, CHUNK), F32),
                   jax.ShapeDtypeStruct((QK_W, CHUNK), F32),
                   jax.ShapeDtypeStruct((QK_W, DV), F32),
                   jax.ShapeDtypeStruct((QK_W, DV), F32),
                   jax.ShapeDtypeStruct((d, n_rest), BF16),
                   jax.ShapeDtypeStruct((d, n_kv), BF16),
                   jax.ShapeDtypeStruct((RET_W + GM_W, n_out_padded), BF16),
                   jax.ShapeDtypeStruct((GROUPS, CHUNK, CHUNK), BF16),
                   jax.ShapeDtypeStruct((GROUPS, CHUNK, DG), F32)],
        compiler_params=pltpu.CompilerParams(dimension_semantics=("arbitrary",)),
        name="prologue",
    )(ret_decay_f[0], ret_decay_b[0], c, c_ctx[None, :], w_ada[0], b_ada[0][None, :], invf_lane,
      w_in[0], w_out[0], w_s[0], b_s[0])

    s_cf, s_cb = pl.pallas_call(
        functools.partial(_ctx_kernel, bsz),
        grid=(bsz + 1,),
        in_specs=[pl.BlockSpec((1, ctx_len, d), lambda s: (jnp.minimum(s, bsz - 1), 0, 0)),
                  _full((rows2, d3)),
                  _full((1, d)), _full((d, n_kv)),
                  _full((QK_W, CHUNK)), _full((QK_W, CHUNK)),
                  _full((QK_W, DV)), _full((QK_W, DV))],
        out_specs=[pl.BlockSpec((1, QK_W, DV), lambda s: (jnp.maximum(s - 1, 0), 0, 0)),
                   pl.BlockSpec((1, QK_W, DV), lambda s: (jnp.maximum(s - 1, 0), 0, 0))],
        out_shape=[jax.ShapeDtypeStruct((bsz, QK_W, DV), F32),
                   jax.ShapeDtypeStruct((bsz, QK_W, DV), F32)],
        scratch_shapes=[pltpu.VMEM((ctx_len, d), BF16),
                        pltpu.VMEM((ctx_len, n_kv), F32),
                        pltpu.VMEM((ctx_len, n_kv), F32)],
        compiler_params=pltpu.CompilerParams(dimension_semantics=("arbitrary",)),
        name="ctx_states",
    )(ctx, mod, nw, w_kv, kdft, kdbt, gf, gb)

    def tile_maps(n_tiles, per_batch, reverse):
        def where(t):
            j = t % per_batch
            return t // per_batch, (per_batch - 1 - j) if reverse else j
        return [lambda s, k=k: where(jnp.clip(s - k, 0, n_tiles - 1)) for k in range(3)]

    nt_kv = seq // TILE_KV
    cpt_kv = TILE_KV // CHUNK
    cur, prv, _ = tile_maps(bsz * nt_kv, nt_kv, reverse=True)
    h, kt, v, st_b = pl.pallas_call(
        functools.partial(_kv_kernel, bsz * nt_kv, nt_kv),
        grid=(bsz * nt_kv + 1,),
        in_specs=[pl.BlockSpec(memory_space=pl.ANY),
                  _full((rows2, d3)),
                  _full((1, d)), _full((d, n_kv)),
                  pl.BlockSpec((TILE_KV, LANES), lambda s: (prv(s)[1], 0)),
                  pl.BlockSpec((TILE_KV, LANES), lambda s: (prv(s)[1], 0)),
                  _full((QK_W, CHUNK)), _full((QK_W, DV)),
                  pl.BlockSpec((1, QK_W, DV), lambda s: (prv(s)[0], 0, 0))],
        out_specs=[pl.BlockSpec((1, TILE_KV, d), lambda s: (*cur(s), 0)),
                   pl.BlockSpec((1, cpt_kv, QK_W, CHUNK), lambda s: (*prv(s), 0, 0)),
                   pl.BlockSpec((1, TILE_KV, RET_W), lambda s: (*prv(s), 0)),
                   pl.BlockSpec((1, cpt_kv, QK_W, DV), lambda s: (*prv(s), 0, 0))],
        out_shape=[jax.ShapeDtypeStruct((bsz, seq, d), BF16),
                   jax.ShapeDtypeStruct((bsz, n_chunks, QK_W, CHUNK), BF16),
                   jax.ShapeDtypeStruct((bsz, seq, RET_W), BF16),
                   jax.ShapeDtypeStruct((bsz, n_chunks, QK_W, DV), BF16)],
        scratch_shapes=[pltpu.VMEM((QK_W, DV), F32),
                        pltpu.VMEM((TILE_KV, n_kv), F32),
                        pltpu.VMEM((TILE_KV, n_kv), F32),
                        pltpu.VMEM((X_RING, TILE_KV, d), F32),
                        pltpu.SemaphoreType.DMA((X_RING,))],
        compiler_params=pltpu.CompilerParams(dimension_semantics=("arbitrary",)),
        name="kv_sweep",
    )(x, mod, nw, w_kv, cos_t, sin_t, kdbt, gb, s_cb)

    nt = seq // TILE_MAIN
    cpt = TILE_MAIN // CHUNK
    cur, prv, prv2 = tile_maps(bsz * nt, nt, reverse=False)
    out = pl.pallas_call(
        functools.partial(_main_kernel, bsz * nt, nt),
        grid=(bsz * nt + 2,),
        in_specs=[pl.BlockSpec((1, TILE_MAIN, d), lambda s: (*cur(s), 0)),
                  pl.BlockSpec((1, TILE_MAIN, d), lambda s: (*prv2(s), 0)),
                  _full((rows2, d3)),
                  _full((d, n_rest)),
                  pl.BlockSpec((TILE_MAIN, LANES), lambda s: (prv(s)[1], 0)),
                  pl.BlockSpec((TILE_MAIN, LANES), lambda s: (prv(s)[1], 0)),
                  pl.BlockSpec((1, cpt, QK_W, CHUNK), lambda s: (*prv(s), 0, 0)),
                  pl.BlockSpec((1, TILE_MAIN, RET_W), lambda s: (*prv(s), 0)),
                  pl.BlockSpec((1, cpt, QK_W, DV), lambda s: (*prv(s), 0, 0)),
                  pl.BlockSpec((1, QK_W, DV), lambda s: (prv(s)[0], 0, 0)),
                  _full((HEADS, CHUNK, CHUNK)),
                  _full((CHUNK, QK_W)), _full((CHUNK, QK_W)),
                  _full((QK_W, CHUNK)), _full((QK_W, DV)),
                  _full((1, RET_W)), _full((1, GM_W)),
                  _full((GROUPS, CHUNK, CHUNK)), _full((GROUPS, CHUNK, DG)),
                  _full((RET_W + GM_W, n_out_padded)), _full((1, d))],
        out_specs=pl.BlockSpec((1, TILE_MAIN, d), lambda s: (*prv2(s), 0)),
        out_shape=jax.ShapeDtypeStruct((bsz, seq, d), x.dtype),
        scratch_shapes=[pltpu.VMEM((QK_W, DV), F32),
                        pltpu.VMEM((TILE_MAIN, n_rest), F32),
                        pltpu.VMEM((TILE_MAIN, n_rest), F32),
                        pltpu.VMEM((TILE_MAIN, RET_W + GM_W), BF16),
                        pltpu.VMEM((TILE_MAIN, RET_W + GM_W), BF16)],
        compiler_params=pltpu.CompilerParams(dimension_semantics=("arbitrary",)),
        name="main_sweep",
    )(h, x, mod, w_rest, cos_t, sin_t, kt, v, st_b, s_cf, dsum, af, ab, kdft, gf,
      ret_gn_w[0][None, :], gmlp_ln_w[0][None, :], w_s_b, b_s_full, w_out_b,
      final_norm_w[None, :])
    return out
```

```python
import functools

import jax
import jax.numpy as jnp
from jax import lax
from jax.experimental import pallas as pl
from jax.experimental.pallas import tpu as pltpu

F32 = jnp.float32
BF16 = jnp.bfloat16

HEADS = 4
DK = 64
DV = 128
CHUNK = 128
GROUPS = 4
DG = 128
GRID_W = 64
ROPE_BASE = 10000.0
EPS = 1e-6
LANES = 128
PAIRS = HEADS * DK // LANES
RET_W = HEADS * DV
GM_W = GROUPS * DG
QK_W = HEADS * DK

DOT_COLS = 512
TILE_KV = 1024
X_RING = 4
TILE_MAIN = 512


def _rms(x, w):
    ms = jnp.mean(x * x, axis=-1, keepdims=True)
    return x * lax.rsqrt(ms + EPS) * w


def _silu(x):
    return x * jax.nn.sigmoid(x)


def _gelu(x):
    return 0.5 * x * (1.0 + lax.erf(x * (0.5 ** 0.5)))


_dot = functools.partial(jnp.dot, preferred_element_type=F32)


def _col_groups(n):
    first = n % DOT_COLS
    edges = ([0] if first == 0 else [0, first]) + list(range(first + DOT_COLS, n + 1, DOT_COLS))
    return list(zip(edges[:-1], edges[1:]))


def _rope(p, cos, sin_signed, upper16):
    partner = jnp.where(upper16, pltpu.roll(p, 16, 1), pltpu.roll(p, LANES - 16, 1))
    return p * cos + partner * sin_signed


def _kv_pairs(kd, v):
    out = []
    for p in range(PAIRS):
        r = _dot(kd[p * LANES:(p + 1) * LANES, :], v[:, p * 2 * DV:(p + 1) * 2 * DV])
        out.append(r[0:DK, 0:DV])
        out.append(r[DK:2 * DK, DV:2 * DV])
    return jnp.concatenate(out, axis=0)


def _emit_pipelined(big_pieces, side_pieces, tasks):
    in_flight = []
    last_tick = max([start + parts for start, _, parts in tasks] +
                    [len(big_pieces), len(side_pieces)])
    for tick in range(last_tick):
        if tick < len(big_pieces):
            big_pieces[tick]()
        urgent, relaxed = side_pieces[tick] if tick < len(side_pieces) else (None, None)
        if urgent is not None:
            urgent()
        in_flight += [[gen, parts] for start, gen, parts in tasks if start == tick]
        for t in [t for t in in_flight if t[1] > 1] + [t for t in in_flight if t[1] == 1]:
            next(t[0], None)
            t[1] -= 1
        in_flight = [t for t in in_flight if t[1] > 0]
        if relaxed is not None:
            relaxed()
    assert not in_flight


def _run_pipeline_step(step, n_tiles, n_stages, body, refs, bufs_a, bufs_b):
    def call(s_static_parity, stages):
        new, old = (bufs_a, bufs_b) if s_static_parity == 0 else (bufs_b, bufs_a)
        body(stages, *refs, *new, *old)

    edge = n_stages - 1
    for s in list(range(edge)) + list(range(n_tiles, n_tiles + edge)):
        stages = tuple(0 <= s - j < n_tiles for j in range(n_stages))
        pl.when(step == s)(functools.partial(call, s % 2, stages))
    steady = (step >= edge) & (step < n_tiles)
    for parity in range(2):
        pl.when(steady & (lax.rem(step, 2) == parity))(
            functools.partial(call, parity, (True,) * n_stages))


def _prologue_kernel(decf_ref, decb_ref, c_ref, cctx_ref, w_ref, b_ref, invf_ref, win_ref,
                     wout_ref, ws_ref, bs_ref,
                     mod_ref, cos_ref, sin_ref, dsum_ref, af_ref, ab_ref,
                     kdft_ref, kdbt_ref, gf_ref, gb_ref,
                     wrest_ref, wkv_ref, woutb_ref, wsb_ref, bsf_ref):
    i = pl.program_id(0)
    d = win_ref.shape[0]
    c = c_ref[...]
    cond = jnp.concatenate([c, jnp.broadcast_to(cctx_ref[...], c.shape)], axis=0)
    mod_ref[...] = _dot(_silu(cond), w_ref[...]) + b_ref[...]

    @pl.when(i == 0)
    def _first_block():
        wrest_ref[:, 0:QK_W] = win_ref[:, 0:QK_W].astype(BF16)
        wkv_ref[...] = win_ref[:, QK_W:].astype(BF16)
        woutb_ref[:, 0:d] = wout_ref[...].astype(BF16)
        woutb_ref[:, d:] = jnp.zeros((woutb_ref.shape[0], woutb_ref.shape[1] - d), BF16)
        wsb_ref[...] = ws_ref[...].astype(BF16)
        for g in range(GROUPS):
            bsf_ref[g] = jnp.broadcast_to(bs_ref[g:g + 1, :], (CHUNK, CHUNK)).T

    for blk in range(1, (wrest_ref.shape[1] - QK_W) // d + 1):
        @pl.when(i == blk)
        def _later_block(blk=blk):
            wrest_ref[:, QK_W + (blk - 1) * d:QK_W + blk * d] = win_ref[...].astype(BF16)

    @pl.when(i == 0)
    def _tables():
        pos = lax.broadcasted_iota(jnp.int32, (GRID_W, LANES), 0).astype(F32)
        lane = lax.broadcasted_iota(jnp.int32, (GRID_W, LANES), 1)
        ang = pos * invf_ref[...]
        cr = jnp.cos(ang)
        sr = jnp.sin(ang)
        sr = jnp.where((lane & 16) == 0, -sr, sr)
        by_row = (lane & 32) == 0
        n_rows = cos_ref.shape[0] // GRID_W
        for r in range(n_rows):
            sl = slice(r * GRID_W, (r + 1) * GRID_W)
            cos_ref[sl, :] = jnp.where(by_row, jnp.broadcast_to(cr[r:r + 1, :], cr.shape), cr)
            sin_ref[sl, :] = jnp.where(by_row, jnp.broadcast_to(sr[r:r + 1, :], sr.shape), sr)

        def log_decay(head_idx, dec_ref):
            d = jnp.full(head_idx.shape, dec_ref[HEADS - 1], F32)
            for h in range(HEADS - 2, -1, -1):
                d = jnp.where(head_idx == h, dec_ref[h], d)
            return -jnp.exp(d)

        n = lax.broadcasted_iota(jnp.int32, (CHUNK, QK_W), 0).astype(F32)
        hl = lax.broadcasted_iota(jnp.int32, (CHUNK, QK_W), 1) // DK
        af_ref[...] = jnp.exp((n + 1.0) * log_decay(hl, decf_ref))
        ab_ref[...] = jnp.exp((CHUNK - n) * log_decay(hl, decb_ref))

        m = lax.broadcasted_iota(jnp.int32, (QK_W, CHUNK), 1).astype(F32)
        hr = lax.broadcasted_iota(jnp.int32, (QK_W, CHUNK), 0) // DK
        lgf = log_decay(hr, decf_ref)
        lgb = log_decay(hr, decb_ref)
        kdft_ref[...] = jnp.exp((CHUNK - 1.0 - m) * lgf)
        kdbt_ref[...] = jnp.exp(m * lgb)
        gf_ref[...] = jnp.exp(float(CHUNK) * lgf)
        gb_ref[...] = jnp.exp(float(CHUNK) * lgb)

        nn = lax.broadcasted_iota(jnp.int32, (CHUNK, CHUNK), 0).astype(F32)
        mm = lax.broadcasted_iota(jnp.int32, (CHUNK, CHUNK), 1).astype(F32)
        for h in range(HEADS):
            lf = -jnp.exp(jnp.full((CHUNK, CHUNK), decf_ref[h], F32))
            lb = -jnp.exp(jnp.full((CHUNK, CHUNK), decb_ref[h], F32))
            df = nn - mm
            db = mm - nn
            fwd = jnp.where(df >= 0, jnp.exp(jnp.maximum(df, 0.0) * lf), 0.0)
            bwd = jnp.where(db >= 0, jnp.exp(jnp.maximum(db, 0.0) * lb), 0.0)
            dsum_ref[h] = fwd + bwd


def _ctx_kernel(n_tiles, ctx_ref, mod_ref, nw_ref, wkv_ref, kdft_ref, kdbt_ref, gf_ref, gb_ref,
                scf_ref, scb_ref, h_ref, pa_ref, pb_ref):
    refs = (ctx_ref, mod_ref, nw_ref, wkv_ref, kdft_ref, kdbt_ref, gf_ref, gb_ref,
            scf_ref, scb_ref, h_ref)
    _run_pipeline_step(pl.program_id(0), n_tiles, 2, _ctx_step, refs, (pa_ref,), (pb_ref,))


def _ctx_step(stages, ctx_ref, mod_ref, nw_ref, wkv_ref, kdft_ref, kdbt_ref, gf_ref, gb_ref,
              scf_ref, scb_ref, h_ref, p_new_ref, p_old):
    project_stage, state_stage = stages
    d = ctx_ref.shape[-1]
    n_chunks = ctx_ref.shape[1] // CHUNK
    ctx_row = mod_ref.shape[0] // 2
    mod = mod_ref[ctx_row:ctx_row + 1, :]

    def norm_rows(rows):
        h = _rms(ctx_ref[0, rows, :], nw_ref[...]) * (1.0 + mod[:, d:2 * d]) + mod[:, 0:d]
        h_ref[rows, :] = h.astype(BF16)

    def project(lo, hi):
        p_new_ref[:, lo:hi] = _dot(h_ref[...], wkv_ref[:, lo:hi])

    new_f, new_b = {}, {}

    def chunk(c):
        sl = slice(c * CHUNK, (c + 1) * CHUNK)
        kt = (p_old[sl, 0:QK_W] * (DK ** -0.5)).T
        v = p_old[sl, QK_W:].astype(BF16)
        new_f[c] = _kv_pairs((kt * kdft_ref[...]).astype(BF16), v)
        new_b[c] = _kv_pairs((kt * kdbt_ref[...]).astype(BF16), v)
        yield

    def combine():
        yield
        sf = jnp.zeros((QK_W, DV), F32)
        for c in range(n_chunks):
            sf = gf_ref[...] * sf + new_f[c]
        sb = jnp.zeros((QK_W, DV), F32)
        for c in range(n_chunks - 1, -1, -1):
            sb = gb_ref[...] * sb + new_b[c]
        scf_ref[0] = sf
        scb_ref[0] = sb

    big, tasks = [], []
    if project_stage:
        for c in range(n_chunks):
            norm_rows(slice(c * CHUNK, (c + 1) * CHUNK))
        big = [functools.partial(project, lo, hi) for lo, hi in _col_groups(p_new_ref.shape[1])]
    if state_stage:
        tasks = [(c, chunk(c), 1) for c in range(n_chunks)] + [(n_chunks - 1, combine(), 2)]
    _emit_pipelined(big, [], tasks)


def _kv_kernel(n_tiles, tiles_per_batch, x_hbm, modc_ref, nw_ref, wkv_ref, cos_ref, sin_ref,
               kdbt_ref, gb_ref, scb_ref, h_ref, kt_ref, v_ref, stb_ref, sb_ref, pa_ref, pb_ref,
               xbuf_ref, xsem_ref):
    s = pl.program_id(0)
    tile = xbuf_ref.shape[1]

    def x_copy(t):
        b = t // tiles_per_batch
        j = tiles_per_batch - 1 - lax.rem(t, tiles_per_batch)
        slot = lax.rem(t, X_RING)
        return pltpu.make_async_copy(x_hbm.at[b, pl.ds(j * tile, tile), :],
                                     xbuf_ref.at[slot], xsem_ref.at[slot])

    @pl.when(s == 0)
    def _prime():
        for t in range(min(X_RING - 1, n_tiles)):
            x_copy(t).start()

    @pl.when(s < n_tiles)
    def _arrived():
        x_copy(s).wait()

    @pl.when(s + X_RING - 1 < n_tiles)
    def _prefetch():
        x_copy(s + X_RING - 1).start()

    @pl.when((s >= 1) & (lax.rem(s - 1, tiles_per_batch) == 0))
    def _init():
        sb_ref[...] = scb_ref[0]

    batch = jnp.minimum(s, n_tiles - 1) // tiles_per_batch
    modc = modc_ref[pl.ds(batch, 1), :]
    xc_ref = xbuf_ref.at[lax.rem(s, X_RING)]
    refs = (xc_ref, modc, nw_ref, wkv_ref, cos_ref, sin_ref, kdbt_ref, gb_ref,
            h_ref, kt_ref, v_ref, stb_ref, sb_ref)
    _run_pipeline_step(s, n_tiles, 2, _kv_step, refs, (pa_ref,), (pb_ref,))


def _kv_step(stages, xc_ref, modc, nw_ref, wkv_ref, cos_ref, sin_ref, kdbt_ref, gb_ref,
             h_ref, kt_ref, v_ref, stb_ref, sb_ref, p_new_ref, p_old):
    project_stage, finish_stage = stages
    d = xc_ref.shape[-1]
    tile = xc_ref.shape[0]
    half = tile // 2
    row_halves = [slice(0, half), slice(half, tile)]
    quarter = [slice(i * half // 2, (i + 1) * half // 2) for i in range(4)]
    upper16 = (lax.broadcasted_iota(jnp.int32, (CHUNK, LANES), 1) & 16) != 0

    def norm_rows(rows):
        h = _rms(xc_ref[rows, :], nw_ref[...]) * (1.0 + modc[:, d:2 * d]) + modc[:, 0:d]
        h_ref[0, rows, :] = h.astype(BF16)

    def project(rows, lo, hi):
        p_new_ref[rows, lo:hi] = _dot(h_ref[0, rows, :], wkv_ref[:, lo:hi])

    def chunk(c):
        sl = slice(c * CHUNK, (c + 1) * CHUNK)
        k = [_rope(p_old[sl, p * LANES:(p + 1) * LANES], cos_ref[sl, :], sin_ref[sl, :], upper16)
             * (DK ** -0.5) for p in range(PAIRS)]
        v = p_old[sl, QK_W:].astype(BF16)
        v_ref[0, sl, :] = v
        kt = jnp.concatenate([kp.T for kp in k], axis=0)
        kt_ref[0, c] = kt.astype(BF16)
        new = _kv_pairs((kt * kdbt_ref[...]).astype(BF16), v)
        yield
        stb_ref[0, c] = sb_ref[...].astype(BF16)
        sb_ref[...] = gb_ref[...] * sb_ref[...] + new

    projections, late_norms, tasks = [], [], []
    if project_stage:
        norm_rows(quarter[0])
        norm_rows(quarter[1])
        projections = [functools.partial(project, row_halves[r], lo, hi)
                       for r in range(2) for lo, hi in _col_groups(p_new_ref.shape[1])]
        late_norms = [(functools.partial(norm_rows, q), None) for q in quarter[2:]]
    if finish_stage:
        tasks = [(tick, chunk(c), 2) for tick, c in enumerate(range(tile // CHUNK - 1, -1, -1))]
    _emit_pipelined(projections, late_norms, tasks)


def _main_kernel(n_tiles, tiles_per_batch, h_ref, x_ref, mod_ref, w_ref, cos_ref,
                 sin_ref, kt_ref, v_ref, stb_ref, scf_ref, dsum_ref, af_ref, ab_ref, kdft_ref,
                 gf_ref, gnw_ref, lnw_ref, ws_ref, bs_ref, wout_ref, fnw_ref, o_ref,
                 sf_ref, pa_ref, pb_ref, ya_ref, yb_ref):
    s = pl.program_id(0)

    @pl.when((s >= 1) & (lax.rem(s - 1, tiles_per_batch) == 0))
    def _init():
        sf_ref[...] = scf_ref[0]

    batch = jnp.clip(s - 2, 0, n_tiles - 1) // tiles_per_batch
    mod = mod_ref[pl.ds(batch, 1), :]
    refs = (h_ref, x_ref, mod, w_ref, cos_ref, sin_ref, kt_ref, v_ref,
            stb_ref, dsum_ref, af_ref, ab_ref, kdft_ref, gf_ref, gnw_ref, lnw_ref, ws_ref,
            bs_ref, wout_ref, fnw_ref, o_ref, sf_ref)
    _run_pipeline_step(s, n_tiles, 3, _main_step, refs, (pa_ref, ya_ref), (pb_ref, yb_ref))


def _main_step(stages, h_ref, x_ref, mod, w_ref, cos_ref, sin_ref, kt_ref, v_ref,
               stb_ref, dsum_ref, af_ref, ab_ref, kdft_ref, gf_ref, gnw_ref, lnw_ref, ws_ref,
               bs_ref, wout_ref, fnw_ref, o_ref, sf_ref, p_new_ref, y_ref, p_all, y_old):
    project_stage, mix_stage, out_stage = stages
    d = x_ref.shape[-1]
    tile = x_ref.shape[1]
    n_chunks = tile // CHUNK
    half = tile // 2
    row_halves = [slice(0, half), slice(half, tile)]
    c_gr = QK_W
    c_u = c_gr + RET_W
    c_vm = c_u + GM_W
    c_gm = c_vm + GM_W
    lane = lax.broadcasted_iota(jnp.int32, (CHUNK, LANES), 1)
    upper16 = (lane & 16) != 0
    head_of_lane = lane // DK

    operand = {}

    def loaded_once(key, load):
        if key not in operand:
            operand[key] = load()
        return operand[key]

    def project(r, lo, hi):
        rows = row_halves[r]
        lhs = loaded_once(("h", r), lambda: h_ref[0, rows, :])
        p_new_ref[rows, lo:hi] = _dot(lhs, w_ref[:, lo:hi])

    def retention(c, p):
        sl = slice(c * CHUNK, (c + 1) * CHUNK)
        ps = slice(p * LANES, (p + 1) * LANES)
        q2 = _rope(p_all[sl, ps], cos_ref[sl, :], sin_ref[sl, :], upper16)
        kt = kt_ref[0, c, ps, :]
        vp = v_ref[0, sl, p * 2 * DV:(p + 1) * 2 * DV]
        qa = q2 * af_ref[:, ps]
        qb = q2 * ab_ref[:, ps]
        scores, qx = [], []
        for hh in range(2):
            mine = head_of_lane == hh
            scores.append(_dot(jnp.where(mine, q2, 0.0).astype(BF16), kt))
            qx.append(jnp.concatenate([jnp.where(mine, qa, 0.0), jnp.where(mine, qb, 0.0)],
                                      axis=1).astype(BF16))
        kd = (kt.astype(F32) * kdft_ref[ps, :]).astype(BF16)
        yield
        state = jnp.concatenate([sf_ref[ps, :].astype(BF16), stb_ref[0, c, ps, :]], axis=0)
        o = []
        for hh in range(2):
            a = (scores[hh] * dsum_ref[2 * p + hh]).astype(BF16)
            o.append(_dot(a, vp[:, hh * DV:(hh + 1) * DV]) + _dot(qx[hh], state))
        r = _dot(kd, vp)
        yield
        new = jnp.concatenate([r[0:DK, 0:DV], r[DK:2 * DK, DV:2 * DV]], axis=0)
        sf_ref[ps, :] = gf_ref[ps, :] * sf_ref[ps, :] + new
        for hh in range(2):
            head = 2 * p + hh
            mu = jnp.mean(o[hh], axis=-1, keepdims=True)
            var = jnp.mean(jnp.square(o[hh] - mu), axis=-1, keepdims=True)
            hs = slice(head * DV, (head + 1) * DV)
            y = (o[hh] - mu) * lax.rsqrt(var + EPS) * gnw_ref[:, hs]
            gate_cols = slice(c_gr + head * DV, c_gr + (head + 1) * DV)
            y_ref[sl, hs] = (y * _silu(p_all[sl, gate_cols])).astype(BF16)

    def gating(c):
        sl = slice(c * CHUNK, (c + 1) * CHUNK)
        vg = _gelu(p_all[sl, c_vm:c_gm])
        mu = jnp.mean(vg, axis=-1, keepdims=True)
        var = jnp.mean(jnp.square(vg - mu), axis=-1, keepdims=True)
        yield
        vn = ((vg - mu) * lax.rsqrt(var + EPS) * lnw_ref[...]).astype(BF16)
        s = [_dot(ws_ref[g], vn[:, g * DG:(g + 1) * DG]) for g in range(GROUPS)]
        yield
        for g in range(GROUPS):
            if g == GROUPS // 2:
                yield
            u = _gelu(p_all[sl, c_u + g * DG:c_u + (g + 1) * DG])
            gm = _silu(p_all[sl, c_gm + g * DG:c_gm + (g + 1) * DG])
            y_ref[sl, RET_W + g * DG:RET_W + (g + 1) * DG] = (u * (s[g] + bs_ref[g]) * gm).astype(BF16)

    out_groups = _col_groups(d)
    xn = {}

    def out_project(r, g):
        rows = row_halves[r]
        lo, hi = out_groups[g]
        lhs = loaded_once(("y", r), lambda: y_old[rows, :])
        out = _dot(lhs, wout_ref[:, lo:hi])
        xn[r, g] = x_ref[0, rows, lo:hi] + mod[:, 2 * d + lo:2 * d + hi] * out

    def finish(r, part):
        sub = slice(part * half // 2, (part + 1) * half // 2)
        rows = slice(row_halves[r].start + sub.start, row_halves[r].start + sub.stop)
        parts = [xn[r, g][sub, :] for g in range(len(out_groups))]
        ms = sum(jnp.sum(t * t, axis=-1, keepdims=True) for t in parts) * (1.0 / d)
        rs = lax.rsqrt(ms + EPS)
        for (lo, hi), t in zip(out_groups, parts):
            o_ref[0, rows, lo:hi] = t * rs * fnw_ref[:, lo:hi]

    tasks = []
    if mix_stage:
        for c in range(n_chunks):
            base = 3 * c
            tasks += [(base + min(p, 1), retention(c, p), 3) for p in range(PAIRS)]
            tasks.append((base + 1, gating(c), 4))
    big, relaxed = [], {}

    def place(tick, piece):
        while tick in relaxed:
            tick += 1
        relaxed[tick] = piece

    for r in range(2):
        outs = [functools.partial(out_project, r, g) for g in range(len(out_groups))] if out_stage else []
        projs = ([functools.partial(project, r, lo, hi) for lo, hi in _col_groups(p_new_ref.shape[1])]
                 if project_stage else [])
        for i in range(max(len(outs), len(projs))):
            big += outs[i:i + 1] + projs[i:i + 1]
            if i == len(outs) - 1:
                place(len(big) + 1, functools.partial(finish, r, 0))
                place(len(big) + 3, functools.partial(finish, r, 1))
    side = [(None, relaxed.get(tick)) for tick in range(max(relaxed, default=-1) + 1)]
    _emit_pipelined(big, side, tasks)


def _full(shape):
    return pl.BlockSpec(shape, lambda *_: (0,) * len(shape))


def kernel(x, c, ctx, c_ctx, w_ada, b_ada, norm_w, w_in, ret_decay_f, ret_decay_b, ret_gn_w,
           gmlp_ln_w, w_s, b_s, w_out, final_norm_w):
    bsz, seq, d = x.shape
    ctx_len = ctx.shape[1]
    depth = w_ada.shape[0]
    assert depth == 1 and d % LANES == 0
    assert seq % TILE_KV == 0 and seq % TILE_MAIN == 0 and ctx_len % CHUNK == 0
    assert seq % GRID_W == 0 and TILE_MAIN % (4 * CHUNK) == 0 and TILE_KV % (4 * CHUNK) == 0
    n_chunks = seq // CHUNK
    d3 = 3 * d
    n_kv = QK_W + RET_W
    n_rest = w_in.shape[2] - n_kv
    n_gates = n_rest - QK_W
    assert QK_W + n_kv == d and n_gates % d == 0 and d3 // d == 1 + n_gates // d
    assert _col_groups(n_rest)[0] == (0, QK_W)

    quarter = DK // 4
    inv_freq = ROPE_BASE ** (-jnp.arange(quarter, dtype=F32) / quarter)
    invf_lane = jnp.tile(inv_freq, LANES // quarter)[None, :]
    nw = norm_w[0][None, :]
    rows2 = 2 * bsz

    n_col_blocks = d3 // d
    smem = pl.BlockSpec(memory_space=pltpu.SMEM)
    n_out_padded = d + QK_W
    (mod, cos_t, sin_t, dsum, af, ab, kdft, kdbt, gf, gb,
     w_rest, w_kv, w_out_b, w_s_b, b_s_full) = pl.pallas_call(
        _prologue_kernel,
        grid=(n_col_blocks,),
        in_specs=[smem, smem, _full((bsz, d)), _full((1, d)),
                  pl.BlockSpec((d, d), lambda i: (0, i)),
                  pl.BlockSpec((1, d), lambda i: (0, i)),
                  _full((1, LANES)),
                  pl.BlockSpec((d, d), lambda i: (0, i)),
                  _full((RET_W + GM_W, d)),
                  _full((GROUPS, CHUNK, CHUNK)), _full((GROUPS, CHUNK))],
        out_specs=[pl.BlockSpec((rows2, d), lambda i: (0, i)),
                   _full((seq, LANES)), _full((seq, LANES)),
                   _full((HEADS, CHUNK, CHUNK)),
                   _full((CHUNK, QK_W)), _full((CHUNK, QK_W)),
                   _full((QK_W, CHUNK)), _full((QK_W, CHUNK)),
                   _full((QK_W, DV)), _full((QK_W, DV)),
                   _full((d, n_rest)), _full((d, n_kv)),
                   _full((RET_W + GM_W, n_out_padded)),
                   _full((GROUPS, CHUNK, CHUNK)), _full((GROUPS, CHUNK, DG))],
        out_shape=[jax.ShapeDtypeStruct((rows2, d3), F32),
                   jax.ShapeDtypeStruct((seq, LANES), F32),
                   jax.ShapeDtypeStruct((seq, LANES), F32),
                   jax.ShapeDtypeStruct((HEADS, CHUNK, CHUNK), F32),
                   jax.ShapeDtypeStruct((CHUNK, QK_W), F32),
                   jax.ShapeDtypeStruct((CHUNK, QK_W), F32),
                   jax.ShapeDtypeStruct((QK_W, CHUNK), F32),
                   jax.ShapeDtypeStruct((QK_W, CHUNK), F32),
                   jax.ShapeDtypeStruct((QK_W, DV), F32),
                   jax.ShapeDtypeStruct((QK_W, DV), F32),
                   jax.ShapeDtypeStruct((d, n_rest), BF16),
                   jax.ShapeDtypeStruct((d, n_kv), BF16),
                   jax.ShapeDtypeStruct((RET_W + GM_W, n_out_padded), BF16),
                   jax.ShapeDtypeStruct((GROUPS, CHUNK, CHUNK), BF16),
                   jax.ShapeDtypeStruct((GROUPS, CHUNK, DG), F32)],
        compiler_params=pltpu.CompilerParams(dimension_semantics=("arbitrary",)),
        name="prologue",
    )(ret_decay_f[0], ret_decay_b[0], c, c_ctx[None, :], w_ada[0], b_ada[0][None, :], invf_lane,
      w_in[0], w_out[0], w_s[0], b_s[0])

    s_cf, s_cb = pl.pallas_call(
        functools.partial(_ctx_kernel, bsz),
        grid=(bsz + 1,),
        in_specs=[pl.BlockSpec((1, ctx_len, d), lambda s: (jnp.minimum(s, bsz - 1), 0, 0)),
                  _full((rows2, d3)),
                  _full((1, d)), _full((d, n_kv)),
                  _full((QK_W, CHUNK)), _full((QK_W, CHUNK)),
                  _full((QK_W, DV)), _full((QK_W, DV))],
        out_specs=[pl.BlockSpec((1, QK_W, DV), lambda s: (jnp.maximum(s - 1, 0), 0, 0)),
                   pl.BlockSpec((1, QK_W, DV), lambda s: (jnp.maximum(s - 1, 0), 0, 0))],
        out_shape=[jax.ShapeDtypeStruct((bsz, QK_W, DV), F32),
                   jax.ShapeDtypeStruct((bsz, QK_W, DV), F32)],
        scratch_shapes=[pltpu.VMEM((ctx_len, d), BF16),
                        pltpu.VMEM((ctx_len, n_kv), F32),
                        pltpu.VMEM((ctx_len, n_kv), F32)],
        compiler_params=pltpu.CompilerParams(dimension_semantics=("arbitrary",)),
        name="ctx_states",
    )(ctx, mod, nw, w_kv, kdft, kdbt, gf, gb)

    def tile_maps(n_tiles, per_batch, reverse):
        def where(t):
            j = t % per_batch
            return t // per_batch, (per_batch - 1 - j) if reverse else j
        return [lambda s, k=k: where(jnp.clip(s - k, 0, n_tiles - 1)) for k in range(3)]

    nt_kv = seq // TILE_KV
    cpt_kv = TILE_KV // CHUNK
    cur, prv, _ = tile_maps(bsz * nt_kv, nt_kv, reverse=True)
    h, kt, v, st_b = pl.pallas_call(
        functools.partial(_kv_kernel, bsz * nt_kv, nt_kv),
        grid=(bsz * nt_kv + 1,),
        in_specs=[pl.BlockSpec(memory_space=pl.ANY),
                  _full((rows2, d3)),
                  _full((1, d)), _full((d, n_kv)),
                  pl.BlockSpec((TILE_KV, LANES), lambda s: (prv(s)[1], 0)),
                  pl.BlockSpec((TILE_KV, LANES), lambda s: (prv(s)[1], 0)),
                  _full((QK_W, CHUNK)), _full((QK_W, DV)),
                  pl.BlockSpec((1, QK_W, DV), lambda s: (prv(s)[0], 0, 0))],
        out_specs=[pl.BlockSpec((1, TILE_KV, d), lambda s: (*cur(s), 0)),
                   pl.BlockSpec((1, cpt_kv, QK_W, CHUNK), lambda s: (*prv(s), 0, 0)),
                   pl.BlockSpec((1, TILE_KV, RET_W), lambda s: (*prv(s), 0)),
                   pl.BlockSpec((1, cpt_kv, QK_W, DV), lambda s: (*prv(s), 0, 0))],
        out_shape=[jax.ShapeDtypeStruct((bsz, seq, d), BF16),
                   jax.ShapeDtypeStruct((bsz, n_chunks, QK_W, CHUNK), BF16),
                   jax.ShapeDtypeStruct((bsz, seq, RET_W), BF16),
                   jax.ShapeDtypeStruct((bsz, n_chunks, QK_W, DV), BF16)],
        scratch_shapes=[pltpu.VMEM((QK_W, DV), F32),
                        pltpu.VMEM((TILE_KV, n_kv), F32),
                        pltpu.VMEM((TILE_KV, n_kv), F32),
                        pltpu.VMEM((X_RING, TILE_KV, d), F32),
                        pltpu.SemaphoreType.DMA((X_RING,))],
        compiler_params=pltpu.CompilerParams(dimension_semantics=("arbitrary",)),
        name="kv_sweep",
    )(x, mod, nw, w_kv, cos_t, sin_t, kdbt, gb, s_cb)

    nt = seq // TILE_MAIN
    cpt = TILE_MAIN // CHUNK
    cur, prv, prv2 = tile_maps(bsz * nt, nt, reverse=False)
    out = pl.pallas_call(
        functools.partial(_main_kernel, bsz * nt, nt),
        grid=(bsz * nt + 2,),
        in_specs=[pl.BlockSpec((1, TILE_MAIN, d), lambda s: (*cur(s), 0)),
                  pl.BlockSpec((1, TILE_MAIN, d), lambda s: (*prv2(s), 0)),
                  _full((rows2, d3)),
                  _full((d, n_rest)),
                  pl.BlockSpec((TILE_MAIN, LANES), lambda s: (prv(s)[1], 0)),
                  pl.BlockSpec((TILE_MAIN, LANES), lambda s: (prv(s)[1], 0)),
                  pl.BlockSpec((1, cpt, QK_W, CHUNK), lambda s: (*prv(s), 0, 0)),
                  pl.BlockSpec((1, TILE_MAIN, RET_W), lambda s: (*prv(s), 0)),
                  pl.BlockSpec((1, cpt, QK_W, DV), lambda s: (*prv(s), 0, 0)),
                  pl.BlockSpec((1, QK_W, DV), lambda s: (prv(s)[0], 0, 0)),
                  _full((HEADS, CHUNK, CHUNK)),
                  _full((CHUNK, QK_W)), _full((CHUNK, QK_W)),
                  _full((QK_W, CHUNK)), _full((QK_W, DV)),
                  _full((1, RET_W)), _full((1, GM_W)),
                  _full((GROUPS, CHUNK, CHUNK)), _full((GROUPS, CHUNK, DG)),
                  _full((RET_W + GM_W, n_out_padded)), _full((1, d))],
        out_specs=pl.BlockSpec((1, TILE_MAIN, d), lambda s: (*prv2(s), 0)),
        out_shape=jax.ShapeDtypeStruct((bsz, seq, d), x.dtype),
        scratch_shapes=[pltpu.VMEM((QK_W, DV), F32),
                        pltpu.VMEM((TILE_MAIN, n_rest), F32),
                        pltpu.VMEM((TILE_MAIN, n_rest), F32),
                        pltpu.VMEM((TILE_MAIN, RET_W + GM_W), BF16),
                        pltpu.VMEM((TILE_MAIN, RET_W + GM_W), BF16)],
        compiler_params=pltpu.CompilerParams(dimension_semantics=("arbitrary",)),
        name="main_sweep",
    )(h, x, mod, w_rest, cos_t, sin_t, kt, v, st_b, s_cf, dsum, af, ab, kdft, gf,
      ret_gn_w[0][None, :], gmlp_ln_w[0][None, :], w_s_b, b_s_full, w_out_b,
      final_norm_w[None, :])
    return out
```

```python
import functools

import jax
import jax.numpy as jnp
from jax import lax
from jax.experimental import pallas as pl
from jax.experimental.pallas import tpu as pltpu

F32 = jnp.float32
BF16 = jnp.bfloat16

HEADS = 4
DK = 64
DV = 128
CHUNK = 128
GROUPS = 4
DG = 128
GRID_W = 64
ROPE_BASE = 10000.0
EPS = 1e-6
LANES = 128
PAIRS = HEADS * DK // LANES
RET_W = HEADS * DV
GM_W = GROUPS * DG
QK_W = HEADS * DK

DOT_COLS = 512
TILE_KV = 2048
X_RING = 3
TILE_MAIN = 512


def _rms(x, w):
    ms = jnp.mean(x * x, axis=-1, keepdims=True)
    return x * lax.rsqrt(ms + EPS) * w


def _silu(x):
    return x * jax.nn.sigmoid(x)


def _gelu(x):
    return 0.5 * x * (1.0 + lax.erf(x * (0.5 ** 0.5)))


_dot = functools.partial(jnp.dot, preferred_element_type=F32)


def _col_groups(n):
    first = n % DOT_COLS
    edges = ([0] if first == 0 else [0, first]) + list(range(first + DOT_COLS, n + 1, DOT_COLS))
    return list(zip(edges[:-1], edges[1:]))


def _rope(p, cos, sin_signed, upper16):
    partner = jnp.where(upper16, pltpu.roll(p, 16, 1), pltpu.roll(p, LANES - 16, 1))
    return p * cos + partner * sin_signed


def _kv_pairs(kd, v):
    out = []
    for p in range(PAIRS):
        r = _dot(kd[p * LANES:(p + 1) * LANES, :], v[:, p * 2 * DV:(p + 1) * 2 * DV])
        out.append(r[0:DK, 0:DV])
        out.append(r[DK:2 * DK, DV:2 * DV])
    return jnp.concatenate(out, axis=0)


def _emit_pipelined(big_pieces, side_pieces, tasks):
    in_flight = []
    last_tick = max([start + parts for start, _, parts in tasks] +
                    [len(big_pieces), len(side_pieces)])
    for tick in range(last_tick):
        if tick < len(big_pieces):
            big_pieces[tick]()
        urgent, relaxed = side_pieces[tick] if tick < len(side_pieces) else (None, None)
        if urgent is not None:
            urgent()
        in_flight += [[gen, parts] for start, gen, parts in tasks if start == tick]
        for t in [t for t in in_flight if t[1] > 1] + [t for t in in_flight if t[1] == 1]:
            next(t[0], None)
            t[1] -= 1
        in_flight = [t for t in in_flight if t[1] > 0]
        if relaxed is not None:
            relaxed()
    assert not in_flight


def _run_pipeline_step(step, n_tiles, n_stages, body, refs, bufs_a, bufs_b):
    def call(s_static_parity, stages):
        new, old = (bufs_a, bufs_b) if s_static_parity == 0 else (bufs_b, bufs_a)
        body(stages, *refs, *new, *old)

    edge = n_stages - 1
    for s in list(range(edge)) + list(range(n_tiles, n_tiles + edge)):
        stages = tuple(0 <= s - j < n_tiles for j in range(n_stages))
        pl.when(step == s)(functools.partial(call, s % 2, stages))
    steady = (step >= edge) & (step < n_tiles)
    for parity in range(2):
        pl.when(steady & (lax.rem(step, 2) == parity))(
            functools.partial(call, parity, (True,) * n_stages))


def _prologue_kernel(decf_ref, decb_ref, c_ref, cctx_ref, w_ref, b_ref, invf_ref, win_ref,
                     wout_ref, ws_ref, bs_ref,
                     mod_ref, cos_ref, sin_ref, dsum_ref, af_ref, ab_ref,
                     kdft_ref, kdbt_ref, gf_ref, gb_ref,
                     wrest_ref, wkv_ref, woutb_ref, wsb_ref, bsf_ref):
    i = pl.program_id(0)
    d = win_ref.shape[0]
    c = c_ref[...]
    cond = jnp.concatenate([c, jnp.broadcast_to(cctx_ref[...], c.shape)], axis=0)
    mod_ref[...] = _dot(_silu(cond), w_ref[...]) + b_ref[...]

    @pl.when(i == 0)
    def _first_block():
        wrest_ref[:, 0:QK_W] = win_ref[:, 0:QK_W].astype(BF16)
        wkv_ref[...] = win_ref[:, QK_W:].astype(BF16)
        woutb_ref[:, 0:d] = wout_ref[...].astype(BF16)
        woutb_ref[:, d:] = jnp.zeros((woutb_ref.shape[0], woutb_ref.shape[1] - d), BF16)
        wsb_ref[...] = ws_ref[...].astype(BF16)
        for g in range(GROUPS):
            bsf_ref[g] = jnp.broadcast_to(bs_ref[g:g + 1, :], (CHUNK, CHUNK)).T

    for blk in range(1, (wrest_ref.shape[1] - QK_W) // d + 1):
        @pl.when(i == blk)
        def _later_block(blk=blk):
            wrest_ref[:, QK_W + (blk - 1) * d:QK_W + blk * d] = win_ref[...].astype(BF16)

    @pl.when(i == 0)
    def _tables():
        pos = lax.broadcasted_iota(jnp.int32, (GRID_W, LANES), 0).astype(F32)
        lane = lax.broadcasted_iota(jnp.int32, (GRID_W, LANES), 1)
        ang = pos * invf_ref[...]
        cr = jnp.cos(ang)
        sr = jnp.sin(ang)
        sr = jnp.where((lane & 16) == 0, -sr, sr)
        by_row = (lane & 32) == 0
        n_rows = cos_ref.shape[0] // GRID_W
        for r in range(n_rows):
            sl = slice(r * GRID_W, (r + 1) * GRID_W)
            cos_ref[sl, :] = jnp.where(by_row, jnp.broadcast_to(cr[r:r + 1, :], cr.shape), cr)
            sin_ref[sl, :] = jnp.where(by_row, jnp.broadcast_to(sr[r:r + 1, :], sr.shape), sr)

        def log_decay(head_idx, dec_ref):
            d = jnp.full(head_idx.shape, dec_ref[HEADS - 1], F32)
            for h in range(HEADS - 2, -1, -1):
                d = jnp.where(head_idx == h, dec_ref[h], d)
            return -jnp.exp(d)

        n = lax.broadcasted_iota(jnp.int32, (CHUNK, QK_W), 0).astype(F32)
        hl = lax.broadcasted_iota(jnp.int32, (CHUNK, QK_W), 1) // DK
        af_ref[...] = jnp.exp((n + 1.0) * log_decay(hl, decf_ref))
        ab_ref[...] = jnp.exp((CHUNK - n) * log_decay(hl, decb_ref))

        m = lax.broadcasted_iota(jnp.int32, (QK_W, CHUNK), 1).astype(F32)
        hr = lax.broadcasted_iota(jnp.int32, (QK_W, CHUNK), 0) // DK
        lgf = log_decay(hr, decf_ref)
        lgb = log_decay(hr, decb_ref)
        kdft_ref[...] = jnp.exp((CHUNK - 1.0 - m) * lgf)
        kdbt_ref[...] = jnp.exp(m * lgb)
        gf_ref[...] = jnp.exp(float(CHUNK) * lgf)
        gb_ref[...] = jnp.exp(float(CHUNK) * lgb)

        nn = lax.broadcasted_iota(jnp.int32, (CHUNK, CHUNK), 0).astype(F32)
        mm = lax.broadcasted_iota(jnp.int32, (CHUNK, CHUNK), 1).astype(F32)
        for h in range(HEADS):
            lf = -jnp.exp(jnp.full((CHUNK, CHUNK), decf_ref[h], F32))
            lb = -jnp.exp(jnp.full((CHUNK, CHUNK), decb_ref[h], F32))
            df = nn - mm
            db = mm - nn
            fwd = jnp.where(df >= 0, jnp.exp(jnp.maximum(df, 0.0) * lf), 0.0)
            bwd = jnp.where(db >= 0, jnp.exp(jnp.maximum(db, 0.0) * lb), 0.0)
            dsum_ref[h] = fwd + bwd


def _ctx_kernel(n_tiles, ctx_ref, mod_ref, nw_ref, wkv_ref, kdft_ref, kdbt_ref, gf_ref, gb_ref,
                scf_ref, scb_ref, h_ref, pa_ref, pb_ref):
    refs = (ctx_ref, mod_ref, nw_ref, wkv_ref, kdft_ref, kdbt_ref, gf_ref, gb_ref,
            scf_ref, scb_ref, h_ref)
    _run_pipeline_step(pl.program_id(0), n_tiles, 2, _ctx_step, refs, (pa_ref,), (pb_ref,))


def _ctx_step(stages, ctx_ref, mod_ref, nw_ref, wkv_ref, kdft_ref, kdbt_ref, gf_ref, gb_ref,
              scf_ref, scb_ref, h_ref, p_new_ref, p_old):
    project_stage, state_stage = stages
    d = ctx_ref.shape[-1]
    n_chunks = ctx_ref.shape[1] // CHUNK
    ctx_row = mod_ref.shape[0] // 2
    mod = mod_ref[ctx_row:ctx_row + 1, :]

    def norm_rows(rows):
        h = _rms(ctx_ref[0, rows, :], nw_ref[...]) * (1.0 + mod[:, d:2 * d]) + mod[:, 0:d]
        h_ref[rows, :] = h.astype(BF16)

    def project(lo, hi):
        p_new_ref[:, lo:hi] = _dot(h_ref[...], wkv_ref[:, lo:hi])

    new_f, new_b = {}, {}

    def chunk(c):
        sl = slice(c * CHUNK, (c + 1) * CHUNK)
        kt = (p_old[sl, 0:QK_W] * (DK ** -0.5)).T
        v = p_old[sl, QK_W:].astype(BF16)
        new_f[c] = _kv_pairs((kt * kdft_ref[...]).astype(BF16), v)
        new_b[c] = _kv_pairs((kt * kdbt_ref[...]).astype(BF16), v)
        yield

    def combine():
        yield
        sf = jnp.zeros((QK_W, DV), F32)
        for c in range(n_chunks):
            sf = gf_ref[...] * sf + new_f[c]
        sb = jnp.zeros((QK_W, DV), F32)
        for c in range(n_chunks - 1, -1, -1):
            sb = gb_ref[...] * sb + new_b[c]
        scf_ref[0] = sf
        scb_ref[0] = sb

    big, tasks = [], []
    if project_stage:
        for c in range(n_chunks):
            norm_rows(slice(c * CHUNK, (c + 1) * CHUNK))
        big = [functools.partial(project, lo, hi) for lo, hi in _col_groups(p_new_ref.shape[1])]
    if state_stage:
        tasks = [(c, chunk(c), 1) for c in range(n_chunks)] + [(n_chunks - 1, combine(), 2)]
    _emit_pipelined(big, [], tasks)


def _kv_kernel(n_tiles, tiles_per_batch, x_hbm, modc_ref, nw_ref, wkv_ref, cos_ref, sin_ref,
               kdbt_ref, gb_ref, scb_ref, h_ref, kt_ref, v_ref, stb_ref, sb_ref,
               pka_ref, pkb_ref, pva_ref, pvb_ref, xbuf_ref, xsem_ref):
    s = pl.program_id(0)
    tile = xbuf_ref.shape[1]

    def x_copy(t):
        b = t // tiles_per_batch
        j = tiles_per_batch - 1 - lax.rem(t, tiles_per_batch)
        slot = lax.rem(t, X_RING)
        return pltpu.make_async_copy(x_hbm.at[b, pl.ds(j * tile, tile), :],
                                     xbuf_ref.at[slot], xsem_ref.at[slot])

    @pl.when(s == 0)
    def _prime():
        for t in range(min(X_RING - 1, n_tiles)):
            x_copy(t).start()

    @pl.when(s < n_tiles)
    def _arrived():
        x_copy(s).wait()

    @pl.when(s + X_RING - 1 < n_tiles)
    def _prefetch():
        x_copy(s + X_RING - 1).start()

    @pl.when((s >= 1) & (lax.rem(s - 1, tiles_per_batch) == 0))
    def _init():
        sb_ref[...] = scb_ref[0]

    batch = jnp.minimum(s, n_tiles - 1) // tiles_per_batch
    modc = modc_ref[pl.ds(batch, 1), :]
    xc_ref = xbuf_ref.at[lax.rem(s, X_RING)]
    refs = (xc_ref, modc, nw_ref, wkv_ref, cos_ref, sin_ref, kdbt_ref, gb_ref,
            h_ref, kt_ref, v_ref, stb_ref, sb_ref)
    _run_pipeline_step(s, n_tiles, 2, _kv_step, refs, (pka_ref, pva_ref), (pkb_ref, pvb_ref))


def _kv_step(stages, xc_ref, modc, nw_ref, wkv_ref, cos_ref, sin_ref, kdbt_ref, gb_ref,
             h_ref, kt_ref, v_ref, stb_ref, sb_ref, pk_new_ref, pv_new_ref, pk_old, pv_old):
    project_stage, finish_stage = stages
    d = xc_ref.shape[-1]
    tile = xc_ref.shape[0]
    half = tile // 2
    row_halves = [slice(0, half), slice(half, tile)]
    quarter = [slice(i * half // 2, (i + 1) * half // 2) for i in range(4)]
    upper16 = (lax.broadcasted_iota(jnp.int32, (CHUNK, LANES), 1) & 16) != 0

    def norm_rows(rows):
        h = _rms(xc_ref[rows, :], nw_ref[...]) * (1.0 + modc[:, d:2 * d]) + modc[:, 0:d]
        h_ref[0, rows, :] = h.astype(BF16)

    def project(rows, lo, hi):
        out = _dot(h_ref[0, rows, :], wkv_ref[:, lo:hi])
        if hi <= QK_W:
            pk_new_ref[rows, lo:hi] = out
        else:
            pv_new_ref[rows, lo - QK_W:hi - QK_W] = out.astype(BF16)

    def chunk(c):
        sl = slice(c * CHUNK, (c + 1) * CHUNK)
        k = [_rope(pk_old[sl, p * LANES:(p + 1) * LANES], cos_ref[sl, :], sin_ref[sl, :], upper16)
             * (DK ** -0.5) for p in range(PAIRS)]
        v = pv_old[sl, :]
        v_ref[0, sl, :] = v
        kt = jnp.concatenate([kp.T for kp in k], axis=0)
        kt_ref[0, c] = kt.astype(BF16)
        new = _kv_pairs((kt * kdbt_ref[...]).astype(BF16), v)
        yield
        stb_ref[0, c] = sb_ref[...].astype(BF16)
        sb_ref[...] = gb_ref[...] * sb_ref[...] + new

    projections, late_norms, tasks = [], [], []
    if project_stage:
        norm_rows(quarter[0])
        norm_rows(quarter[1])
        projections = [functools.partial(project, row_halves[r], lo, hi)
                       for r in range(2) for lo, hi in _col_groups(wkv_ref.shape[1])]
        late_norms = [(functools.partial(norm_rows, q), None) for q in quarter[2:]]
    if finish_stage:
        tasks = [(tick, chunk(c), 2) for tick, c in enumerate(range(tile // CHUNK - 1, -1, -1))]
    _emit_pipelined(projections, late_norms, tasks)


def _main_kernel(n_tiles, tiles_per_batch, h_ref, x_ref, mod_ref, w_ref, cos_ref,
                 sin_ref, kt_ref, v_ref, stb_ref, scf_ref, dsum_ref, af_ref, ab_ref, kdft_ref,
                 gf_ref, gnw_ref, lnw_ref, ws_ref, bs_ref, wout_ref, fnw_ref, o_ref,
                 sf_ref, pa_ref, pb_ref, ya_ref, yb_ref):
    s = pl.program_id(0)

    @pl.when((s >= 1) & (lax.rem(s - 1, tiles_per_batch) == 0))
    def _init():
        sf_ref[...] = scf_ref[0]

    batch = jnp.clip(s - 2, 0, n_tiles - 1) // tiles_per_batch
    mod = mod_ref[pl.ds(batch, 1), :]
    refs = (h_ref, x_ref, mod, w_ref, cos_ref, sin_ref, kt_ref, v_ref,
            stb_ref, dsum_ref, af_ref, ab_ref, kdft_ref, gf_ref, gnw_ref, lnw_ref, ws_ref,
            bs_ref, wout_ref, fnw_ref, o_ref, sf_ref)
    _run_pipeline_step(s, n_tiles, 3, _main_step, refs, (pa_ref, ya_ref), (pb_ref, yb_ref))


def _main_step(stages, h_ref, x_ref, mod, w_ref, cos_ref, sin_ref, kt_ref, v_ref,
               stb_ref, dsum_ref, af_ref, ab_ref, kdft_ref, gf_ref, gnw_ref, lnw_ref, ws_ref,
               bs_ref, wout_ref, fnw_ref, o_ref, sf_ref, p_new_ref, y_ref, p_all, y_old):
    project_stage, mix_stage, out_stage = stages
    d = x_ref.shape[-1]
    tile = x_ref.shape[1]
    n_chunks = tile // CHUNK
    half = tile // 2
    row_halves = [slice(0, half), slice(half, tile)]
    c_gr = QK_W
    c_u = c_gr + RET_W
    c_vm = c_u + GM_W
    c_gm = c_vm + GM_W
    lane = lax.broadcasted_iota(jnp.int32, (CHUNK, LANES), 1)
    upper16 = (lane & 16) != 0
    head_of_lane = lane // DK

    operand = {}

    def loaded_once(key, load):
        if key not in operand:
            operand[key] = load()
        return operand[key]

    def project(r, lo, hi):
        rows = row_halves[r]
        lhs = loaded_once(("h", r), lambda: h_ref[0, rows, :])
        p_new_ref[rows, lo:hi] = _dot(lhs, w_ref[:, lo:hi])

    def retention(c, p):
        sl = slice(c * CHUNK, (c + 1) * CHUNK)
        ps = slice(p * LANES, (p + 1) * LANES)
        q2 = _rope(p_all[sl, ps], cos_ref[sl, :], sin_ref[sl, :], upper16)
        kt = kt_ref[0, c, ps, :]
        vp = v_ref[0, sl, p * 2 * DV:(p + 1) * 2 * DV]
        qa = q2 * af_ref[:, ps]
        qb = q2 * ab_ref[:, ps]
        scores, qx = [], []
        for hh in range(2):
            mine = head_of_lane == hh
            scores.append(_dot(jnp.where(mine, q2, 0.0).astype(BF16), kt))
            qx.append(jnp.concatenate([jnp.where(mine, qa, 0.0), jnp.where(mine, qb, 0.0)],
                                      axis=1).astype(BF16))
        kd = (kt.astype(F32) * kdft_ref[ps, :]).astype(BF16)
        yield
        state = jnp.concatenate([sf_ref[ps, :].astype(BF16), stb_ref[0, c, ps, :]], axis=0)
        o = []
        for hh in range(2):
            a = (scores[hh] * dsum_ref[2 * p + hh]).astype(BF16)
            o.append(_dot(a, vp[:, hh * DV:(hh + 1) * DV]) + _dot(qx[hh], state))
        r = _dot(kd, vp)
        yield
        new = jnp.concatenate([r[0:DK, 0:DV], r[DK:2 * DK, DV:2 * DV]], axis=0)
        sf_ref[ps, :] = gf_ref[ps, :] * sf_ref[ps, :] + new
        for hh in range(2):
            head = 2 * p + hh
            mu = jnp.mean(o[hh], axis=-1, keepdims=True)
            var = jnp.mean(jnp.square(o[hh] - mu), axis=-1, keepdims=True)
            hs = slice(head * DV, (head + 1) * DV)
            y = (o[hh] - mu) * lax.rsqrt(var + EPS) * gnw_ref[:, hs]
            gate_cols = slice(c_gr + head * DV, c_gr + (head + 1) * DV)
            y_ref[sl, hs] = (y * _silu(p_all[sl, gate_cols])).astype(BF16)

    def gating(c):
        sl = slice(c * CHUNK, (c + 1) * CHUNK)
        vg = _gelu(p_all[sl, c_vm:c_gm])
        mu = jnp.mean(vg, axis=-1, keepdims=True)
        var = jnp.mean(jnp.square(vg - mu), axis=-1, keepdims=True)
        yield
        vn = ((vg - mu) * lax.rsqrt(var + EPS) * lnw_ref[...]).astype(BF16)
        s = [_dot(ws_ref[g], vn[:, g * DG:(g + 1) * DG]) for g in range(GROUPS)]
        yield
        for g in range(GROUPS):
            if g == GROUPS // 2:
                yield
            u = _gelu(p_all[sl, c_u + g * DG:c_u + (g + 1) * DG])
            gm = _silu(p_all[sl, c_gm + g * DG:c_gm + (g + 1) * DG])
            y_ref[sl, RET_W + g * DG:RET_W + (g + 1) * DG] = (u * (s[g] + bs_ref[g]) * gm).astype(BF16)

    out_groups = _col_groups(d)
    xn = {}

    def out_project(r, g):
        rows = row_halves[r]
        lo, hi = out_groups[g]
        lhs = loaded_once(("y", r), lambda: y_old[rows, :])
        out = _dot(lhs, wout_ref[:, lo:hi])
        xn[r, g] = x_ref[0, rows, lo:hi] + mod[:, 2 * d + lo:2 * d + hi] * out

    def finish(r, part):
        sub = slice(part * half // 2, (part + 1) * half // 2)
        rows = slice(row_halves[r].start + sub.start, row_halves[r].start + sub.stop)
        parts = [xn[r, g][sub, :] for g in range(len(out_groups))]
        ms = sum(jnp.sum(t * t, axis=-1, keepdims=True) for t in parts) * (1.0 / d)
        rs = lax.rsqrt(ms + EPS)
        for (lo, hi), t in zip(out_groups, parts):
            o_ref[0, rows, lo:hi] = t * rs * fnw_ref[:, lo:hi]

    tasks = []
    if mix_stage:
        for c in range(n_chunks):
            base = 3 * c
            tasks += [(base + min(p, 1), retention(c, p), 3) for p in range(PAIRS)]
            tasks.append((base + 1, gating(c), 4))
    big, relaxed = [], {}

    def place(tick, piece):
        while tick in relaxed:
            tick += 1
        relaxed[tick] = piece

    for r in range(2):
        outs = [functools.partial(out_project, r, g) for g in range(len(out_groups))] if out_stage else []
        projs = ([functools.partial(project, r, lo, hi) for lo, hi in _col_groups(p_new_ref.shape[1])]
                 if project_stage else [])
        for i in range(max(len(outs), len(projs))):
            big += outs[i:i + 1] + projs[i:i + 1]
            if i == len(outs) - 1:
                place(len(big) + 1, functools.partial(finish, r, 0))
                place(len(big) + 3, functools.partial(finish, r, 1))
    side = [(None, relaxed.get(tick)) for tick in range(max(relaxed, default=-1) + 1)]
    _emit_pipelined(big, side, tasks)


def _full(shape):
    return pl.BlockSpec(shape, lambda *_: (0,) * len(shape))


def kernel(x, c, ctx, c_ctx, w_ada, b_ada, norm_w, w_in, ret_decay_f, ret_decay_b, ret_gn_w,
           gmlp_ln_w, w_s, b_s, w_out, final_norm_w):
    bsz, seq, d = x.shape
    ctx_len = ctx.shape[1]
    depth = w_ada.shape[0]
    assert depth == 1 and d % LANES == 0
    assert seq % TILE_KV == 0 and seq % TILE_MAIN == 0 and ctx_len % CHUNK == 0
    assert seq % GRID_W == 0 and TILE_MAIN % (4 * CHUNK) == 0 and TILE_KV % (4 * CHUNK) == 0
    n_chunks = seq // CHUNK
    d3 = 3 * d
    n_kv = QK_W + RET_W
    n_rest = w_in.shape[2] - n_kv
    n_gates = n_rest - QK_W
    assert QK_W + n_kv == d and n_gates % d == 0 and d3 // d == 1 + n_gates // d
    assert _col_groups(n_rest)[0] == (0, QK_W)

    quarter = DK // 4
    inv_freq = ROPE_BASE ** (-jnp.arange(quarter, dtype=F32) / quarter)
    invf_lane = jnp.tile(inv_freq, LANES // quarter)[None, :]
    nw = norm_w[0][None, :]
    rows2 = 2 * bsz

    n_col_blocks = d3 // d
    smem = pl.BlockSpec(memory_space=pltpu.SMEM)
    n_out_padded = d + QK_W
    (mod, cos_t, sin_t, dsum, af, ab, kdft, kdbt, gf, gb,
     w_rest, w_kv, w_out_b, w_s_b, b_s_full) = pl.pallas_call(
        _prologue_kernel,
        grid=(n_col_blocks,),
        in_specs=[smem, smem, _full((bsz, d)), _full((1, d)),
                  pl.BlockSpec((d, d), lambda i: (0, i)),
                  pl.BlockSpec((1, d), lambda i: (0, i)),
                  _full((1, LANES)),
                  pl.BlockSpec((d, d), lambda i: (0, i)),
                  _full((RET_W + GM_W, d)),
                  _full((GROUPS, CHUNK, CHUNK)), _full((GROUPS, CHUNK))],
        out_specs=[pl.BlockSpec((rows2, d), lambda i: (0, i)),
                   _full((seq, LANES)), _full((seq, LANES)),
                   _full((HEADS, CHUNK, CHUNK)),
                   _full((CHUNK, QK_W)), _full((CHUNK, QK_W)),
                   _full((QK_W, CHUNK)), _full((QK_W, CHUNK)),
                   _full((QK_W, DV)), _full((QK_W, DV)),
                   _full((d, n_rest)), _full((d, n_kv)),
                   _full((RET_W + GM_W, n_out_padded)),
                   _full((GROUPS, CHUNK, CHUNK)), _full((GROUPS, CHUNK, DG))],
        out_shape=[jax.ShapeDtypeStruct((rows2, d3), F32),
                   jax.ShapeDtypeStruct((seq, LANES), F32),
                   jax.ShapeDtypeStruct((seq, LANES), F32),
                   jax.ShapeDtypeStruct((HEADS, CHUNK, CHUNK), F32),
                   jax.ShapeDtypeStruct((CHUNK, QK_W), F32),
                   jax.ShapeDtypeStruct((CHUNK, QK_W), F32),
                   jax.ShapeDtypeStruct((QK_W, CHUNK), F32),
                   jax.ShapeDtypeStruct((QK_W, CHUNK), F32),
                   jax.ShapeDtypeStruct((QK_W, DV), F32),
                   jax.ShapeDtypeStruct((QK_W, DV), F32),
                   jax.ShapeDtypeStruct((d, n_rest), BF16),
                   jax.ShapeDtypeStruct((d, n_kv), BF16),
                   jax.ShapeDtypeStruct((RET_W + GM_W, n_out_padded), BF16),
                   jax.ShapeDtypeStruct((GROUPS, CHUNK, CHUNK), BF16),
                   jax.ShapeDtypeStruct((GROUPS, CHUNK, DG), F32)],
        compiler_params=pltpu.CompilerParams(dimension_semantics=("arbitrary",)),
        name="prologue",
    )(ret_decay_f[0], ret_decay_b[0], c, c_ctx[None, :], w_ada[0], b_ada[0][None, :], invf_lane,
      w_in[0], w_out[0], w_s[0], b_s[0])

    s_cf, s_cb = pl.pallas_call(
        functools.partial(_ctx_kernel, bsz),
        grid=(bsz + 1,),
        in_specs=[pl.BlockSpec((1, ctx_len, d), lambda s: (jnp.minimum(s, bsz - 1), 0, 0)),
                  _full((rows2, d3)),
                  _full((1, d)), _full((d, n_kv)),
                  _full((QK_W, CHUNK)), _full((QK_W, CHUNK)),
                  _full((QK_W, DV)), _full((QK_W, DV))],
        out_specs=[pl.BlockSpec((1, QK_W, DV), lambda s: (jnp.maximum(s - 1, 0), 0, 0)),
                   pl.BlockSpec((1, QK_W, DV), lambda s: (jnp.maximum(s - 1, 0), 0, 0))],
        out_shape=[jax.ShapeDtypeStruct((bsz, QK_W, DV), F32),
                   jax.ShapeDtypeStruct((bsz, QK_W, DV), F32)],
        scratch_shapes=[pltpu.VMEM((ctx_len, d), BF16),
                        pltpu.VMEM((ctx_len, n_kv), F32),
                        pltpu.VMEM((ctx_len, n_kv), F32)],
        compiler_params=pltpu.CompilerParams(dimension_semantics=("arbitrary",)),
        name="ctx_states",
    )(ctx, mod, nw, w_kv, kdft, kdbt, gf, gb)

    def tile_maps(n_tiles, per_batch, reverse):
        def where(t):
            j = t % per_batch
            return t // per_batch, (per_batch - 1 - j) if reverse else j
        return [lambda s, k=k: where(jnp.clip(s - k, 0, n_tiles - 1)) for k in range(3)]

    nt_kv = seq // TILE_KV
    cpt_kv = TILE_KV // CHUNK
    cur, prv, _ = tile_maps(bsz * nt_kv, nt_kv, reverse=True)
    h, kt, v, st_b = pl.pallas_call(
        functools.partial(_kv_kernel, bsz * nt_kv, nt_kv),
        grid=(bsz * nt_kv + 1,),
        in_specs=[pl.BlockSpec(memory_space=pl.ANY),
                  _full((rows2, d3)),
                  _full((1, d)), _full((d, n_kv)),
                  pl.BlockSpec((TILE_KV, LANES), lambda s: (prv(s)[1], 0)),
                  pl.BlockSpec((TILE_KV, LANES), lambda s: (prv(s)[1], 0)),
                  _full((QK_W, CHUNK)), _full((QK_W, DV)),
                  pl.BlockSpec((1, QK_W, DV), lambda s: (prv(s)[0], 0, 0))],
        out_specs=[pl.BlockSpec((1, TILE_KV, d), lambda s: (*cur(s), 0)),
                   pl.BlockSpec((1, cpt_kv, QK_W, CHUNK), lambda s: (*prv(s), 0, 0)),
                   pl.BlockSpec((1, TILE_KV, RET_W), lambda s: (*prv(s), 0)),
                   pl.BlockSpec((1, cpt_kv, QK_W, DV), lambda s: (*prv(s), 0, 0))],
        out_shape=[jax.ShapeDtypeStruct((bsz, seq, d), BF16),
                   jax.ShapeDtypeStruct((bsz, n_chunks, QK_W, CHUNK), BF16),
                   jax.ShapeDtypeStruct((bsz, seq, RET_W), BF16),
                   jax.ShapeDtypeStruct((bsz, n_chunks, QK_W, DV), BF16)],
        scratch_shapes=[pltpu.VMEM((QK_W, DV), F32),
                        pltpu.VMEM((TILE_KV, QK_W), F32),
                        pltpu.VMEM((TILE_KV, QK_W), F32),
                        pltpu.VMEM((TILE_KV, RET_W), BF16),
                        pltpu.VMEM((TILE_KV, RET_W), BF16),
                        pltpu.VMEM((X_RING, TILE_KV, d), F32),
                        pltpu.SemaphoreType.DMA((X_RING,))],
        compiler_params=pltpu.CompilerParams(dimension_semantics=("arbitrary",)),
        name="kv_sweep",
    )(x, mod, nw, w_kv, cos_t, sin_t, kdbt, gb, s_cb)

    nt = seq // TILE_MAIN
    cpt = TILE_MAIN // CHUNK
    cur, prv, prv2 = tile_maps(bsz * nt, nt, reverse=False)
    out = pl.pallas_call(
        functools.partial(_main_kernel, bsz * nt, nt),
        grid=(bsz * nt + 2,),
        in_specs=[pl.BlockSpec((1, TILE_MAIN, d), lambda s: (*cur(s), 0)),
                  pl.BlockSpec((1, TILE_MAIN, d), lambda s: (*prv2(s), 0)),
                  _full((rows2, d3)),
                  _full((d, n_rest)),
                  pl.BlockSpec((TILE_MAIN, LANES), lambda s: (prv(s)[1], 0)),
                  pl.BlockSpec((TILE_MAIN, LANES), lambda s: (prv(s)[1], 0)),
                  pl.BlockSpec((1, cpt, QK_W, CHUNK), lambda s: (*prv(s), 0, 0)),
                  pl.BlockSpec((1, TILE_MAIN, RET_W), lambda s: (*prv(s), 0)),
                  pl.BlockSpec((1, cpt, QK_W, DV), lambda s: (*prv(s), 0, 0)),
                  pl.BlockSpec((1, QK_W, DV), lambda s: (prv(s)[0], 0, 0)),
                  _full((HEADS, CHUNK, CHUNK)),
                  _full((CHUNK, QK_W)), _full((CHUNK, QK_W)),
                  _full((QK_W, CHUNK)), _full((QK_W, DV)),
                  _full((1, RET_W)), _full((1, GM_W)),
                  _full((GROUPS, CHUNK, CHUNK)), _full((GROUPS, CHUNK, DG)),
                  _full((RET_W + GM_W, n_out_padded)), _full((1, d))],
        out_specs=pl.BlockSpec((1, TILE_MAIN, d), lambda s: (*prv2(s), 0)),
        out_shape=jax.ShapeDtypeStruct((bsz, seq, d), x.dtype),
        scratch_shapes=[pltpu.VMEM((QK_W, DV), F32),
                        pltpu.VMEM((TILE_MAIN, n_rest), F32),
                        pltpu.VMEM((TILE_MAIN, n_rest), F32),
                        pltpu.VMEM((TILE_MAIN, RET_W + GM_W), BF16),
                        pltpu.VMEM((TILE_MAIN, RET_W + GM_W), BF16)],
        compiler_params=pltpu.CompilerParams(dimension_semantics=("arbitrary",)),
        name="main_sweep",
    )(h, x, mod, w_rest, cos_t, sin_t, kt, v, st_b, s_cf, dsum, af, ab, kdft, gf,
      ret_gn_w[0][None, :], gmlp_ln_w[0][None, :], w_s_b, b_s_full, w_out_b,
      final_norm_w[None, :])
    return out
```

```python
import functools

import jax
import jax.numpy as jnp
from jax import lax
from jax.experimental import pallas as pl
from jax.experimental.pallas import tpu as pltpu

F32 = jnp.float32
BF16 = jnp.bfloat16

HEADS = 4
DK = 64
DV = 128
CHUNK = 128
GROUPS = 4
DG = 128
GRID_W = 64
ROPE_BASE = 10000.0
EPS = 1e-6
LANES = 128
PAIRS = HEADS * DK // LANES
RET_W = HEADS * DV
GM_W = GROUPS * DG
QK_W = HEADS * DK

DOT_COLS = 512
TILE_KV = 1024
X_RING = 3
TILE_MAIN = 512


def _rms(x, w):
    ms = jnp.mean(x * x, axis=-1, keepdims=True)
    return x * lax.rsqrt(ms + EPS) * w


def _silu(x):
    return x * jax.nn.sigmoid(x)


def _gelu(x):
    return 0.5 * x * (1.0 + lax.erf(x * (0.5 ** 0.5)))


_dot = functools.partial(jnp.dot, preferred_element_type=F32)


def _col_groups(n):
    first = n % DOT_COLS
    edges = ([0] if first == 0 else [0, first]) + list(range(first + DOT_COLS, n + 1, DOT_COLS))
    return list(zip(edges[:-1], edges[1:]))


def _rope(p, cos, sin_signed, upper16):
    partner = jnp.where(upper16, pltpu.roll(p, 16, 1), pltpu.roll(p, LANES - 16, 1))
    return p * cos + partner * sin_signed


def _kv_pairs(kd, v):
    out = []
    for p in range(PAIRS):
        r = _dot(kd[p * LANES:(p + 1) * LANES, :], v[:, p * 2 * DV:(p + 1) * 2 * DV])
        out.append(r[0:DK, 0:DV])
        out.append(r[DK:2 * DK, DV:2 * DV])
    return jnp.concatenate(out, axis=0)


def _emit_pipelined(big_pieces, side_pieces, tasks):
    in_flight = []
    last_tick = max([start + parts for start, _, parts in tasks] +
                    [len(big_pieces), len(side_pieces)])
    for tick in range(last_tick):
        if tick < len(big_pieces):
            big_pieces[tick]()
        urgent, relaxed = side_pieces[tick] if tick < len(side_pieces) else (None, None)
        if urgent is not None:
            urgent()
        in_flight += [[gen, parts] for start, gen, parts in tasks if start == tick]
        for t in [t for t in in_flight if t[1] > 1] + [t for t in in_flight if t[1] == 1]:
            next(t[0], None)
            t[1] -= 1
        in_flight = [t for t in in_flight if t[1] > 0]
        if relaxed is not None:
            relaxed()
    assert not in_flight


def _run_pipeline_step(step, n_tiles, n_stages, body, refs, bufs_a, bufs_b):
    def call(s_static_parity, stages):
        new, old = (bufs_a, bufs_b) if s_static_parity == 0 else (bufs_b, bufs_a)
        body(stages, *refs, *new, *old)

    edge = n_stages - 1
    for s in list(range(edge)) + list(range(n_tiles, n_tiles + edge)):
        stages = tuple(0 <= s - j < n_tiles for j in range(n_stages))
        pl.when(step == s)(functools.partial(call, s % 2, stages))
    steady = (step >= edge) & (step < n_tiles)
    for parity in range(2):
        pl.when(steady & (lax.rem(step, 2) == parity))(
            functools.partial(call, parity, (True,) * n_stages))


def _prologue_kernel(decf_ref, decb_ref, c_ref, cctx_ref, w_ref, b_ref, invf_ref, win_ref,
                     wout_ref, ws_ref, bs_ref,
                     mod_ref, cos_ref, sin_ref, dsum_ref, af_ref, ab_ref,
                     kdft_ref, kdbt_ref, gf_ref, gb_ref,
                     wrest_ref, wkv_ref, woutb_ref, wsb_ref, bsf_ref):
    i = pl.program_id(0)
    d = win_ref.shape[0]
    c = c_ref[...]
    cond = jnp.concatenate([c, jnp.broadcast_to(cctx_ref[...], c.shape)], axis=0)
    mod_ref[...] = _dot(_silu(cond), w_ref[...]) + b_ref[...]

    @pl.when(i == 0)
    def _first_block():
        wrest_ref[:, 0:QK_W] = win_ref[:, 0:QK_W].astype(BF16)
        wkv_ref[...] = win_ref[:, QK_W:].astype(BF16)
        woutb_ref[:, 0:d] = wout_ref[...].astype(BF16)
        woutb_ref[:, d:] = jnp.zeros((woutb_ref.shape[0], woutb_ref.shape[1] - d), BF16)
        wsb_ref[...] = ws_ref[...].astype(BF16)
        for g in range(GROUPS):
            bsf_ref[g] = jnp.broadcast_to(bs_ref[g:g + 1, :], (CHUNK, CHUNK)).T

    for blk in range(1, (wrest_ref.shape[1] - QK_W) // d + 1):
        @pl.when(i == blk)
        def _later_block(blk=blk):
            wrest_ref[:, QK_W + (blk - 1) * d:QK_W + blk * d] = win_ref[...].astype(BF16)

    @pl.when(i == 0)
    def _tables():
        pos = lax.broadcasted_iota(jnp.int32, (GRID_W, LANES), 0).astype(F32)
        lane = lax.broadcasted_iota(jnp.int32, (GRID_W, LANES), 1)
        ang = pos * invf_ref[...]
        cr = jnp.cos(ang)
        sr = jnp.sin(ang)
        sr = jnp.where((lane & 16) == 0, -sr, sr)
        by_row = (lane & 32) == 0
        n_rows = cos_ref.shape[0] // GRID_W
        for r in range(n_rows):
            sl = slice(r * GRID_W, (r + 1) * GRID_W)
            cos_ref[sl, :] = jnp.where(by_row, jnp.broadcast_to(cr[r:r + 1, :], cr.shape), cr)
            sin_ref[sl, :] = jnp.where(by_row, jnp.broadcast_to(sr[r:r + 1, :], sr.shape), sr)

        def log_decay(head_idx, dec_ref):
            d = jnp.full(head_idx.shape, dec_ref[HEADS - 1], F32)
            for h in range(HEADS - 2, -1, -1):
                d = jnp.where(head_idx == h, dec_ref[h], d)
            return -jnp.exp(d)

        n = lax.broadcasted_iota(jnp.int32, (CHUNK, QK_W), 0).astype(F32)
        hl = lax.broadcasted_iota(jnp.int32, (CHUNK, QK_W), 1) // DK
        af_ref[...] = jnp.exp((n + 1.0) * log_decay(hl, decf_ref))
        ab_ref[...] = jnp.exp((CHUNK - n) * log_decay(hl, decb_ref))

        m = lax.broadcasted_iota(jnp.int32, (QK_W, CHUNK), 1).astype(F32)
        hr = lax.broadcasted_iota(jnp.int32, (QK_W, CHUNK), 0) // DK
        lgf = log_decay(hr, decf_ref)
        lgb = log_decay(hr, decb_ref)
        kdft_ref[...] = jnp.exp((CHUNK - 1.0 - m) * lgf)
        kdbt_ref[...] = jnp.exp(m * lgb)
        gf_ref[...] = jnp.exp(float(CHUNK) * lgf)
        gb_ref[...] = jnp.exp(float(CHUNK) * lgb)

        nn = lax.broadcasted_iota(jnp.int32, (CHUNK, CHUNK), 0).astype(F32)
        mm = lax.broadcasted_iota(jnp.int32, (CHUNK, CHUNK), 1).astype(F32)
        for h in range(HEADS):
            lf = -jnp.exp(jnp.full((CHUNK, CHUNK), decf_ref[h], F32))
            lb = -jnp.exp(jnp.full((CHUNK, CHUNK), decb_ref[h], F32))
            df = nn - mm
            db = mm - nn
            fwd = jnp.where(df >= 0, jnp.exp(jnp.maximum(df, 0.0) * lf), 0.0)
            bwd = jnp.where(db >= 0, jnp.exp(jnp.maximum(db, 0.0) * lb), 0.0)
            dsum_ref[h] = fwd + bwd


def _ctx_kernel(n_tiles, ctx_ref, mod_ref, nw_ref, wkv_ref, kdft_ref, kdbt_ref, gf_ref, gb_ref,
                scf_ref, scb_ref, h_ref, pa_ref, pb_ref):
    refs = (ctx_ref, mod_ref, nw_ref, wkv_ref, kdft_ref, kdbt_ref, gf_ref, gb_ref,
            scf_ref, scb_ref, h_ref)
    _run_pipeline_step(pl.program_id(0), n_tiles, 2, _ctx_step, refs, (pa_ref,), (pb_ref,))


def _ctx_step(stages, ctx_ref, mod_ref, nw_ref, wkv_ref, kdft_ref, kdbt_ref, gf_ref, gb_ref,
              scf_ref, scb_ref, h_ref, p_new_ref, p_old):
    project_stage, state_stage = stages
    d = ctx_ref.shape[-1]
    n_chunks = ctx_ref.shape[1] // CHUNK
    ctx_row = mod_ref.shape[0] // 2
    mod = mod_ref[ctx_row:ctx_row + 1, :]

    def norm_rows(rows):
        h = _rms(ctx_ref[0, rows, :], nw_ref[...]) * (1.0 + mod[:, d:2 * d]) + mod[:, 0:d]
        h_ref[rows, :] = h.astype(BF16)

    def project(lo, hi):
        p_new_ref[:, lo:hi] = _dot(h_ref[...], wkv_ref[:, lo:hi])

    new_f, new_b = {}, {}

    def chunk(c):
        sl = slice(c * CHUNK, (c + 1) * CHUNK)
        kt = (p_old[sl, 0:QK_W] * (DK ** -0.5)).T
        v = p_old[sl, QK_W:].astype(BF16)
        new_f[c] = _kv_pairs((kt * kdft_ref[...]).astype(BF16), v)
        new_b[c] = _kv_pairs((kt * kdbt_ref[...]).astype(BF16), v)
        yield

    def combine():
        yield
        sf = jnp.zeros((QK_W, DV), F32)
        for c in range(n_chunks):
            sf = gf_ref[...] * sf + new_f[c]
        sb = jnp.zeros((QK_W, DV), F32)
        for c in range(n_chunks - 1, -1, -1):
            sb = gb_ref[...] * sb + new_b[c]
        scf_ref[0] = sf
        scb_ref[0] = sb

    big, tasks = [], []
    if project_stage:
        for c in range(n_chunks):
            norm_rows(slice(c * CHUNK, (c + 1) * CHUNK))
        big = [functools.partial(project, lo, hi) for lo, hi in _col_groups(p_new_ref.shape[1])]
    if state_stage:
        tasks = [(c, chunk(c), 1) for c in range(n_chunks)] + [(n_chunks - 1, combine(), 2)]
    _emit_pipelined(big, [], tasks)


def _kv_kernel(n_tiles, tiles_per_batch, x_hbm, modc_ref, nw_ref, wkv_ref, cos_ref, sin_ref,
               kdbt_ref, gb_ref, scb_ref, h_ref, kt_ref, v_ref, stb_ref, sb_ref, pa_ref, pb_ref,
               xbuf_ref, xsem_ref):
    s = pl.program_id(0)
    tile = xbuf_ref.shape[1]

    def x_copy(t):
        b = t // tiles_per_batch
        j = tiles_per_batch - 1 - lax.rem(t, tiles_per_batch)
        slot = lax.rem(t, X_RING)
        return pltpu.make_async_copy(x_hbm.at[b, pl.ds(j * tile, tile), :],
                                     xbuf_ref.at[slot], xsem_ref.at[slot])

    @pl.when(s == 0)
    def _prime():
        for t in range(min(X_RING - 1, n_tiles)):
            x_copy(t).start()

    @pl.when(s < n_tiles)
    def _arrived():
        x_copy(s).wait()

    @pl.when(s + X_RING - 1 < n_tiles)
    def _prefetch():
        x_copy(s + X_RING - 1).start()

    @pl.when((s >= 1) & (lax.rem(s - 1, tiles_per_batch) == 0))
    def _init():
        sb_ref[...] = scb_ref[0]

    batch = jnp.minimum(s, n_tiles - 1) // tiles_per_batch
    modc = modc_ref[pl.ds(batch, 1), :]
    xc_ref = xbuf_ref.at[lax.rem(s, X_RING)]
    refs = (xc_ref, modc, nw_ref, wkv_ref, cos_ref, sin_ref, kdbt_ref, gb_ref,
            h_ref, kt_ref, v_ref, stb_ref, sb_ref)
    _run_pipeline_step(s, n_tiles, 2, _kv_step, refs, (pa_ref,), (pb_ref,))


def _kv_step(stages, xc_ref, modc, nw_ref, wkv_ref, cos_ref, sin_ref, kdbt_ref, gb_ref,
             h_ref, kt_ref, v_ref, stb_ref, sb_ref, p_new_ref, p_old):
    project_stage, finish_stage = stages
    d = xc_ref.shape[-1]
    tile = xc_ref.shape[0]
    half = tile // 2
    row_halves = [slice(0, half), slice(half, tile)]
    quarter = [slice(i * half // 2, (i + 1) * half // 2) for i in range(4)]
    upper16 = (lax.broadcasted_iota(jnp.int32, (CHUNK, LANES), 1) & 16) != 0

    def norm_rows(rows):
        h = _rms(xc_ref[rows, :], nw_ref[...]) * (1.0 + modc[:, d:2 * d]) + modc[:, 0:d]
        h_ref[0, rows, :] = h.astype(BF16)

    def project(rows, lo, hi):
        p_new_ref[rows, lo:hi] = _dot(h_ref[0, rows, :], wkv_ref[:, lo:hi])

    def chunk(c):
        sl = slice(c * CHUNK, (c + 1) * CHUNK)
        k = [_rope(p_old[sl, p * LANES:(p + 1) * LANES], cos_ref[sl, :], sin_ref[sl, :], upper16)
             * (DK ** -0.5) for p in range(PAIRS)]
        v = p_old[sl, QK_W:].astype(BF16)
        v_ref[0, sl, :] = v
        kt = jnp.concatenate([kp.T for kp in k], axis=0)
        kt_ref[0, c] = kt.astype(BF16)
        new = _kv_pairs((kt * kdbt_ref[...]).astype(BF16), v)
        yield
        stb_ref[0, c] = sb_ref[...].astype(BF16)
        sb_ref[...] = gb_ref[...] * sb_ref[...] + new

    projections, late_norms, tasks = [], [], []
    if project_stage:
        norm_rows(quarter[0])
        norm_rows(quarter[1])
        projections = [functools.partial(project, row_halves[r], lo, hi)
                       for r in range(2) for lo, hi in _col_groups(p_new_ref.shape[1])]
        late_norms = [(functools.partial(norm_rows, q), None) for q in quarter[2:]]
    if finish_stage:
        tasks = [(tick, chunk(c), 2) for tick, c in enumerate(range(tile // CHUNK - 1, -1, -1))]
    _emit_pipelined(projections, late_norms, tasks)


def _main_kernel(n_tiles, tiles_per_batch, h_ref, x_ref, mod_ref, w_ref, cos_ref,
                 sin_ref, kt_ref, v_ref, stb_ref, scf_ref, dsum_ref, af_ref, ab_ref, kdft_ref,
                 gf_ref, gnw_ref, lnw_ref, ws_ref, bs_ref, wout_ref, fnw_ref, o_ref,
                 sf_ref, pa_ref, pb_ref, ya_ref, yb_ref):
    s = pl.program_id(0)

    @pl.when((s >= 1) & (lax.rem(s - 1, tiles_per_batch) == 0))
    def _init():
        sf_ref[...] = scf_ref[0]

    batch = jnp.clip(s - 2, 0, n_tiles - 1) // tiles_per_batch
    mod = mod_ref[pl.ds(batch, 1), :]
    refs = (h_ref, x_ref, mod, w_ref, cos_ref, sin_ref, kt_ref, v_ref,
            stb_ref, dsum_ref, af_ref, ab_ref, kdft_ref, gf_ref, gnw_ref, lnw_ref, ws_ref,
            bs_ref, wout_ref, fnw_ref, o_ref, sf_ref)
    _run_pipeline_step(s, n_tiles, 3, _main_step, refs, (pa_ref, ya_ref), (pb_ref, yb_ref))


def _main_step(stages, h_ref, x_ref, mod, w_ref, cos_ref, sin_ref, kt_ref, v_ref,
               stb_ref, dsum_ref, af_ref, ab_ref, kdft_ref, gf_ref, gnw_ref, lnw_ref, ws_ref,
               bs_ref, wout_ref, fnw_ref, o_ref, sf_ref, p_new_ref, y_ref, p_all, y_old):
    project_stage, mix_stage, out_stage = stages
    d = x_ref.shape[-1]
    tile = x_ref.shape[1]
    n_chunks = tile // CHUNK
    half = tile // 2
    row_halves = [slice(0, half), slice(half, tile)]
    c_gr = QK_W
    c_u = c_gr + RET_W
    c_vm = c_u + GM_W
    c_gm = c_vm + GM_W
    lane = lax.broadcasted_iota(jnp.int32, (CHUNK, LANES), 1)
    upper16 = (lane & 16) != 0
    head_of_lane = lane // DK

    operand = {}

    def loaded_once(key, load):
        if key not in operand:
            operand[key] = load()
        return operand[key]

    def project(r, lo, hi):
        rows = row_halves[r]
        lhs = loaded_once(("h", r), lambda: h_ref[0, rows, :])
        p_new_ref[rows, lo:hi] = _dot(lhs, w_ref[:, lo:hi])

    def retention(c, p):
        sl = slice(c * CHUNK, (c + 1) * CHUNK)
        ps = slice(p * LANES, (p + 1) * LANES)
        q2 = _rope(p_all[sl, ps], cos_ref[sl, :], sin_ref[sl, :], upper16)
        kt = kt_ref[0, c, ps, :]
        vp = v_ref[0, sl, p * 2 * DV:(p + 1) * 2 * DV]
        qa = q2 * af_ref[:, ps]
        qb = q2 * ab_ref[:, ps]
        scores, qx = [], []
        for hh in range(2):
            mine = head_of_lane == hh
            scores.append(_dot(jnp.where(mine, q2, 0.0).astype(BF16), kt))
            qx.append(jnp.concatenate([jnp.where(mine, qa, 0.0), jnp.where(mine, qb, 0.0)],
                                      axis=1).astype(BF16))
        kd = (kt.astype(F32) * kdft_ref[ps, :]).astype(BF16)
        yield
        state = jnp.concatenate([sf_ref[ps, :].astype(BF16), stb_ref[0, c, ps, :]], axis=0)
        o = []
        for hh in range(2):
            a = (scores[hh] * dsum_ref[2 * p + hh]).astype(BF16)
            o.append(_dot(a, vp[:, hh * DV:(hh + 1) * DV]) + _dot(qx[hh], state))
        r = _dot(kd, vp)
        yield
        new = jnp.concatenate([r[0:DK, 0:DV], r[DK:2 * DK, DV:2 * DV]], axis=0)
        sf_ref[ps, :] = gf_ref[ps, :] * sf_ref[ps, :] + new
        for hh in range(2):
            head = 2 * p + hh
            mu = jnp.mean(o[hh], axis=-1, keepdims=True)
            var = jnp.mean(jnp.square(o[hh] - mu), axis=-1, keepdims=True)
            hs = slice(head * DV, (head + 1) * DV)
            y = (o[hh] - mu) * lax.rsqrt(var + EPS) * gnw_ref[:, hs]
            gate_cols = slice(c_gr + head * DV, c_gr + (head + 1) * DV)
            y_ref[sl, hs] = (y * _silu(p_all[sl, gate_cols])).astype(BF16)

    def gating(c):
        sl = slice(c * CHUNK, (c + 1) * CHUNK)
        vg = _gelu(p_all[sl, c_vm:c_gm])
        mu = jnp.mean(vg, axis=-1, keepdims=True)
        var = jnp.mean(jnp.square(vg - mu), axis=-1, keepdims=True)
        yield
        vn = ((vg - mu) * lax.rsqrt(var + EPS) * lnw_ref[...]).astype(BF16)
        s = [_dot(ws_ref[g], vn[:, g * DG:(g + 1) * DG]) for g in range(GROUPS)]
        yield
        for g in range(GROUPS):
            if g == GROUPS // 2:
                yield
            u = _gelu(p_all[sl, c_u + g * DG:c_u + (g + 1) * DG])
            gm = _silu(p_all[sl, c_gm + g * DG:c_gm + (g + 1) * DG])
            y_ref[sl, RET_W + g * DG:RET_W + (g + 1) * DG] = (u * (s[g] + bs_ref[g]) * gm).astype(BF16)

    out_groups = _col_groups(d)
    xn = {}

    def out_project(r, g):
        rows = row_halves[r]
        lo, hi = out_groups[g]
        lhs = loaded_once(("y", r), lambda: y_old[rows, :])
        out = _dot(lhs, wout_ref[:, lo:hi])
        xn[r, g] = x_ref[0, rows, lo:hi] + mod[:, 2 * d + lo:2 * d + hi] * out

    def finish(r, part):
        sub = slice(part * half // 2, (part + 1) * half // 2)
        rows = slice(row_halves[r].start + sub.start, row_halves[r].start + sub.stop)
        parts = [xn[r, g][sub, :] for g in range(len(out_groups))]
        ms = sum(jnp.sum(t * t, axis=-1, keepdims=True) for t in parts) * (1.0 / d)
        rs = lax.rsqrt(ms + EPS)
        for (lo, hi), t in zip(out_groups, parts):
            o_ref[0, rows, lo:hi] = t * rs * fnw_ref[:, lo:hi]

    tasks = []
    if mix_stage:
        for c in range(n_chunks):
            base = 3 * c
            tasks += [(base + min(p, 1), retention(c, p), 3) for p in range(PAIRS)]
            tasks.append((base + 1, gating(c), 4))
    big, relaxed = [], {}

    def place(tick, piece):
        while tick in relaxed:
            tick += 1
        relaxed[tick] = piece

    for r in range(2):
        outs = [functools.partial(out_project, r, g) for g in range(len(out_groups))] if out_stage else []
        projs = ([functools.partial(project, r, lo, hi) for lo, hi in _col_groups(p_new_ref.shape[1])]
                 if project_stage else [])
        for i in range(max(len(outs), len(projs))):
            big += outs[i:i + 1] + projs[i:i + 1]
            if i == len(outs) - 1:
                place(len(big) + 1, functools.partial(finish, r, 0))
                place(len(big) + 3, functools.partial(finish, r, 1))
    side = [(None, relaxed.get(tick)) for tick in range(max(relaxed, default=-1) + 1)]
    _emit_pipelined(big, side, tasks)


def _full(shape):
    return pl.BlockSpec(shape, lambda *_: (0,) * len(shape))


def kernel(x, c, ctx, c_ctx, w_ada, b_ada, norm_w, w_in, ret_decay_f, ret_decay_b, ret_gn_w,
           gmlp_ln_w, w_s, b_s, w_out, final_norm_w):
    bsz, seq, d = x.shape
    ctx_len = ctx.shape[1]
    depth = w_ada.shape[0]
    assert depth == 1 and d % LANES == 0
    assert seq % TILE_KV == 0 and seq % TILE_MAIN == 0 and ctx_len % CHUNK == 0
    assert seq % GRID_W == 0 and TILE_MAIN % (4 * CHUNK) == 0 and TILE_KV % (4 * CHUNK) == 0
    n_chunks = seq // CHUNK
    d3 = 3 * d
    n_kv = QK_W + RET_W
    n_rest = w_in.shape[2] - n_kv
    n_gates = n_rest - QK_W
    assert QK_W + n_kv == d and n_gates % d == 0 and d3 // d == 1 + n_gates // d
    assert _col_groups(n_rest)[0] == (0, QK_W)

    quarter = DK // 4
    inv_freq = ROPE_BASE ** (-jnp.arange(quarter, dtype=F32) / quarter)
    invf_lane = jnp.tile(inv_freq, LANES // quarter)[None, :]
    nw = norm_w[0][None, :]
    rows2 = 2 * bsz

    n_col_blocks = d3 // d
    smem = pl.BlockSpec(memory_space=pltpu.SMEM)
    n_out_padded = d + QK_W
    (mod, cos_t, sin_t, dsum, af, ab, kdft, kdbt, gf, gb,
     w_rest, w_kv, w_out_b, w_s_b, b_s_full) = pl.pallas_call(
        _prologue_kernel,
        grid=(n_col_blocks,),
        in_specs=[smem, smem, _full((bsz, d)), _full((1, d)),
                  pl.BlockSpec((d, d), lambda i: (0, i)),
                  pl.BlockSpec((1, d), lambda i: (0, i)),
                  _full((1, LANES)),
                  pl.BlockSpec((d, d), lambda i: (0, i)),
                  _full((RET_W + GM_W, d)),
                  _full((GROUPS, CHUNK, CHUNK)), _full((GROUPS, CHUNK))],
        out_specs=[pl.BlockSpec((rows2, d), lambda i: (0, i)),
                   _full((seq, LANES)), _full((seq, LANES)),
                   _full((HEADS, CHUNK, CHUNK)),
                   _full((CHUNK, QK_W)), _full((CHUNK, QK_W)),
                   _full((QK_W, CHUNK)), _full((QK_W, CHUNK)),
                   _full((QK_W, DV)), _full((QK_W, DV)),
                   _full((d, n_rest)), _full((d, n_kv)),
                   _full((RET_W + GM_W, n_out_padded)),
                   _full((GROUPS, CHUNK, CHUNK)), _full((GROUPS, CHUNK, DG))],
        out_shape=[jax.ShapeDtypeStruct((rows2, d3), F32),
                   jax.ShapeDtypeStruct((seq, LANES), F32),
                   jax.ShapeDtypeStruct((seq, LANES), F32),
                   jax.ShapeDtypeStruct((HEADS, CHUNK, CHUNK), F32),
                   jax.ShapeDtypeStruct((CHUNK, QK_W), F32),
                   jax.ShapeDtypeStruct((CHUNK, QK_W), F32),
                   jax.ShapeDtypeStruct((QK_W, CHUNK), F32),
                   jax.ShapeDtypeStruct((QK_W, CHUNK), F32),
                   jax.ShapeDtypeStruct((QK_W, DV), F32),
                   jax.ShapeDtypeStruct((QK_W, DV), F32),
                   jax.ShapeDtypeStruct((d, n_rest), BF16),
                   jax.ShapeDtypeStruct((d, n_kv), BF16),
                   jax.ShapeDtypeStruct((RET_W + GM_W, n_out_padded), BF16),
                   jax.ShapeDtypeStruct((GROUPS, CHUNK, CHUNK), BF16),
                   jax.ShapeDtypeStruct((GROUPS, CHUNK, DG), F32)],
        compiler_params=pltpu.CompilerParams(dimension_semantics=("arbitrary",)),
        name="prologue",
    )(ret_decay_f[0], ret_decay_b[0], c, c_ctx[None, :], w_ada[0], b_ada[0][None, :], invf_lane,
      w_in[0], w_out[0], w_s[0], b_s[0])

    s_cf, s_cb = pl.pallas_call(
        functools.partial(_ctx_kernel, bsz),
        grid=(bsz + 1,),
        in_specs=[pl.BlockSpec((1, ctx_len, d), lambda s: (jnp.minimum(s, bsz - 1), 0, 0)),
                  _full((rows2, d3)),
                  _full((1, d)), _full((d, n_kv)),
                  _full((QK_W, CHUNK)), _full((QK_W, CHUNK)),
                  _full((QK_W, DV)), _full((QK_W, DV))],
        out_specs=[pl.BlockSpec((1, QK_W, DV), lambda s: (jnp.maximum(s - 1, 0), 0, 0)),
                   pl.BlockSpec((1, QK_W, DV), lambda s: (jnp.maximum(s - 1, 0), 0, 0))],
        out_shape=[jax.ShapeDtypeStruct((bsz, QK_W, DV), F32),
                   jax.ShapeDtypeStruct((bsz, QK_W, DV), F32)],
        scratch_shapes=[pltpu.VMEM((ctx_len, d), BF16),
                        pltpu.VMEM((ctx_len, n_kv), F32),
                        pltpu.VMEM((ctx_len, n_kv), F32)],
        compiler_params=pltpu.CompilerParams(dimension_semantics=("arbitrary",)),
        name="ctx_states",
    )(ctx, mod, nw, w_kv, kdft, kdbt, gf, gb)

    def tile_maps(n_tiles, per_batch, reverse):
        def where(t):
            j = t % per_batch
            return t // per_batch, (per_batch - 1 - j) if reverse else j
        return [lambda s, k=k: where(jnp.clip(s - k, 0, n_tiles - 1)) for k in range(3)]

    nt_kv = seq // TILE_KV
    cpt_kv = TILE_KV // CHUNK
    cur, prv, _ = tile_maps(bsz * nt_kv, nt_kv, reverse=True)
    h, kt, v, st_b = pl.pallas_call(
        functools.partial(_kv_kernel, bsz * nt_kv, nt_kv),
        grid=(bsz * nt_kv + 1,),
        in_specs=[pl.BlockSpec(memory_space=pl.ANY),
                  _full((rows2, d3)),
                  _full((1, d)), _full((d, n_kv)),
                  pl.BlockSpec((TILE_KV, LANES), lambda s: (prv(s)[1], 0)),
                  pl.BlockSpec((TILE_KV, LANES), lambda s: (prv(s)[1], 0)),
                  _full((QK_W, CHUNK)), _full((QK_W, DV)),
                  pl.BlockSpec((1, QK_W, DV), lambda s: (prv(s)[0], 0, 0))],
        out_specs=[pl.BlockSpec((1, TILE_KV, d), lambda s: (*cur(s), 0)),
                   pl.BlockSpec((1, cpt_kv, QK_W, CHUNK), lambda s: (*prv(s), 0, 0)),
                   pl.BlockSpec((1, TILE_KV, RET_W), lambda s: (*prv(s), 0)),
                   pl.BlockSpec((1, cpt_kv, QK_W, DV), lambda s: (*prv(s), 0, 0))],
        out_shape=[jax.ShapeDtypeStruct((bsz, seq, d), BF16),
                   jax.ShapeDtypeStruct((bsz, n_chunks, QK_W, CHUNK), BF16),
                   jax.ShapeDtypeStruct((bsz, seq, RET_W), BF16),
                   jax.ShapeDtypeStruct((bsz, n_chunks, QK_W, DV), BF16)],
        scratch_shapes=[pltpu.VMEM((QK_W, DV), F32),
                        pltpu.VMEM((TILE_KV, n_kv), F32),
                        pltpu.VMEM((TILE_KV, n_kv), F32),
                        pltpu.VMEM((X_RING, TILE_KV, d), F32),
                        pltpu.SemaphoreType.DMA((X_RING,))],
        compiler_params=pltpu.CompilerParams(dimension_semantics=("arbitrary",)),
        name="kv_sweep",
    )(x, mod, nw, w_kv, cos_t, sin_t, kdbt, gb, s_cb)

    nt = seq // TILE_MAIN
    cpt = TILE_MAIN // CHUNK
    cur, prv, prv2 = tile_maps(bsz * nt, nt, reverse=False)
    out = pl.pallas_call(
        functools.partial(_main_kernel, bsz * nt, nt),
        grid=(bsz * nt + 2,),
        in_specs=[pl.BlockSpec((1, TILE_MAIN, d), lambda s: (*cur(s), 0)),
                  pl.BlockSpec((1, TILE_MAIN, d), lambda s: (*prv2(s), 0)),
                  _full((rows2, d3)),
                  _full((d, n_rest)),
                  pl.BlockSpec((TILE_MAIN, LANES), lambda s: (prv(s)[1], 0)),
                  pl.BlockSpec((TILE_MAIN, LANES), lambda s: (prv(s)[1], 0)),
                  pl.BlockSpec((1, cpt, QK_W, CHUNK), lambda s: (*prv(s), 0, 0)),
                  pl.BlockSpec((1, TILE_MAIN, RET_W), lambda s: (*prv(s), 0)),
                  pl.BlockSpec((1, cpt, QK_W, DV), lambda s: (*prv(s), 0, 0)),
                  pl.BlockSpec((1, QK_W, DV), lambda s: (prv(s)[0], 0, 0)),
                  _full((HEADS, CHUNK, CHUNK)),
                  _full((CHUNK, QK_W)), _full((CHUNK, QK_W)),
                  _full((QK_W, CHUNK)), _full((QK_W, DV)),
                  _full((1, RET_W)), _full((1, GM_W)),
                  _full((GROUPS, CHUNK, CHUNK)), _full((GROUPS, CHUNK, DG)),
                  _full((RET_W + GM_W, n_out_padded)), _full((1, d))],
        out_specs=pl.BlockSpec((1, TILE_MAIN, d), lambda s: (*prv2(s), 0)),
        out_shape=jax.ShapeDtypeStruct((bsz, seq, d), x.dtype),
        scratch_shapes=[pltpu.VMEM((QK_W, DV), F32),
                        pltpu.VMEM((TILE_MAIN, n_rest), F32),
                        pltpu.VMEM((TILE_MAIN, n_rest), F32),
                        pltpu.VMEM((TILE_MAIN, RET_W + GM_W), BF16),
                        pltpu.VMEM((TILE_MAIN, RET_W + GM_W), BF16)],
        compiler_params=pltpu.CompilerParams(dimension_semantics=("arbitrary",)),
        name="main_sweep",
    )(h, x, mod, w_rest, cos_t, sin_t, kt, v, st_b, s_cf, dsum, af, ab, kdft, gf,
      ret_gn_w[0][None, :], gmlp_ln_w[0][None, :], w_s_b, b_s_full, w_out_b,
      final_norm_w[None, :])
    return out
```

```python
import functools

import jax
import jax.numpy as jnp
from jax import lax
from jax.experimental import pallas as pl
from jax.experimental.pallas import tpu as pltpu

F32 = jnp.float32
BF16 = jnp.bfloat16

HEADS = 4
DK = 64
DV = 128
CHUNK = 128
GROUPS = 4
DG = 128
GRID_W = 64
ROPE_BASE = 10000.0
EPS = 1e-6
LANES = 128
PAIRS = HEADS * DK // LANES
RET_W = HEADS * DV
GM_W = GROUPS * DG
QK_W = HEADS * DK

DOT_COLS = 512
TILE_KV = 1024
X_RING = 3
TILE_MAIN = 512


def _rms(x, w):
    ms = jnp.mean(x * x, axis=-1, keepdims=True)
    return x * lax.rsqrt(ms + EPS) * w


def _silu(x):
    return x * jax.nn.sigmoid(x)


def _gelu(x):
    return 0.5 * x * (1.0 + lax.erf(x * (0.5 ** 0.5)))


_dot = functools.partial(jnp.dot, preferred_element_type=F32)


def _col_groups(n):
    first = n % DOT_COLS
    edges = ([0] if first == 0 else [0, first]) + list(range(first + DOT_COLS, n + 1, DOT_COLS))
    return list(zip(edges[:-1], edges[1:]))


def _rope(p, cos, sin_signed, upper16):
    partner = jnp.where(upper16, pltpu.roll(p, 16, 1), pltpu.roll(p, LANES - 16, 1))
    return p * cos + partner * sin_signed


def _kv_pairs(kd, v):
    out = []
    for p in range(PAIRS):
        r = _dot(kd[p * LANES:(p + 1) * LANES, :], v[:, p * 2 * DV:(p + 1) * 2 * DV])
        out.append(r[0:DK, 0:DV])
        out.append(r[DK:2 * DK, DV:2 * DV])
    return jnp.concatenate(out, axis=0)


def _emit_pipelined(big_pieces, side_pieces, tasks):
    in_flight = []
    last_tick = max([start + parts for start, _, parts in tasks] +
                    [len(big_pieces), len(side_pieces)])
    for tick in range(last_tick):
        if tick < len(big_pieces):
            big_pieces[tick]()
        urgent, relaxed = side_pieces[tick] if tick < len(side_pieces) else (None, None)
        if urgent is not None:
            urgent()
        in_flight += [[gen, parts] for start, gen, parts in tasks if start == tick]
        for t in [t for t in in_flight if t[1] > 1] + [t for t in in_flight if t[1] == 1]:
            next(t[0], None)
            t[1] -= 1
        in_flight = [t for t in in_flight if t[1] > 0]
        if relaxed is not None:
            relaxed()
    assert not in_flight


def _run_pipeline_step(step, n_tiles, n_stages, body, refs, bufs_a, bufs_b):
    def call(s_static_parity, stages):
        new, old = (bufs_a, bufs_b) if s_static_parity == 0 else (bufs_b, bufs_a)
        body(stages, *refs, *new, *old)

    edge = n_stages - 1
    for s in list(range(edge)) + list(range(n_tiles, n_tiles + edge)):
        stages = tuple(0 <= s - j < n_tiles for j in range(n_stages))
        pl.when(step == s)(functools.partial(call, s % 2, stages))
    steady = (step >= edge) & (step < n_tiles)
    for parity in range(2):
        pl.when(steady & (lax.rem(step, 2) == parity))(
            functools.partial(call, parity, (True,) * n_stages))


def _prologue_kernel(decf_ref, decb_ref, c_ref, cctx_ref, w_ref, b_ref, invf_ref, win_ref,
                     wout_ref, ws_ref, bs_ref,
                     mod_ref, cos_ref, sin_ref, dsum_ref, af_ref, ab_ref,
                     kdft_ref, kdbt_ref, gf_ref, gb_ref,
                     wrest_ref, wkv_ref, woutb_ref, wsb_ref, bsf_ref):
    i = pl.program_id(0)
    d = win_ref.shape[0]
    c = c_ref[...]
    cond = jnp.concatenate([c, jnp.broadcast_to(cctx_ref[...], c.shape)], axis=0)
    mod_ref[...] = _dot(_silu(cond), w_ref[...]) + b_ref[...]

    @pl.when(i == 0)
    def _first_block():
        wrest_ref[:, 0:QK_W] = win_ref[:, 0:QK_W].astype(BF16)
        wkv_ref[...] = win_ref[:, QK_W:].astype(BF16)
        woutb_ref[:, 0:d] = wout_ref[...].astype(BF16)
        woutb_ref[:, d:] = jnp.zeros((woutb_ref.shape[0], woutb_ref.shape[1] - d), BF16)
        wsb_ref[...] = ws_ref[...].astype(BF16)
        for g in range(GROUPS):
            bsf_ref[g] = jnp.broadcast_to(bs_ref[g:g + 1, :], (CHUNK, CHUNK)).T

    for blk in range(1, (wrest_ref.shape[1] - QK_W) // d + 1):
        @pl.when(i == blk)
        def _later_block(blk=blk):
            wrest_ref[:, QK_W + (blk - 1) * d:QK_W + blk * d] = win_ref[...].astype(BF16)

    @pl.when(i == 0)
    def _tables():
        pos = lax.broadcasted_iota(jnp.int32, (GRID_W, LANES), 0).astype(F32)
        lane = lax.broadcasted_iota(jnp.int32, (GRID_W, LANES), 1)
        ang = pos * invf_ref[...]
        cr = jnp.cos(ang)
        sr = jnp.sin(ang)
        sr = jnp.where((lane & 16) == 0, -sr, sr)
        by_row = (lane & 32) == 0
        n_rows = cos_ref.shape[0] // GRID_W
        for r in range(n_rows):
            sl = slice(r * GRID_W, (r + 1) * GRID_W)
            cos_ref[sl, :] = jnp.where(by_row, jnp.broadcast_to(cr[r:r + 1, :], cr.shape), cr)
            sin_ref[sl, :] = jnp.where(by_row, jnp.broadcast_to(sr[r:r + 1, :], sr.shape), sr)

        def log_decay(head_idx, dec_ref):
            d = jnp.full(head_idx.shape, dec_ref[HEADS - 1], F32)
            for h in range(HEADS - 2, -1, -1):
                d = jnp.where(head_idx == h, dec_ref[h], d)
            return -jnp.exp(d)

        n = lax.broadcasted_iota(jnp.int32, (CHUNK, QK_W), 0).astype(F32)
        hl = lax.broadcasted_iota(jnp.int32, (CHUNK, QK_W), 1) // DK
        af_ref[...] = jnp.exp((n + 1.0) * log_decay(hl, decf_ref))
        ab_ref[...] = jnp.exp((CHUNK - n) * log_decay(hl, decb_ref))

        m = lax.broadcasted_iota(jnp.int32, (QK_W, CHUNK), 1).astype(F32)
        hr = lax.broadcasted_iota(jnp.int32, (QK_W, CHUNK), 0) // DK
        lgf = log_decay(hr, decf_ref)
        lgb = log_decay(hr, decb_ref)
        kdft_ref[...] = jnp.exp((CHUNK - 1.0 - m) * lgf)
        kdbt_ref[...] = jnp.exp(m * lgb)
        gf_ref[...] = jnp.exp(float(CHUNK) * lgf)
        gb_ref[...] = jnp.exp(float(CHUNK) * lgb)

        nn = lax.broadcasted_iota(jnp.int32, (CHUNK, CHUNK), 0).astype(F32)
        mm = lax.broadcasted_iota(jnp.int32, (CHUNK, CHUNK), 1).astype(F32)
        for h in range(HEADS):
            lf = -jnp.exp(jnp.full((CHUNK, CHUNK), decf_ref[h], F32))
            lb = -jnp.exp(jnp.full((CHUNK, CHUNK), decb_ref[h], F32))
            df = nn - mm
            db = mm - nn
            fwd = jnp.where(df >= 0, jnp.exp(jnp.maximum(df, 0.0) * lf), 0.0)
            bwd = jnp.where(db >= 0, jnp.exp(jnp.maximum(db, 0.0) * lb), 0.0)
            dsum_ref[h] = fwd + bwd


def _ctx_kernel(n_tiles, ctx_ref, mod_ref, nw_ref, wkv_ref, kdft_ref, kdbt_ref, gf_ref, gb_ref,
                scf_ref, scb_ref, h_ref, pa_ref, pb_ref):
    refs = (ctx_ref, mod_ref, nw_ref, wkv_ref, kdft_ref, kdbt_ref, gf_ref, gb_ref,
            scf_ref, scb_ref, h_ref)
    _run_pipeline_step(pl.program_id(0), n_tiles, 2, _ctx_step, refs, (pa_ref,), (pb_ref,))


def _ctx_step(stages, ctx_ref, mod_ref, nw_ref, wkv_ref, kdft_ref, kdbt_ref, gf_ref, gb_ref,
              scf_ref, scb_ref, h_ref, p_new_ref, p_old):
    project_stage, state_stage = stages
    d = ctx_ref.shape[-1]
    n_chunks = ctx_ref.shape[1] // CHUNK
    ctx_row = mod_ref.shape[0] // 2
    mod = mod_ref[ctx_row:ctx_row + 1, :]

    def norm_rows(rows):
        h = _rms(ctx_ref[0, rows, :], nw_ref[...]) * (1.0 + mod[:, d:2 * d]) + mod[:, 0:d]
        h_ref[rows, :] = h.astype(BF16)

    def project(lo, hi):
        p_new_ref[:, lo:hi] = _dot(h_ref[...], wkv_ref[:, lo:hi])

    new_f, new_b = {}, {}

    def chunk(c):
        sl = slice(c * CHUNK, (c + 1) * CHUNK)
        kt = (p_old[sl, 0:QK_W] * (DK ** -0.5)).T
        v = p_old[sl, QK_W:].astype(BF16)
        new_f[c] = _kv_pairs((kt * kdft_ref[...]).astype(BF16), v)
        new_b[c] = _kv_pairs((kt * kdbt_ref[...]).astype(BF16), v)
        yield

    def combine():
        yield
        sf = jnp.zeros((QK_W, DV), F32)
        for c in range(n_chunks):
            sf = gf_ref[...] * sf + new_f[c]
        sb = jnp.zeros((QK_W, DV), F32)
        for c in range(n_chunks - 1, -1, -1):
            sb = gb_ref[...] * sb + new_b[c]
        scf_ref[0] = sf
        scb_ref[0] = sb

    big, tasks = [], []
    if project_stage:
        for c in range(n_chunks):
            norm_rows(slice(c * CHUNK, (c + 1) * CHUNK))
        big = [functools.partial(project, lo, hi) for lo, hi in _col_groups(p_new_ref.shape[1])]
    if state_stage:
        tasks = [(c, chunk(c), 1) for c in range(n_chunks)] + [(n_chunks - 1, combine(), 2)]
    _emit_pipelined(big, [], tasks)


def _kv_kernel(n_tiles, tiles_per_batch, x_hbm, modc_ref, nw_ref, wkv_ref, cos_ref, sin_ref,
               kdbt_ref, gb_ref, scb_ref, h_ref, kt_ref, v_ref, stb_ref, sb_ref, pa_ref, pb_ref,
               xbuf_ref, xsem_ref):
    s = pl.program_id(0)
    tile = xbuf_ref.shape[1]

    def x_copy(t):
        b = t // tiles_per_batch
        j = tiles_per_batch - 1 - lax.rem(t, tiles_per_batch)
        slot = lax.rem(t, X_RING)
        return pltpu.make_async_copy(x_hbm.at[b, pl.ds(j * tile, tile), :],
                                     xbuf_ref.at[slot], xsem_ref.at[slot])

    @pl.when(s == 0)
    def _prime():
        for t in range(min(X_RING - 1, n_tiles)):
            x_copy(t).start()

    @pl.when(s < n_tiles)
    def _arrived():
        x_copy(s).wait()

    @pl.when(s + X_RING - 1 < n_tiles)
    def _prefetch():
        x_copy(s + X_RING - 1).start()

    @pl.when((s >= 1) & (lax.rem(s - 1, tiles_per_batch) == 0))
    def _init():
        sb_ref[...] = scb_ref[0]

    batch = jnp.minimum(s, n_tiles - 1) // tiles_per_batch
    modc = modc_ref[pl.ds(batch, 1), :]
    xc_ref = xbuf_ref.at[lax.rem(s, X_RING)]
    finished = jnp.clip(s - 1, 0, n_tiles - 1)
    first_row = pl.multiple_of(
        (tiles_per_batch - 1 - lax.rem(finished, tiles_per_batch)) * tile, tile)
    cos_ref = cos_ref.at[pl.ds(first_row, tile), :]
    sin_ref = sin_ref.at[pl.ds(first_row, tile), :]
    refs = (xc_ref, modc, nw_ref, wkv_ref, cos_ref, sin_ref, kdbt_ref, gb_ref,
            h_ref, kt_ref, v_ref, stb_ref, sb_ref)
    _run_pipeline_step(s, n_tiles, 2, _kv_step, refs, (pa_ref,), (pb_ref,))


def _kv_step(stages, xc_ref, modc, nw_ref, wkv_ref, cos_ref, sin_ref, kdbt_ref, gb_ref,
             h_ref, kt_ref, v_ref, stb_ref, sb_ref, p_new_ref, p_old):
    project_stage, finish_stage = stages
    d = xc_ref.shape[-1]
    tile = xc_ref.shape[0]
    half = tile // 2
    row_halves = [slice(0, half), slice(half, tile)]
    quarter = [slice(i * half // 2, (i + 1) * half // 2) for i in range(4)]
    upper16 = (lax.broadcasted_iota(jnp.int32, (CHUNK, LANES), 1) & 16) != 0

    def norm_rows(rows):
        h = _rms(xc_ref[rows, :], nw_ref[...]) * (1.0 + modc[:, d:2 * d]) + modc[:, 0:d]
        h_ref[0, rows, :] = h.astype(BF16)

    def project(rows, lo, hi):
        p_new_ref[rows, lo:hi] = _dot(h_ref[0, rows, :], wkv_ref[:, lo:hi])

    def chunk(c):
        sl = slice(c * CHUNK, (c + 1) * CHUNK)
        k = [_rope(p_old[sl, p * LANES:(p + 1) * LANES], cos_ref[sl, :], sin_ref[sl, :], upper16)
             * (DK ** -0.5) for p in range(PAIRS)]
        v = p_old[sl, QK_W:].astype(BF16)
        v_ref[0, sl, :] = v
        kt = jnp.concatenate([kp.T for kp in k], axis=0)
        kt_ref[0, c] = kt.astype(BF16)
        new = _kv_pairs((kt * kdbt_ref[...]).astype(BF16), v)
        yield
        stb_ref[0, c] = sb_ref[...].astype(BF16)
        sb_ref[...] = gb_ref[...] * sb_ref[...] + new

    projections, late_norms, tasks = [], [], []
    if project_stage:
        norm_rows(quarter[0])
        norm_rows(quarter[1])
        projections = [functools.partial(project, row_halves[r], lo, hi)
                       for r in range(2) for lo, hi in _col_groups(p_new_ref.shape[1])]
        late_norms = [(functools.partial(norm_rows, q), None) for q in quarter[2:]]
    if finish_stage:
        tasks = [(tick, chunk(c), 2) for tick, c in enumerate(range(tile // CHUNK - 1, -1, -1))]
    _emit_pipelined(projections, late_norms, tasks)


def _main_kernel(n_tiles, tiles_per_batch, h_ref, x_ref, mod_ref, w_ref, cos_ref,
                 sin_ref, kt_ref, v_ref, stb_ref, scf_ref, dsum_ref, af_ref, ab_ref, kdft_ref,
                 gf_ref, gnw_ref, lnw_ref, ws_ref, bs_ref, wout_ref, fnw_ref, o_ref,
                 sf_ref, pa_ref, pb_ref, ya_ref, yb_ref):
    s = pl.program_id(0)

    @pl.when((s >= 1) & (lax.rem(s - 1, tiles_per_batch) == 0))
    def _init():
        sf_ref[...] = scf_ref[0]

    batch = jnp.clip(s - 2, 0, n_tiles - 1) // tiles_per_batch
    mod = mod_ref[pl.ds(batch, 1), :]
    refs = (h_ref, x_ref, mod, w_ref, cos_ref, sin_ref, kt_ref, v_ref,
            stb_ref, dsum_ref, af_ref, ab_ref, kdft_ref, gf_ref, gnw_ref, lnw_ref, ws_ref,
            bs_ref, wout_ref, fnw_ref, o_ref, sf_ref)
    _run_pipeline_step(s, n_tiles, 3, _main_step, refs, (pa_ref, ya_ref), (pb_ref, yb_ref))


def _main_step(stages, h_ref, x_ref, mod, w_ref, cos_ref, sin_ref, kt_ref, v_ref,
               stb_ref, dsum_ref, af_ref, ab_ref, kdft_ref, gf_ref, gnw_ref, lnw_ref, ws_ref,
               bs_ref, wout_ref, fnw_ref, o_ref, sf_ref, p_new_ref, y_ref, p_all, y_old):
    project_stage, mix_stage, out_stage = stages
    d = x_ref.shape[-1]
    tile = x_ref.shape[1]
    n_chunks = tile // CHUNK
    half = tile // 2
    row_halves = [slice(0, half), slice(half, tile)]
    c_gr = QK_W
    c_u = c_gr + RET_W
    c_vm = c_u + GM_W
    c_gm = c_vm + GM_W
    lane = lax.broadcasted_iota(jnp.int32, (CHUNK, LANES), 1)
    upper16 = (lane & 16) != 0
    head_of_lane = lane // DK

    operand = {}

    def loaded_once(key, load):
        if key not in operand:
            operand[key] = load()
        return operand[key]

    def project(r, lo, hi):
        rows = row_halves[r]
        lhs = loaded_once(("h", r), lambda: h_ref[0, rows, :])
        p_new_ref[rows, lo:hi] = _dot(lhs, w_ref[:, lo:hi])

    def retention(c, p):
        sl = slice(c * CHUNK, (c + 1) * CHUNK)
        ps = slice(p * LANES, (p + 1) * LANES)
        q2 = _rope(p_all[sl, ps], cos_ref[sl, :], sin_ref[sl, :], upper16)
        kt = kt_ref[0, c, ps, :]
        vp = v_ref[0, sl, p * 2 * DV:(p + 1) * 2 * DV]
        qa = q2 * af_ref[:, ps]
        qb = q2 * ab_ref[:, ps]
        scores, qx = [], []
        for hh in range(2):
            mine = head_of_lane == hh
            scores.append(_dot(jnp.where(mine, q2, 0.0).astype(BF16), kt))
            qx.append(jnp.concatenate([jnp.where(mine, qa, 0.0), jnp.where(mine, qb, 0.0)],
                                      axis=1).astype(BF16))
        kd = (kt.astype(F32) * kdft_ref[ps, :]).astype(BF16)
        yield
        state = jnp.concatenate([sf_ref[ps, :].astype(BF16), stb_ref[0, c, ps, :]], axis=0)
        o = []
        for hh in range(2):
            a = (scores[hh] * dsum_ref[2 * p + hh]).astype(BF16)
            o.append(_dot(a, vp[:, hh * DV:(hh + 1) * DV]) + _dot(qx[hh], state))
        r = _dot(kd, vp)
        yield
        new = jnp.concatenate([r[0:DK, 0:DV], r[DK:2 * DK, DV:2 * DV]], axis=0)
        sf_ref[ps, :] = gf_ref[ps, :] * sf_ref[ps, :] + new
        for hh in range(2):
            head = 2 * p + hh
            mu = jnp.mean(o[hh], axis=-1, keepdims=True)
            var = jnp.mean(jnp.square(o[hh] - mu), axis=-1, keepdims=True)
            hs = slice(head * DV, (head + 1) * DV)
            y = (o[hh] - mu) * lax.rsqrt(var + EPS) * gnw_ref[:, hs]
            gate_cols = slice(c_gr + head * DV, c_gr + (head + 1) * DV)
            y_ref[sl, hs] = (y * _silu(p_all[sl, gate_cols])).astype(BF16)

    def gating(c):
        sl = slice(c * CHUNK, (c + 1) * CHUNK)
        vg = _gelu(p_all[sl, c_vm:c_gm])
        mu = jnp.mean(vg, axis=-1, keepdims=True)
        var = jnp.mean(jnp.square(vg - mu), axis=-1, keepdims=True)
        yield
        vn = ((vg - mu) * lax.rsqrt(var + EPS) * lnw_ref[...]).astype(BF16)
        s = [_dot(ws_ref[g], vn[:, g * DG:(g + 1) * DG]) for g in range(GROUPS)]
        yield
        for g in range(GROUPS):
            if g == GROUPS // 2:
                yield
            u = _gelu(p_all[sl, c_u + g * DG:c_u + (g + 1) * DG])
            gm = _silu(p_all[sl, c_gm + g * DG:c_gm + (g + 1) * DG])
            y_ref[sl, RET_W + g * DG:RET_W + (g + 1) * DG] = (u * (s[g] + bs_ref[g]) * gm).astype(BF16)

    out_groups = _col_groups(d)
    xn = {}

    def out_project(r, g):
        rows = row_halves[r]
        lo, hi = out_groups[g]
        lhs = loaded_once(("y", r), lambda: y_old[rows, :])
        out = _dot(lhs, wout_ref[:, lo:hi])
        xn[r, g] = x_ref[0, rows, lo:hi] + mod[:, 2 * d + lo:2 * d + hi] * out

    def finish(r, part):
        sub = slice(part * half // 2, (part + 1) * half // 2)
        rows = slice(row_halves[r].start + sub.start, row_halves[r].start + sub.stop)
        parts = [xn[r, g][sub, :] for g in range(len(out_groups))]
        ms = sum(jnp.sum(t * t, axis=-1, keepdims=True) for t in parts) * (1.0 / d)
        rs = lax.rsqrt(ms + EPS)
        for (lo, hi), t in zip(out_groups, parts):
            o_ref[0, rows, lo:hi] = t * rs * fnw_ref[:, lo:hi]

    tasks = []
    if mix_stage:
        for c in range(n_chunks):
            base = 3 * c
            tasks += [(base + min(p, 1), retention(c, p), 3) for p in range(PAIRS)]
            tasks.append((base + 1, gating(c), 4))
    big, relaxed = [], {}

    def place(tick, piece):
        while tick in relaxed:
            tick += 1
        relaxed[tick] = piece

    for r in range(2):
        outs = [functools.partial(out_project, r, g) for g in range(len(out_groups))] if out_stage else []
        projs = ([functools.partial(project, r, lo, hi) for lo, hi in _col_groups(p_new_ref.shape[1])]
                 if project_stage else [])
        for i in range(max(len(outs), len(projs))):
            big += outs[i:i + 1] + projs[i:i + 1]
            if i == len(outs) - 1:
                place(len(big) + 1, functools.partial(finish, r, 0))
                place(len(big) + 3, functools.partial(finish, r, 1))
    side = [(None, relaxed.get(tick)) for tick in range(max(relaxed, default=-1) + 1)]
    _emit_pipelined(big, side, tasks)


def _full(shape):
    return pl.BlockSpec(shape, lambda *_: (0,) * len(shape))


def kernel(x, c, ctx, c_ctx, w_ada, b_ada, norm_w, w_in, ret_decay_f, ret_decay_b, ret_gn_w,
           gmlp_ln_w, w_s, b_s, w_out, final_norm_w):
    bsz, seq, d = x.shape
    ctx_len = ctx.shape[1]
    depth = w_ada.shape[0]
    assert depth == 1 and d % LANES == 0
    assert seq % TILE_KV == 0 and seq % TILE_MAIN == 0 and ctx_len % CHUNK == 0
    assert seq % GRID_W == 0 and TILE_MAIN % (4 * CHUNK) == 0 and TILE_KV % (4 * CHUNK) == 0
    n_chunks = seq // CHUNK
    d3 = 3 * d
    n_kv = QK_W + RET_W
    n_rest = w_in.shape[2] - n_kv
    n_gates = n_rest - QK_W
    assert QK_W + n_kv == d and n_gates % d == 0 and d3 // d == 1 + n_gates // d
    assert _col_groups(n_rest)[0] == (0, QK_W)

    quarter = DK // 4
    inv_freq = ROPE_BASE ** (-jnp.arange(quarter, dtype=F32) / quarter)
    invf_lane = jnp.tile(inv_freq, LANES // quarter)[None, :]
    nw = norm_w[0][None, :]
    rows2 = 2 * bsz

    n_col_blocks = d3 // d
    smem = pl.BlockSpec(memory_space=pltpu.SMEM)
    n_out_padded = d + QK_W
    (mod, cos_t, sin_t, dsum, af, ab, kdft, kdbt, gf, gb,
     w_rest, w_kv, w_out_b, w_s_b, b_s_full) = pl.pallas_call(
        _prologue_kernel,
        grid=(n_col_blocks,),
        in_specs=[smem, smem, _full((bsz, d)), _full((1, d)),
                  pl.BlockSpec((d, d), lambda i: (0, i)),
                  pl.BlockSpec((1, d), lambda i: (0, i)),
                  _full((1, LANES)),
                  pl.BlockSpec((d, d), lambda i: (0, i)),
                  _full((RET_W + GM_W, d)),
                  _full((GROUPS, CHUNK, CHUNK)), _full((GROUPS, CHUNK))],
        out_specs=[pl.BlockSpec((rows2, d), lambda i: (0, i)),
                   _full((seq, LANES)), _full((seq, LANES)),
                   _full((HEADS, CHUNK, CHUNK)),
                   _full((CHUNK, QK_W)), _full((CHUNK, QK_W)),
                   _full((QK_W, CHUNK)), _full((QK_W, CHUNK)),
                   _full((QK_W, DV)), _full((QK_W, DV)),
                   _full((d, n_rest)), _full((d, n_kv)),
                   _full((RET_W + GM_W, n_out_padded)),
                   _full((GROUPS, CHUNK, CHUNK)), _full((GROUPS, CHUNK, DG))],
        out_shape=[jax.ShapeDtypeStruct((rows2, d3), F32),
                   jax.ShapeDtypeStruct((seq, LANES), F32),
                   jax.ShapeDtypeStruct((seq, LANES), F32),
                   jax.ShapeDtypeStruct((HEADS, CHUNK, CHUNK), F32),
                   jax.ShapeDtypeStruct((CHUNK, QK_W), F32),
                   jax.ShapeDtypeStruct((CHUNK, QK_W), F32),
                   jax.ShapeDtypeStruct((QK_W, CHUNK), F32),
                   jax.ShapeDtypeStruct((QK_W, CHUNK), F32),
                   jax.ShapeDtypeStruct((QK_W, DV), F32),
                   jax.ShapeDtypeStruct((QK_W, DV), F32),
                   jax.ShapeDtypeStruct((d, n_rest), BF16),
                   jax.ShapeDtypeStruct((d, n_kv), BF16),
                   jax.ShapeDtypeStruct((RET_W + GM_W, n_out_padded), BF16),
                   jax.ShapeDtypeStruct((GROUPS, CHUNK, CHUNK), BF16),
                   jax.ShapeDtypeStruct((GROUPS, CHUNK, DG), F32)],
        compiler_params=pltpu.CompilerParams(dimension_semantics=("arbitrary",)),
        name="prologue",
    )(ret_decay_f[0], ret_decay_b[0], c, c_ctx[None, :], w_ada[0], b_ada[0][None, :], invf_lane,
      w_in[0], w_out[0], w_s[0], b_s[0])

    s_cf, s_cb = pl.pallas_call(
        functools.partial(_ctx_kernel, bsz),
        grid=(bsz + 1,),
        in_specs=[pl.BlockSpec((1, ctx_len, d), lambda s: (jnp.minimum(s, bsz - 1), 0, 0)),
                  _full((rows2, d3)),
                  _full((1, d)), _full((d, n_kv)),
                  _full((QK_W, CHUNK)), _full((QK_W, CHUNK)),
                  _full((QK_W, DV)), _full((QK_W, DV))],
        out_specs=[pl.BlockSpec((1, QK_W, DV), lambda s: (jnp.maximum(s - 1, 0), 0, 0)),
                   pl.BlockSpec((1, QK_W, DV), lambda s: (jnp.maximum(s - 1, 0), 0, 0))],
        out_shape=[jax.ShapeDtypeStruct((bsz, QK_W, DV), F32),
                   jax.ShapeDtypeStruct((bsz, QK_W, DV), F32)],
        scratch_shapes=[pltpu.VMEM((ctx_len, d), BF16),
                        pltpu.VMEM((ctx_len, n_kv), F32),
                        pltpu.VMEM((ctx_len, n_kv), F32)],
        compiler_params=pltpu.CompilerParams(dimension_semantics=("arbitrary",)),
        name="ctx_states",
    )(ctx, mod, nw, w_kv, kdft, kdbt, gf, gb)

    def tile_maps(n_tiles, per_batch, reverse):
        def where(t):
            j = t % per_batch
            return t // per_batch, (per_batch - 1 - j) if reverse else j
        return [lambda s, k=k: where(jnp.clip(s - k, 0, n_tiles - 1)) for k in range(3)]

    nt_kv = seq // TILE_KV
    cpt_kv = TILE_KV // CHUNK
    cur, prv, _ = tile_maps(bsz * nt_kv, nt_kv, reverse=True)
    h, kt, v, st_b = pl.pallas_call(
        functools.partial(_kv_kernel, bsz * nt_kv, nt_kv),
        grid=(bsz * nt_kv + 1,),
        in_specs=[pl.BlockSpec(memory_space=pl.ANY),
                  _full((rows2, d3)),
                  _full((1, d)), _full((d, n_kv)),
                  _full((seq, LANES)), _full((seq, LANES)),
                  _full((QK_W, CHUNK)), _full((QK_W, DV)),
                  pl.BlockSpec((1, QK_W, DV), lambda s: (prv(s)[0], 0, 0))],
        out_specs=[pl.BlockSpec((1, TILE_KV, d), lambda s: (*cur(s), 0)),
                   pl.BlockSpec((1, cpt_kv, QK_W, CHUNK), lambda s: (*prv(s), 0, 0)),
                   pl.BlockSpec((1, TILE_KV, RET_W), lambda s: (*prv(s), 0)),
                   pl.BlockSpec((1, cpt_kv, QK_W, DV), lambda s: (*prv(s), 0, 0))],
        out_shape=[jax.ShapeDtypeStruct((bsz, seq, d), BF16),
                   jax.ShapeDtypeStruct((bsz, n_chunks, QK_W, CHUNK), BF16),
                   jax.ShapeDtypeStruct((bsz, seq, RET_W), BF16),
                   jax.ShapeDtypeStruct((bsz, n_chunks, QK_W, DV), BF16)],
        scratch_shapes=[pltpu.VMEM((QK_W, DV), F32),
                        pltpu.VMEM((TILE_KV, n_kv), F32),
                        pltpu.VMEM((TILE_KV, n_kv), F32),
                        pltpu.VMEM((X_RING, TILE_KV, d), F32),
                        pltpu.SemaphoreType.DMA((X_RING,))],
        compiler_params=pltpu.CompilerParams(dimension_semantics=("arbitrary",)),
        name="kv_sweep",
    )(x, mod, nw, w_kv, cos_t, sin_t, kdbt, gb, s_cb)

    nt = seq // TILE_MAIN
    cpt = TILE_MAIN // CHUNK
    cur, prv, prv2 = tile_maps(bsz * nt, nt, reverse=False)
    out = pl.pallas_call(
        functools.partial(_main_kernel, bsz * nt, nt),
        grid=(bsz * nt + 2,),
        in_specs=[pl.BlockSpec((1, TILE_MAIN, d), lambda s: (*cur(s), 0)),
                  pl.BlockSpec((1, TILE_MAIN, d), lambda s: (*prv2(s), 0)),
                  _full((rows2, d3)),
                  _full((d, n_rest)),
                  pl.BlockSpec((TILE_MAIN, LANES), lambda s: (prv(s)[1], 0)),
                  pl.BlockSpec((TILE_MAIN, LANES), lambda s: (prv(s)[1], 0)),
                  pl.BlockSpec((1, cpt, QK_W, CHUNK), lambda s: (*prv(s), 0, 0)),
                  pl.BlockSpec((1, TILE_MAIN, RET_W), lambda s: (*prv(s), 0)),
                  pl.BlockSpec((1, cpt, QK_W, DV), lambda s: (*prv(s), 0, 0)),
                  pl.BlockSpec((1, QK_W, DV), lambda s: (prv(s)[0], 0, 0)),
                  _full((HEADS, CHUNK, CHUNK)),
                  _full((CHUNK, QK_W)), _full((CHUNK, QK_W)),
                  _full((QK_W, CHUNK)), _full((QK_W, DV)),
                  _full((1, RET_W)), _full((1, GM_W)),
                  _full((GROUPS, CHUNK, CHUNK)), _full((GROUPS, CHUNK, DG)),
                  _full((RET_W + GM_W, n_out_padded)), _full((1, d))],
        out_specs=pl.BlockSpec((1, TILE_MAIN, d), lambda s: (*prv2(s), 0)),
        out_shape=jax.ShapeDtypeStruct((bsz, seq, d), x.dtype),
        scratch_shapes=[pltpu.VMEM((QK_W, DV), F32),
                        pltpu.VMEM((TILE_MAIN, n_rest), F32),
                        pltpu.VMEM((TILE_MAIN, n_rest), F32),
                        pltpu.VMEM((TILE_MAIN, RET_W + GM_W), BF16),
                        pltpu.VMEM((TILE_MAIN, RET_W + GM_W), BF16)],
        compiler_params=pltpu.CompilerParams(dimension_semantics=("arbitrary",)),
        name="main_sweep",
    )(h, x, mod, w_rest, cos_t, sin_t, kt, v, st_b, s_cf, dsum, af, ab, kdft, gf,
      ret_gn_w[0][None, :], gmlp_ln_w[0][None, :], w_s_b, b_s_full, w_out_b,
      final_norm_w[None, :])
    return out
```

```python
import functools

import jax
import jax.numpy as jnp
from jax import lax
from jax.experimental import pallas as pl
from jax.experimental.pallas import tpu as pltpu

F32 = jnp.float32
BF16 = jnp.bfloat16

HEADS = 4
DK = 64
DV = 128
CHUNK = 128
GROUPS = 4
DG = 128
GRID_W = 64
ROPE_BASE = 10000.0
EPS = 1e-6
LANES = 128
PAIRS = HEADS * DK // LANES
RET_W = HEADS * DV
GM_W = GROUPS * DG
QK_W = HEADS * DK

DOT_COLS = 512
TILE_KV = 1024
X_RING = 3
TILE_MAIN = 512


def _rms(x, w):
    ms = jnp.mean(x * x, axis=-1, keepdims=True)
    return x * lax.rsqrt(ms + EPS) * w


def _silu(x):
    return x * jax.nn.sigmoid(x)


def _gelu(x):
    return 0.5 * x * (1.0 + lax.erf(x * (0.5 ** 0.5)))


_dot = functools.partial(jnp.dot, preferred_element_type=F32)


def _col_groups(n):
    first = n % DOT_COLS
    edges = ([0] if first == 0 else [0, first]) + list(range(first + DOT_COLS, n + 1, DOT_COLS))
    return list(zip(edges[:-1], edges[1:]))


def _rope(p, cos, sin_signed, upper16):
    partner = jnp.where(upper16, pltpu.roll(p, 16, 1), pltpu.roll(p, LANES - 16, 1))
    return p * cos + partner * sin_signed


def _kv_pairs(kd, v):
    out = []
    for p in range(PAIRS):
        r = _dot(kd[p * LANES:(p + 1) * LANES, :], v[:, p * 2 * DV:(p + 1) * 2 * DV])
        out.append(r[0:DK, 0:DV])
        out.append(r[DK:2 * DK, DV:2 * DV])
    return jnp.concatenate(out, axis=0)


def _emit_pipelined(big_pieces, side_pieces, tasks):
    in_flight = []
    last_tick = max([start + parts for start, _, parts in tasks] +
                    [len(big_pieces), len(side_pieces)])
    for tick in range(last_tick):
        if tick < len(big_pieces):
            big_pieces[tick]()
        urgent, relaxed = side_pieces[tick] if tick < len(side_pieces) else (None, None)
        if urgent is not None:
            urgent()
        in_flight += [[gen, parts] for start, gen, parts in tasks if start == tick]
        for t in [t for t in in_flight if t[1] > 1] + [t for t in in_flight if t[1] == 1]:
            next(t[0], None)
            t[1] -= 1
        in_flight = [t for t in in_flight if t[1] > 0]
        if relaxed is not None:
            relaxed()
    assert not in_flight


def _run_pipeline_step(step, n_tiles, n_stages, body, refs, bufs_a, bufs_b):
    def call(s_static_parity, stages):
        new, old = (bufs_a, bufs_b) if s_static_parity == 0 else (bufs_b, bufs_a)
        body(stages, *refs, *new, *old)

    edge = n_stages - 1
    for s in list(range(edge)) + list(range(n_tiles, n_tiles + edge)):
        stages = tuple(0 <= s - j < n_tiles for j in range(n_stages))
        pl.when(step == s)(functools.partial(call, s % 2, stages))
    steady = (step >= edge) & (step < n_tiles)
    for parity in range(2):
        pl.when(steady & (lax.rem(step, 2) == parity))(
            functools.partial(call, parity, (True,) * n_stages))


def _prologue_kernel(decf_ref, decb_ref, c_ref, cctx_ref, w_ref, b_ref, invf_ref, wqkv_ref,
                     ws_ref, bs_ref,
                     mod_ref, cos_ref, sin_ref, dsum_ref, af_ref, ab_ref,
                     kdft_ref, kdbt_ref, gf_ref, gb_ref,
                     wkv_ref, wsb_ref, bsf_ref):
    i = pl.program_id(0)
    c = c_ref[...]
    cond = jnp.concatenate([c, jnp.broadcast_to(cctx_ref[...], c.shape)], axis=0)
    mod_ref[...] = _dot(_silu(cond), w_ref[...]) + b_ref[...]

    @pl.when(i == 0)
    def _small_weights():
        wkv_ref[...] = wqkv_ref[:, QK_W:].astype(BF16)
        wsb_ref[...] = ws_ref[...].astype(BF16)
        for g in range(GROUPS):
            bsf_ref[g] = jnp.broadcast_to(bs_ref[g:g + 1, :], (CHUNK, CHUNK)).T

    @pl.when(i == 0)
    def _tables():
        pos = lax.broadcasted_iota(jnp.int32, (GRID_W, LANES), 0).astype(F32)
        lane = lax.broadcasted_iota(jnp.int32, (GRID_W, LANES), 1)
        ang = pos * invf_ref[...]
        cr = jnp.cos(ang)
        sr = jnp.sin(ang)
        sr = jnp.where((lane & 16) == 0, -sr, sr)
        by_row = (lane & 32) == 0
        n_rows = cos_ref.shape[0] // GRID_W
        for r in range(n_rows):
            sl = slice(r * GRID_W, (r + 1) * GRID_W)
            cos_ref[sl, :] = jnp.where(by_row, jnp.broadcast_to(cr[r:r + 1, :], cr.shape), cr)
            sin_ref[sl, :] = jnp.where(by_row, jnp.broadcast_to(sr[r:r + 1, :], sr.shape), sr)

        def log_decay(head_idx, dec_ref):
            d = jnp.full(head_idx.shape, dec_ref[HEADS - 1], F32)
            for h in range(HEADS - 2, -1, -1):
                d = jnp.where(head_idx == h, dec_ref[h], d)
            return -jnp.exp(d)

        n = lax.broadcasted_iota(jnp.int32, (CHUNK, QK_W), 0).astype(F32)
        hl = lax.broadcasted_iota(jnp.int32, (CHUNK, QK_W), 1) // DK
        af_ref[...] = jnp.exp((n + 1.0) * log_decay(hl, decf_ref))
        ab_ref[...] = jnp.exp((CHUNK - n) * log_decay(hl, decb_ref))

        m = lax.broadcasted_iota(jnp.int32, (QK_W, CHUNK), 1).astype(F32)
        hr = lax.broadcasted_iota(jnp.int32, (QK_W, CHUNK), 0) // DK
        lgf = log_decay(hr, decf_ref)
        lgb = log_decay(hr, decb_ref)
        kdft_ref[...] = jnp.exp((CHUNK - 1.0 - m) * lgf)
        kdbt_ref[...] = jnp.exp(m * lgb)
        gf_ref[...] = jnp.exp(float(CHUNK) * lgf)
        gb_ref[...] = jnp.exp(float(CHUNK) * lgb)

        nn = lax.broadcasted_iota(jnp.int32, (CHUNK, CHUNK), 0).astype(F32)
        mm = lax.broadcasted_iota(jnp.int32, (CHUNK, CHUNK), 1).astype(F32)
        for h in range(HEADS):
            lf = -jnp.exp(jnp.full((CHUNK, CHUNK), decf_ref[h], F32))
            lb = -jnp.exp(jnp.full((CHUNK, CHUNK), decb_ref[h], F32))
            df = nn - mm
            db = mm - nn
            fwd = jnp.where(df >= 0, jnp.exp(jnp.maximum(df, 0.0) * lf), 0.0)
            bwd = jnp.where(db >= 0, jnp.exp(jnp.maximum(db, 0.0) * lb), 0.0)
            dsum_ref[h] = fwd + bwd


def _ctx_kernel(n_tiles, n_out_blocks, ctx_ref, mod_ref, nw_ref, wkv_ref, kdft_ref, kdbt_ref,
                gf_ref, gb_ref, win_ref, wout_ref,
                scf_ref, scb_ref, wrest_ref, woutb_ref, h_ref, pa_ref, pb_ref):
    s = pl.program_id(0)
    keep = s < n_out_blocks
    refs = (ctx_ref, mod_ref, nw_ref, wkv_ref, kdft_ref, kdbt_ref, gf_ref, gb_ref,
            win_ref, wout_ref, keep, scf_ref, scb_ref, wrest_ref, woutb_ref, h_ref)
    _run_pipeline_step(s, n_tiles, 2, _ctx_step, refs, (pa_ref,), (pb_ref,))


def _ctx_step(stages, ctx_ref, mod_ref, nw_ref, wkv_ref, kdft_ref, kdbt_ref, gf_ref, gb_ref,
              win_ref, wout_ref, keep, scf_ref, scb_ref, wrest_ref, woutb_ref, h_ref,
              p_new_ref, p_old):
    project_stage, state_stage = stages
    wrest_ref[...] = win_ref[...].astype(BF16)
    woutb_ref[...] = jnp.where(keep, wout_ref[...], 0.0).astype(BF16)
    d = ctx_ref.shape[-1]
    n_chunks = ctx_ref.shape[1] // CHUNK
    ctx_row = mod_ref.shape[0] // 2
    mod = mod_ref[ctx_row:ctx_row + 1, :]

    def norm_rows(rows):
        h = _rms(ctx_ref[0, rows, :], nw_ref[...]) * (1.0 + mod[:, d:2 * d]) + mod[:, 0:d]
        h_ref[rows, :] = h.astype(BF16)

    def project(lo, hi):
        p_new_ref[:, lo:hi] = _dot(h_ref[...], wkv_ref[:, lo:hi])

    new_f, new_b = {}, {}

    def chunk(c):
        sl = slice(c * CHUNK, (c + 1) * CHUNK)
        kt = (p_old[sl, 0:QK_W] * (DK ** -0.5)).T
        v = p_old[sl, QK_W:].astype(BF16)
        new_f[c] = _kv_pairs((kt * kdft_ref[...]).astype(BF16), v)
        new_b[c] = _kv_pairs((kt * kdbt_ref[...]).astype(BF16), v)
        yield

    def combine():
        yield
        sf = jnp.zeros((QK_W, DV), F32)
        for c in range(n_chunks):
            sf = gf_ref[...] * sf + new_f[c]
        sb = jnp.zeros((QK_W, DV), F32)
        for c in range(n_chunks - 1, -1, -1):
            sb = gb_ref[...] * sb + new_b[c]
        scf_ref[0] = sf
        scb_ref[0] = sb

    big, tasks = [], []
    if project_stage:
        for c in range(n_chunks):
            norm_rows(slice(c * CHUNK, (c + 1) * CHUNK))
        big = [functools.partial(project, lo, hi) for lo, hi in _col_groups(p_new_ref.shape[1])]
    if state_stage:
        tasks = [(c, chunk(c), 1) for c in range(n_chunks)] + [(n_chunks - 1, combine(), 2)]
    _emit_pipelined(big, [], tasks)


def _kv_kernel(n_tiles, tiles_per_batch, x_hbm, modc_ref, nw_ref, wkv_ref, cos_ref, sin_ref,
               kdbt_ref, gb_ref, scb_ref, h_ref, kt_ref, v_ref, stb_ref, sb_ref, pa_ref, pb_ref,
               xbuf_ref, xsem_ref):
    s = pl.program_id(0)
    tile = xbuf_ref.shape[1]

    def x_copy(t):
        b = t // tiles_per_batch
        j = tiles_per_batch - 1 - lax.rem(t, tiles_per_batch)
        slot = lax.rem(t, X_RING)
        return pltpu.make_async_copy(x_hbm.at[b, pl.ds(j * tile, tile), :],
                                     xbuf_ref.at[slot], xsem_ref.at[slot])

    @pl.when(s == 0)
    def _prime():
        for t in range(min(X_RING - 1, n_tiles)):
            x_copy(t).start()

    @pl.when(s < n_tiles)
    def _arrived():
        x_copy(s).wait()

    @pl.when(s + X_RING - 1 < n_tiles)
    def _prefetch():
        x_copy(s + X_RING - 1).start()

    @pl.when((s >= 1) & (lax.rem(s - 1, tiles_per_batch) == 0))
    def _init():
        sb_ref[...] = scb_ref[0]

    batch = jnp.minimum(s, n_tiles - 1) // tiles_per_batch
    modc = modc_ref[pl.ds(batch, 1), :]
    xc_ref = xbuf_ref.at[lax.rem(s, X_RING)]
    finished = jnp.clip(s - 1, 0, n_tiles - 1)
    first_row = pl.multiple_of(
        (tiles_per_batch - 1 - lax.rem(finished, tiles_per_batch)) * tile, tile)
    cos_ref = cos_ref.at[pl.ds(first_row, tile), :]
    sin_ref = sin_ref.at[pl.ds(first_row, tile), :]
    refs = (xc_ref, modc, nw_ref, wkv_ref, cos_ref, sin_ref, kdbt_ref, gb_ref,
            h_ref, kt_ref, v_ref, stb_ref, sb_ref)
    _run_pipeline_step(s, n_tiles, 2, _kv_step, refs, (pa_ref,), (pb_ref,))


def _kv_step(stages, xc_ref, modc, nw_ref, wkv_ref, cos_ref, sin_ref, kdbt_ref, gb_ref,
             h_ref, kt_ref, v_ref, stb_ref, sb_ref, p_new_ref, p_old):
    project_stage, finish_stage = stages
    d = xc_ref.shape[-1]
    tile = xc_ref.shape[0]
    half = tile // 2
    row_halves = [slice(0, half), slice(half, tile)]
    quarter = [slice(i * half // 2, (i + 1) * half // 2) for i in range(4)]
    upper16 = (lax.broadcasted_iota(jnp.int32, (CHUNK, LANES), 1) & 16) != 0

    def norm_rows(rows):
        h = _rms(xc_ref[rows, :], nw_ref[...]) * (1.0 + modc[:, d:2 * d]) + modc[:, 0:d]
        h_ref[0, rows, :] = h.astype(BF16)

    def project(rows, lo, hi):
        p_new_ref[rows, lo:hi] = _dot(h_ref[0, rows, :], wkv_ref[:, lo:hi])

    def chunk(c):
        sl = slice(c * CHUNK, (c + 1) * CHUNK)
        k = [_rope(p_old[sl, p * LANES:(p + 1) * LANES], cos_ref[sl, :], sin_ref[sl, :], upper16)
             * (DK ** -0.5) for p in range(PAIRS)]
        v = p_old[sl, QK_W:].astype(BF16)
        v_ref[0, sl, :] = v
        kt = jnp.concatenate([kp.T for kp in k], axis=0)
        kt_ref[0, c] = kt.astype(BF16)
        new = _kv_pairs((kt * kdbt_ref[...]).astype(BF16), v)
        yield
        stb_ref[0, c] = sb_ref[...].astype(BF16)
        sb_ref[...] = gb_ref[...] * sb_ref[...] + new

    projections, late_norms, tasks = [], [], []
    if project_stage:
        norm_rows(quarter[0])
        norm_rows(quarter[1])
        projections = [functools.partial(project, row_halves[r], lo, hi)
                       for r in range(2) for lo, hi in _col_groups(p_new_ref.shape[1])]
        late_norms = [(functools.partial(norm_rows, q), None) for q in quarter[2:]]
    if finish_stage:
        tasks = [(tick, chunk(c), 2) for tick, c in enumerate(range(tile // CHUNK - 1, -1, -1))]
    _emit_pipelined(projections, late_norms, tasks)


def _main_kernel(n_tiles, tiles_per_batch, h_ref, x_ref, mod_ref, w_ref, cos_ref,
                 sin_ref, kt_ref, v_ref, stb_ref, scf_ref, dsum_ref, af_ref, ab_ref, kdft_ref,
                 gf_ref, gnw_ref, lnw_ref, ws_ref, bs_ref, wout_ref, fnw_ref, o_ref,
                 sf_ref, pa_ref, pb_ref, ya_ref, yb_ref):
    s = pl.program_id(0)

    @pl.when((s >= 1) & (lax.rem(s - 1, tiles_per_batch) == 0))
    def _init():
        sf_ref[...] = scf_ref[0]

    batch = jnp.clip(s - 2, 0, n_tiles - 1) // tiles_per_batch
    mod = mod_ref[pl.ds(batch, 1), :]
    refs = (h_ref, x_ref, mod, w_ref, cos_ref, sin_ref, kt_ref, v_ref,
            stb_ref, dsum_ref, af_ref, ab_ref, kdft_ref, gf_ref, gnw_ref, lnw_ref, ws_ref,
            bs_ref, wout_ref, fnw_ref, o_ref, sf_ref)
    _run_pipeline_step(s, n_tiles, 3, _main_step, refs, (pa_ref, ya_ref), (pb_ref, yb_ref))


def _main_step(stages, h_ref, x_ref, mod, w_ref, cos_ref, sin_ref, kt_ref, v_ref,
               stb_ref, dsum_ref, af_ref, ab_ref, kdft_ref, gf_ref, gnw_ref, lnw_ref, ws_ref,
               bs_ref, wout_ref, fnw_ref, o_ref, sf_ref, p_new_ref, y_ref, p_all, y_old):
    project_stage, mix_stage, out_stage = stages
    d = x_ref.shape[-1]
    tile = x_ref.shape[1]
    n_chunks = tile // CHUNK
    half = tile // 2
    row_halves = [slice(0, half), slice(half, tile)]
    c_gr = QK_W
    c_u = c_gr + RET_W
    c_vm = c_u + GM_W
    c_gm = c_vm + GM_W
    lane = lax.broadcasted_iota(jnp.int32, (CHUNK, LANES), 1)
    upper16 = (lane & 16) != 0
    head_of_lane = lane // DK

    operand = {}

    def loaded_once(key, load):
        if key not in operand:
            operand[key] = load()
        return operand[key]

    def project(r, lo, hi):
        rows = row_halves[r]
        lhs = loaded_once(("h", r), lambda: h_ref[0, rows, :])
        p_new_ref[rows, lo:hi] = _dot(lhs, w_ref[:, lo:hi])

    def retention(c, p):
        sl = slice(c * CHUNK, (c + 1) * CHUNK)
        ps = slice(p * LANES, (p + 1) * LANES)
        q2 = _rope(p_all[sl, ps], cos_ref[sl, :], sin_ref[sl, :], upper16)
        kt = kt_ref[0, c, ps, :]
        vp = v_ref[0, sl, p * 2 * DV:(p + 1) * 2 * DV]
        qa = q2 * af_ref[:, ps]
        qb = q2 * ab_ref[:, ps]
        scores, qx = [], []
        for hh in range(2):
            mine = head_of_lane == hh
            scores.append(_dot(jnp.where(mine, q2, 0.0).astype(BF16), kt))
            qx.append(jnp.concatenate([jnp.where(mine, qa, 0.0), jnp.where(mine, qb, 0.0)],
                                      axis=1).astype(BF16))
        kd = (kt.astype(F32) * kdft_ref[ps, :]).astype(BF16)
        yield
        state = jnp.concatenate([sf_ref[ps, :].astype(BF16), stb_ref[0, c, ps, :]], axis=0)
        o = []
        for hh in range(2):
            a = (scores[hh] * dsum_ref[2 * p + hh]).astype(BF16)
            o.append(_dot(a, vp[:, hh * DV:(hh + 1) * DV]) + _dot(qx[hh], state))
        r = _dot(kd, vp)
        yield
        new = jnp.concatenate([r[0:DK, 0:DV], r[DK:2 * DK, DV:2 * DV]], axis=0)
        sf_ref[ps, :] = gf_ref[ps, :] * sf_ref[ps, :] + new
        for hh in range(2):
            head = 2 * p + hh
            mu = jnp.mean(o[hh], axis=-1, keepdims=True)
            var = jnp.mean(jnp.square(o[hh] - mu), axis=-1, keepdims=True)
            hs = slice(head * DV, (head + 1) * DV)
            y = (o[hh] - mu) * lax.rsqrt(var + EPS) * gnw_ref[:, hs]
            gate_cols = slice(c_gr + head * DV, c_gr + (head + 1) * DV)
            y_ref[sl, hs] = (y * _silu(p_all[sl, gate_cols])).astype(BF16)

    def gating(c):
        sl = slice(c * CHUNK, (c + 1) * CHUNK)
        vg = _gelu(p_all[sl, c_vm:c_gm])
        mu = jnp.mean(vg, axis=-1, keepdims=True)
        var = jnp.mean(jnp.square(vg - mu), axis=-1, keepdims=True)
        yield
        vn = ((vg - mu) * lax.rsqrt(var + EPS) * lnw_ref[...]).astype(BF16)
        s = [_dot(ws_ref[g], vn[:, g * DG:(g + 1) * DG]) for g in range(GROUPS)]
        yield
        for g in range(GROUPS):
            if g == GROUPS // 2:
                yield
            u = _gelu(p_all[sl, c_u + g * DG:c_u + (g + 1) * DG])
            gm = _silu(p_all[sl, c_gm + g * DG:c_gm + (g + 1) * DG])
            y_ref[sl, RET_W + g * DG:RET_W + (g + 1) * DG] = (u * (s[g] + bs_ref[g]) * gm).astype(BF16)

    out_groups = _col_groups(d)
    xn = {}

    def out_project(r, g):
        rows = row_halves[r]
        lo, hi = out_groups[g]
        lhs = loaded_once(("y", r), lambda: y_old[rows, :])
        out = _dot(lhs, wout_ref[:, lo:hi])
        xn[r, g] = x_ref[0, rows, lo:hi] + mod[:, 2 * d + lo:2 * d + hi] * out

    def finish(r, part):
        sub = slice(part * half // 2, (part + 1) * half // 2)
        rows = slice(row_halves[r].start + sub.start, row_halves[r].start + sub.stop)
        parts = [xn[r, g][sub, :] for g in range(len(out_groups))]
        ms = sum(jnp.sum(t * t, axis=-1, keepdims=True) for t in parts) * (1.0 / d)
        rs = lax.rsqrt(ms + EPS)
        for (lo, hi), t in zip(out_groups, parts):
            o_ref[0, rows, lo:hi] = t * rs * fnw_ref[:, lo:hi]

    tasks = []
    if mix_stage:
        for c in range(n_chunks):
            base = 3 * c
            tasks += [(base + min(p, 1), retention(c, p), 3) for p in range(PAIRS)]
            tasks.append((base + 1, gating(c), 4))
    big, relaxed = [], {}

    def place(tick, piece):
        while tick in relaxed:
            tick += 1
        relaxed[tick] = piece

    for r in range(2):
        outs = [functools.partial(out_project, r, g) for g in range(len(out_groups))] if out_stage else []
        projs = ([functools.partial(project, r, lo, hi) for lo, hi in _col_groups(p_new_ref.shape[1])]
                 if project_stage else [])
        for i in range(max(len(outs), len(projs))):
            big += outs[i:i + 1] + projs[i:i + 1]
            if i == len(outs) - 1:
                place(len(big) + 1, functools.partial(finish, r, 0))
                place(len(big) + 3, functools.partial(finish, r, 1))
    side = [(None, relaxed.get(tick)) for tick in range(max(relaxed, default=-1) + 1)]
    _emit_pipelined(big, side, tasks)


def _full(shape):
    return pl.BlockSpec(shape, lambda *_: (0,) * len(shape))


def kernel(x, c, ctx, c_ctx, w_ada, b_ada, norm_w, w_in, ret_decay_f, ret_decay_b, ret_gn_w,
           gmlp_ln_w, w_s, b_s, w_out, final_norm_w):
    bsz, seq, d = x.shape
    ctx_len = ctx.shape[1]
    depth = w_ada.shape[0]
    assert depth == 1 and d % LANES == 0
    assert seq % TILE_KV == 0 and seq % TILE_MAIN == 0 and ctx_len % CHUNK == 0
    assert seq % GRID_W == 0 and TILE_MAIN % (4 * CHUNK) == 0 and TILE_KV % (4 * CHUNK) == 0
    n_chunks = seq // CHUNK
    d3 = 3 * d
    n_kv = QK_W + RET_W
    n_rest = w_in.shape[2] - n_kv
    assert QK_W + n_kv == d
    assert _col_groups(n_rest)[0] == (0, QK_W)
    n_rest_blocks = n_rest // QK_W
    n_out_blocks = d // QK_W
    assert n_rest % QK_W == 0 and d % QK_W == 0
    assert max(n_rest_blocks, n_out_blocks + 1) <= bsz + 1

    quarter = DK // 4
    inv_freq = ROPE_BASE ** (-jnp.arange(quarter, dtype=F32) / quarter)
    invf_lane = jnp.tile(inv_freq, LANES // quarter)[None, :]
    nw = norm_w[0][None, :]
    rows2 = 2 * bsz

    n_col_blocks = d3 // d
    smem = pl.BlockSpec(memory_space=pltpu.SMEM)
    n_out_padded = d + QK_W
    (mod, cos_t, sin_t, dsum, af, ab, kdft, kdbt, gf, gb,
     w_kv, w_s_b, b_s_full) = pl.pallas_call(
        _prologue_kernel,
        grid=(n_col_blocks,),
        in_specs=[smem, smem, _full((bsz, d)), _full((1, d)),
                  pl.BlockSpec((d, d), lambda i: (0, i)),
                  pl.BlockSpec((1, d), lambda i: (0, i)),
                  _full((1, LANES)),
                  pl.BlockSpec((d, d), lambda i: (0, 0)),
                  _full((GROUPS, CHUNK, CHUNK)), _full((GROUPS, CHUNK))],
        out_specs=[pl.BlockSpec((rows2, d), lambda i: (0, i)),
                   _full((seq, LANES)), _full((seq, LANES)),
                   _full((HEADS, CHUNK, CHUNK)),
                   _full((CHUNK, QK_W)), _full((CHUNK, QK_W)),
                   _full((QK_W, CHUNK)), _full((QK_W, CHUNK)),
                   _full((QK_W, DV)), _full((QK_W, DV)),
                   _full((d, n_kv)),
                   _full((GROUPS, CHUNK, CHUNK)), _full((GROUPS, CHUNK, DG))],
        out_shape=[jax.ShapeDtypeStruct((rows2, d3), F32),
                   jax.ShapeDtypeStruct((seq, LANES), F32),
                   jax.ShapeDtypeStruct((seq, LANES), F32),
                   jax.ShapeDtypeStruct((HEADS, CHUNK, CHUNK), F32),
                   jax.ShapeDtypeStruct((CHUNK, QK_W), F32),
                   jax.ShapeDtypeStruct((CHUNK, QK_W), F32),
                   jax.ShapeDtypeStruct((QK_W, CHUNK), F32),
                   jax.ShapeDtypeStruct((QK_W, CHUNK), F32),
                   jax.ShapeDtypeStruct((QK_W, DV), F32),
                   jax.ShapeDtypeStruct((QK_W, DV), F32),
                   jax.ShapeDtypeStruct((d, n_kv), BF16),
                   jax.ShapeDtypeStruct((GROUPS, CHUNK, CHUNK), BF16),
                   jax.ShapeDtypeStruct((GROUPS, CHUNK, DG), F32)],
        compiler_params=pltpu.CompilerParams(dimension_semantics=("arbitrary",)),
        name="prologue",
    )(ret_decay_f[0], ret_decay_b[0], c, c_ctx[None, :], w_ada[0], b_ada[0][None, :], invf_lane,
      w_in[0], w_s[0], b_s[0])

    def rest_block(s):
        return jnp.minimum(s, n_rest_blocks - 1)

    def rest_source(s):
        return jnp.where(s == 0, 0, rest_block(s) + n_kv // QK_W)

    s_cf, s_cb, w_rest, w_out_b = pl.pallas_call(
        functools.partial(_ctx_kernel, bsz, n_out_blocks),
        grid=(bsz + 1,),
        in_specs=[pl.BlockSpec((1, ctx_len, d), lambda s: (jnp.minimum(s, bsz - 1), 0, 0)),
                  _full((rows2, d3)),
                  _full((1, d)), _full((d, n_kv)),
                  _full((QK_W, CHUNK)), _full((QK_W, CHUNK)),
                  _full((QK_W, DV)), _full((QK_W, DV)),
                  pl.BlockSpec((d, QK_W), lambda s: (0, rest_source(s))),
                  pl.BlockSpec((RET_W + GM_W, QK_W),
                               lambda s: (0, jnp.minimum(s, n_out_blocks - 1)))],
        out_specs=[pl.BlockSpec((1, QK_W, DV), lambda s: (jnp.maximum(s - 1, 0), 0, 0)),
                   pl.BlockSpec((1, QK_W, DV), lambda s: (jnp.maximum(s - 1, 0), 0, 0)),
                   pl.BlockSpec((d, QK_W), lambda s: (0, rest_block(s))),
                   pl.BlockSpec((RET_W + GM_W, QK_W),
                                lambda s: (0, jnp.minimum(s, n_out_blocks)))],
        out_shape=[jax.ShapeDtypeStruct((bsz, QK_W, DV), F32),
                   jax.ShapeDtypeStruct((bsz, QK_W, DV), F32),
                   jax.ShapeDtypeStruct((d, n_rest), BF16),
                   jax.ShapeDtypeStruct((RET_W + GM_W, n_out_padded), BF16)],
        scratch_shapes=[pltpu.VMEM((ctx_len, d), BF16),
                        pltpu.VMEM((ctx_len, n_kv), F32),
                        pltpu.VMEM((ctx_len, n_kv), F32)],
        compiler_params=pltpu.CompilerParams(dimension_semantics=("arbitrary",)),
        name="ctx_states",
    )(ctx, mod, nw, w_kv, kdft, kdbt, gf, gb, w_in[0], w_out[0])

    def tile_maps(n_tiles, per_batch, reverse):
        def where(t):
            j = t % per_batch
            return t // per_batch, (per_batch - 1 - j) if reverse else j
        return [lambda s, k=k: where(jnp.clip(s - k, 0, n_tiles - 1)) for k in range(3)]

    nt_kv = seq // TILE_KV
    cpt_kv = TILE_KV // CHUNK
    cur, prv, _ = tile_maps(bsz * nt_kv, nt_kv, reverse=True)
    h, kt, v, st_b = pl.pallas_call(
        functools.partial(_kv_kernel, bsz * nt_kv, nt_kv),
        grid=(bsz * nt_kv + 1,),
        in_specs=[pl.BlockSpec(memory_space=pl.ANY),
                  _full((rows2, d3)),
                  _full((1, d)), _full((d, n_kv)),
                  _full((seq, LANES)), _full((seq, LANES)),
                  _full((QK_W, CHUNK)), _full((QK_W, DV)),
                  pl.BlockSpec((1, QK_W, DV), lambda s: (prv(s)[0], 0, 0))],
        out_specs=[pl.BlockSpec((1, TILE_KV, d), lambda s: (*cur(s), 0)),
                   pl.BlockSpec((1, cpt_kv, QK_W, CHUNK), lambda s: (*prv(s), 0, 0)),
                   pl.BlockSpec((1, TILE_KV, RET_W), lambda s: (*prv(s), 0)),
                   pl.BlockSpec((1, cpt_kv, QK_W, DV), lambda s: (*prv(s), 0, 0))],
        out_shape=[jax.ShapeDtypeStruct((bsz, seq, d), BF16),
                   jax.ShapeDtypeStruct((bsz, n_chunks, QK_W, CHUNK), BF16),
                   jax.ShapeDtypeStruct((bsz, seq, RET_W), BF16),
                   jax.ShapeDtypeStruct((bsz, n_chunks, QK_W, DV), BF16)],
        scratch_shapes=[pltpu.VMEM((QK_W, DV), F32),
                        pltpu.VMEM((TILE_KV, n_kv), F32),
                        pltpu.VMEM((TILE_KV, n_kv), F32),
                        pltpu.VMEM((X_RING, TILE_KV, d), F32),
                        pltpu.SemaphoreType.DMA((X_RING,))],
        compiler_params=pltpu.CompilerParams(dimension_semantics=("arbitrary",)),
        name="kv_sweep",
    )(x, mod, nw, w_kv, cos_t, sin_t, kdbt, gb, s_cb)

    nt = seq // TILE_MAIN
    cpt = TILE_MAIN // CHUNK
    cur, prv, prv2 = tile_maps(bsz * nt, nt, reverse=False)
    out = pl.pallas_call(
        functools.partial(_main_kernel, bsz * nt, nt),
        grid=(bsz * nt + 2,),
        in_specs=[pl.BlockSpec((1, TILE_MAIN, d), lambda s: (*cur(s), 0)),
                  pl.BlockSpec((1, TILE_MAIN, d), lambda s: (*prv2(s), 0)),
                  _full((rows2, d3)),
                  _full((d, n_rest)),
                  pl.BlockSpec((TILE_MAIN, LANES), lambda s: (prv(s)[1], 0)),
                  pl.BlockSpec((TILE_MAIN, LANES), lambda s: (prv(s)[1], 0)),
                  pl.BlockSpec((1, cpt, QK_W, CHUNK), lambda s: (*prv(s), 0, 0)),
                  pl.BlockSpec((1, TILE_MAIN, RET_W), lambda s: (*prv(s), 0)),
                  pl.BlockSpec((1, cpt, QK_W, DV), lambda s: (*prv(s), 0, 0)),
                  pl.BlockSpec((1, QK_W, DV), lambda s: (prv(s)[0], 0, 0)),
                  _full((HEADS, CHUNK, CHUNK)),
                  _full((CHUNK, QK_W)), _full((CHUNK, QK_W)),
                  _full((QK_W, CHUNK)), _full((QK_W, DV)),
                  _full((1, RET_W)), _full((1, GM_W)),
                  _full((GROUPS, CHUNK, CHUNK)), _full((GROUPS, CHUNK, DG)),
                  _full((RET_W + GM_W, n_out_padded)), _full((1, d))],
        out_specs=pl.BlockSpec((1, TILE_MAIN, d), lambda s: (*prv2(s), 0)),
        out_shape=jax.ShapeDtypeStruct((bsz, seq, d), x.dtype),
        scratch_shapes=[pltpu.VMEM((QK_W, DV), F32),
                        pltpu.VMEM((TILE_MAIN, n_rest), F32),
                        pltpu.VMEM((TILE_MAIN, n_rest), F32),
                        pltpu.VMEM((TILE_MAIN, RET_W + GM_W), BF16),
                        pltpu.VMEM((TILE_MAIN, RET_W + GM_W), BF16)],
        compiler_params=pltpu.CompilerParams(dimension_semantics=("arbitrary",)),
        name="main_sweep",
    )(h, x, mod, w_rest, cos_t, sin_t, kt, v, st_b, s_cf, dsum, af, ab, kdft, gf,
      ret_gn_w[0][None, :], gmlp_ln_w[0][None, :], w_s_b, b_s_full, w_out_b,
      final_norm_w[None, :])
    return out
```

```python
import functools

import jax
import jax.numpy as jnp
from jax import lax
from jax.experimental import pallas as pl
from jax.experimental.pallas import tpu as pltpu

F32 = jnp.float32
BF16 = jnp.bfloat16

HEADS = 4
DK = 64
DV = 128
CHUNK = 128
GROUPS = 4
DG = 128
GRID_W = 64
ROPE_BASE = 10000.0
EPS = 1e-6
LANES = 128
PAIRS = HEADS * DK // LANES
RET_W = HEADS * DV
GM_W = GROUPS * DG
QK_W = HEADS * DK

DOT_COLS = 512
TILE_KV = 1024
X_RING = 3
TILE_MAIN = 512


def _rms(x, w):
    ms = jnp.mean(x * x, axis=-1, keepdims=True)
    return x * lax.rsqrt(ms + EPS) * w


def _silu(x):
    return x * jax.nn.sigmoid(x)


def _gelu(x):
    return 0.5 * x * (1.0 + lax.erf(x * (0.5 ** 0.5)))


_dot = functools.partial(jnp.dot, preferred_element_type=F32)


def _col_groups(n):
    first = n % DOT_COLS
    edges = ([0] if first == 0 else [0, first]) + list(range(first + DOT_COLS, n + 1, DOT_COLS))
    return list(zip(edges[:-1], edges[1:]))


def _rope(p, cos, sin_signed, upper16):
    partner = jnp.where(upper16, pltpu.roll(p, 16, 1), pltpu.roll(p, LANES - 16, 1))
    return p * cos + partner * sin_signed


def _table_rows(table_ref, tile_in_batch, tile):
    return table_ref.at[pl.ds(pl.multiple_of(tile_in_batch * tile, tile), tile), :]


def _keys_and_state(ks_ref):
    return ks_ref.at[:, :, pl.ds(0, QK_W), :], ks_ref.at[:, :, pl.ds(QK_W, QK_W), :]


def _kv_pairs(kd, v):
    out = []
    for p in range(PAIRS):
        r = _dot(kd[p * LANES:(p + 1) * LANES, :], v[:, p * 2 * DV:(p + 1) * 2 * DV])
        out.append(r[0:DK, 0:DV])
        out.append(r[DK:2 * DK, DV:2 * DV])
    return jnp.concatenate(out, axis=0)


def _emit_pipelined(big_pieces, side_pieces, tasks):
    in_flight = []
    last_tick = max([start + parts for start, _, parts in tasks] +
                    [len(big_pieces), len(side_pieces)])
    for tick in range(last_tick):
        if tick < len(big_pieces):
            big_pieces[tick]()
        urgent, relaxed = side_pieces[tick] if tick < len(side_pieces) else (None, None)
        if urgent is not None:
            urgent()
        in_flight += [[gen, parts] for start, gen, parts in tasks if start == tick]
        for t in [t for t in in_flight if t[1] > 1] + [t for t in in_flight if t[1] == 1]:
            next(t[0], None)
            t[1] -= 1
        in_flight = [t for t in in_flight if t[1] > 0]
        if relaxed is not None:
            relaxed()
    assert not in_flight


def _run_pipeline_step(step, n_tiles, n_stages, body, refs, bufs_a, bufs_b):
    def call(s_static_parity, stages):
        new, old = (bufs_a, bufs_b) if s_static_parity == 0 else (bufs_b, bufs_a)
        body(stages, *refs, *new, *old)

    edge = n_stages - 1
    for s in list(range(edge)) + list(range(n_tiles, n_tiles + edge)):
        stages = tuple(0 <= s - j < n_tiles for j in range(n_stages))
        pl.when(step == s)(functools.partial(call, s % 2, stages))
    steady = (step >= edge) & (step < n_tiles)
    for parity in range(2):
        pl.when(steady & (lax.rem(step, 2) == parity))(
            functools.partial(call, parity, (True,) * n_stages))


def _prologue_kernel(decf_ref, decb_ref, c_ref, cctx_ref, w_ref, b_ref, invf_ref, wqkv_ref,
                     ws_ref, bs_ref,
                     mod_ref, cos_ref, sin_ref, dsum_ref, af_ref, ab_ref,
                     kdft_ref, kdbt_ref, gf_ref, gb_ref,
                     wkv_ref, wsb_ref, bsf_ref):
    i = pl.program_id(0)
    c = c_ref[...]
    cond = jnp.concatenate([c, jnp.broadcast_to(cctx_ref[...], c.shape)], axis=0)
    mod_ref[...] = _dot(_silu(cond), w_ref[...]) + b_ref[...]

    @pl.when(i == 0)
    def _small_weights():
        wkv_ref[...] = wqkv_ref[:, QK_W:].astype(BF16)
        wsb_ref[...] = ws_ref[...].astype(BF16)
        for g in range(GROUPS):
            bsf_ref[g] = jnp.broadcast_to(bs_ref[g:g + 1, :], (CHUNK, CHUNK)).T

    @pl.when(i == 0)
    def _tables():
        pos = lax.broadcasted_iota(jnp.int32, (GRID_W, LANES), 0).astype(F32)
        lane = lax.broadcasted_iota(jnp.int32, (GRID_W, LANES), 1)
        ang = pos * invf_ref[...]
        cr = jnp.cos(ang)
        sr = jnp.sin(ang)
        sr = jnp.where((lane & 16) == 0, -sr, sr)
        by_row = (lane & 32) == 0
        n_rows = cos_ref.shape[0] // GRID_W
        for r in range(n_rows):
            sl = slice(r * GRID_W, (r + 1) * GRID_W)
            cos_ref[sl, :] = jnp.where(by_row, jnp.broadcast_to(cr[r:r + 1, :], cr.shape), cr)
            sin_ref[sl, :] = jnp.where(by_row, jnp.broadcast_to(sr[r:r + 1, :], sr.shape), sr)

        def log_decay(head_idx, dec_ref):
            d = jnp.full(head_idx.shape, dec_ref[HEADS - 1], F32)
            for h in range(HEADS - 2, -1, -1):
                d = jnp.where(head_idx == h, dec_ref[h], d)
            return -jnp.exp(d)

        n = lax.broadcasted_iota(jnp.int32, (CHUNK, QK_W), 0).astype(F32)
        hl = lax.broadcasted_iota(jnp.int32, (CHUNK, QK_W), 1) // DK
        af_ref[...] = jnp.exp((n + 1.0) * log_decay(hl, decf_ref))
        ab_ref[...] = jnp.exp((CHUNK - n) * log_decay(hl, decb_ref))

        m = lax.broadcasted_iota(jnp.int32, (QK_W, CHUNK), 1).astype(F32)
        hr = lax.broadcasted_iota(jnp.int32, (QK_W, CHUNK), 0) // DK
        lgf = log_decay(hr, decf_ref)
        lgb = log_decay(hr, decb_ref)
        kdft_ref[...] = jnp.exp((CHUNK - 1.0 - m) * lgf)
        kdbt_ref[...] = jnp.exp(m * lgb)
        gf_ref[...] = jnp.exp(float(CHUNK) * lgf)
        gb_ref[...] = jnp.exp(float(CHUNK) * lgb)

        nn = lax.broadcasted_iota(jnp.int32, (CHUNK, CHUNK), 0).astype(F32)
        mm = lax.broadcasted_iota(jnp.int32, (CHUNK, CHUNK), 1).astype(F32)
        for h in range(HEADS):
            lf = -jnp.exp(jnp.full((CHUNK, CHUNK), decf_ref[h], F32))
            lb = -jnp.exp(jnp.full((CHUNK, CHUNK), decb_ref[h], F32))
            df = nn - mm
            db = mm - nn
            fwd = jnp.where(df >= 0, jnp.exp(jnp.maximum(df, 0.0) * lf), 0.0)
            bwd = jnp.where(db >= 0, jnp.exp(jnp.maximum(db, 0.0) * lb), 0.0)
            dsum_ref[h] = fwd + bwd


def _ctx_kernel(n_tiles, n_out_blocks, ctx_ref, mod_ref, nw_ref, wkv_ref, kdft_ref, kdbt_ref,
                gf_ref, gb_ref, win_ref, wout_ref,
                scf_ref, scb_ref, wrest_ref, woutb_ref, h_ref, pa_ref, pb_ref):
    s = pl.program_id(0)
    keep = s < n_out_blocks
    refs = (ctx_ref, mod_ref, nw_ref, wkv_ref, kdft_ref, kdbt_ref, gf_ref, gb_ref,
            win_ref, wout_ref, keep, scf_ref, scb_ref, wrest_ref, woutb_ref, h_ref)
    _run_pipeline_step(s, n_tiles, 2, _ctx_step, refs, (pa_ref,), (pb_ref,))


def _ctx_step(stages, ctx_ref, mod_ref, nw_ref, wkv_ref, kdft_ref, kdbt_ref, gf_ref, gb_ref,
              win_ref, wout_ref, keep, scf_ref, scb_ref, wrest_ref, woutb_ref, h_ref,
              p_new_ref, p_old):
    project_stage, state_stage = stages
    wrest_ref[...] = win_ref[...].astype(BF16)
    woutb_ref[...] = jnp.where(keep, wout_ref[...], 0.0).astype(BF16)
    d = ctx_ref.shape[-1]
    n_chunks = ctx_ref.shape[1] // CHUNK
    ctx_row = mod_ref.shape[0] // 2
    mod = mod_ref[ctx_row:ctx_row + 1, :]

    def norm_rows(rows):
        h = _rms(ctx_ref[0, rows, :], nw_ref[...]) * (1.0 + mod[:, d:2 * d]) + mod[:, 0:d]
        h_ref[rows, :] = h.astype(BF16)

    def project(lo, hi):
        p_new_ref[:, lo:hi] = _dot(h_ref[...], wkv_ref[:, lo:hi])

    new_f, new_b = {}, {}

    def chunk(c):
        sl = slice(c * CHUNK, (c + 1) * CHUNK)
        kt = (p_old[sl, 0:QK_W] * (DK ** -0.5)).T
        v = p_old[sl, QK_W:].astype(BF16)
        new_f[c] = _kv_pairs((kt * kdft_ref[...]).astype(BF16), v)
        new_b[c] = _kv_pairs((kt * kdbt_ref[...]).astype(BF16), v)
        yield

    def combine():
        yield
        sf = jnp.zeros((QK_W, DV), F32)
        for c in range(n_chunks):
            sf = gf_ref[...] * sf + new_f[c]
        sb = jnp.zeros((QK_W, DV), F32)
        for c in range(n_chunks - 1, -1, -1):
            sb = gb_ref[...] * sb + new_b[c]
        scf_ref[0] = sf
        scb_ref[0] = sb

    big, tasks = [], []
    if project_stage:
        for c in range(n_chunks):
            norm_rows(slice(c * CHUNK, (c + 1) * CHUNK))
        big = [functools.partial(project, lo, hi) for lo, hi in _col_groups(p_new_ref.shape[1])]
    if state_stage:
        tasks = [(c, chunk(c), 1) for c in range(n_chunks)] + [(n_chunks - 1, combine(), 2)]
    _emit_pipelined(big, [], tasks)


def _kv_kernel(n_tiles, tiles_per_batch, x_hbm, modc_ref, nw_ref, wkv_ref, cos_ref, sin_ref,
               kdbt_ref, gb_ref, scb_ref, h_ref, ks_ref, v_ref, sb_ref, pa_ref, pb_ref,
               xbuf_ref, xsem_ref):
    s = pl.program_id(0)
    tile = xbuf_ref.shape[1]

    def x_copy(t):
        b = t // tiles_per_batch
        j = tiles_per_batch - 1 - lax.rem(t, tiles_per_batch)
        slot = lax.rem(t, X_RING)
        return pltpu.make_async_copy(x_hbm.at[b, pl.ds(j * tile, tile), :],
                                     xbuf_ref.at[slot], xsem_ref.at[slot])

    @pl.when(s == 0)
    def _prime():
        for t in range(min(X_RING - 1, n_tiles)):
            x_copy(t).start()

    @pl.when(s < n_tiles)
    def _arrived():
        x_copy(s).wait()

    @pl.when(s + X_RING - 1 < n_tiles)
    def _prefetch():
        x_copy(s + X_RING - 1).start()

    @pl.when((s >= 1) & (lax.rem(s - 1, tiles_per_batch) == 0))
    def _init():
        sb_ref[...] = scb_ref[0]

    batch = jnp.minimum(s, n_tiles - 1) // tiles_per_batch
    modc = modc_ref[pl.ds(batch, 1), :]
    xc_ref = xbuf_ref.at[lax.rem(s, X_RING)]
    rotated = tiles_per_batch - 1 - lax.rem(jnp.clip(s - 1, 0, n_tiles - 1), tiles_per_batch)
    cos_ref = _table_rows(cos_ref, rotated, tile)
    sin_ref = _table_rows(sin_ref, rotated, tile)
    kt_ref, stb_ref = _keys_and_state(ks_ref)
    refs = (xc_ref, modc, nw_ref, wkv_ref, cos_ref, sin_ref, kdbt_ref, gb_ref,
            h_ref, kt_ref, v_ref, stb_ref, sb_ref)
    _run_pipeline_step(s, n_tiles, 2, _kv_step, refs, (pa_ref,), (pb_ref,))


def _kv_step(stages, xc_ref, modc, nw_ref, wkv_ref, cos_ref, sin_ref, kdbt_ref, gb_ref,
             h_ref, kt_ref, v_ref, stb_ref, sb_ref, p_new_ref, p_old):
    project_stage, finish_stage = stages
    d = xc_ref.shape[-1]
    tile = xc_ref.shape[0]
    half = tile // 2
    row_halves = [slice(0, half), slice(half, tile)]
    quarter = [slice(i * half // 2, (i + 1) * half // 2) for i in range(4)]
    upper16 = (lax.broadcasted_iota(jnp.int32, (CHUNK, LANES), 1) & 16) != 0

    def norm_rows(rows):
        h = _rms(xc_ref[rows, :], nw_ref[...]) * (1.0 + modc[:, d:2 * d]) + modc[:, 0:d]
        h_ref[0, rows, :] = h.astype(BF16)

    def project(rows, lo, hi):
        p_new_ref[rows, lo:hi] = _dot(h_ref[0, rows, :], wkv_ref[:, lo:hi])

    def chunk(c):
        sl = slice(c * CHUNK, (c + 1) * CHUNK)
        k = [_rope(p_old[sl, p * LANES:(p + 1) * LANES], cos_ref[sl, :], sin_ref[sl, :], upper16)
             * (DK ** -0.5) for p in range(PAIRS)]
        v = p_old[sl, QK_W:].astype(BF16)
        v_ref[0, sl, :] = v
        kt = jnp.concatenate([kp.T for kp in k], axis=0)
        kt_ref[0, c] = kt.astype(BF16)
        new = _kv_pairs((kt * kdbt_ref[...]).astype(BF16), v)
        yield
        stb_ref[0, c] = sb_ref[...].astype(BF16)
        sb_ref[...] = gb_ref[...] * sb_ref[...] + new

    projections, late_norms, tasks = [], [], []
    if project_stage:
        norm_rows(quarter[0])
        norm_rows(quarter[1])
        projections = [functools.partial(project, row_halves[r], lo, hi)
                       for r in range(2) for lo, hi in _col_groups(p_new_ref.shape[1])]
        late_norms = [(functools.partial(norm_rows, q), None) for q in quarter[2:]]
    if finish_stage:
        tasks = [(tick, chunk(c), 2) for tick, c in enumerate(range(tile // CHUNK - 1, -1, -1))]
    _emit_pipelined(projections, late_norms, tasks)


def _main_kernel(n_tiles, tiles_per_batch, h_ref, x_ref, mod_ref, w_ref, cos_ref,
                 sin_ref, ks_ref, v_ref, scf_ref, dsum_ref, af_ref, ab_ref, kdft_ref,
                 gf_ref, gnw_ref, lnw_ref, ws_ref, bs_ref, wout_ref, fnw_ref, o_ref,
                 sf_ref, pa_ref, pb_ref, ya_ref, yb_ref):
    s = pl.program_id(0)

    @pl.when((s >= 1) & (lax.rem(s - 1, tiles_per_batch) == 0))
    def _init():
        sf_ref[...] = scf_ref[0]

    batch = jnp.clip(s - 2, 0, n_tiles - 1) // tiles_per_batch
    mod = mod_ref[pl.ds(batch, 1), :]
    rotated = lax.rem(jnp.clip(s - 1, 0, n_tiles - 1), tiles_per_batch)
    cos_ref = _table_rows(cos_ref, rotated, h_ref.shape[1])
    sin_ref = _table_rows(sin_ref, rotated, h_ref.shape[1])
    kt_ref, stb_ref = _keys_and_state(ks_ref)
    refs = (h_ref, x_ref, mod, w_ref, cos_ref, sin_ref, kt_ref, v_ref,
            stb_ref, dsum_ref, af_ref, ab_ref, kdft_ref, gf_ref, gnw_ref, lnw_ref, ws_ref,
            bs_ref, wout_ref, fnw_ref, o_ref, sf_ref)
    _run_pipeline_step(s, n_tiles, 3, _main_step, refs, (pa_ref, ya_ref), (pb_ref, yb_ref))


def _main_step(stages, h_ref, x_ref, mod, w_ref, cos_ref, sin_ref, kt_ref, v_ref,
               stb_ref, dsum_ref, af_ref, ab_ref, kdft_ref, gf_ref, gnw_ref, lnw_ref, ws_ref,
               bs_ref, wout_ref, fnw_ref, o_ref, sf_ref, p_new_ref, y_ref, p_all, y_old):
    project_stage, mix_stage, out_stage = stages
    d = x_ref.shape[-1]
    tile = x_ref.shape[1]
    n_chunks = tile // CHUNK
    half = tile // 2
    row_halves = [slice(0, half), slice(half, tile)]
    c_gr = QK_W
    c_u = c_gr + RET_W
    c_vm = c_u + GM_W
    c_gm = c_vm + GM_W
    lane = lax.broadcasted_iota(jnp.int32, (CHUNK, LANES), 1)
    upper16 = (lane & 16) != 0
    head_of_lane = lane // DK

    operand = {}

    def loaded_once(key, load):
        if key not in operand:
            operand[key] = load()
        return operand[key]

    def project(r, lo, hi):
        rows = row_halves[r]
        lhs = loaded_once(("h", r), lambda: h_ref[0, rows, :])
        p_new_ref[rows, lo:hi] = _dot(lhs, w_ref[:, lo:hi])

    def retention(c, p):
        sl = slice(c * CHUNK, (c + 1) * CHUNK)
        ps = slice(p * LANES, (p + 1) * LANES)
        q2 = _rope(p_all[sl, ps], cos_ref[sl, :], sin_ref[sl, :], upper16)
        kt = kt_ref[0, c, ps, :]
        vp = v_ref[0, sl, p * 2 * DV:(p + 1) * 2 * DV]
        qa = q2 * af_ref[:, ps]
        qb = q2 * ab_ref[:, ps]
        scores, qx = [], []
        for hh in range(2):
            mine = head_of_lane == hh
            scores.append(_dot(jnp.where(mine, q2, 0.0).astype(BF16), kt))
            qx.append(jnp.concatenate([jnp.where(mine, qa, 0.0), jnp.where(mine, qb, 0.0)],
                                      axis=1).astype(BF16))
        kd = (kt.astype(F32) * kdft_ref[ps, :]).astype(BF16)
        yield
        state = jnp.concatenate([sf_ref[ps, :].astype(BF16), stb_ref[0, c, ps, :]], axis=0)
        o = []
        for hh in range(2):
            a = (scores[hh] * dsum_ref[2 * p + hh]).astype(BF16)
            o.append(_dot(a, vp[:, hh * DV:(hh + 1) * DV]) + _dot(qx[hh], state))
        r = _dot(kd, vp)
        yield
        new = jnp.concatenate([r[0:DK, 0:DV], r[DK:2 * DK, DV:2 * DV]], axis=0)
        sf_ref[ps, :] = gf_ref[ps, :] * sf_ref[ps, :] + new
        for hh in range(2):
            head = 2 * p + hh
            mu = jnp.mean(o[hh], axis=-1, keepdims=True)
            var = jnp.mean(jnp.square(o[hh] - mu), axis=-1, keepdims=True)
            hs = slice(head * DV, (head + 1) * DV)
            y = (o[hh] - mu) * lax.rsqrt(var + EPS) * gnw_ref[:, hs]
            gate_cols = slice(c_gr + head * DV, c_gr + (head + 1) * DV)
            y_ref[sl, hs] = (y * _silu(p_all[sl, gate_cols])).astype(BF16)

    def gating(c):
        sl = slice(c * CHUNK, (c + 1) * CHUNK)
        vg = _gelu(p_all[sl, c_vm:c_gm])
        mu = jnp.mean(vg, axis=-1, keepdims=True)
        var = jnp.mean(jnp.square(vg - mu), axis=-1, keepdims=True)
        yield
        vn = ((vg - mu) * lax.rsqrt(var + EPS) * lnw_ref[...]).astype(BF16)
        s = [_dot(ws_ref[g], vn[:, g * DG:(g + 1) * DG]) for g in range(GROUPS)]
        yield
        for g in range(GROUPS):
            if g == GROUPS // 2:
                yield
            u = _gelu(p_all[sl, c_u + g * DG:c_u + (g + 1) * DG])
            gm = _silu(p_all[sl, c_gm + g * DG:c_gm + (g + 1) * DG])
            y_ref[sl, RET_W + g * DG:RET_W + (g + 1) * DG] = (u * (s[g] + bs_ref[g]) * gm).astype(BF16)

    out_groups = _col_groups(d)
    xn = {}

    def out_project(r, g):
        rows = row_halves[r]
        lo, hi = out_groups[g]
        lhs = loaded_once(("y", r), lambda: y_old[rows, :])
        out = _dot(lhs, wout_ref[:, lo:hi])
        xn[r, g] = x_ref[0, rows, lo:hi] + mod[:, 2 * d + lo:2 * d + hi] * out

    def finish(r, part):
        sub = slice(part * half // 2, (part + 1) * half // 2)
        rows = slice(row_halves[r].start + sub.start, row_halves[r].start + sub.stop)
        parts = [xn[r, g][sub, :] for g in range(len(out_groups))]
        ms = sum(jnp.sum(t * t, axis=-1, keepdims=True) for t in parts) * (1.0 / d)
        rs = lax.rsqrt(ms + EPS)
        for (lo, hi), t in zip(out_groups, parts):
            o_ref[0, rows, lo:hi] = t * rs * fnw_ref[:, lo:hi]

    tasks = []
    if mix_stage:
        for c in range(n_chunks):
            base = 3 * c
            tasks += [(base + min(p, 1), retention(c, p), 3) for p in range(PAIRS)]
            tasks.append((base + 1, gating(c), 4))
    big, relaxed = [], {}

    def place(tick, piece):
        while tick in relaxed:
            tick += 1
        relaxed[tick] = piece

    for r in range(2):
        outs = [functools.partial(out_project, r, g) for g in range(len(out_groups))] if out_stage else []
        projs = ([functools.partial(project, r, lo, hi) for lo, hi in _col_groups(p_new_ref.shape[1])]
                 if project_stage else [])
        for i in range(max(len(outs), len(projs))):
            big += outs[i:i + 1] + projs[i:i + 1]
            if i == len(outs) - 1:
                place(len(big) + 1, functools.partial(finish, r, 0))
                place(len(big) + 3, functools.partial(finish, r, 1))
    side = [(None, relaxed.get(tick)) for tick in range(max(relaxed, default=-1) + 1)]
    _emit_pipelined(big, side, tasks)


def _full(shape):
    return pl.BlockSpec(shape, lambda *_: (0,) * len(shape))


def kernel(x, c, ctx, c_ctx, w_ada, b_ada, norm_w, w_in, ret_decay_f, ret_decay_b, ret_gn_w,
           gmlp_ln_w, w_s, b_s, w_out, final_norm_w):
    bsz, seq, d = x.shape
    ctx_len = ctx.shape[1]
    depth = w_ada.shape[0]
    assert depth == 1 and d % LANES == 0
    assert seq % TILE_KV == 0 and seq % TILE_MAIN == 0 and ctx_len % CHUNK == 0
    assert seq % GRID_W == 0 and TILE_MAIN % (4 * CHUNK) == 0 and TILE_KV % (4 * CHUNK) == 0
    n_chunks = seq // CHUNK
    d3 = 3 * d
    n_kv = QK_W + RET_W
    n_rest = w_in.shape[2] - n_kv
    assert QK_W + n_kv == d
    assert _col_groups(n_rest)[0] == (0, QK_W)
    n_rest_blocks = n_rest // QK_W
    n_out_blocks = d // QK_W
    assert n_rest % QK_W == 0 and d % QK_W == 0
    assert max(n_rest_blocks, n_out_blocks + 1) <= bsz + 1

    quarter = DK // 4
    inv_freq = ROPE_BASE ** (-jnp.arange(quarter, dtype=F32) / quarter)
    invf_lane = jnp.tile(inv_freq, LANES // quarter)[None, :]
    nw = norm_w[0][None, :]
    rows2 = 2 * bsz

    n_col_blocks = d3 // d
    smem = pl.BlockSpec(memory_space=pltpu.SMEM)
    n_out_padded = d + QK_W
    (mod, cos_t, sin_t, dsum, af, ab, kdft, kdbt, gf, gb,
     w_kv, w_s_b, b_s_full) = pl.pallas_call(
        _prologue_kernel,
        grid=(n_col_blocks,),
        in_specs=[smem, smem, _full((bsz, d)), _full((1, d)),
                  pl.BlockSpec((d, d), lambda i: (0, i)),
                  pl.BlockSpec((1, d), lambda i: (0, i)),
                  _full((1, LANES)),
                  pl.BlockSpec((d, d), lambda i: (0, 0)),
                  _full((GROUPS, CHUNK, CHUNK)), _full((GROUPS, CHUNK))],
        out_specs=[pl.BlockSpec((rows2, d), lambda i: (0, i)),
                   _full((seq, LANES)), _full((seq, LANES)),
                   _full((HEADS, CHUNK, CHUNK)),
                   _full((CHUNK, QK_W)), _full((CHUNK, QK_W)),
                   _full((QK_W, CHUNK)), _full((QK_W, CHUNK)),
                   _full((QK_W, DV)), _full((QK_W, DV)),
                   _full((d, n_kv)),
                   _full((GROUPS, CHUNK, CHUNK)), _full((GROUPS, CHUNK, DG))],
        out_shape=[jax.ShapeDtypeStruct((rows2, d3), F32),
                   jax.ShapeDtypeStruct((seq, LANES), F32),
                   jax.ShapeDtypeStruct((seq, LANES), F32),
                   jax.ShapeDtypeStruct((HEADS, CHUNK, CHUNK), F32),
                   jax.ShapeDtypeStruct((CHUNK, QK_W), F32),
                   jax.ShapeDtypeStruct((CHUNK, QK_W), F32),
                   jax.ShapeDtypeStruct((QK_W, CHUNK), F32),
                   jax.ShapeDtypeStruct((QK_W, CHUNK), F32),
                   jax.ShapeDtypeStruct((QK_W, DV), F32),
                   jax.ShapeDtypeStruct((QK_W, DV), F32),
                   jax.ShapeDtypeStruct((d, n_kv), BF16),
                   jax.ShapeDtypeStruct((GROUPS, CHUNK, CHUNK), BF16),
                   jax.ShapeDtypeStruct((GROUPS, CHUNK, DG), F32)],
        compiler_params=pltpu.CompilerParams(dimension_semantics=("arbitrary",)),
        name="prologue",
    )(ret_decay_f[0], ret_decay_b[0], c, c_ctx[None, :], w_ada[0], b_ada[0][None, :], invf_lane,
      w_in[0], w_s[0], b_s[0])

    def rest_block(s):
        return jnp.minimum(s, n_rest_blocks - 1)

    def rest_source(s):
        return jnp.where(s == 0, 0, rest_block(s) + n_kv // QK_W)

    s_cf, s_cb, w_rest, w_out_b = pl.pallas_call(
        functools.partial(_ctx_kernel, bsz, n_out_blocks),
        grid=(bsz + 1,),
        in_specs=[pl.BlockSpec((1, ctx_len, d), lambda s: (jnp.minimum(s, bsz - 1), 0, 0)),
                  _full((rows2, d3)),
                  _full((1, d)), _full((d, n_kv)),
                  _full((QK_W, CHUNK)), _full((QK_W, CHUNK)),
                  _full((QK_W, DV)), _full((QK_W, DV)),
                  pl.BlockSpec((d, QK_W), lambda s: (0, rest_source(s))),
                  pl.BlockSpec((RET_W + GM_W, QK_W),
                               lambda s: (0, jnp.minimum(s, n_out_blocks - 1)))],
        out_specs=[pl.BlockSpec((1, QK_W, DV), lambda s: (jnp.maximum(s - 1, 0), 0, 0)),
                   pl.BlockSpec((1, QK_W, DV), lambda s: (jnp.maximum(s - 1, 0), 0, 0)),
                   pl.BlockSpec((d, QK_W), lambda s: (0, rest_block(s))),
                   pl.BlockSpec((RET_W + GM_W, QK_W),
                                lambda s: (0, jnp.minimum(s, n_out_blocks)))],
        out_shape=[jax.ShapeDtypeStruct((bsz, QK_W, DV), F32),
                   jax.ShapeDtypeStruct((bsz, QK_W, DV), F32),
                   jax.ShapeDtypeStruct((d, n_rest), BF16),
                   jax.ShapeDtypeStruct((RET_W + GM_W, n_out_padded), BF16)],
        scratch_shapes=[pltpu.VMEM((ctx_len, d), BF16),
                        pltpu.VMEM((ctx_len, n_kv), F32),
                        pltpu.VMEM((ctx_len, n_kv), F32)],
        compiler_params=pltpu.CompilerParams(dimension_semantics=("arbitrary",)),
        name="ctx_states",
    )(ctx, mod, nw, w_kv, kdft, kdbt, gf, gb, w_in[0], w_out[0])

    def tile_maps(n_tiles, per_batch, reverse):
        def where(t):
            j = t % per_batch
            return t // per_batch, (per_batch - 1 - j) if reverse else j
        return [lambda s, k=k: where(jnp.clip(s - k, 0, n_tiles - 1)) for k in range(3)]

    nt_kv = seq // TILE_KV
    cpt_kv = TILE_KV // CHUNK
    cur, prv, _ = tile_maps(bsz * nt_kv, nt_kv, reverse=True)
    assert CHUNK == DV
    h, keys_states, v = pl.pallas_call(
        functools.partial(_kv_kernel, bsz * nt_kv, nt_kv),
        grid=(bsz * nt_kv + 1,),
        in_specs=[pl.BlockSpec(memory_space=pl.ANY),
                  _full((rows2, d3)),
                  _full((1, d)), _full((d, n_kv)),
                  _full((seq, LANES)), _full((seq, LANES)),
                  _full((QK_W, CHUNK)), _full((QK_W, DV)),
                  pl.BlockSpec((1, QK_W, DV), lambda s: (prv(s)[0], 0, 0))],
        out_specs=[pl.BlockSpec((1, TILE_KV, d), lambda s: (*cur(s), 0)),
                   pl.BlockSpec((1, cpt_kv, 2 * QK_W, DV), lambda s: (*prv(s), 0, 0)),
                   pl.BlockSpec((1, TILE_KV, RET_W), lambda s: (*prv(s), 0))],
        out_shape=[jax.ShapeDtypeStruct((bsz, seq, d), BF16),
                   jax.ShapeDtypeStruct((bsz, n_chunks, 2 * QK_W, DV), BF16),
                   jax.ShapeDtypeStruct((bsz, seq, RET_W), BF16)],
        scratch_shapes=[pltpu.VMEM((QK_W, DV), F32),
                        pltpu.VMEM((TILE_KV, n_kv), F32),
                        pltpu.VMEM((TILE_KV, n_kv), F32),
                        pltpu.VMEM((X_RING, TILE_KV, d), F32),
                        pltpu.SemaphoreType.DMA((X_RING,))],
        compiler_params=pltpu.CompilerParams(dimension_semantics=("arbitrary",)),
        name="kv_sweep",
    )(x, mod, nw, w_kv, cos_t, sin_t, kdbt, gb, s_cb)

    nt = seq // TILE_MAIN
    cpt = TILE_MAIN // CHUNK
    cur, prv, prv2 = tile_maps(bsz * nt, nt, reverse=False)
    out = pl.pallas_call(
        functools.partial(_main_kernel, bsz * nt, nt),
        grid=(bsz * nt + 2,),
        in_specs=[pl.BlockSpec((1, TILE_MAIN, d), lambda s: (*cur(s), 0)),
                  pl.BlockSpec((1, TILE_MAIN, d), lambda s: (*prv2(s), 0)),
                  _full((rows2, d3)),
                  _full((d, n_rest)),
                  _full((seq, LANES)), _full((seq, LANES)),
                  pl.BlockSpec((1, cpt, 2 * QK_W, DV), lambda s: (*prv(s), 0, 0)),
                  pl.BlockSpec((1, TILE_MAIN, RET_W), lambda s: (*prv(s), 0)),
                  pl.BlockSpec((1, QK_W, DV), lambda s: (prv(s)[0], 0, 0)),
                  _full((HEADS, CHUNK, CHUNK)),
                  _full((CHUNK, QK_W)), _full((CHUNK, QK_W)),
                  _full((QK_W, CHUNK)), _full((QK_W, DV)),
                  _full((1, RET_W)), _full((1, GM_W)),
                  _full((GROUPS, CHUNK, CHUNK)), _full((GROUPS, CHUNK, DG)),
                  _full((RET_W + GM_W, n_out_padded)), _full((1, d))],
        out_specs=pl.BlockSpec((1, TILE_MAIN, d), lambda s: (*prv2(s), 0)),
        out_shape=jax.ShapeDtypeStruct((bsz, seq, d), x.dtype),
        scratch_shapes=[pltpu.VMEM((QK_W, DV), F32),
                        pltpu.VMEM((TILE_MAIN, n_rest), F32),
                        pltpu.VMEM((TILE_MAIN, n_rest), F32),
                        pltpu.VMEM((TILE_MAIN, RET_W + GM_W), BF16),
                        pltpu.VMEM((TILE_MAIN, RET_W + GM_W), BF16)],
        compiler_params=pltpu.CompilerParams(dimension_semantics=("arbitrary",)),
        name="main_sweep",
    )(h, x, mod, w_rest, cos_t, sin_t, keys_states, v, s_cf, dsum, af, ab, kdft, gf,
      ret_gn_w[0][None, :], gmlp_ln_w[0][None, :], w_s_b, b_s_full, w_out_b,
      final_norm_w[None, :])
    return out
```

```python
import functools

import jax
import jax.numpy as jnp
from jax import lax
from jax.experimental import pallas as pl
from jax.experimental.pallas import tpu as pltpu

F32 = jnp.float32
BF16 = jnp.bfloat16

HEADS = 4
DK = 64
DV = 128
CHUNK = 128
GROUPS = 4
DG = 128
GRID_W = 64
ROPE_BASE = 10000.0
EPS = 1e-6
LANES = 128
PAIRS = HEADS * DK // LANES
RET_W = HEADS * DV
GM_W = GROUPS * DG
QK_W = HEADS * DK

DOT_COLS = 512
TILE_KV = 1024
X_RING = 3
TILE_MAIN = 512


def _rms(x, w):
    ms = jnp.mean(x * x, axis=-1, keepdims=True)
    return x * lax.rsqrt(ms + EPS) * w


def _silu(x):
    return x * jax.nn.sigmoid(x)


def _gelu(x):
    return 0.5 * x * (1.0 + lax.erf(x * (0.5 ** 0.5)))


_dot = functools.partial(jnp.dot, preferred_element_type=F32)


def _col_groups(n):
    first = n % DOT_COLS
    edges = ([0] if first == 0 else [0, first]) + list(range(first + DOT_COLS, n + 1, DOT_COLS))
    return list(zip(edges[:-1], edges[1:]))


def _rope(p, cos, sin_signed, upper16):
    partner = jnp.where(upper16, pltpu.roll(p, 16, 1), pltpu.roll(p, LANES - 16, 1))
    return p * cos + partner * sin_signed


def _table_rows(table_ref, tile_in_batch, tile):
    return table_ref.at[pl.ds(pl.multiple_of(tile_in_batch * tile, tile), tile), :]


def _keys_and_state(ks_ref):
    return ks_ref.at[:, :, pl.ds(0, QK_W), :], ks_ref.at[:, :, pl.ds(QK_W, QK_W), :]


def _kv_pairs(kd, v):
    out = []
    for p in range(PAIRS):
        r = _dot(kd[p * LANES:(p + 1) * LANES, :], v[:, p * 2 * DV:(p + 1) * 2 * DV])
        out.append(r[0:DK, 0:DV])
        out.append(r[DK:2 * DK, DV:2 * DV])
    return jnp.concatenate(out, axis=0)


def _emit_pipelined(big_pieces, side_pieces, tasks):
    in_flight = []
    last_tick = max([start + parts for start, _, parts in tasks] +
                    [len(big_pieces), len(side_pieces)])
    for tick in range(last_tick):
        if tick < len(big_pieces):
            big_pieces[tick]()
        urgent, relaxed = side_pieces[tick] if tick < len(side_pieces) else (None, None)
        if urgent is not None:
            urgent()
        in_flight += [[gen, parts] for start, gen, parts in tasks if start == tick]
        for t in [t for t in in_flight if t[1] > 1] + [t for t in in_flight if t[1] == 1]:
            next(t[0], None)
            t[1] -= 1
        in_flight = [t for t in in_flight if t[1] > 0]
        if relaxed is not None:
            relaxed()
    assert not in_flight


def _run_pipeline_step(step, n_tiles, n_stages, body, refs, bufs_a, bufs_b):
    def call(s_static_parity, stages):
        new, old = (bufs_a, bufs_b) if s_static_parity == 0 else (bufs_b, bufs_a)
        body(stages, *refs, *new, *old)

    edge = n_stages - 1
    for s in list(range(edge)) + list(range(n_tiles, n_tiles + edge)):
        stages = tuple(0 <= s - j < n_tiles for j in range(n_stages))
        pl.when(step == s)(functools.partial(call, s % 2, stages))
    steady = (step >= edge) & (step < n_tiles)
    for parity in range(2):
        pl.when(steady & (lax.rem(step, 2) == parity))(
            functools.partial(call, parity, (True,) * n_stages))


def _prologue_kernel(decf_ref, decb_ref, c_ref, cctx_ref, w_ref, b_ref, invf_ref, wqkv_ref,
                     ws_ref, bs_ref,
                     mod_ref, cos_ref, sin_ref, dsum_ref, af_ref, ab_ref,
                     kdft_ref, kdbt_ref, gf_ref, gb_ref,
                     wkv_ref, wsb_ref, bsf_ref):
    i = pl.program_id(0)
    c = c_ref[...]
    cond = jnp.concatenate([c, jnp.broadcast_to(cctx_ref[...], c.shape)], axis=0)
    mod_ref[...] = _dot(_silu(cond), w_ref[...]) + b_ref[...]

    @pl.when(i == 0)
    def _small_weights():
        wkv_ref[...] = wqkv_ref[:, QK_W:].astype(BF16)
        wsb_ref[...] = ws_ref[...].astype(BF16)
        for g in range(GROUPS):
            bsf_ref[g] = jnp.broadcast_to(bs_ref[g:g + 1, :], (CHUNK, CHUNK)).T

    @pl.when(i == 0)
    def _tables():
        pos = lax.broadcasted_iota(jnp.int32, (GRID_W, LANES), 0).astype(F32)
        lane = lax.broadcasted_iota(jnp.int32, (GRID_W, LANES), 1)
        ang = pos * invf_ref[...]
        cr = jnp.cos(ang)
        sr = jnp.sin(ang)
        sr = jnp.where((lane & 16) == 0, -sr, sr)
        by_row = (lane & 32) == 0
        n_rows = cos_ref.shape[0] // GRID_W
        for r in range(n_rows):
            sl = slice(r * GRID_W, (r + 1) * GRID_W)
            cos_ref[sl, :] = jnp.where(by_row, jnp.broadcast_to(cr[r:r + 1, :], cr.shape), cr)
            sin_ref[sl, :] = jnp.where(by_row, jnp.broadcast_to(sr[r:r + 1, :], sr.shape), sr)

        def log_decay(head_idx, dec_ref):
            d = jnp.full(head_idx.shape, dec_ref[HEADS - 1], F32)
            for h in range(HEADS - 2, -1, -1):
                d = jnp.where(head_idx == h, dec_ref[h], d)
            return -jnp.exp(d)

        n = lax.broadcasted_iota(jnp.int32, (CHUNK, QK_W), 0).astype(F32)
        hl = lax.broadcasted_iota(jnp.int32, (CHUNK, QK_W), 1) // DK
        af_ref[...] = jnp.exp((n + 1.0) * log_decay(hl, decf_ref))
        ab_ref[...] = jnp.exp((CHUNK - n) * log_decay(hl, decb_ref))

        m = lax.broadcasted_iota(jnp.int32, (QK_W, CHUNK), 1).astype(F32)
        hr = lax.broadcasted_iota(jnp.int32, (QK_W, CHUNK), 0) // DK
        lgf = log_decay(hr, decf_ref)
        lgb = log_decay(hr, decb_ref)
        kdft_ref[...] = jnp.exp((CHUNK - 1.0 - m) * lgf)
        kdbt_ref[...] = jnp.exp(m * lgb)
        gf_ref[...] = jnp.exp(float(CHUNK) * lgf)
        gb_ref[...] = jnp.exp(float(CHUNK) * lgb)

        nn = lax.broadcasted_iota(jnp.int32, (CHUNK, CHUNK), 0).astype(F32)
        mm = lax.broadcasted_iota(jnp.int32, (CHUNK, CHUNK), 1).astype(F32)
        for h in range(HEADS):
            lf = -jnp.exp(jnp.full((CHUNK, CHUNK), decf_ref[h], F32))
            lb = -jnp.exp(jnp.full((CHUNK, CHUNK), decb_ref[h], F32))
            df = nn - mm
            db = mm - nn
            fwd = jnp.where(df >= 0, jnp.exp(jnp.maximum(df, 0.0) * lf), 0.0)
            bwd = jnp.where(db >= 0, jnp.exp(jnp.maximum(db, 0.0) * lb), 0.0)
            dsum_ref[h] = fwd + bwd


def _ctx_kernel(n_tiles, ctx_ref, mod_ref, nw_ref, wkv_ref, kdft_ref, kdbt_ref,
                gf_ref, gb_ref, win_ref, wout_ref,
                scf_ref, scb_ref, wrest_ref, woutb_ref, h_ref, pa_ref, pb_ref):
    refs = (ctx_ref, mod_ref, nw_ref, wkv_ref, kdft_ref, kdbt_ref, gf_ref, gb_ref,
            win_ref, wout_ref, scf_ref, scb_ref, wrest_ref, woutb_ref, h_ref)
    _run_pipeline_step(pl.program_id(0), n_tiles, 2, _ctx_step, refs, (pa_ref,), (pb_ref,))


def _ctx_step(stages, ctx_ref, mod_ref, nw_ref, wkv_ref, kdft_ref, kdbt_ref, gf_ref, gb_ref,
              win_ref, wout_ref, scf_ref, scb_ref, wrest_ref, woutb_ref, h_ref,
              p_new_ref, p_old):
    project_stage, state_stage = stages
    wrest_ref[...] = win_ref[...].astype(BF16)
    n_out = wout_ref.shape[1]
    woutb_ref[:, 0:n_out] = wout_ref[...].astype(BF16)
    woutb_ref[:, n_out:] = jnp.zeros((woutb_ref.shape[0], woutb_ref.shape[1] - n_out), BF16)
    d = ctx_ref.shape[-1]
    n_chunks = ctx_ref.shape[1] // CHUNK
    ctx_row = mod_ref.shape[0] // 2
    mod = mod_ref[ctx_row:ctx_row + 1, :]

    def norm_rows(rows):
        h = _rms(ctx_ref[0, rows, :], nw_ref[...]) * (1.0 + mod[:, d:2 * d]) + mod[:, 0:d]
        h_ref[rows, :] = h.astype(BF16)

    def project(lo, hi):
        p_new_ref[:, lo:hi] = _dot(h_ref[...], wkv_ref[:, lo:hi])

    new_f, new_b = {}, {}

    def chunk(c):
        sl = slice(c * CHUNK, (c + 1) * CHUNK)
        kt = (p_old[sl, 0:QK_W] * (DK ** -0.5)).T
        v = p_old[sl, QK_W:].astype(BF16)
        new_f[c] = _kv_pairs((kt * kdft_ref[...]).astype(BF16), v)
        new_b[c] = _kv_pairs((kt * kdbt_ref[...]).astype(BF16), v)
        yield

    def combine():
        yield
        sf = jnp.zeros((QK_W, DV), F32)
        for c in range(n_chunks):
            sf = gf_ref[...] * sf + new_f[c]
        sb = jnp.zeros((QK_W, DV), F32)
        for c in range(n_chunks - 1, -1, -1):
            sb = gb_ref[...] * sb + new_b[c]
        scf_ref[0] = sf
        scb_ref[0] = sb

    big, tasks = [], []
    if project_stage:
        for c in range(n_chunks):
            norm_rows(slice(c * CHUNK, (c + 1) * CHUNK))
        big = [functools.partial(project, lo, hi) for lo, hi in _col_groups(p_new_ref.shape[1])]
    if state_stage:
        tasks = [(c, chunk(c), 1) for c in range(n_chunks)] + [(n_chunks - 1, combine(), 2)]
    _emit_pipelined(big, [], tasks)


def _kv_kernel(n_tiles, tiles_per_batch, x_hbm, modc_ref, nw_ref, wkv_ref, cos_ref, sin_ref,
               kdbt_ref, gb_ref, scb_ref, h_ref, ks_ref, v_ref, sb_ref, pa_ref, pb_ref,
               xbuf_ref, xsem_ref):
    s = pl.program_id(0)
    tile = xbuf_ref.shape[1]

    def x_copy(t):
        b = t // tiles_per_batch
        j = tiles_per_batch - 1 - lax.rem(t, tiles_per_batch)
        slot = lax.rem(t, X_RING)
        return pltpu.make_async_copy(x_hbm.at[b, pl.ds(j * tile, tile), :],
                                     xbuf_ref.at[slot], xsem_ref.at[slot])

    @pl.when(s == 0)
    def _prime():
        for t in range(min(X_RING - 1, n_tiles)):
            x_copy(t).start()

    @pl.when(s < n_tiles)
    def _arrived():
        x_copy(s).wait()

    @pl.when(s + X_RING - 1 < n_tiles)
    def _prefetch():
        x_copy(s + X_RING - 1).start()

    @pl.when((s >= 1) & (lax.rem(s - 1, tiles_per_batch) == 0))
    def _init():
        sb_ref[...] = scb_ref[0]

    batch = jnp.minimum(s, n_tiles - 1) // tiles_per_batch
    modc = modc_ref[pl.ds(batch, 1), :]
    xc_ref = xbuf_ref.at[lax.rem(s, X_RING)]
    rotated = tiles_per_batch - 1 - lax.rem(jnp.clip(s - 1, 0, n_tiles - 1), tiles_per_batch)
    cos_ref = _table_rows(cos_ref, rotated, tile)
    sin_ref = _table_rows(sin_ref, rotated, tile)
    kt_ref, stb_ref = _keys_and_state(ks_ref)
    refs = (xc_ref, modc, nw_ref, wkv_ref, cos_ref, sin_ref, kdbt_ref, gb_ref,
            h_ref, kt_ref, v_ref, stb_ref, sb_ref)
    _run_pipeline_step(s, n_tiles, 2, _kv_step, refs, (pa_ref,), (pb_ref,))


def _kv_step(stages, xc_ref, modc, nw_ref, wkv_ref, cos_ref, sin_ref, kdbt_ref, gb_ref,
             h_ref, kt_ref, v_ref, stb_ref, sb_ref, p_new_ref, p_old):
    project_stage, finish_stage = stages
    d = xc_ref.shape[-1]
    tile = xc_ref.shape[0]
    half = tile // 2
    row_halves = [slice(0, half), slice(half, tile)]
    quarter = [slice(i * half // 2, (i + 1) * half // 2) for i in range(4)]
    upper16 = (lax.broadcasted_iota(jnp.int32, (CHUNK, LANES), 1) & 16) != 0

    def norm_rows(rows):
        h = _rms(xc_ref[rows, :], nw_ref[...]) * (1.0 + modc[:, d:2 * d]) + modc[:, 0:d]
        h_ref[0, rows, :] = h.astype(BF16)

    def project(rows, lo, hi):
        p_new_ref[rows, lo:hi] = _dot(h_ref[0, rows, :], wkv_ref[:, lo:hi])

    def chunk(c):
        sl = slice(c * CHUNK, (c + 1) * CHUNK)
        k = [_rope(p_old[sl, p * LANES:(p + 1) * LANES], cos_ref[sl, :], sin_ref[sl, :], upper16)
             * (DK ** -0.5) for p in range(PAIRS)]
        v = p_old[sl, QK_W:].astype(BF16)
        v_ref[0, sl, :] = v
        kt = jnp.concatenate([kp.T for kp in k], axis=0)
        kt_ref[0, c] = kt.astype(BF16)
        new = _kv_pairs((kt * kdbt_ref[...]).astype(BF16), v)
        yield
        stb_ref[0, c] = sb_ref[...].astype(BF16)
        sb_ref[...] = gb_ref[...] * sb_ref[...] + new

    projections, late_norms, tasks = [], [], []
    if project_stage:
        norm_rows(quarter[0])
        norm_rows(quarter[1])
        projections = [functools.partial(project, row_halves[r], lo, hi)
                       for r in range(2) for lo, hi in _col_groups(p_new_ref.shape[1])]
        late_norms = [(functools.partial(norm_rows, q), None) for q in quarter[2:]]
    if finish_stage:
        tasks = [(tick, chunk(c), 2) for tick, c in enumerate(range(tile // CHUNK - 1, -1, -1))]
    _emit_pipelined(projections, late_norms, tasks)


def _main_kernel(n_tiles, tiles_per_batch, h_ref, x_ref, mod_ref, w_ref, cos_ref,
                 sin_ref, ks_ref, v_ref, scf_ref, dsum_ref, af_ref, ab_ref, kdft_ref,
                 gf_ref, gnw_ref, lnw_ref, ws_ref, bs_ref, wout_ref, fnw_ref, o_ref,
                 sf_ref, pa_ref, pb_ref, ya_ref, yb_ref):
    s = pl.program_id(0)

    @pl.when((s >= 1) & (lax.rem(s - 1, tiles_per_batch) == 0))
    def _init():
        sf_ref[...] = scf_ref[0]

    batch = jnp.clip(s - 2, 0, n_tiles - 1) // tiles_per_batch
    mod = mod_ref[pl.ds(batch, 1), :]
    rotated = lax.rem(jnp.clip(s - 1, 0, n_tiles - 1), tiles_per_batch)
    cos_ref = _table_rows(cos_ref, rotated, h_ref.shape[1])
    sin_ref = _table_rows(sin_ref, rotated, h_ref.shape[1])
    kt_ref, stb_ref = _keys_and_state(ks_ref)
    refs = (h_ref, x_ref, mod, w_ref, cos_ref, sin_ref, kt_ref, v_ref,
            stb_ref, dsum_ref, af_ref, ab_ref, kdft_ref, gf_ref, gnw_ref, lnw_ref, ws_ref,
            bs_ref, wout_ref, fnw_ref, o_ref, sf_ref)
    _run_pipeline_step(s, n_tiles, 3, _main_step, refs, (pa_ref, ya_ref), (pb_ref, yb_ref))


def _main_step(stages, h_ref, x_ref, mod, w_ref, cos_ref, sin_ref, kt_ref, v_ref,
               stb_ref, dsum_ref, af_ref, ab_ref, kdft_ref, gf_ref, gnw_ref, lnw_ref, ws_ref,
               bs_ref, wout_ref, fnw_ref, o_ref, sf_ref, p_new_ref, y_ref, p_all, y_old):
    project_stage, mix_stage, out_stage = stages
    d = x_ref.shape[-1]
    tile = x_ref.shape[1]
    n_chunks = tile // CHUNK
    half = tile // 2
    row_halves = [slice(0, half), slice(half, tile)]
    c_gr = QK_W
    c_u = c_gr + RET_W
    c_vm = c_u + GM_W
    c_gm = c_vm + GM_W
    lane = lax.broadcasted_iota(jnp.int32, (CHUNK, LANES), 1)
    upper16 = (lane & 16) != 0
    head_of_lane = lane // DK

    operand = {}

    def loaded_once(key, load):
        if key not in operand:
            operand[key] = load()
        return operand[key]

    def project(r, lo, hi):
        rows = row_halves[r]
        lhs = loaded_once(("h", r), lambda: h_ref[0, rows, :])
        p_new_ref[rows, lo:hi] = _dot(lhs, w_ref[:, lo:hi])

    def retention(c, p):
        sl = slice(c * CHUNK, (c + 1) * CHUNK)
        ps = slice(p * LANES, (p + 1) * LANES)
        q2 = _rope(p_all[sl, ps], cos_ref[sl, :], sin_ref[sl, :], upper16)
        kt = kt_ref[0, c, ps, :]
        vp = v_ref[0, sl, p * 2 * DV:(p + 1) * 2 * DV]
        qa = q2 * af_ref[:, ps]
        qb = q2 * ab_ref[:, ps]
        scores, qx = [], []
        for hh in range(2):
            mine = head_of_lane == hh
            scores.append(_dot(jnp.where(mine, q2, 0.0).astype(BF16), kt))
            qx.append(jnp.concatenate([jnp.where(mine, qa, 0.0), jnp.where(mine, qb, 0.0)],
                                      axis=1).astype(BF16))
        kd = (kt.astype(F32) * kdft_ref[ps, :]).astype(BF16)
        yield
        state = jnp.concatenate([sf_ref[ps, :].astype(BF16), stb_ref[0, c, ps, :]], axis=0)
        o = []
        for hh in range(2):
            a = (scores[hh] * dsum_ref[2 * p + hh]).astype(BF16)
            o.append(_dot(a, vp[:, hh * DV:(hh + 1) * DV]) + _dot(qx[hh], state))
        r = _dot(kd, vp)
        yield
        new = jnp.concatenate([r[0:DK, 0:DV], r[DK:2 * DK, DV:2 * DV]], axis=0)
        sf_ref[ps, :] = gf_ref[ps, :] * sf_ref[ps, :] + new
        for hh in range(2):
            head = 2 * p + hh
            mu = jnp.mean(o[hh], axis=-1, keepdims=True)
            var = jnp.mean(jnp.square(o[hh] - mu), axis=-1, keepdims=True)
            hs = slice(head * DV, (head + 1) * DV)
            y = (o[hh] - mu) * lax.rsqrt(var + EPS) * gnw_ref[:, hs]
            gate_cols = slice(c_gr + head * DV, c_gr + (head + 1) * DV)
            y_ref[sl, hs] = (y * _silu(p_all[sl, gate_cols])).astype(BF16)

    def gating(c):
        sl = slice(c * CHUNK, (c + 1) * CHUNK)
        vg = _gelu(p_all[sl, c_vm:c_gm])
        mu = jnp.mean(vg, axis=-1, keepdims=True)
        var = jnp.mean(jnp.square(vg - mu), axis=-1, keepdims=True)
        yield
        vn = ((vg - mu) * lax.rsqrt(var + EPS) * lnw_ref[...]).astype(BF16)
        s = [_dot(ws_ref[g], vn[:, g * DG:(g + 1) * DG]) for g in range(GROUPS)]
        yield
        for g in range(GROUPS):
            if g == GROUPS // 2:
                yield
            u = _gelu(p_all[sl, c_u + g * DG:c_u + (g + 1) * DG])
            gm = _silu(p_all[sl, c_gm + g * DG:c_gm + (g + 1) * DG])
            y_ref[sl, RET_W + g * DG:RET_W + (g + 1) * DG] = (u * (s[g] + bs_ref[g]) * gm).astype(BF16)

    out_groups = _col_groups(d)
    xn = {}

    def out_project(r, g):
        rows = row_halves[r]
        lo, hi = out_groups[g]
        lhs = loaded_once(("y", r), lambda: y_old[rows, :])
        out = _dot(lhs, wout_ref[:, lo:hi])
        xn[r, g] = x_ref[0, rows, lo:hi] + mod[:, 2 * d + lo:2 * d + hi] * out

    def finish(r, part):
        sub = slice(part * half // 2, (part + 1) * half // 2)
        rows = slice(row_halves[r].start + sub.start, row_halves[r].start + sub.stop)
        parts = [xn[r, g][sub, :] for g in range(len(out_groups))]
        ms = sum(jnp.sum(t * t, axis=-1, keepdims=True) for t in parts) * (1.0 / d)
        rs = lax.rsqrt(ms + EPS)
        for (lo, hi), t in zip(out_groups, parts):
            o_ref[0, rows, lo:hi] = t * rs * fnw_ref[:, lo:hi]

    tasks = []
    if mix_stage:
        for c in range(n_chunks):
            base = 3 * c
            tasks += [(base + min(p, 1), retention(c, p), 3) for p in range(PAIRS)]
            tasks.append((base + 1, gating(c), 4))
    big, relaxed = [], {}

    def place(tick, piece):
        while tick in relaxed:
            tick += 1
        relaxed[tick] = piece

    for r in range(2):
        outs = [functools.partial(out_project, r, g) for g in range(len(out_groups))] if out_stage else []
        projs = ([functools.partial(project, r, lo, hi) for lo, hi in _col_groups(p_new_ref.shape[1])]
                 if project_stage else [])
        for i in range(max(len(outs), len(projs))):
            big += outs[i:i + 1] + projs[i:i + 1]
            if i == len(outs) - 1:
                place(len(big) + 1, functools.partial(finish, r, 0))
                place(len(big) + 3, functools.partial(finish, r, 1))
    side = [(None, relaxed.get(tick)) for tick in range(max(relaxed, default=-1) + 1)]
    _emit_pipelined(big, side, tasks)


def _full(shape):
    return pl.BlockSpec(shape, lambda *_: (0,) * len(shape))


def kernel(x, c, ctx, c_ctx, w_ada, b_ada, norm_w, w_in, ret_decay_f, ret_decay_b, ret_gn_w,
           gmlp_ln_w, w_s, b_s, w_out, final_norm_w):
    bsz, seq, d = x.shape
    ctx_len = ctx.shape[1]
    depth = w_ada.shape[0]
    assert depth == 1 and d % LANES == 0
    assert seq % TILE_KV == 0 and seq % TILE_MAIN == 0 and ctx_len % CHUNK == 0
    assert seq % GRID_W == 0 and TILE_MAIN % (4 * CHUNK) == 0 and TILE_KV % (4 * CHUNK) == 0
    n_chunks = seq // CHUNK
    d3 = 3 * d
    n_kv = QK_W + RET_W
    n_rest = w_in.shape[2] - n_kv
    assert QK_W + n_kv == d
    assert _col_groups(n_rest)[0] == (0, QK_W)
    n_rest_blocks = n_rest // QK_W
    n_out_blocks = (RET_W + GM_W) // CHUNK
    assert n_rest % QK_W == 0 and n_kv % QK_W == 0
    assert max(n_rest_blocks, n_out_blocks) <= bsz + 1

    quarter = DK // 4
    inv_freq = ROPE_BASE ** (-jnp.arange(quarter, dtype=F32) / quarter)
    invf_lane = jnp.tile(inv_freq, LANES // quarter)[None, :]
    nw = norm_w[0][None, :]
    rows2 = 2 * bsz

    n_col_blocks = d3 // d
    smem = pl.BlockSpec(memory_space=pltpu.SMEM)
    n_out_padded = d + QK_W
    (mod, cos_t, sin_t, dsum, af, ab, kdft, kdbt, gf, gb,
     w_kv, w_s_b, b_s_full) = pl.pallas_call(
        _prologue_kernel,
        grid=(n_col_blocks,),
        in_specs=[smem, smem, _full((bsz, d)), _full((1, d)),
                  pl.BlockSpec((d, d), lambda i: (0, i)),
                  pl.BlockSpec((1, d), lambda i: (0, i)),
                  _full((1, LANES)),
                  pl.BlockSpec((d, d), lambda i: (0, 0)),
                  _full((GROUPS, CHUNK, CHUNK)), _full((GROUPS, CHUNK))],
        out_specs=[pl.BlockSpec((rows2, d), lambda i: (0, i)),
                   _full((seq, LANES)), _full((seq, LANES)),
                   _full((HEADS, CHUNK, CHUNK)),
                   _full((CHUNK, QK_W)), _full((CHUNK, QK_W)),
                   _full((QK_W, CHUNK)), _full((QK_W, CHUNK)),
                   _full((QK_W, DV)), _full((QK_W, DV)),
                   _full((d, n_kv)),
                   _full((GROUPS, CHUNK, CHUNK)), _full((GROUPS, CHUNK, DG))],
        out_shape=[jax.ShapeDtypeStruct((rows2, d3), F32),
                   jax.ShapeDtypeStruct((seq, LANES), F32),
                   jax.ShapeDtypeStruct((seq, LANES), F32),
                   jax.ShapeDtypeStruct((HEADS, CHUNK, CHUNK), F32),
                   jax.ShapeDtypeStruct((CHUNK, QK_W), F32),
                   jax.ShapeDtypeStruct((CHUNK, QK_W), F32),
                   jax.ShapeDtypeStruct((QK_W, CHUNK), F32),
                   jax.ShapeDtypeStruct((QK_W, CHUNK), F32),
                   jax.ShapeDtypeStruct((QK_W, DV), F32),
                   jax.ShapeDtypeStruct((QK_W, DV), F32),
                   jax.ShapeDtypeStruct((d, n_kv), BF16),
                   jax.ShapeDtypeStruct((GROUPS, CHUNK, CHUNK), BF16),
                   jax.ShapeDtypeStruct((GROUPS, CHUNK, DG), F32)],
        compiler_params=pltpu.CompilerParams(dimension_semantics=("arbitrary",)),
        name="prologue",
    )(ret_decay_f[0], ret_decay_b[0], c, c_ctx[None, :], w_ada[0], b_ada[0][None, :], invf_lane,
      w_in[0], w_s[0], b_s[0])

    def rest_block(s):
        return jnp.minimum(s, n_rest_blocks - 1)

    def rest_source(s):
        return jnp.where(s == 0, 0, rest_block(s) + n_kv // QK_W)

    def out_block(s):
        return jnp.minimum(s, n_out_blocks - 1)

    s_cf, s_cb, w_rest, w_out_b = pl.pallas_call(
        functools.partial(_ctx_kernel, bsz),
        grid=(bsz + 1,),
        in_specs=[pl.BlockSpec((1, ctx_len, d), lambda s: (jnp.minimum(s, bsz - 1), 0, 0)),
                  _full((rows2, d3)),
                  _full((1, d)), _full((d, n_kv)),
                  _full((QK_W, CHUNK)), _full((QK_W, CHUNK)),
                  _full((QK_W, DV)), _full((QK_W, DV)),
                  pl.BlockSpec((d, QK_W), lambda s: (0, rest_source(s))),
                  pl.BlockSpec((CHUNK, d), lambda s: (out_block(s), 0))],
        out_specs=[pl.BlockSpec((1, QK_W, DV), lambda s: (jnp.maximum(s - 1, 0), 0, 0)),
                   pl.BlockSpec((1, QK_W, DV), lambda s: (jnp.maximum(s - 1, 0), 0, 0)),
                   pl.BlockSpec((d, QK_W), lambda s: (0, rest_block(s))),
                   pl.BlockSpec((CHUNK, n_out_padded), lambda s: (out_block(s), 0))],
        out_shape=[jax.ShapeDtypeStruct((bsz, QK_W, DV), F32),
                   jax.ShapeDtypeStruct((bsz, QK_W, DV), F32),
                   jax.ShapeDtypeStruct((d, n_rest), BF16),
                   jax.ShapeDtypeStruct((RET_W + GM_W, n_out_padded), BF16)],
        scratch_shapes=[pltpu.VMEM((ctx_len, d), BF16),
                        pltpu.VMEM((ctx_len, n_kv), F32),
                        pltpu.VMEM((ctx_len, n_kv), F32)],
        compiler_params=pltpu.CompilerParams(dimension_semantics=("arbitrary",)),
        name="ctx_states",
    )(ctx, mod, nw, w_kv, kdft, kdbt, gf, gb, w_in[0], w_out[0])

    def tile_maps(n_tiles, per_batch, reverse):
        def where(t):
            j = t % per_batch
            return t // per_batch, (per_batch - 1 - j) if reverse else j
        return [lambda s, k=k: where(jnp.clip(s - k, 0, n_tiles - 1)) for k in range(3)]

    nt_kv = seq // TILE_KV
    cpt_kv = TILE_KV // CHUNK
    cur, prv, _ = tile_maps(bsz * nt_kv, nt_kv, reverse=True)
    assert CHUNK == DV
    h, keys_states, v = pl.pallas_call(
        functools.partial(_kv_kernel, bsz * nt_kv, nt_kv),
        grid=(bsz * nt_kv + 1,),
        in_specs=[pl.BlockSpec(memory_space=pl.ANY),
                  _full((rows2, d3)),
                  _full((1, d)), _full((d, n_kv)),
                  _full((seq, LANES)), _full((seq, LANES)),
                  _full((QK_W, CHUNK)), _full((QK_W, DV)),
                  pl.BlockSpec((1, QK_W, DV), lambda s: (prv(s)[0], 0, 0))],
        out_specs=[pl.BlockSpec((1, TILE_KV, d), lambda s: (*cur(s), 0)),
                   pl.BlockSpec((1, cpt_kv, 2 * QK_W, DV), lambda s: (*prv(s), 0, 0)),
                   pl.BlockSpec((1, TILE_KV, RET_W), lambda s: (*prv(s), 0))],
        out_shape=[jax.ShapeDtypeStruct((bsz, seq, d), BF16),
                   jax.ShapeDtypeStruct((bsz, n_chunks, 2 * QK_W, DV), BF16),
                   jax.ShapeDtypeStruct((bsz, seq, RET_W), BF16)],
        scratch_shapes=[pltpu.VMEM((QK_W, DV), F32),
                        pltpu.VMEM((TILE_KV, n_kv), F32),
                        pltpu.VMEM((TILE_KV, n_kv), F32),
                        pltpu.VMEM((X_RING, TILE_KV, d), F32),
                        pltpu.SemaphoreType.DMA((X_RING,))],
        compiler_params=pltpu.CompilerParams(dimension_semantics=("arbitrary",)),
        name="kv_sweep",
    )(x, mod, nw, w_kv, cos_t, sin_t, kdbt, gb, s_cb)

    nt = seq // TILE_MAIN
    cpt = TILE_MAIN // CHUNK
    cur, prv, prv2 = tile_maps(bsz * nt, nt, reverse=False)
    out = pl.pallas_call(
        functools.partial(_main_kernel, bsz * nt, nt),
        grid=(bsz * nt + 2,),
        in_specs=[pl.BlockSpec((1, TILE_MAIN, d), lambda s: (*cur(s), 0)),
                  pl.BlockSpec((1, TILE_MAIN, d), lambda s: (*prv2(s), 0)),
                  _full((rows2, d3)),
                  _full((d, n_rest)),
                  _full((seq, LANES)), _full((seq, LANES)),
                  pl.BlockSpec((1, cpt, 2 * QK_W, DV), lambda s: (*prv(s), 0, 0)),
                  pl.BlockSpec((1, TILE_MAIN, RET_W), lambda s: (*prv(s), 0)),
                  pl.BlockSpec((1, QK_W, DV), lambda s: (prv(s)[0], 0, 0)),
                  _full((HEADS, CHUNK, CHUNK)),
                  _full((CHUNK, QK_W)), _full((CHUNK, QK_W)),
                  _full((QK_W, CHUNK)), _full((QK_W, DV)),
                  _full((1, RET_W)), _full((1, GM_W)),
                  _full((GROUPS, CHUNK, CHUNK)), _full((GROUPS, CHUNK, DG)),
                  _full((RET_W + GM_W, n_out_padded)), _full((1, d))],
        out_specs=pl.BlockSpec((1, TILE_MAIN, d), lambda s: (*prv2(s), 0)),
        out_shape=jax.ShapeDtypeStruct((bsz, seq, d), x.dtype),
        scratch_shapes=[pltpu.VMEM((QK_W, DV), F32),
                        pltpu.VMEM((TILE_MAIN, n_rest), F32),
                        pltpu.VMEM((TILE_MAIN, n_rest), F32),
                        pltpu.VMEM((TILE_MAIN, RET_W + GM_W), BF16),
                        pltpu.VMEM((TILE_MAIN, RET_W + GM_W), BF16)],
        compiler_params=pltpu.CompilerParams(dimension_semantics=("arbitrary",)),
        name="main_sweep",
    )(h, x, mod, w_rest, cos_t, sin_t, keys_states, v, s_cf, dsum, af, ab, kdft, gf,
      ret_gn_w[0][None, :], gmlp_ln_w[0][None, :], w_s_b, b_s_full, w_out_b,
      final_norm_w[None, :])
    return out
```

```python
import functools

import jax
import jax.numpy as jnp
from jax import lax
from jax.experimental import pallas as pl
from jax.experimental.pallas import tpu as pltpu

F32 = jnp.float32
BF16 = jnp.bfloat16

HEADS = 4
DK = 64
DV = 128
CHUNK = 128
GROUPS = 4
DG = 128
GRID_W = 64
ROPE_BASE = 10000.0
EPS = 1e-6
LANES = 128
PAIRS = HEADS * DK // LANES
RET_W = HEADS * DV
GM_W = GROUPS * DG
QK_W = HEADS * DK

DOT_COLS = 512
TILE_KV = 1024
X_RING = 3
TILE_MAIN = 512


def _rms(x, w):
    ms = jnp.mean(x * x, axis=-1, keepdims=True)
    return x * lax.rsqrt(ms + EPS) * w


def _silu(x):
    return x * jax.nn.sigmoid(x)


def _gelu(x):
    return 0.5 * x * (1.0 + lax.erf(x * (0.5 ** 0.5)))


_dot = functools.partial(jnp.dot, preferred_element_type=F32)


def _col_groups(n):
    first = n % DOT_COLS
    edges = ([0] if first == 0 else [0, first]) + list(range(first + DOT_COLS, n + 1, DOT_COLS))
    return list(zip(edges[:-1], edges[1:]))


def _rope(p, cos, sin_signed, upper16):
    partner = jnp.where(upper16, pltpu.roll(p, 16, 1), pltpu.roll(p, LANES - 16, 1))
    return p * cos + partner * sin_signed


def _kv_pairs(kd, v):
    out = []
    for p in range(PAIRS):
        r = _dot(kd[p * LANES:(p + 1) * LANES, :], v[:, p * 2 * DV:(p + 1) * 2 * DV])
        out.append(r[0:DK, 0:DV])
        out.append(r[DK:2 * DK, DV:2 * DV])
    return jnp.concatenate(out, axis=0)


def _emit_pipelined(big_pieces, side_pieces, tasks):
    in_flight = []
    last_tick = max([start + parts for start, _, parts in tasks] +
                    [len(big_pieces), len(side_pieces)])
    for tick in range(last_tick):
        if tick < len(big_pieces):
            big_pieces[tick]()
        urgent, relaxed = side_pieces[tick] if tick < len(side_pieces) else (None, None)
        if urgent is not None:
            urgent()
        in_flight += [[gen, parts] for start, gen, parts in tasks if start == tick]
        for t in [t for t in in_flight if t[1] > 1] + [t for t in in_flight if t[1] == 1]:
            next(t[0], None)
            t[1] -= 1
        in_flight = [t for t in in_flight if t[1] > 0]
        if relaxed is not None:
            relaxed()
    assert not in_flight


def _run_pipeline_step(step, n_tiles, n_stages, body, refs, bufs_a, bufs_b):
    def call(s_static_parity, stages):
        new, old = (bufs_a, bufs_b) if s_static_parity == 0 else (bufs_b, bufs_a)
        body(stages, *refs, *new, *old)

    edge = n_stages - 1
    for s in list(range(edge)) + list(range(n_tiles, n_tiles + edge)):
        stages = tuple(0 <= s - j < n_tiles for j in range(n_stages))
        pl.when(step == s)(functools.partial(call, s % 2, stages))
    steady = (step >= edge) & (step < n_tiles)
    for parity in range(2):
        pl.when(steady & (lax.rem(step, 2) == parity))(
            functools.partial(call, parity, (True,) * n_stages))


def _prologue_kernel(decf_ref, decb_ref, c_ref, cctx_ref, w_ref, b_ref, invf_ref, wqkv_ref,
                     ws_ref, bs_ref,
                     mod_ref, cos_ref, sin_ref, dsum_ref, af_ref, ab_ref,
                     kdft_ref, kdbt_ref, gf_ref, gb_ref,
                     wkv_ref, wsb_ref, bsf_ref):
    i = pl.program_id(0)
    c = c_ref[...]
    cond = jnp.concatenate([c, jnp.broadcast_to(cctx_ref[...], c.shape)], axis=0)
    mod_ref[...] = _dot(_silu(cond), w_ref[...]) + b_ref[...]

    @pl.when(i == 0)
    def _small_weights():
        wkv_ref[...] = wqkv_ref[:, QK_W:].astype(BF16)
        wsb_ref[...] = ws_ref[...].astype(BF16)
        for g in range(GROUPS):
            bsf_ref[g] = jnp.broadcast_to(bs_ref[g:g + 1, :], (CHUNK, CHUNK)).T

    @pl.when(i == 0)
    def _tables():
        pos = lax.broadcasted_iota(jnp.int32, (GRID_W, LANES), 0).astype(F32)
        lane = lax.broadcasted_iota(jnp.int32, (GRID_W, LANES), 1)
        ang = pos * invf_ref[...]
        cr = jnp.cos(ang)
        sr = jnp.sin(ang)
        sr = jnp.where((lane & 16) == 0, -sr, sr)
        by_row = (lane & 32) == 0
        n_rows = cos_ref.shape[0] // GRID_W
        for r in range(n_rows):
            sl = slice(r * GRID_W, (r + 1) * GRID_W)
            cos_ref[sl, :] = jnp.where(by_row, jnp.broadcast_to(cr[r:r + 1, :], cr.shape), cr)
            sin_ref[sl, :] = jnp.where(by_row, jnp.broadcast_to(sr[r:r + 1, :], sr.shape), sr)

        def log_decay(head_idx, dec_ref):
            d = jnp.full(head_idx.shape, dec_ref[HEADS - 1], F32)
            for h in range(HEADS - 2, -1, -1):
                d = jnp.where(head_idx == h, dec_ref[h], d)
            return -jnp.exp(d)

        n = lax.broadcasted_iota(jnp.int32, (CHUNK, QK_W), 0).astype(F32)
        hl = lax.broadcasted_iota(jnp.int32, (CHUNK, QK_W), 1) // DK
        af_ref[...] = jnp.exp((n + 1.0) * log_decay(hl, decf_ref))
        ab_ref[...] = jnp.exp((CHUNK - n) * log_decay(hl, decb_ref))

        m = lax.broadcasted_iota(jnp.int32, (QK_W, CHUNK), 1).astype(F32)
        hr = lax.broadcasted_iota(jnp.int32, (QK_W, CHUNK), 0) // DK
        lgf = log_decay(hr, decf_ref)
        lgb = log_decay(hr, decb_ref)
        kdft_ref[...] = jnp.exp((CHUNK - 1.0 - m) * lgf)
        kdbt_ref[...] = jnp.exp(m * lgb)
        gf_ref[...] = jnp.exp(float(CHUNK) * lgf)
        gb_ref[...] = jnp.exp(float(CHUNK) * lgb)

        nn = lax.broadcasted_iota(jnp.int32, (CHUNK, CHUNK), 0).astype(F32)
        mm = lax.broadcasted_iota(jnp.int32, (CHUNK, CHUNK), 1).astype(F32)
        for h in range(HEADS):
            lf = -jnp.exp(jnp.full((CHUNK, CHUNK), decf_ref[h], F32))
            lb = -jnp.exp(jnp.full((CHUNK, CHUNK), decb_ref[h], F32))
            df = nn - mm
            db = mm - nn
            fwd = jnp.where(df >= 0, jnp.exp(jnp.maximum(df, 0.0) * lf), 0.0)
            bwd = jnp.where(db >= 0, jnp.exp(jnp.maximum(db, 0.0) * lb), 0.0)
            dsum_ref[h] = fwd + bwd


def _ctx_kernel(n_tiles, n_out_blocks, ctx_ref, mod_ref, nw_ref, wkv_ref, kdft_ref, kdbt_ref,
                gf_ref, gb_ref, win_ref, wout_ref,
                scf_ref, scb_ref, wrest_ref, woutb_ref, h_ref, pa_ref, pb_ref):
    s = pl.program_id(0)
    keep = s < n_out_blocks
    refs = (ctx_ref, mod_ref, nw_ref, wkv_ref, kdft_ref, kdbt_ref, gf_ref, gb_ref,
            win_ref, wout_ref, keep, scf_ref, scb_ref, wrest_ref, woutb_ref, h_ref)
    _run_pipeline_step(s, n_tiles, 2, _ctx_step, refs, (pa_ref,), (pb_ref,))


def _ctx_step(stages, ctx_ref, mod_ref, nw_ref, wkv_ref, kdft_ref, kdbt_ref, gf_ref, gb_ref,
              win_ref, wout_ref, keep, scf_ref, scb_ref, wrest_ref, woutb_ref, h_ref,
              p_new_ref, p_old):
    project_stage, state_stage = stages
    wrest_ref[...] = win_ref[...].astype(BF16)
    woutb_ref[...] = jnp.where(keep, wout_ref[...], 0.0).astype(BF16)
    d = ctx_ref.shape[-1]
    n_chunks = ctx_ref.shape[1] // CHUNK
    ctx_row = mod_ref.shape[0] // 2
    mod = mod_ref[ctx_row:ctx_row + 1, :]

    def norm_rows(rows):
        h = _rms(ctx_ref[0, rows, :], nw_ref[...]) * (1.0 + mod[:, d:2 * d]) + mod[:, 0:d]
        h_ref[rows, :] = h.astype(BF16)

    def project(lo, hi):
        p_new_ref[:, lo:hi] = _dot(h_ref[...], wkv_ref[:, lo:hi])

    new_f, new_b = {}, {}

    def chunk(c):
        sl = slice(c * CHUNK, (c + 1) * CHUNK)
        kt = (p_old[sl, 0:QK_W] * (DK ** -0.5)).T
        v = p_old[sl, QK_W:].astype(BF16)
        new_f[c] = _kv_pairs((kt * kdft_ref[...]).astype(BF16), v)
        new_b[c] = _kv_pairs((kt * kdbt_ref[...]).astype(BF16), v)
        yield

    def combine():
        yield
        sf = jnp.zeros((QK_W, DV), F32)
        for c in range(n_chunks):
            sf = gf_ref[...] * sf + new_f[c]
        sb = jnp.zeros((QK_W, DV), F32)
        for c in range(n_chunks - 1, -1, -1):
            sb = gb_ref[...] * sb + new_b[c]
        scf_ref[0] = sf
        scb_ref[0] = sb

    big, tasks = [], []
    if project_stage:
        for c in range(n_chunks):
            norm_rows(slice(c * CHUNK, (c + 1) * CHUNK))
        big = [functools.partial(project, lo, hi) for lo, hi in _col_groups(p_new_ref.shape[1])]
    if state_stage:
        tasks = [(c, chunk(c), 1) for c in range(n_chunks)] + [(n_chunks - 1, combine(), 2)]
    _emit_pipelined(big, [], tasks)


def _kv_kernel(n_tiles, tiles_per_batch, x_hbm, modc_ref, nw_ref, wkv_ref, cos_ref, sin_ref,
               kdbt_ref, gb_ref, scb_ref, h_ref, kt_ref, v_ref, stb_ref, sb_ref, pa_ref, pb_ref,
               xbuf_ref, xsem_ref):
    s = pl.program_id(0)
    tile = xbuf_ref.shape[1]

    def x_copy(t):
        b = t // tiles_per_batch
        j = tiles_per_batch - 1 - lax.rem(t, tiles_per_batch)
        slot = lax.rem(t, X_RING)
        return pltpu.make_async_copy(x_hbm.at[b, pl.ds(j * tile, tile), :],
                                     xbuf_ref.at[slot], xsem_ref.at[slot])

    @pl.when(s == 0)
    def _prime():
        for t in range(min(X_RING - 1, n_tiles)):
            x_copy(t).start()

    @pl.when(s < n_tiles)
    def _arrived():
        x_copy(s).wait()

    @pl.when(s + X_RING - 1 < n_tiles)
    def _prefetch():
        x_copy(s + X_RING - 1).start()

    @pl.when((s >= 1) & (lax.rem(s - 1, tiles_per_batch) == 0))
    def _init():
        sb_ref[...] = scb_ref[0]

    batch = jnp.minimum(s, n_tiles - 1) // tiles_per_batch
    modc = modc_ref[pl.ds(batch, 1), :]
    xc_ref = xbuf_ref.at[lax.rem(s, X_RING)]
    finished = jnp.clip(s - 1, 0, n_tiles - 1)
    first_row = pl.multiple_of(
        (tiles_per_batch - 1 - lax.rem(finished, tiles_per_batch)) * tile, tile)
    cos_ref = cos_ref.at[pl.ds(first_row, tile), :]
    sin_ref = sin_ref.at[pl.ds(first_row, tile), :]
    refs = (xc_ref, modc, nw_ref, wkv_ref, cos_ref, sin_ref, kdbt_ref, gb_ref,
            h_ref, kt_ref, v_ref, stb_ref, sb_ref)
    _run_pipeline_step(s, n_tiles, 2, _kv_step, refs, (pa_ref,), (pb_ref,))


def _kv_step(stages, xc_ref, modc, nw_ref, wkv_ref, cos_ref, sin_ref, kdbt_ref, gb_ref,
             h_ref, kt_ref, v_ref, stb_ref, sb_ref, p_new_ref, p_old):
    project_stage, finish_stage = stages
    d = xc_ref.shape[-1]
    tile = xc_ref.shape[0]
    half = tile // 2
    row_halves = [slice(0, half), slice(half, tile)]
    quarter = [slice(i * half // 2, (i + 1) * half // 2) for i in range(4)]
    upper16 = (lax.broadcasted_iota(jnp.int32, (CHUNK, LANES), 1) & 16) != 0

    def norm_rows(rows):
        h = _rms(xc_ref[rows, :], nw_ref[...]) * (1.0 + modc[:, d:2 * d]) + modc[:, 0:d]
        h_ref[0, rows, :] = h.astype(BF16)

    def project(rows, lo, hi):
        p_new_ref[rows, lo:hi] = _dot(h_ref[0, rows, :], wkv_ref[:, lo:hi])

    def chunk(c):
        sl = slice(c * CHUNK, (c + 1) * CHUNK)
        k = [_rope(p_old[sl, p * LANES:(p + 1) * LANES], cos_ref[sl, :], sin_ref[sl, :], upper16)
             * (DK ** -0.5) for p in range(PAIRS)]
        v = p_old[sl, QK_W:].astype(BF16)
        v_ref[0, sl, :] = v
        kt = jnp.concatenate([kp.T for kp in k], axis=0)
        kt_ref[0, c] = kt.astype(BF16)
        new = _kv_pairs((kt * kdbt_ref[...]).astype(BF16), v)
        yield
        stb_ref[0, c] = sb_ref[...].astype(BF16)
        sb_ref[...] = gb_ref[...] * sb_ref[...] + new

    projections, late_norms, tasks = [], [], []
    if project_stage:
        norm_rows(quarter[0])
        norm_rows(quarter[1])
        projections = [functools.partial(project, row_halves[r], lo, hi)
                       for r in range(2) for lo, hi in _col_groups(p_new_ref.shape[1])]
        late_norms = [(functools.partial(norm_rows, q), None) for q in quarter[2:]]
    if finish_stage:
        tasks = [(tick, chunk(c), 2) for tick, c in enumerate(range(tile // CHUNK - 1, -1, -1))]
    _emit_pipelined(projections, late_norms, tasks)


def _main_kernel(n_tiles, tiles_per_batch, h_ref, x_ref, mod_ref, w_ref, cos_ref,
                 sin_ref, kt_ref, v_ref, stb_ref, scf_ref, dsum_ref, af_ref, ab_ref, kdft_ref,
                 gf_ref, gnw_ref, lnw_ref, ws_ref, bs_ref, wout_ref, fnw_ref, o_ref,
                 sf_ref, pa_ref, pb_ref, ya_ref, yb_ref):
    s = pl.program_id(0)

    @pl.when((s >= 1) & (lax.rem(s - 1, tiles_per_batch) == 0))
    def _init():
        sf_ref[...] = scf_ref[0]

    batch = jnp.clip(s - 2, 0, n_tiles - 1) // tiles_per_batch
    mod = mod_ref[pl.ds(batch, 1), :]
    refs = (h_ref, x_ref, mod, w_ref, cos_ref, sin_ref, kt_ref, v_ref,
            stb_ref, dsum_ref, af_ref, ab_ref, kdft_ref, gf_ref, gnw_ref, lnw_ref, ws_ref,
            bs_ref, wout_ref, fnw_ref, o_ref, sf_ref)
    _run_pipeline_step(s, n_tiles, 3, _main_step, refs, (pa_ref, ya_ref), (pb_ref, yb_ref))


def _main_step(stages, h_ref, x_ref, mod, w_ref, cos_ref, sin_ref, kt_ref, v_ref,
               stb_ref, dsum_ref, af_ref, ab_ref, kdft_ref, gf_ref, gnw_ref, lnw_ref, ws_ref,
               bs_ref, wout_ref, fnw_ref, o_ref, sf_ref, p_new_ref, y_ref, p_all, y_old):
    project_stage, mix_stage, out_stage = stages
    d = x_ref.shape[-1]
    tile = x_ref.shape[1]
    n_chunks = tile // CHUNK
    half = tile // 2
    row_halves = [slice(0, half), slice(half, tile)]
    c_gr = QK_W
    c_u = c_gr + RET_W
    c_vm = c_u + GM_W
    c_gm = c_vm + GM_W
    lane = lax.broadcasted_iota(jnp.int32, (CHUNK, LANES), 1)
    upper16 = (lane & 16) != 0
    head_of_lane = lane // DK

    operand = {}

    def loaded_once(key, load):
        if key not in operand:
            operand[key] = load()
        return operand[key]

    def project(r, lo, hi):
        rows = row_halves[r]
        lhs = loaded_once(("h", r), lambda: h_ref[0, rows, :])
        p_new_ref[rows, lo:hi] = _dot(lhs, w_ref[:, lo:hi])

    def retention(c, p):
        sl = slice(c * CHUNK, (c + 1) * CHUNK)
        ps = slice(p * LANES, (p + 1) * LANES)
        q2 = _rope(p_all[sl, ps], cos_ref[sl, :], sin_ref[sl, :], upper16)
        kt = kt_ref[0, c, ps, :]
        vp = v_ref[0, sl, p * 2 * DV:(p + 1) * 2 * DV]
        qa = q2 * af_ref[:, ps]
        qb = q2 * ab_ref[:, ps]
        mine = [head_of_lane == hh for hh in range(2)]
        scores = _dot(jnp.concatenate([jnp.where(m, q2, 0.0) for m in mine], axis=0).astype(BF16), kt)
        qx = jnp.concatenate(
            [jnp.concatenate([jnp.where(m, qa, 0.0), jnp.where(m, qb, 0.0)], axis=1) for m in mine],
            axis=0).astype(BF16)
        kd = (kt.astype(F32) * kdft_ref[ps, :]).astype(BF16)
        yield
        state = jnp.concatenate([sf_ref[ps, :].astype(BF16), stb_ref[0, c, ps, :]], axis=0)
        from_state = _dot(qx, state)
        o = []
        for hh in range(2):
            rows = slice(hh * CHUNK, (hh + 1) * CHUNK)
            a = (scores[rows, :] * dsum_ref[2 * p + hh]).astype(BF16)
            o.append(_dot(a, vp[:, hh * DV:(hh + 1) * DV]) + from_state[rows, :])
        r = _dot(kd, vp)
        yield
        new = jnp.concatenate([r[0:DK, 0:DV], r[DK:2 * DK, DV:2 * DV]], axis=0)
        sf_ref[ps, :] = gf_ref[ps, :] * sf_ref[ps, :] + new
        for hh in range(2):
            head = 2 * p + hh
            mu = jnp.mean(o[hh], axis=-1, keepdims=True)
            var = jnp.mean(jnp.square(o[hh] - mu), axis=-1, keepdims=True)
            hs = slice(head * DV, (head + 1) * DV)
            y = (o[hh] - mu) * lax.rsqrt(var + EPS) * gnw_ref[:, hs]
            gate_cols = slice(c_gr + head * DV, c_gr + (head + 1) * DV)
            y_ref[sl, hs] = (y * _silu(p_all[sl, gate_cols])).astype(BF16)

    def gating(c):
        sl = slice(c * CHUNK, (c + 1) * CHUNK)
        vg = _gelu(p_all[sl, c_vm:c_gm])
        mu = jnp.mean(vg, axis=-1, keepdims=True)
        var = jnp.mean(jnp.square(vg - mu), axis=-1, keepdims=True)
        yield
        vn = ((vg - mu) * lax.rsqrt(var + EPS) * lnw_ref[...]).astype(BF16)
        s = [_dot(ws_ref[g], vn[:, g * DG:(g + 1) * DG]) for g in range(GROUPS)]
        yield
        for g in range(GROUPS):
            if g == GROUPS // 2:
                yield
            u = _gelu(p_all[sl, c_u + g * DG:c_u + (g + 1) * DG])
            gm = _silu(p_all[sl, c_gm + g * DG:c_gm + (g + 1) * DG])
            y_ref[sl, RET_W + g * DG:RET_W + (g + 1) * DG] = (u * (s[g] + bs_ref[g]) * gm).astype(BF16)

    out_groups = _col_groups(d)
    xn = {}

    def out_project(r, g):
        rows = row_halves[r]
        lo, hi = out_groups[g]
        lhs = loaded_once(("y", r), lambda: y_old[rows, :])
        out = _dot(lhs, wout_ref[:, lo:hi])
        xn[r, g] = x_ref[0, rows, lo:hi] + mod[:, 2 * d + lo:2 * d + hi] * out

    def finish(r, part):
        sub = slice(part * half // 2, (part + 1) * half // 2)
        rows = slice(row_halves[r].start + sub.start, row_halves[r].start + sub.stop)
        parts = [xn[r, g][sub, :] for g in range(len(out_groups))]
        ms = sum(jnp.sum(t * t, axis=-1, keepdims=True) for t in parts) * (1.0 / d)
        rs = lax.rsqrt(ms + EPS)
        for (lo, hi), t in zip(out_groups, parts):
            o_ref[0, rows, lo:hi] = t * rs * fnw_ref[:, lo:hi]

    tasks = []
    if mix_stage:
        for c in range(n_chunks):
            base = 3 * c
            tasks += [(base + min(p, 1), retention(c, p), 3) for p in range(PAIRS)]
            tasks.append((base + 1, gating(c), 4))
    big, relaxed = [], {}

    def place(tick, piece):
        while tick in relaxed:
            tick += 1
        relaxed[tick] = piece

    for r in range(2):
        outs = [functools.partial(out_project, r, g) for g in range(len(out_groups))] if out_stage else []
        projs = ([functools.partial(project, r, lo, hi) for lo, hi in _col_groups(p_new_ref.shape[1])]
                 if project_stage else [])
        for i in range(max(len(outs), len(projs))):
            big += outs[i:i + 1] + projs[i:i + 1]
            if i == len(outs) - 1:
                place(len(big) + 1, functools.partial(finish, r, 0))
                place(len(big) + 3, functools.partial(finish, r, 1))
    side = [(None, relaxed.get(tick)) for tick in range(max(relaxed, default=-1) + 1)]
    _emit_pipelined(big, side, tasks)


def _full(shape):
    return pl.BlockSpec(shape, lambda *_: (0,) * len(shape))


def kernel(x, c, ctx, c_ctx, w_ada, b_ada, norm_w, w_in, ret_decay_f, ret_decay_b, ret_gn_w,
           gmlp_ln_w, w_s, b_s, w_out, final_norm_w):
    bsz, seq, d = x.shape
    ctx_len = ctx.shape[1]
    depth = w_ada.shape[0]
    assert depth == 1 and d % LANES == 0
    assert seq % TILE_KV == 0 and seq % TILE_MAIN == 0 and ctx_len % CHUNK == 0
    assert seq % GRID_W == 0 and TILE_MAIN % (4 * CHUNK) == 0 and TILE_KV % (4 * CHUNK) == 0
    n_chunks = seq // CHUNK
    d3 = 3 * d
    n_kv = QK_W + RET_W
    n_rest = w_in.shape[2] - n_kv
    assert QK_W + n_kv == d
    assert _col_groups(n_rest)[0] == (0, QK_W)
    n_rest_blocks = n_rest // QK_W
    n_out_blocks = d // QK_W
    assert n_rest % QK_W == 0 and d % QK_W == 0
    assert max(n_rest_blocks, n_out_blocks + 1) <= bsz + 1

    quarter = DK // 4
    inv_freq = ROPE_BASE ** (-jnp.arange(quarter, dtype=F32) / quarter)
    invf_lane = jnp.tile(inv_freq, LANES // quarter)[None, :]
    nw = norm_w[0][None, :]
    rows2 = 2 * bsz

    n_col_blocks = d3 // d
    smem = pl.BlockSpec(memory_space=pltpu.SMEM)
    n_out_padded = d + QK_W
    (mod, cos_t, sin_t, dsum, af, ab, kdft, kdbt, gf, gb,
     w_kv, w_s_b, b_s_full) = pl.pallas_call(
        _prologue_kernel,
        grid=(n_col_blocks,),
        in_specs=[smem, smem, _full((bsz, d)), _full((1, d)),
                  pl.BlockSpec((d, d), lambda i: (0, i)),
                  pl.BlockSpec((1, d), lambda i: (0, i)),
                  _full((1, LANES)),
                  pl.BlockSpec((d, d), lambda i: (0, 0)),
                  _full((GROUPS, CHUNK, CHUNK)), _full((GROUPS, CHUNK))],
        out_specs=[pl.BlockSpec((rows2, d), lambda i: (0, i)),
                   _full((seq, LANES)), _full((seq, LANES)),
                   _full((HEADS, CHUNK, CHUNK)),
                   _full((CHUNK, QK_W)), _full((CHUNK, QK_W)),
                   _full((QK_W, CHUNK)), _full((QK_W, CHUNK)),
                   _full((QK_W, DV)), _full((QK_W, DV)),
                   _full((d, n_kv)),
                   _full((GROUPS, CHUNK, CHUNK)), _full((GROUPS, CHUNK, DG))],
        out_shape=[jax.ShapeDtypeStruct((rows2, d3), F32),
                   jax.ShapeDtypeStruct((seq, LANES), F32),
                   jax.ShapeDtypeStruct((seq, LANES), F32),
                   jax.ShapeDtypeStruct((HEADS, CHUNK, CHUNK), F32),
                   jax.ShapeDtypeStruct((CHUNK, QK_W), F32),
                   jax.ShapeDtypeStruct((CHUNK, QK_W), F32),
                   jax.ShapeDtypeStruct((QK_W, CHUNK), F32),
                   jax.ShapeDtypeStruct((QK_W, CHUNK), F32),
                   jax.ShapeDtypeStruct((QK_W, DV), F32),
                   jax.ShapeDtypeStruct((QK_W, DV), F32),
                   jax.ShapeDtypeStruct((d, n_kv), BF16),
                   jax.ShapeDtypeStruct((GROUPS, CHUNK, CHUNK), BF16),
                   jax.ShapeDtypeStruct((GROUPS, CHUNK, DG), F32)],
        compiler_params=pltpu.CompilerParams(dimension_semantics=("arbitrary",)),
        name="prologue",
    )(ret_decay_f[0], ret_decay_b[0], c, c_ctx[None, :], w_ada[0], b_ada[0][None, :], invf_lane,
      w_in[0], w_s[0], b_s[0])

    def rest_block(s):
        return jnp.minimum(s, n_rest_blocks - 1)

    def rest_source(s):
        return jnp.where(s == 0, 0, rest_block(s) + n_kv // QK_W)

    s_cf, s_cb, w_rest, w_out_b = pl.pallas_call(
        functools.partial(_ctx_kernel, bsz, n_out_blocks),
        grid=(bsz + 1,),
        in_specs=[pl.BlockSpec((1, ctx_len, d), lambda s: (jnp.minimum(s, bsz - 1), 0, 0)),
                  _full((rows2, d3)),
                  _full((1, d)), _full((d, n_kv)),
                  _full((QK_W, CHUNK)), _full((QK_W, CHUNK)),
                  _full((QK_W, DV)), _full((QK_W, DV)),
                  pl.BlockSpec((d, QK_W), lambda s: (0, rest_source(s))),
                  pl.BlockSpec((RET_W + GM_W, QK_W),
                               lambda s: (0, jnp.minimum(s, n_out_blocks - 1)))],
        out_specs=[pl.BlockSpec((1, QK_W, DV), lambda s: (jnp.maximum(s - 1, 0), 0, 0)),
                   pl.BlockSpec((1, QK_W, DV), lambda s: (jnp.maximum(s - 1, 0), 0, 0)),
                   pl.BlockSpec((d, QK_W), lambda s: (0, rest_block(s))),
                   pl.BlockSpec((RET_W + GM_W, QK_W),
                                lambda s: (0, jnp.minimum(s, n_out_blocks)))],
        out_shape=[jax.ShapeDtypeStruct((bsz, QK_W, DV), F32),
                   jax.ShapeDtypeStruct((bsz, QK_W, DV), F32),
                   jax.ShapeDtypeStruct((d, n_rest), BF16),
                   jax.ShapeDtypeStruct((RET_W + GM_W, n_out_padded), BF16)],
        scratch_shapes=[pltpu.VMEM((ctx_len, d), BF16),
                        pltpu.VMEM((ctx_len, n_kv), F32),
                        pltpu.VMEM((ctx_len, n_kv), F32)],
        compiler_params=pltpu.CompilerParams(dimension_semantics=("arbitrary",)),
        name="ctx_states",
    )(ctx, mod, nw, w_kv, kdft, kdbt, gf, gb, w_in[0], w_out[0])

    def tile_maps(n_tiles, per_batch, reverse):
        def where(t):
            j = t % per_batch
            return t // per_batch, (per_batch - 1 - j) if reverse else j
        return [lambda s, k=k: where(jnp.clip(s - k, 0, n_tiles - 1)) for k in range(3)]

    nt_kv = seq // TILE_KV
    cpt_kv = TILE_KV // CHUNK
    cur, prv, _ = tile_maps(bsz * nt_kv, nt_kv, reverse=True)
    h, kt, v, st_b = pl.pallas_call(
        functools.partial(_kv_kernel, bsz * nt_kv, nt_kv),
        grid=(bsz * nt_kv + 1,),
        in_specs=[pl.BlockSpec(memory_space=pl.ANY),
                  _full((rows2, d3)),
                  _full((1, d)), _full((d, n_kv)),
                  _full((seq, LANES)), _full((seq, LANES)),
                  _full((QK_W, CHUNK)), _full((QK_W, DV)),
                  pl.BlockSpec((1, QK_W, DV), lambda s: (prv(s)[0], 0, 0))],
        out_specs=[pl.BlockSpec((1, TILE_KV, d), lambda s: (*cur(s), 0)),
                   pl.BlockSpec((1, cpt_kv, QK_W, CHUNK), lambda s: (*prv(s), 0, 0)),
                   pl.BlockSpec((1, TILE_KV, RET_W), lambda s: (*prv(s), 0)),
                   pl.BlockSpec((1, cpt_kv, QK_W, DV), lambda s: (*prv(s), 0, 0))],
        out_shape=[jax.ShapeDtypeStruct((bsz, seq, d), BF16),
                   jax.ShapeDtypeStruct((bsz, n_chunks, QK_W, CHUNK), BF16),
                   jax.ShapeDtypeStruct((bsz, seq, RET_W), BF16),
                   jax.ShapeDtypeStruct((bsz, n_chunks, QK_W, DV), BF16)],
        scratch_shapes=[pltpu.VMEM((QK_W, DV), F32),
                        pltpu.VMEM((TILE_KV, n_kv), F32),
                        pltpu.VMEM((TILE_KV, n_kv), F32),
                        pltpu.VMEM((X_RING, TILE_KV, d), F32),
                        pltpu.SemaphoreType.DMA((X_RING,))],
        compiler_params=pltpu.CompilerParams(dimension_semantics=("arbitrary",)),
        name="kv_sweep",
    )(x, mod, nw, w_kv, cos_t, sin_t, kdbt, gb, s_cb)

    nt = seq // TILE_MAIN
    cpt = TILE_MAIN // CHUNK
    cur, prv, prv2 = tile_maps(bsz * nt, nt, reverse=False)
    out = pl.pallas_call(
        functools.partial(_main_kernel, bsz * nt, nt),
        grid=(bsz * nt + 2,),
        in_specs=[pl.BlockSpec((1, TILE_MAIN, d), lambda s: (*cur(s), 0)),
                  pl.BlockSpec((1, TILE_MAIN, d), lambda s: (*prv2(s), 0)),
                  _full((rows2, d3)),
                  _full((d, n_rest)),
                  pl.BlockSpec((TILE_MAIN, LANES), lambda s: (prv(s)[1], 0)),
                  pl.BlockSpec((TILE_MAIN, LANES), lambda s: (prv(s)[1], 0)),
                  pl.BlockSpec((1, cpt, QK_W, CHUNK), lambda s: (*prv(s), 0, 0)),
                  pl.BlockSpec((1, TILE_MAIN, RET_W), lambda s: (*prv(s), 0)),
                  pl.BlockSpec((1, cpt, QK_W, DV), lambda s: (*prv(s), 0, 0)),
                  pl.BlockSpec((1, QK_W, DV), lambda s: (prv(s)[0], 0, 0)),
                  _full((HEADS, CHUNK, CHUNK)),
                  _full((CHUNK, QK_W)), _full((CHUNK, QK_W)),
                  _full((QK_W, CHUNK)), _full((QK_W, DV)),
                  _full((1, RET_W)), _full((1, GM_W)),
                  _full((GROUPS, CHUNK, CHUNK)), _full((GROUPS, CHUNK, DG)),
                  _full((RET_W + GM_W, n_out_padded)), _full((1, d))],
        out_specs=pl.BlockSpec((1, TILE_MAIN, d), lambda s: (*prv2(s), 0)),
        out_shape=jax.ShapeDtypeStruct((bsz, seq, d), x.dtype),
        scratch_shapes=[pltpu.VMEM((QK_W, DV), F32),
                        pltpu.VMEM((TILE_MAIN, n_rest), F32),
                        pltpu.VMEM((TILE_MAIN, n_rest), F32),
                        pltpu.VMEM((TILE_MAIN, RET_W + GM_W), BF16),
                        pltpu.VMEM((TILE_MAIN, RET_W + GM_W), BF16)],
        compiler_params=pltpu.CompilerParams(dimension_semantics=("arbitrary",)),
        name="main_sweep",
    )(h, x, mod, w_rest, cos_t, sin_t, kt, v, st_b, s_cf, dsum, af, ab, kdft, gf,
      ret_gn_w[0][None, :], gmlp_ln_w[0][None, :], w_s_b, b_s_full, w_out_b,
      final_norm_w[None, :])
    return out
```

```python
import functools

import jax
import jax.numpy as jnp
from jax import lax
from jax.experimental import pallas as pl
from jax.experimental.pallas import tpu as pltpu

F32 = jnp.float32
BF16 = jnp.bfloat16

HEADS = 4
DK = 64
DV = 128
CHUNK = 128
GROUPS = 4
DG = 128
GRID_W = 64
ROPE_BASE = 10000.0
EPS = 1e-6
LANES = 128
PAIRS = HEADS * DK // LANES
RET_W = HEADS * DV
GM_W = GROUPS * DG
QK_W = HEADS * DK

DOT_COLS = 512
TILE_KV = 1024
X_RING = 3
TILE_MAIN = 512


def _rms(x, w):
    ms = jnp.mean(x * x, axis=-1, keepdims=True)
    return x * lax.rsqrt(ms + EPS) * w


def _silu(x):
    return x * jax.nn.sigmoid(x)


def _gelu(x):
    return 0.5 * x * (1.0 + lax.erf(x * (0.5 ** 0.5)))


_dot = functools.partial(jnp.dot, preferred_element_type=F32)


def _col_groups(n):
    first = n % DOT_COLS
    edges = ([0] if first == 0 else [0, first]) + list(range(first + DOT_COLS, n + 1, DOT_COLS))
    return list(zip(edges[:-1], edges[1:]))


def _rope(p, cos, sin_signed, upper16):
    partner = jnp.where(upper16, pltpu.roll(p, 16, 1), pltpu.roll(p, LANES - 16, 1))
    return p * cos + partner * sin_signed


def _per_head_columns(m):
    head = (lax.broadcasted_iota(jnp.int32, m.shape, 0) // DK) % 2
    return jnp.concatenate([jnp.where(head == hh, m, jnp.zeros_like(m)) for hh in range(2)], axis=1)


def _block_diagonal(a, b):
    return jnp.concatenate([jnp.concatenate([a, jnp.zeros_like(b)], axis=1),
                            jnp.concatenate([jnp.zeros_like(a), b], axis=1)], axis=0)


def _kv_pairs(kd, v):
    out = []
    for p in range(PAIRS):
        r = _dot(kd[p * LANES:(p + 1) * LANES, :], v[:, p * 2 * DV:(p + 1) * 2 * DV])
        out.append(r[0:DK, 0:DV])
        out.append(r[DK:2 * DK, DV:2 * DV])
    return jnp.concatenate(out, axis=0)


def _emit_pipelined(big_pieces, side_pieces, tasks):
    in_flight = []
    last_tick = max([start + parts for start, _, parts in tasks] +
                    [len(big_pieces), len(side_pieces)])
    for tick in range(last_tick):
        if tick < len(big_pieces):
            big_pieces[tick]()
        urgent, relaxed = side_pieces[tick] if tick < len(side_pieces) else (None, None)
        if urgent is not None:
            urgent()
        in_flight += [[gen, parts] for start, gen, parts in tasks if start == tick]
        for t in [t for t in in_flight if t[1] > 1] + [t for t in in_flight if t[1] == 1]:
            next(t[0], None)
            t[1] -= 1
        in_flight = [t for t in in_flight if t[1] > 0]
        if relaxed is not None:
            relaxed()
    assert not in_flight


def _run_pipeline_step(step, n_tiles, n_stages, body, refs, bufs_a, bufs_b):
    def call(s_static_parity, stages):
        new, old = (bufs_a, bufs_b) if s_static_parity == 0 else (bufs_b, bufs_a)
        body(stages, *refs, *new, *old)

    edge = n_stages - 1
    for s in list(range(edge)) + list(range(n_tiles, n_tiles + edge)):
        stages = tuple(0 <= s - j < n_tiles for j in range(n_stages))
        pl.when(step == s)(functools.partial(call, s % 2, stages))
    steady = (step >= edge) & (step < n_tiles)
    for parity in range(2):
        pl.when(steady & (lax.rem(step, 2) == parity))(
            functools.partial(call, parity, (True,) * n_stages))


def _prologue_kernel(decf_ref, decb_ref, c_ref, cctx_ref, w_ref, b_ref, invf_ref, wqkv_ref,
                     ws_ref, bs_ref,
                     mod_ref, cos_ref, sin_ref, dsum_ref, af_ref, ab_ref,
                     kdft_ref, kdbt_ref, gf_ref, gb_ref,
                     wkv_ref, wsb_ref, bsf_ref):
    i = pl.program_id(0)
    c = c_ref[...]
    cond = jnp.concatenate([c, jnp.broadcast_to(cctx_ref[...], c.shape)], axis=0)
    mod_ref[...] = _dot(_silu(cond), w_ref[...]) + b_ref[...]

    @pl.when(i == 0)
    def _small_weights():
        wkv_ref[...] = wqkv_ref[:, QK_W:].astype(BF16)
        wsb_ref[...] = ws_ref[...].astype(BF16)
        for g in range(GROUPS):
            bsf_ref[g] = jnp.broadcast_to(bs_ref[g:g + 1, :], (CHUNK, CHUNK)).T

    @pl.when(i == 0)
    def _tables():
        pos = lax.broadcasted_iota(jnp.int32, (GRID_W, LANES), 0).astype(F32)
        lane = lax.broadcasted_iota(jnp.int32, (GRID_W, LANES), 1)
        ang = pos * invf_ref[...]
        cr = jnp.cos(ang)
        sr = jnp.sin(ang)
        sr = jnp.where((lane & 16) == 0, -sr, sr)
        by_row = (lane & 32) == 0
        n_rows = cos_ref.shape[0] // GRID_W
        for r in range(n_rows):
            sl = slice(r * GRID_W, (r + 1) * GRID_W)
            cos_ref[sl, :] = jnp.where(by_row, jnp.broadcast_to(cr[r:r + 1, :], cr.shape), cr)
            sin_ref[sl, :] = jnp.where(by_row, jnp.broadcast_to(sr[r:r + 1, :], sr.shape), sr)

        def log_decay(head_idx, dec_ref):
            d = jnp.full(head_idx.shape, dec_ref[HEADS - 1], F32)
            for h in range(HEADS - 2, -1, -1):
                d = jnp.where(head_idx == h, dec_ref[h], d)
            return -jnp.exp(d)

        n = lax.broadcasted_iota(jnp.int32, (CHUNK, QK_W), 0).astype(F32)
        hl = lax.broadcasted_iota(jnp.int32, (CHUNK, QK_W), 1) // DK
        af_ref[...] = jnp.exp((n + 1.0) * log_decay(hl, decf_ref))
        ab_ref[...] = jnp.exp((CHUNK - n) * log_decay(hl, decb_ref))

        m = lax.broadcasted_iota(jnp.int32, (QK_W, CHUNK), 1).astype(F32)
        hr = lax.broadcasted_iota(jnp.int32, (QK_W, CHUNK), 0) // DK
        lgf = log_decay(hr, decf_ref)
        lgb = log_decay(hr, decb_ref)
        kdft_ref[...] = jnp.exp((CHUNK - 1.0 - m) * lgf)
        kdbt_ref[...] = jnp.exp(m * lgb)
        gf_ref[...] = jnp.exp(float(CHUNK) * lgf)
        gb_ref[...] = jnp.exp(float(CHUNK) * lgb)

        nn = lax.broadcasted_iota(jnp.int32, (CHUNK, CHUNK), 0).astype(F32)
        mm = lax.broadcasted_iota(jnp.int32, (CHUNK, CHUNK), 1).astype(F32)
        for h in range(HEADS):
            lf = -jnp.exp(jnp.full((CHUNK, CHUNK), decf_ref[h], F32))
            lb = -jnp.exp(jnp.full((CHUNK, CHUNK), decb_ref[h], F32))
            df = nn - mm
            db = mm - nn
            fwd = jnp.where(df >= 0, jnp.exp(jnp.maximum(df, 0.0) * lf), 0.0)
            bwd = jnp.where(db >= 0, jnp.exp(jnp.maximum(db, 0.0) * lb), 0.0)
            dsum_ref[h] = fwd + bwd


def _ctx_kernel(n_tiles, n_out_blocks, ctx_ref, mod_ref, nw_ref, wkv_ref, kdft_ref, kdbt_ref,
                gf_ref, gb_ref, win_ref, wout_ref,
                scf_ref, scb_ref, wrest_ref, woutb_ref, h_ref, pa_ref, pb_ref):
    s = pl.program_id(0)
    keep = s < n_out_blocks
    refs = (ctx_ref, mod_ref, nw_ref, wkv_ref, kdft_ref, kdbt_ref, gf_ref, gb_ref,
            win_ref, wout_ref, keep, scf_ref, scb_ref, wrest_ref, woutb_ref, h_ref)
    _run_pipeline_step(s, n_tiles, 2, _ctx_step, refs, (pa_ref,), (pb_ref,))


def _ctx_step(stages, ctx_ref, mod_ref, nw_ref, wkv_ref, kdft_ref, kdbt_ref, gf_ref, gb_ref,
              win_ref, wout_ref, keep, scf_ref, scb_ref, wrest_ref, woutb_ref, h_ref,
              p_new_ref, p_old):
    project_stage, state_stage = stages
    wrest_ref[...] = win_ref[...].astype(BF16)
    woutb_ref[...] = jnp.where(keep, wout_ref[...], 0.0).astype(BF16)
    d = ctx_ref.shape[-1]
    n_chunks = ctx_ref.shape[1] // CHUNK
    ctx_row = mod_ref.shape[0] // 2
    mod = mod_ref[ctx_row:ctx_row + 1, :]

    def norm_rows(rows):
        h = _rms(ctx_ref[0, rows, :], nw_ref[...]) * (1.0 + mod[:, d:2 * d]) + mod[:, 0:d]
        h_ref[rows, :] = h.astype(BF16)

    def project(lo, hi):
        p_new_ref[:, lo:hi] = _dot(h_ref[...], wkv_ref[:, lo:hi])

    new_f, new_b = {}, {}

    def chunk(c):
        sl = slice(c * CHUNK, (c + 1) * CHUNK)
        kt = (p_old[sl, 0:QK_W] * (DK ** -0.5)).T
        v = p_old[sl, QK_W:].astype(BF16)
        new_f[c] = _kv_pairs((kt * kdft_ref[...]).astype(BF16), v)
        new_b[c] = _kv_pairs((kt * kdbt_ref[...]).astype(BF16), v)
        yield

    def combine():
        yield
        sf = jnp.zeros((QK_W, DV), F32)
        for c in range(n_chunks):
            sf = gf_ref[...] * sf + new_f[c]
        sb = jnp.zeros((QK_W, DV), F32)
        for c in range(n_chunks - 1, -1, -1):
            sb = gb_ref[...] * sb + new_b[c]
        scf_ref[0] = sf
        scb_ref[0] = sb

    big, tasks = [], []
    if project_stage:
        for c in range(n_chunks):
            norm_rows(slice(c * CHUNK, (c + 1) * CHUNK))
        big = [functools.partial(project, lo, hi) for lo, hi in _col_groups(p_new_ref.shape[1])]
    if state_stage:
        tasks = [(c, chunk(c), 1) for c in range(n_chunks)] + [(n_chunks - 1, combine(), 2)]
    _emit_pipelined(big, [], tasks)


def _kv_kernel(n_tiles, tiles_per_batch, x_hbm, modc_ref, nw_ref, wkv_ref, cos_ref, sin_ref,
               kdbt_ref, gb_ref, scb_ref, h_ref, kt_ref, v_ref, stb_ref, sb_ref, pa_ref, pb_ref,
               xbuf_ref, xsem_ref):
    s = pl.program_id(0)
    tile = xbuf_ref.shape[1]

    def x_copy(t):
        b = t // tiles_per_batch
        j = tiles_per_batch - 1 - lax.rem(t, tiles_per_batch)
        slot = lax.rem(t, X_RING)
        return pltpu.make_async_copy(x_hbm.at[b, pl.ds(j * tile, tile), :],
                                     xbuf_ref.at[slot], xsem_ref.at[slot])

    @pl.when(s == 0)
    def _prime():
        for t in range(min(X_RING - 1, n_tiles)):
            x_copy(t).start()

    @pl.when(s < n_tiles)
    def _arrived():
        x_copy(s).wait()

    @pl.when(s + X_RING - 1 < n_tiles)
    def _prefetch():
        x_copy(s + X_RING - 1).start()

    @pl.when((s >= 1) & (lax.rem(s - 1, tiles_per_batch) == 0))
    def _init():
        sb_ref[...] = scb_ref[0]

    batch = jnp.minimum(s, n_tiles - 1) // tiles_per_batch
    modc = modc_ref[pl.ds(batch, 1), :]
    xc_ref = xbuf_ref.at[lax.rem(s, X_RING)]
    finished = jnp.clip(s - 1, 0, n_tiles - 1)
    first_row = pl.multiple_of(
        (tiles_per_batch - 1 - lax.rem(finished, tiles_per_batch)) * tile, tile)
    cos_ref = cos_ref.at[pl.ds(first_row, tile), :]
    sin_ref = sin_ref.at[pl.ds(first_row, tile), :]
    refs = (xc_ref, modc, nw_ref, wkv_ref, cos_ref, sin_ref, kdbt_ref, gb_ref,
            h_ref, kt_ref, v_ref, stb_ref, sb_ref)
    _run_pipeline_step(s, n_tiles, 2, _kv_step, refs, (pa_ref,), (pb_ref,))


def _kv_step(stages, xc_ref, modc, nw_ref, wkv_ref, cos_ref, sin_ref, kdbt_ref, gb_ref,
             h_ref, kt_ref, v_ref, stb_ref, sb_ref, p_new_ref, p_old):
    project_stage, finish_stage = stages
    d = xc_ref.shape[-1]
    tile = xc_ref.shape[0]
    half = tile // 2
    row_halves = [slice(0, half), slice(half, tile)]
    quarter = [slice(i * half // 2, (i + 1) * half // 2) for i in range(4)]
    upper16 = (lax.broadcasted_iota(jnp.int32, (CHUNK, LANES), 1) & 16) != 0

    def norm_rows(rows):
        h = _rms(xc_ref[rows, :], nw_ref[...]) * (1.0 + modc[:, d:2 * d]) + modc[:, 0:d]
        h_ref[0, rows, :] = h.astype(BF16)

    def project(rows, lo, hi):
        p_new_ref[rows, lo:hi] = _dot(h_ref[0, rows, :], wkv_ref[:, lo:hi])

    def chunk(c):
        sl = slice(c * CHUNK, (c + 1) * CHUNK)
        k = [_rope(p_old[sl, p * LANES:(p + 1) * LANES], cos_ref[sl, :], sin_ref[sl, :], upper16)
             * (DK ** -0.5) for p in range(PAIRS)]
        v = p_old[sl, QK_W:].astype(BF16)
        v_ref[0, sl, :] = v
        kt = jnp.concatenate([kp.T for kp in k], axis=0)
        kt_ref[0, c] = kt.astype(BF16)
        new = _kv_pairs((kt * kdbt_ref[...]).astype(BF16), v)
        yield
        stb_ref[0, c] = sb_ref[...].astype(BF16)
        sb_ref[...] = gb_ref[...] * sb_ref[...] + new

    projections, late_norms, tasks = [], [], []
    if project_stage:
        norm_rows(quarter[0])
        norm_rows(quarter[1])
        projections = [functools.partial(project, row_halves[r], lo, hi)
                       for r in range(2) for lo, hi in _col_groups(p_new_ref.shape[1])]
        late_norms = [(functools.partial(norm_rows, q), None) for q in quarter[2:]]
    if finish_stage:
        tasks = [(tick, chunk(c), 2) for tick, c in enumerate(range(tile // CHUNK - 1, -1, -1))]
    _emit_pipelined(projections, late_norms, tasks)


def _main_kernel(n_tiles, tiles_per_batch, h_ref, x_ref, mod_ref, w_ref, cos_ref,
                 sin_ref, kt_ref, v_ref, stb_ref, scf_ref, dsum_ref, af_ref, ab_ref, kdft_ref,
                 gf_ref, gnw_ref, lnw_ref, ws_ref, bs_ref, wout_ref, fnw_ref, o_ref,
                 sf_ref, pa_ref, pb_ref, ya_ref, yb_ref):
    s = pl.program_id(0)

    @pl.when((s >= 1) & (lax.rem(s - 1, tiles_per_batch) == 0))
    def _init():
        sf_ref[...] = scf_ref[0]

    batch = jnp.clip(s - 2, 0, n_tiles - 1) // tiles_per_batch
    mod = mod_ref[pl.ds(batch, 1), :]
    refs = (h_ref, x_ref, mod, w_ref, cos_ref, sin_ref, kt_ref, v_ref,
            stb_ref, dsum_ref, af_ref, ab_ref, kdft_ref, gf_ref, gnw_ref, lnw_ref, ws_ref,
            bs_ref, wout_ref, fnw_ref, o_ref, sf_ref)
    _run_pipeline_step(s, n_tiles, 3, _main_step, refs, (pa_ref, ya_ref), (pb_ref, yb_ref))


def _main_step(stages, h_ref, x_ref, mod, w_ref, cos_ref, sin_ref, kt_ref, v_ref,
               stb_ref, dsum_ref, af_ref, ab_ref, kdft_ref, gf_ref, gnw_ref, lnw_ref, ws_ref,
               bs_ref, wout_ref, fnw_ref, o_ref, sf_ref, p_new_ref, y_ref, p_all, y_old):
    project_stage, mix_stage, out_stage = stages
    d = x_ref.shape[-1]
    tile = x_ref.shape[1]
    n_chunks = tile // CHUNK
    half = tile // 2
    row_halves = [slice(0, half), slice(half, tile)]
    c_gr = QK_W
    c_u = c_gr + RET_W
    c_vm = c_u + GM_W
    c_gm = c_vm + GM_W
    lane = lax.broadcasted_iota(jnp.int32, (CHUNK, LANES), 1)
    upper16 = (lane & 16) != 0

    operand = {}

    def loaded_once(key, load):
        if key not in operand:
            operand[key] = load()
        return operand[key]

    def project(r, lo, hi):
        rows = row_halves[r]
        lhs = loaded_once(("h", r), lambda: h_ref[0, rows, :])
        p_new_ref[rows, lo:hi] = _dot(lhs, w_ref[:, lo:hi])

    def retention(c, p):
        sl = slice(c * CHUNK, (c + 1) * CHUNK)
        ps = slice(p * LANES, (p + 1) * LANES)
        q2 = _rope(p_all[sl, ps], cos_ref[sl, :], sin_ref[sl, :], upper16)
        kt = kt_ref[0, c, ps, :]
        vp = v_ref[0, sl, p * 2 * DV:(p + 1) * 2 * DV]
        qa = q2 * af_ref[:, ps]
        qb = q2 * ab_ref[:, ps]
        scores = _dot(q2.astype(BF16), _per_head_columns(kt))
        qx = jnp.concatenate([qa, qb], axis=1).astype(BF16)
        kd = (kt.astype(F32) * kdft_ref[ps, :]).astype(BF16)
        yield
        state = jnp.concatenate([sf_ref[ps, :].astype(BF16), stb_ref[0, c, ps, :]], axis=0)
        from_state = _dot(qx, _per_head_columns(state))
        decay = jnp.concatenate([dsum_ref[2 * p + hh] for hh in range(2)], axis=1)
        both = _dot((scores * decay).astype(BF16), _block_diagonal(vp[:, 0:DV], vp[:, DV:])) + from_state
        o = [both[:, hh * DV:(hh + 1) * DV] for hh in range(2)]
        r = _dot(kd, vp)
        yield
        new = jnp.concatenate([r[0:DK, 0:DV], r[DK:2 * DK, DV:2 * DV]], axis=0)
        sf_ref[ps, :] = gf_ref[ps, :] * sf_ref[ps, :] + new
        for hh in range(2):
            head = 2 * p + hh
            mu = jnp.mean(o[hh], axis=-1, keepdims=True)
            var = jnp.mean(jnp.square(o[hh] - mu), axis=-1, keepdims=True)
            hs = slice(head * DV, (head + 1) * DV)
            y = (o[hh] - mu) * lax.rsqrt(var + EPS) * gnw_ref[:, hs]
            gate_cols = slice(c_gr + head * DV, c_gr + (head + 1) * DV)
            y_ref[sl, hs] = (y * _silu(p_all[sl, gate_cols])).astype(BF16)

    def gating(c):
        sl = slice(c * CHUNK, (c + 1) * CHUNK)
        vg = _gelu(p_all[sl, c_vm:c_gm])
        mu = jnp.mean(vg, axis=-1, keepdims=True)
        var = jnp.mean(jnp.square(vg - mu), axis=-1, keepdims=True)
        yield
        vn = ((vg - mu) * lax.rsqrt(var + EPS) * lnw_ref[...]).astype(BF16)
        s = [_dot(ws_ref[g], vn[:, g * DG:(g + 1) * DG]) for g in range(GROUPS)]
        yield
        for g in range(GROUPS):
            if g == GROUPS // 2:
                yield
            u = _gelu(p_all[sl, c_u + g * DG:c_u + (g + 1) * DG])
            gm = _silu(p_all[sl, c_gm + g * DG:c_gm + (g + 1) * DG])
            y_ref[sl, RET_W + g * DG:RET_W + (g + 1) * DG] = (u * (s[g] + bs_ref[g]) * gm).astype(BF16)

    out_groups = _col_groups(d)
    xn = {}

    def out_project(r, g):
        rows = row_halves[r]
        lo, hi = out_groups[g]
        lhs = loaded_once(("y", r), lambda: y_old[rows, :])
        out = _dot(lhs, wout_ref[:, lo:hi])
        xn[r, g] = x_ref[0, rows, lo:hi] + mod[:, 2 * d + lo:2 * d + hi] * out

    def finish(r, part):
        sub = slice(part * half // 2, (part + 1) * half // 2)
        rows = slice(row_halves[r].start + sub.start, row_halves[r].start + sub.stop)
        parts = [xn[r, g][sub, :] for g in range(len(out_groups))]
        ms = sum(jnp.sum(t * t, axis=-1, keepdims=True) for t in parts) * (1.0 / d)
        rs = lax.rsqrt(ms + EPS)
        for (lo, hi), t in zip(out_groups, parts):
            o_ref[0, rows, lo:hi] = t * rs * fnw_ref[:, lo:hi]

    tasks = []
    if mix_stage:
        for c in range(n_chunks):
            base = 3 * c
            tasks += [(base + min(p, 1), retention(c, p), 3) for p in range(PAIRS)]
            tasks.append((base + 1, gating(c), 4))
    big, relaxed = [], {}

    def place(tick, piece):
        while tick in relaxed:
            tick += 1
        relaxed[tick] = piece

    for r in range(2):
        outs = [functools.partial(out_project, r, g) for g in range(len(out_groups))] if out_stage else []
        projs = ([functools.partial(project, r, lo, hi) for lo, hi in _col_groups(p_new_ref.shape[1])]
                 if project_stage else [])
        for i in range(max(len(outs), len(projs))):
            big += outs[i:i + 1] + projs[i:i + 1]
            if i == len(outs) - 1:
                place(len(big) + 1, functools.partial(finish, r, 0))
                place(len(big) + 3, functools.partial(finish, r, 1))
    side = [(None, relaxed.get(tick)) for tick in range(max(relaxed, default=-1) + 1)]
    _emit_pipelined(big, side, tasks)


def _full(shape):
    return pl.BlockSpec(shape, lambda *_: (0,) * len(shape))


def kernel(x, c, ctx, c_ctx, w_ada, b_ada, norm_w, w_in, ret_decay_f, ret_decay_b, ret_gn_w,
           gmlp_ln_w, w_s, b_s, w_out, final_norm_w):
    bsz, seq, d = x.shape
    ctx_len = ctx.shape[1]
    depth = w_ada.shape[0]
    assert depth == 1 and d % LANES == 0
    assert seq % TILE_KV == 0 and seq % TILE_MAIN == 0 and ctx_len % CHUNK == 0
    assert seq % GRID_W == 0 and TILE_MAIN % (4 * CHUNK) == 0 and TILE_KV % (4 * CHUNK) == 0
    n_chunks = seq // CHUNK
    d3 = 3 * d
    n_kv = QK_W + RET_W
    n_rest = w_in.shape[2] - n_kv
    assert QK_W + n_kv == d
    assert _col_groups(n_rest)[0] == (0, QK_W)
    n_rest_blocks = n_rest // QK_W
    n_out_blocks = d // QK_W
    assert n_rest % QK_W == 0 and d % QK_W == 0
    assert max(n_rest_blocks, n_out_blocks + 1) <= bsz + 1

    quarter = DK // 4
    inv_freq = ROPE_BASE ** (-jnp.arange(quarter, dtype=F32) / quarter)
    invf_lane = jnp.tile(inv_freq, LANES // quarter)[None, :]
    nw = norm_w[0][None, :]
    rows2 = 2 * bsz

    n_col_blocks = d3 // d
    smem = pl.BlockSpec(memory_space=pltpu.SMEM)
    n_out_padded = d + QK_W
    (mod, cos_t, sin_t, dsum, af, ab, kdft, kdbt, gf, gb,
     w_kv, w_s_b, b_s_full) = pl.pallas_call(
        _prologue_kernel,
        grid=(n_col_blocks,),
        in_specs=[smem, smem, _full((bsz, d)), _full((1, d)),
                  pl.BlockSpec((d, d), lambda i: (0, i)),
                  pl.BlockSpec((1, d), lambda i: (0, i)),
                  _full((1, LANES)),
                  pl.BlockSpec((d, d), lambda i: (0, 0)),
                  _full((GROUPS, CHUNK, CHUNK)), _full((GROUPS, CHUNK))],
        out_specs=[pl.BlockSpec((rows2, d), lambda i: (0, i)),
                   _full((seq, LANES)), _full((seq, LANES)),
                   _full((HEADS, CHUNK, CHUNK)),
                   _full((CHUNK, QK_W)), _full((CHUNK, QK_W)),
                   _full((QK_W, CHUNK)), _full((QK_W, CHUNK)),
                   _full((QK_W, DV)), _full((QK_W, DV)),
                   _full((d, n_kv)),
                   _full((GROUPS, CHUNK, CHUNK)), _full((GROUPS, CHUNK, DG))],
        out_shape=[jax.ShapeDtypeStruct((rows2, d3), F32),
                   jax.ShapeDtypeStruct((seq, LANES), F32),
                   jax.ShapeDtypeStruct((seq, LANES), F32),
                   jax.ShapeDtypeStruct((HEADS, CHUNK, CHUNK), F32),
                   jax.ShapeDtypeStruct((CHUNK, QK_W), F32),
                   jax.ShapeDtypeStruct((CHUNK, QK_W), F32),
                   jax.ShapeDtypeStruct((QK_W, CHUNK), F32),
                   jax.ShapeDtypeStruct((QK_W, CHUNK), F32),
                   jax.ShapeDtypeStruct((QK_W, DV), F32),
                   jax.ShapeDtypeStruct((QK_W, DV), F32),
                   jax.ShapeDtypeStruct((d, n_kv), BF16),
                   jax.ShapeDtypeStruct((GROUPS, CHUNK, CHUNK), BF16),
                   jax.ShapeDtypeStruct((GROUPS, CHUNK, DG), F32)],
        compiler_params=pltpu.CompilerParams(dimension_semantics=("arbitrary",)),
        name="prologue",
    )(ret_decay_f[0], ret_decay_b[0], c, c_ctx[None, :], w_ada[0], b_ada[0][None, :], invf_lane,
      w_in[0], w_s[0], b_s[0])

    def rest_block(s):
        return jnp.minimum(s, n_rest_blocks - 1)

    def rest_source(s):
        return jnp.where(s == 0, 0, rest_block(s) + n_kv // QK_W)

    s_cf, s_cb, w_rest, w_out_b = pl.pallas_call(
        functools.partial(_ctx_kernel, bsz, n_out_blocks),
        grid=(bsz + 1,),
        in_specs=[pl.BlockSpec((1, ctx_len, d), lambda s: (jnp.minimum(s, bsz - 1), 0, 0)),
                  _full((rows2, d3)),
                  _full((1, d)), _full((d, n_kv)),
                  _full((QK_W, CHUNK)), _full((QK_W, CHUNK)),
                  _full((QK_W, DV)), _full((QK_W, DV)),
                  pl.BlockSpec((d, QK_W), lambda s: (0, rest_source(s))),
                  pl.BlockSpec((RET_W + GM_W, QK_W),
                               lambda s: (0, jnp.minimum(s, n_out_blocks - 1)))],
        out_specs=[pl.BlockSpec((1, QK_W, DV), lambda s: (jnp.maximum(s - 1, 0), 0, 0)),
                   pl.BlockSpec((1, QK_W, DV), lambda s: (jnp.maximum(s - 1, 0), 0, 0)),
                   pl.BlockSpec((d, QK_W), lambda s: (0, rest_block(s))),
                   pl.BlockSpec((RET_W + GM_W, QK_W),
                                lambda s: (0, jnp.minimum(s, n_out_blocks)))],
        out_shape=[jax.ShapeDtypeStruct((bsz, QK_W, DV), F32),
                   jax.ShapeDtypeStruct((bsz, QK_W, DV), F32),
                   jax.ShapeDtypeStruct((d, n_rest), BF16),
                   jax.ShapeDtypeStruct((RET_W + GM_W, n_out_padded), BF16)],
        scratch_shapes=[pltpu.VMEM((ctx_len, d), BF16),
                        pltpu.VMEM((ctx_len, n_kv), F32),
                        pltpu.VMEM((ctx_len, n_kv), F32)],
        compiler_params=pltpu.CompilerParams(dimension_semantics=("arbitrary",)),
        name="ctx_states",
    )(ctx, mod, nw, w_kv, kdft, kdbt, gf, gb, w_in[0], w_out[0])

    def tile_maps(n_tiles, per_batch, reverse):
        def where(t):
            j = t % per_batch
            return t // per_batch, (per_batch - 1 - j) if reverse else j
        return [lambda s, k=k: where(jnp.clip(s - k, 0, n_tiles - 1)) for k in range(3)]

    nt_kv = seq // TILE_KV
    cpt_kv = TILE_KV // CHUNK
    cur, prv, _ = tile_maps(bsz * nt_kv, nt_kv, reverse=True)
    h, kt, v, st_b = pl.pallas_call(
        functools.partial(_kv_kernel, bsz * nt_kv, nt_kv),
        grid=(bsz * nt_kv + 1,),
        in_specs=[pl.BlockSpec(memory_space=pl.ANY),
                  _full((rows2, d3)),
                  _full((1, d)), _full((d, n_kv)),
                  _full((seq, LANES)), _full((seq, LANES)),
                  _full((QK_W, CHUNK)), _full((QK_W, DV)),
                  pl.BlockSpec((1, QK_W, DV), lambda s: (prv(s)[0], 0, 0))],
        out_specs=[pl.BlockSpec((1, TILE_KV, d), lambda s: (*cur(s), 0)),
                   pl.BlockSpec((1, cpt_kv, QK_W, CHUNK), lambda s: (*prv(s), 0, 0)),
                   pl.BlockSpec((1, TILE_KV, RET_W), lambda s: (*prv(s), 0)),
                   pl.BlockSpec((1, cpt_kv, QK_W, DV), lambda s: (*prv(s), 0, 0))],
        out_shape=[jax.ShapeDtypeStruct((bsz, seq, d), BF16),
                   jax.ShapeDtypeStruct((bsz, n_chunks, QK_W, CHUNK), BF16),
                   jax.ShapeDtypeStruct((bsz, seq, RET_W), BF16),
                   jax.ShapeDtypeStruct((bsz, n_chunks, QK_W, DV), BF16)],
        scratch_shapes=[pltpu.VMEM((QK_W, DV), F32),
                        pltpu.VMEM((TILE_KV, n_kv), F32),
                        pltpu.VMEM((TILE_KV, n_kv), F32),
                        pltpu.VMEM((X_RING, TILE_KV, d), F32),
                        pltpu.SemaphoreType.DMA((X_RING,))],
        compiler_params=pltpu.CompilerParams(dimension_semantics=("arbitrary",)),
        name="kv_sweep",
    )(x, mod, nw, w_kv, cos_t, sin_t, kdbt, gb, s_cb)

    nt = seq // TILE_MAIN
    cpt = TILE_MAIN // CHUNK
    cur, prv, prv2 = tile_maps(bsz * nt, nt, reverse=False)
    out = pl.pallas_call(
        functools.partial(_main_kernel, bsz * nt, nt),
        grid=(bsz * nt + 2,),
        in_specs=[pl.BlockSpec((1, TILE_MAIN, d), lambda s: (*cur(s), 0)),
                  pl.BlockSpec((1, TILE_MAIN, d), lambda s: (*prv2(s), 0)),
                  _full((rows2, d3)),
                  _full((d, n_rest)),
                  pl.BlockSpec((TILE_MAIN, LANES), lambda s: (prv(s)[1], 0)),
                  pl.BlockSpec((TILE_MAIN, LANES), lambda s: (prv(s)[1], 0)),
                  pl.BlockSpec((1, cpt, QK_W, CHUNK), lambda s: (*prv(s), 0, 0)),
                  pl.BlockSpec((1, TILE_MAIN, RET_W), lambda s: (*prv(s), 0)),
                  pl.BlockSpec((1, cpt, QK_W, DV), lambda s: (*prv(s), 0, 0)),
                  pl.BlockSpec((1, QK_W, DV), lambda s: (prv(s)[0], 0, 0)),
                  _full((HEADS, CHUNK, CHUNK)),
                  _full((CHUNK, QK_W)), _full((CHUNK, QK_W)),
                  _full((QK_W, CHUNK)), _full((QK_W, DV)),
                  _full((1, RET_W)), _full((1, GM_W)),
                  _full((GROUPS, CHUNK, CHUNK)), _full((GROUPS, CHUNK, DG)),
                  _full((RET_W + GM_W, n_out_padded)), _full((1, d))],
        out_specs=pl.BlockSpec((1, TILE_MAIN, d), lambda s: (*prv2(s), 0)),
        out_shape=jax.ShapeDtypeStruct((bsz, seq, d), x.dtype),
        scratch_shapes=[pltpu.VMEM((QK_W, DV), F32),
                        pltpu.VMEM((TILE_MAIN, n_rest), F32),
                        pltpu.VMEM((TILE_MAIN, n_rest), F32),
                        pltpu.VMEM((TILE_MAIN, RET_W + GM_W), BF16),
                        pltpu.VMEM((TILE_MAIN, RET_W + GM_W), BF16)],
        compiler_params=pltpu.CompilerParams(dimension_semantics=("arbitrary",)),
        name="main_sweep",
    )(h, x, mod, w_rest, cos_t, sin_t, kt, v, st_b, s_cf, dsum, af, ab, kdft, gf,
      ret_gn_w[0][None, :], gmlp_ln_w[0][None, :], w_s_b, b_s_full, w_out_b,
      final_norm_w[None, :])
    return out
```

```python
import functools

import jax
import jax.numpy as jnp
from jax import lax
from jax.experimental import pallas as pl
from jax.experimental.pallas import tpu as pltpu

F32 = jnp.float32
BF16 = jnp.bfloat16

HEADS = 4
DK = 64
DV = 128
CHUNK = 128
GROUPS = 4
DG = 128
GRID_W = 64
ROPE_BASE = 10000.0
EPS = 1e-6
LANES = 128
PAIRS = HEADS * DK // LANES
RET_W = HEADS * DV
GM_W = GROUPS * DG
QK_W = HEADS * DK

DOT_COLS = 512
TILE_KV = 1024
X_RING = 3
X_DMA_PRIORITY = 1
TILE_MAIN = 512


def _rms(x, w):
    ms = jnp.mean(x * x, axis=-1, keepdims=True)
    return x * lax.rsqrt(ms + EPS) * w


def _silu(x):
    return x * jax.nn.sigmoid(x)


def _gelu(x):
    return 0.5 * x * (1.0 + lax.erf(x * (0.5 ** 0.5)))


_dot = functools.partial(jnp.dot, preferred_element_type=F32)


def _col_groups(n):
    first = n % DOT_COLS
    edges = ([0] if first == 0 else [0, first]) + list(range(first + DOT_COLS, n + 1, DOT_COLS))
    return list(zip(edges[:-1], edges[1:]))


def _rope(p, cos, sin_signed, upper16):
    partner = jnp.where(upper16, pltpu.roll(p, 16, 1), pltpu.roll(p, LANES - 16, 1))
    return p * cos + partner * sin_signed


def _per_head_columns(m):
    head = (lax.broadcasted_iota(jnp.int32, m.shape, 0) // DK) % 2
    return jnp.concatenate([jnp.where(head == hh, m, jnp.zeros_like(m)) for hh in range(2)], axis=1)


def _block_diagonal(a, b):
    return jnp.concatenate([jnp.concatenate([a, jnp.zeros_like(b)], axis=1),
                            jnp.concatenate([jnp.zeros_like(a), b], axis=1)], axis=0)


def _kv_pairs(kd, v):
    out = []
    for p in range(PAIRS):
        r = _dot(kd[p * LANES:(p + 1) * LANES, :], v[:, p * 2 * DV:(p + 1) * 2 * DV])
        out.append(r[0:DK, 0:DV])
        out.append(r[DK:2 * DK, DV:2 * DV])
    return jnp.concatenate(out, axis=0)


def _emit_pipelined(big_pieces, side_pieces, tasks):
    in_flight = []
    last_tick = max([start + parts for start, _, parts in tasks] +
                    [len(big_pieces), len(side_pieces)])
    for tick in range(last_tick):
        if tick < len(big_pieces):
            big_pieces[tick]()
        urgent, relaxed = side_pieces[tick] if tick < len(side_pieces) else (None, None)
        if urgent is not None:
            urgent()
        in_flight += [[gen, parts] for start, gen, parts in tasks if start == tick]
        for t in [t for t in in_flight if t[1] > 1] + [t for t in in_flight if t[1] == 1]:
            next(t[0], None)
            t[1] -= 1
        in_flight = [t for t in in_flight if t[1] > 0]
        if relaxed is not None:
            relaxed()
    assert not in_flight


def _run_pipeline_step(step, n_tiles, n_stages, body, refs, bufs_a, bufs_b):
    def call(s_static_parity, stages):
        new, old = (bufs_a, bufs_b) if s_static_parity == 0 else (bufs_b, bufs_a)
        body(stages, *refs, *new, *old)

    edge = n_stages - 1
    for s in list(range(edge)) + list(range(n_tiles, n_tiles + edge)):
        stages = tuple(0 <= s - j < n_tiles for j in range(n_stages))
        pl.when(step == s)(functools.partial(call, s % 2, stages))
    steady = (step >= edge) & (step < n_tiles)
    for parity in range(2):
        pl.when(steady & (lax.rem(step, 2) == parity))(
            functools.partial(call, parity, (True,) * n_stages))


def _prologue_kernel(decf_ref, decb_ref, c_ref, cctx_ref, w_ref, b_ref, invf_ref, wqkv_ref,
                     ws_ref, bs_ref,
                     mod_ref, cos_ref, sin_ref, dsum_ref, af_ref, ab_ref,
                     kdft_ref, kdbt_ref, gf_ref, gb_ref,
                     wkv_ref, wsb_ref, bsf_ref):
    i = pl.program_id(0)
    c = c_ref[...]
    cond = jnp.concatenate([c, jnp.broadcast_to(cctx_ref[...], c.shape)], axis=0)
    mod_ref[...] = _dot(_silu(cond), w_ref[...]) + b_ref[...]

    @pl.when(i == 0)
    def _small_weights():
        wkv_ref[...] = wqkv_ref[:, QK_W:].astype(BF16)
        wsb_ref[...] = ws_ref[...].astype(BF16)
        for g in range(GROUPS):
            bsf_ref[g] = jnp.broadcast_to(bs_ref[g:g + 1, :], (CHUNK, CHUNK)).T

    @pl.when(i == 0)
    def _tables():
        pos = lax.broadcasted_iota(jnp.int32, (GRID_W, LANES), 0).astype(F32)
        lane = lax.broadcasted_iota(jnp.int32, (GRID_W, LANES), 1)
        ang = pos * invf_ref[...]
        cr = jnp.cos(ang)
        sr = jnp.sin(ang)
        sr = jnp.where((lane & 16) == 0, -sr, sr)
        by_row = (lane & 32) == 0
        n_rows = cos_ref.shape[0] // GRID_W
        for r in range(n_rows):
            sl = slice(r * GRID_W, (r + 1) * GRID_W)
            cos_ref[sl, :] = jnp.where(by_row, jnp.broadcast_to(cr[r:r + 1, :], cr.shape), cr)
            sin_ref[sl, :] = jnp.where(by_row, jnp.broadcast_to(sr[r:r + 1, :], sr.shape), sr)

        def log_decay(head_idx, dec_ref):
            d = jnp.full(head_idx.shape, dec_ref[HEADS - 1], F32)
            for h in range(HEADS - 2, -1, -1):
                d = jnp.where(head_idx == h, dec_ref[h], d)
            return -jnp.exp(d)

        n = lax.broadcasted_iota(jnp.int32, (CHUNK, QK_W), 0).astype(F32)
        hl = lax.broadcasted_iota(jnp.int32, (CHUNK, QK_W), 1) // DK
        af_ref[...] = jnp.exp((n + 1.0) * log_decay(hl, decf_ref))
        ab_ref[...] = jnp.exp((CHUNK - n) * log_decay(hl, decb_ref))

        m = lax.broadcasted_iota(jnp.int32, (QK_W, CHUNK), 1).astype(F32)
        hr = lax.broadcasted_iota(jnp.int32, (QK_W, CHUNK), 0) // DK
        lgf = log_decay(hr, decf_ref)
        lgb = log_decay(hr, decb_ref)
        kdft_ref[...] = jnp.exp((CHUNK - 1.0 - m) * lgf)
        kdbt_ref[...] = jnp.exp(m * lgb)
        gf_ref[...] = jnp.exp(float(CHUNK) * lgf)
        gb_ref[...] = jnp.exp(float(CHUNK) * lgb)

        nn = lax.broadcasted_iota(jnp.int32, (CHUNK, CHUNK), 0).astype(F32)
        mm = lax.broadcasted_iota(jnp.int32, (CHUNK, CHUNK), 1).astype(F32)
        for h in range(HEADS):
            lf = -jnp.exp(jnp.full((CHUNK, CHUNK), decf_ref[h], F32))
            lb = -jnp.exp(jnp.full((CHUNK, CHUNK), decb_ref[h], F32))
            df = nn - mm
            db = mm - nn
            fwd = jnp.where(df >= 0, jnp.exp(jnp.maximum(df, 0.0) * lf), 0.0)
            bwd = jnp.where(db >= 0, jnp.exp(jnp.maximum(db, 0.0) * lb), 0.0)
            dsum_ref[h] = fwd + bwd


def _ctx_kernel(n_tiles, n_out_blocks, ctx_ref, mod_ref, nw_ref, wkv_ref, kdft_ref, kdbt_ref,
                gf_ref, gb_ref, win_ref, wout_ref,
                scf_ref, scb_ref, wrest_ref, woutb_ref, h_ref, pa_ref, pb_ref):
    s = pl.program_id(0)
    keep = s < n_out_blocks
    refs = (ctx_ref, mod_ref, nw_ref, wkv_ref, kdft_ref, kdbt_ref, gf_ref, gb_ref,
            win_ref, wout_ref, keep, scf_ref, scb_ref, wrest_ref, woutb_ref, h_ref)
    _run_pipeline_step(s, n_tiles, 2, _ctx_step, refs, (pa_ref,), (pb_ref,))


def _ctx_step(stages, ctx_ref, mod_ref, nw_ref, wkv_ref, kdft_ref, kdbt_ref, gf_ref, gb_ref,
              win_ref, wout_ref, keep, scf_ref, scb_ref, wrest_ref, woutb_ref, h_ref,
              p_new_ref, p_old):
    project_stage, state_stage = stages
    wrest_ref[...] = win_ref[...].astype(BF16)
    woutb_ref[...] = jnp.where(keep, wout_ref[...], 0.0).astype(BF16)
    d = ctx_ref.shape[-1]
    n_chunks = ctx_ref.shape[1] // CHUNK
    ctx_row = mod_ref.shape[0] // 2
    mod = mod_ref[ctx_row:ctx_row + 1, :]

    def norm_rows(rows):
        h = _rms(ctx_ref[0, rows, :], nw_ref[...]) * (1.0 + mod[:, d:2 * d]) + mod[:, 0:d]
        h_ref[rows, :] = h.astype(BF16)

    def project(lo, hi):
        p_new_ref[:, lo:hi] = _dot(h_ref[...], wkv_ref[:, lo:hi])

    new_f, new_b = {}, {}

    def chunk(c):
        sl = slice(c * CHUNK, (c + 1) * CHUNK)
        kt = (p_old[sl, 0:QK_W] * (DK ** -0.5)).T
        v = p_old[sl, QK_W:].astype(BF16)
        new_f[c] = _kv_pairs((kt * kdft_ref[...]).astype(BF16), v)
        new_b[c] = _kv_pairs((kt * kdbt_ref[...]).astype(BF16), v)
        yield

    def combine():
        yield
        sf = jnp.zeros((QK_W, DV), F32)
        for c in range(n_chunks):
            sf = gf_ref[...] * sf + new_f[c]
        sb = jnp.zeros((QK_W, DV), F32)
        for c in range(n_chunks - 1, -1, -1):
            sb = gb_ref[...] * sb + new_b[c]
        scf_ref[0] = sf
        scb_ref[0] = sb

    big, tasks = [], []
    if project_stage:
        for c in range(n_chunks):
            norm_rows(slice(c * CHUNK, (c + 1) * CHUNK))
        big = [functools.partial(project, lo, hi) for lo, hi in _col_groups(p_new_ref.shape[1])]
    if state_stage:
        tasks = [(c, chunk(c), 1) for c in range(n_chunks)] + [(n_chunks - 1, combine(), 2)]
    _emit_pipelined(big, [], tasks)


def _kv_kernel(n_tiles, tiles_per_batch, x_hbm, modc_ref, nw_ref, wkv_ref, cos_ref, sin_ref,
               kdbt_ref, gb_ref, scb_ref, h_ref, kt_ref, v_ref, stb_ref, sb_ref, pa_ref, pb_ref,
               xbuf_ref, xsem_ref):
    s = pl.program_id(0)
    tile = xbuf_ref.shape[1]

    def x_copy(t):
        b = t // tiles_per_batch
        j = tiles_per_batch - 1 - lax.rem(t, tiles_per_batch)
        slot = lax.rem(t, X_RING)
        return pltpu.make_async_copy(x_hbm.at[b, pl.ds(j * tile, tile), :],
                                     xbuf_ref.at[slot], xsem_ref.at[slot])

    @pl.when(s == 0)
    def _prime():
        for t in range(min(X_RING - 1, n_tiles)):
            x_copy(t).start(priority=X_DMA_PRIORITY)

    @pl.when(s < n_tiles)
    def _arrived():
        x_copy(s).wait()

    @pl.when(s + X_RING - 1 < n_tiles)
    def _prefetch():
        x_copy(s + X_RING - 1).start(priority=X_DMA_PRIORITY)

    @pl.when((s >= 1) & (lax.rem(s - 1, tiles_per_batch) == 0))
    def _init():
        sb_ref[...] = scb_ref[0]

    batch = jnp.minimum(s, n_tiles - 1) // tiles_per_batch
    modc = modc_ref[pl.ds(batch, 1), :]
    xc_ref = xbuf_ref.at[lax.rem(s, X_RING)]
    finished = jnp.clip(s - 1, 0, n_tiles - 1)
    first_row = pl.multiple_of(
        (tiles_per_batch - 1 - lax.rem(finished, tiles_per_batch)) * tile, tile)
    cos_ref = cos_ref.at[pl.ds(first_row, tile), :]
    sin_ref = sin_ref.at[pl.ds(first_row, tile), :]
    refs = (xc_ref, modc, nw_ref, wkv_ref, cos_ref, sin_ref, kdbt_ref, gb_ref,
            h_ref, kt_ref, v_ref, stb_ref, sb_ref)
    _run_pipeline_step(s, n_tiles, 2, _kv_step, refs, (pa_ref,), (pb_ref,))


def _kv_step(stages, xc_ref, modc, nw_ref, wkv_ref, cos_ref, sin_ref, kdbt_ref, gb_ref,
             h_ref, kt_ref, v_ref, stb_ref, sb_ref, p_new_ref, p_old):
    project_stage, finish_stage = stages
    d = xc_ref.shape[-1]
    tile = xc_ref.shape[0]
    half = tile // 2
    row_halves = [slice(0, half), slice(half, tile)]
    quarter = [slice(i * half // 2, (i + 1) * half // 2) for i in range(4)]
    upper16 = (lax.broadcasted_iota(jnp.int32, (CHUNK, LANES), 1) & 16) != 0

    def norm_rows(rows):
        h = _rms(xc_ref[rows, :], nw_ref[...]) * (1.0 + modc[:, d:2 * d]) + modc[:, 0:d]
        h_ref[0, rows, :] = h.astype(BF16)

    def project(rows, lo, hi):
        p_new_ref[rows, lo:hi] = _dot(h_ref[0, rows, :], wkv_ref[:, lo:hi])

    def chunk(c):
        sl = slice(c * CHUNK, (c + 1) * CHUNK)
        k = [_rope(p_old[sl, p * LANES:(p + 1) * LANES], cos_ref[sl, :], sin_ref[sl, :], upper16)
             * (DK ** -0.5) for p in range(PAIRS)]
        v = p_old[sl, QK_W:].astype(BF16)
        v_ref[0, sl, :] = v
        kt = jnp.concatenate([kp.T for kp in k], axis=0)
        kt_ref[0, c] = kt.astype(BF16)
        new = _kv_pairs((kt * kdbt_ref[...]).astype(BF16), v)
        yield
        stb_ref[0, c] = sb_ref[...].astype(BF16)
        sb_ref[...] = gb_ref[...] * sb_ref[...] + new

    projections, late_norms, tasks = [], [], []
    if project_stage:
        norm_rows(quarter[0])
        norm_rows(quarter[1])
        projections = [functools.partial(project, row_halves[r], lo, hi)
                       for r in range(2) for lo, hi in _col_groups(p_new_ref.shape[1])]
        late_norms = [(functools.partial(norm_rows, q), None) for q in quarter[2:]]
    if finish_stage:
        tasks = [(tick, chunk(c), 2) for tick, c in enumerate(range(tile // CHUNK - 1, -1, -1))]
    _emit_pipelined(projections, late_norms, tasks)


def _main_kernel(n_tiles, tiles_per_batch, h_ref, x_ref, mod_ref, w_ref, cos_ref,
                 sin_ref, kt_ref, v_ref, stb_ref, scf_ref, dsum_ref, af_ref, ab_ref, kdft_ref,
                 gf_ref, gnw_ref, lnw_ref, ws_ref, bs_ref, wout_ref, fnw_ref, o_ref,
                 sf_ref, pa_ref, pb_ref, ya_ref, yb_ref):
    s = pl.program_id(0)

    @pl.when((s >= 1) & (lax.rem(s - 1, tiles_per_batch) == 0))
    def _init():
        sf_ref[...] = scf_ref[0]

    batch = jnp.clip(s - 2, 0, n_tiles - 1) // tiles_per_batch
    mod = mod_ref[pl.ds(batch, 1), :]
    refs = (h_ref, x_ref, mod, w_ref, cos_ref, sin_ref, kt_ref, v_ref,
            stb_ref, dsum_ref, af_ref, ab_ref, kdft_ref, gf_ref, gnw_ref, lnw_ref, ws_ref,
            bs_ref, wout_ref, fnw_ref, o_ref, sf_ref)
    _run_pipeline_step(s, n_tiles, 3, _main_step, refs, (pa_ref, ya_ref), (pb_ref, yb_ref))


def _main_step(stages, h_ref, x_ref, mod, w_ref, cos_ref, sin_ref, kt_ref, v_ref,
               stb_ref, dsum_ref, af_ref, ab_ref, kdft_ref, gf_ref, gnw_ref, lnw_ref, ws_ref,
               bs_ref, wout_ref, fnw_ref, o_ref, sf_ref, p_new_ref, y_ref, p_all, y_old):
    project_stage, mix_stage, out_stage = stages
    d = x_ref.shape[-1]
    tile = x_ref.shape[1]
    n_chunks = tile // CHUNK
    half = tile // 2
    row_halves = [slice(0, half), slice(half, tile)]
    c_gr = QK_W
    c_u = c_gr + RET_W
    c_vm = c_u + GM_W
    c_gm = c_vm + GM_W
    lane = lax.broadcasted_iota(jnp.int32, (CHUNK, LANES), 1)
    upper16 = (lane & 16) != 0

    operand = {}

    def loaded_once(key, load):
        if key not in operand:
            operand[key] = load()
        return operand[key]

    def project(r, lo, hi):
        rows = row_halves[r]
        lhs = loaded_once(("h", r), lambda: h_ref[0, rows, :])
        p_new_ref[rows, lo:hi] = _dot(lhs, w_ref[:, lo:hi])

    def retention(c, p):
        sl = slice(c * CHUNK, (c + 1) * CHUNK)
        ps = slice(p * LANES, (p + 1) * LANES)
        q2 = _rope(p_all[sl, ps], cos_ref[sl, :], sin_ref[sl, :], upper16)
        kt = kt_ref[0, c, ps, :]
        vp = v_ref[0, sl, p * 2 * DV:(p + 1) * 2 * DV]
        qa = q2 * af_ref[:, ps]
        qb = q2 * ab_ref[:, ps]
        scores = _dot(q2.astype(BF16), _per_head_columns(kt))
        qx = jnp.concatenate([qa, qb], axis=1).astype(BF16)
        kd = (kt.astype(F32) * kdft_ref[ps, :]).astype(BF16)
        yield
        state = jnp.concatenate([sf_ref[ps, :].astype(BF16), stb_ref[0, c, ps, :]], axis=0)
        from_state = _dot(qx, _per_head_columns(state))
        decay = jnp.concatenate([dsum_ref[2 * p + hh] for hh in range(2)], axis=1)
        both = _dot((scores * decay).astype(BF16), _block_diagonal(vp[:, 0:DV], vp[:, DV:])) + from_state
        o = [both[:, hh * DV:(hh + 1) * DV] for hh in range(2)]
        r = _dot(kd, vp)
        yield
        new = jnp.concatenate([r[0:DK, 0:DV], r[DK:2 * DK, DV:2 * DV]], axis=0)
        sf_ref[ps, :] = gf_ref[ps, :] * sf_ref[ps, :] + new
        for hh in range(2):
            head = 2 * p + hh
            mu = jnp.mean(o[hh], axis=-1, keepdims=True)
            var = jnp.mean(jnp.square(o[hh] - mu), axis=-1, keepdims=True)
            hs = slice(head * DV, (head + 1) * DV)
            y = (o[hh] - mu) * lax.rsqrt(var + EPS) * gnw_ref[:, hs]
            gate_cols = slice(c_gr + head * DV, c_gr + (head + 1) * DV)
            y_ref[sl, hs] = (y * _silu(p_all[sl, gate_cols])).astype(BF16)

    def gating(c):
        sl = slice(c * CHUNK, (c + 1) * CHUNK)
        vg = _gelu(p_all[sl, c_vm:c_gm])
        mu = jnp.mean(vg, axis=-1, keepdims=True)
        var = jnp.mean(jnp.square(vg - mu), axis=-1, keepdims=True)
        yield
        vn = ((vg - mu) * lax.rsqrt(var + EPS) * lnw_ref[...]).astype(BF16)
        s = [_dot(ws_ref[g], vn[:, g * DG:(g + 1) * DG]) for g in range(GROUPS)]
        yield
        for g in range(GROUPS):
            if g == GROUPS // 2:
                yield
            u = _gelu(p_all[sl, c_u + g * DG:c_u + (g + 1) * DG])
            gm = _silu(p_all[sl, c_gm + g * DG:c_gm + (g + 1) * DG])
            y_ref[sl, RET_W + g * DG:RET_W + (g + 1) * DG] = (u * (s[g] + bs_ref[g]) * gm).astype(BF16)

    out_groups = _col_groups(d)
    xn = {}

    def out_project(r, g):
        rows = row_halves[r]
        lo, hi = out_groups[g]
        lhs = loaded_once(("y", r), lambda: y_old[rows, :])
        out = _dot(lhs, wout_ref[:, lo:hi])
        xn[r, g] = x_ref[0, rows, lo:hi] + mod[:, 2 * d + lo:2 * d + hi] * out

    def finish(r, part):
        sub = slice(part * half // 2, (part + 1) * half // 2)
        rows = slice(row_halves[r].start + sub.start, row_halves[r].start + sub.stop)
        parts = [xn[r, g][sub, :] for g in range(len(out_groups))]
        ms = sum(jnp.sum(t * t, axis=-1, keepdims=True) for t in parts) * (1.0 / d)
        rs = lax.rsqrt(ms + EPS)
        for (lo, hi), t in zip(out_groups, parts):
            o_ref[0, rows, lo:hi] = t * rs * fnw_ref[:, lo:hi]

    tasks = []
    if mix_stage:
        for c in range(n_chunks):
            base = 3 * c
            tasks += [(base + min(p, 1), retention(c, p), 3) for p in range(PAIRS)]
            tasks.append((base + 1, gating(c), 4))
    big, relaxed = [], {}

    def place(tick, piece):
        while tick in relaxed:
            tick += 1
        relaxed[tick] = piece

    for r in range(2):
        outs = [functools.partial(out_project, r, g) for g in range(len(out_groups))] if out_stage else []
        projs = ([functools.partial(project, r, lo, hi) for lo, hi in _col_groups(p_new_ref.shape[1])]
                 if project_stage else [])
        for i in range(max(len(outs), len(projs))):
            big += outs[i:i + 1] + projs[i:i + 1]
            if i == len(outs) - 1:
                place(len(big) + 1, functools.partial(finish, r, 0))
                place(len(big) + 3, functools.partial(finish, r, 1))
    side = [(None, relaxed.get(tick)) for tick in range(max(relaxed, default=-1) + 1)]
    _emit_pipelined(big, side, tasks)


def _full(shape):
    return pl.BlockSpec(shape, lambda *_: (0,) * len(shape))


def kernel(x, c, ctx, c_ctx, w_ada, b_ada, norm_w, w_in, ret_decay_f, ret_decay_b, ret_gn_w,
           gmlp_ln_w, w_s, b_s, w_out, final_norm_w):
    bsz, seq, d = x.shape
    ctx_len = ctx.shape[1]
    depth = w_ada.shape[0]
    assert depth == 1 and d % LANES == 0
    assert seq % TILE_KV == 0 and seq % TILE_MAIN == 0 and ctx_len % CHUNK == 0
    assert seq % GRID_W == 0 and TILE_MAIN % (4 * CHUNK) == 0 and TILE_KV % (4 * CHUNK) == 0
    n_chunks = seq // CHUNK
    d3 = 3 * d
    n_kv = QK_W + RET_W
    n_rest = w_in.shape[2] - n_kv
    assert QK_W + n_kv == d
    assert _col_groups(n_rest)[0] == (0, QK_W)
    n_rest_blocks = n_rest // QK_W
    n_out_blocks = d // QK_W
    assert n_rest % QK_W == 0 and d % QK_W == 0
    assert max(n_rest_blocks, n_out_blocks + 1) <= bsz + 1

    quarter = DK // 4
    inv_freq = ROPE_BASE ** (-jnp.arange(quarter, dtype=F32) / quarter)
    invf_lane = jnp.tile(inv_freq, LANES // quarter)[None, :]
    nw = norm_w[0][None, :]
    rows2 = 2 * bsz

    n_col_blocks = d3 // d
    smem = pl.BlockSpec(memory_space=pltpu.SMEM)
    n_out_padded = d + QK_W
    (mod, cos_t, sin_t, dsum, af, ab, kdft, kdbt, gf, gb,
     w_kv, w_s_b, b_s_full) = pl.pallas_call(
        _prologue_kernel,
        grid=(n_col_blocks,),
        in_specs=[smem, smem, _full((bsz, d)), _full((1, d)),
                  pl.BlockSpec((d, d), lambda i: (0, i)),
                  pl.BlockSpec((1, d), lambda i: (0, i)),
                  _full((1, LANES)),
                  pl.BlockSpec((d, d), lambda i: (0, 0)),
                  _full((GROUPS, CHUNK, CHUNK)), _full((GROUPS, CHUNK))],
        out_specs=[pl.BlockSpec((rows2, d), lambda i: (0, i)),
                   _full((seq, LANES)), _full((seq, LANES)),
                   _full((HEADS, CHUNK, CHUNK)),
                   _full((CHUNK, QK_W)), _full((CHUNK, QK_W)),
                   _full((QK_W, CHUNK)), _full((QK_W, CHUNK)),
                   _full((QK_W, DV)), _full((QK_W, DV)),
                   _full((d, n_kv)),
                   _full((GROUPS, CHUNK, CHUNK)), _full((GROUPS, CHUNK, DG))],
        out_shape=[jax.ShapeDtypeStruct((rows2, d3), F32),
                   jax.ShapeDtypeStruct((seq, LANES), F32),
                   jax.ShapeDtypeStruct((seq, LANES), F32),
                   jax.ShapeDtypeStruct((HEADS, CHUNK, CHUNK), F32),
                   jax.ShapeDtypeStruct((CHUNK, QK_W), F32),
                   jax.ShapeDtypeStruct((CHUNK, QK_W), F32),
                   jax.ShapeDtypeStruct((QK_W, CHUNK), F32),
                   jax.ShapeDtypeStruct((QK_W, CHUNK), F32),
                   jax.ShapeDtypeStruct((QK_W, DV), F32),
                   jax.ShapeDtypeStruct((QK_W, DV), F32),
                   jax.ShapeDtypeStruct((d, n_kv), BF16),
                   jax.ShapeDtypeStruct((GROUPS, CHUNK, CHUNK), BF16),
                   jax.ShapeDtypeStruct((GROUPS, CHUNK, DG), F32)],
        compiler_params=pltpu.CompilerParams(dimension_semantics=("arbitrary",)),
        name="prologue",
    )(ret_decay_f[0], ret_decay_b[0], c, c_ctx[None, :], w_ada[0], b_ada[0][None, :], invf_lane,
      w_in[0], w_s[0], b_s[0])

    def rest_block(s):
        return jnp.minimum(s, n_rest_blocks - 1)

    def rest_source(s):
        return jnp.where(s == 0, 0, rest_block(s) + n_kv // QK_W)

    s_cf, s_cb, w_rest, w_out_b = pl.pallas_call(
        functools.partial(_ctx_kernel, bsz, n_out_blocks),
        grid=(bsz + 1,),
        in_specs=[pl.BlockSpec((1, ctx_len, d), lambda s: (jnp.minimum(s, bsz - 1), 0, 0)),
                  _full((rows2, d3)),
                  _full((1, d)), _full((d, n_kv)),
                  _full((QK_W, CHUNK)), _full((QK_W, CHUNK)),
                  _full((QK_W, DV)), _full((QK_W, DV)),
                  pl.BlockSpec((d, QK_W), lambda s: (0, rest_source(s))),
                  pl.BlockSpec((RET_W + GM_W, QK_W),
                               lambda s: (0, jnp.minimum(s, n_out_blocks - 1)))],
        out_specs=[pl.BlockSpec((1, QK_W, DV), lambda s: (jnp.maximum(s - 1, 0), 0, 0)),
                   pl.BlockSpec((1, QK_W, DV), lambda s: (jnp.maximum(s - 1, 0), 0, 0)),
                   pl.BlockSpec((d, QK_W), lambda s: (0, rest_block(s))),
                   pl.BlockSpec((RET_W + GM_W, QK_W),
                                lambda s: (0, jnp.minimum(s, n_out_blocks)))],
        out_shape=[jax.ShapeDtypeStruct((bsz, QK_W, DV), F32),
                   jax.ShapeDtypeStruct((bsz, QK_W, DV), F32),
                   jax.ShapeDtypeStruct((d, n_rest), BF16),
                   jax.ShapeDtypeStruct((RET_W + GM_W, n_out_padded), BF16)],
        scratch_shapes=[pltpu.VMEM((ctx_len, d), BF16),
                        pltpu.VMEM((ctx_len, n_kv), F32),
                        pltpu.VMEM((ctx_len, n_kv), F32)],
        compiler_params=pltpu.CompilerParams(dimension_semantics=("arbitrary",)),
        name="ctx_states",
    )(ctx, mod, nw, w_kv, kdft, kdbt, gf, gb, w_in[0], w_out[0])

    def tile_maps(n_tiles, per_batch, reverse):
        def where(t):
            j = t % per_batch
            return t // per_batch, (per_batch - 1 - j) if reverse else j
        return [lambda s, k=k: where(jnp.clip(s - k, 0, n_tiles - 1)) for k in range(3)]

    nt_kv = seq // TILE_KV
    cpt_kv = TILE_KV // CHUNK
    cur, prv, _ = tile_maps(bsz * nt_kv, nt_kv, reverse=True)
    h, kt, v, st_b = pl.pallas_call(
        functools.partial(_kv_kernel, bsz * nt_kv, nt_kv),
        grid=(bsz * nt_kv + 1,),
        in_specs=[pl.BlockSpec(memory_space=pl.ANY),
                  _full((rows2, d3)),
                  _full((1, d)), _full((d, n_kv)),
                  _full((seq, LANES)), _full((seq, LANES)),
                  _full((QK_W, CHUNK)), _full((QK_W, DV)),
                  pl.BlockSpec((1, QK_W, DV), lambda s: (prv(s)[0], 0, 0))],
        out_specs=[pl.BlockSpec((1, TILE_KV, d), lambda s: (*cur(s), 0)),
                   pl.BlockSpec((1, cpt_kv, QK_W, CHUNK), lambda s: (*prv(s), 0, 0)),
                   pl.BlockSpec((1, TILE_KV, RET_W), lambda s: (*prv(s), 0)),
                   pl.BlockSpec((1, cpt_kv, QK_W, DV), lambda s: (*prv(s), 0, 0))],
        out_shape=[jax.ShapeDtypeStruct((bsz, seq, d), BF16),
                   jax.ShapeDtypeStruct((bsz, n_chunks, QK_W, CHUNK), BF16),
                   jax.ShapeDtypeStruct((bsz, seq, RET_W), BF16),
                   jax.ShapeDtypeStruct((bsz, n_chunks, QK_W, DV), BF16)],
        scratch_shapes=[pltpu.VMEM((QK_W, DV), F32),
                        pltpu.VMEM((TILE_KV, n_kv), F32),
                        pltpu.VMEM((TILE_KV, n_kv), F32),
                        pltpu.VMEM((X_RING, TILE_KV, d), F32),
                        pltpu.SemaphoreType.DMA((X_RING,))],
        compiler_params=pltpu.CompilerParams(dimension_semantics=("arbitrary",)),
        name="kv_sweep",
    )(x, mod, nw, w_kv, cos_t, sin_t, kdbt, gb, s_cb)

    nt = seq // TILE_MAIN
    cpt = TILE_MAIN // CHUNK
    cur, prv, prv2 = tile_maps(bsz * nt, nt, reverse=False)
    out = pl.pallas_call(
        functools.partial(_main_kernel, bsz * nt, nt),
        grid=(bsz * nt + 2,),
        in_specs=[pl.BlockSpec((1, TILE_MAIN, d), lambda s: (*cur(s), 0)),
                  pl.BlockSpec((1, TILE_MAIN, d), lambda s: (*prv2(s), 0)),
                  _full((rows2, d3)),
                  _full((d, n_rest)),
                  pl.BlockSpec((TILE_MAIN, LANES), lambda s: (prv(s)[1], 0)),
                  pl.BlockSpec((TILE_MAIN, LANES), lambda s: (prv(s)[1], 0)),
                  pl.BlockSpec((1, cpt, QK_W, CHUNK), lambda s: (*prv(s), 0, 0)),
                  pl.BlockSpec((1, TILE_MAIN, RET_W), lambda s: (*prv(s), 0)),
                  pl.BlockSpec((1, cpt, QK_W, DV), lambda s: (*prv(s), 0, 0)),
                  pl.BlockSpec((1, QK_W, DV), lambda s: (prv(s)[0], 0, 0)),
                  _full((HEADS, CHUNK, CHUNK)),
                  _full((CHUNK, QK_W)), _full((CHUNK, QK_W)),
                  _full((QK_W, CHUNK)), _full((QK_W, DV)),
                  _full((1, RET_W)), _full((1, GM_W)),
                  _full((GROUPS, CHUNK, CHUNK)), _full((GROUPS, CHUNK, DG)),
                  _full((RET_W + GM_W, n_out_padded)), _full((1, d))],
        out_specs=pl.BlockSpec((1, TILE_MAIN, d), lambda s: (*prv2(s), 0)),
        out_shape=jax.ShapeDtypeStruct((bsz, seq, d), x.dtype),
        scratch_shapes=[pltpu.VMEM((QK_W, DV), F32),
                        pltpu.VMEM((TILE_MAIN, n_rest), F32),
                        pltpu.VMEM((TILE_MAIN, n_rest), F32),
                        pltpu.VMEM((TILE_MAIN, RET_W + GM_W), BF16),
                        pltpu.VMEM((TILE_MAIN, RET_W + GM_W), BF16)],
        compiler_params=pltpu.CompilerParams(dimension_semantics=("arbitrary",)),
        name="main_sweep",
    )(h, x, mod, w_rest, cos_t, sin_t, kt, v, st_b, s_cf, dsum, af, ab, kdft, gf,
      ret_gn_w[0][None, :], gmlp_ln_w[0][None, :], w_s_b, b_s_full, w_out_b,
      final_norm_w[None, :])
    return out
```

```python
import functools

import jax
import jax.numpy as jnp
from jax import lax
from jax.experimental import pallas as pl
from jax.experimental.pallas import tpu as pltpu

F32 = jnp.float32
BF16 = jnp.bfloat16

HEADS = 4
DK = 64
DV = 128
CHUNK = 128
GROUPS = 4
DG = 128
GRID_W = 64
ROPE_BASE = 10000.0
EPS = 1e-6
LANES = 128
PAIRS = HEADS * DK // LANES
RET_W = HEADS * DV
GM_W = GROUPS * DG
QK_W = HEADS * DK

DOT_COLS = 512
TILE_KV = 1024
X_RING = 3
TILE_MAIN = 512


def _rms(x, w):
    ms = jnp.mean(x * x, axis=-1, keepdims=True)
    return x * lax.rsqrt(ms + EPS) * w


def _silu(x):
    return x * jax.nn.sigmoid(x)


def _gelu(x):
    return 0.5 * x * (1.0 + lax.erf(x * (0.5 ** 0.5)))


_dot = functools.partial(jnp.dot, preferred_element_type=F32)


def _col_groups(n):
    first = n % DOT_COLS
    edges = ([0] if first == 0 else [0, first]) + list(range(first + DOT_COLS, n + 1, DOT_COLS))
    return list(zip(edges[:-1], edges[1:]))


def _rope(p, cos, sin_signed, upper16):
    partner = jnp.where(upper16, pltpu.roll(p, 16, 1), pltpu.roll(p, LANES - 16, 1))
    return p * cos + partner * sin_signed


def _per_head_columns(m):
    head = (lax.broadcasted_iota(jnp.int32, m.shape, 0) // DK) % 2
    return jnp.concatenate([jnp.where(head == hh, m, jnp.zeros_like(m)) for hh in range(2)], axis=1)


def _block_diagonal(a, b):
    return jnp.concatenate([jnp.concatenate([a, jnp.zeros_like(b)], axis=1),
                            jnp.concatenate([jnp.zeros_like(a), b], axis=1)], axis=0)


def _kv_pairs(kd, v):
    out = []
    for p in range(PAIRS):
        r = _dot(kd[p * LANES:(p + 1) * LANES, :], v[:, p * 2 * DV:(p + 1) * 2 * DV])
        out.append(r[0:DK, 0:DV])
        out.append(r[DK:2 * DK, DV:2 * DV])
    return jnp.concatenate(out, axis=0)


def _emit_pipelined(big_pieces, side_pieces, tasks):
    in_flight = []
    last_tick = max([start + parts for start, _, parts in tasks] +
                    [len(big_pieces), len(side_pieces)])
    for tick in range(last_tick):
        if tick < len(big_pieces):
            big_pieces[tick]()
        urgent, relaxed = side_pieces[tick] if tick < len(side_pieces) else (None, None)
        if urgent is not None:
            urgent()
        in_flight += [[gen, parts] for start, gen, parts in tasks if start == tick]
        for t in [t for t in in_flight if t[1] > 1] + [t for t in in_flight if t[1] == 1]:
            next(t[0], None)
            t[1] -= 1
        in_flight = [t for t in in_flight if t[1] > 0]
        if relaxed is not None:
            relaxed()
    assert not in_flight


def _run_pipeline_step(step, n_tiles, n_stages, body, refs, bufs_a, bufs_b):
    def call(s_static_parity, stages):
        new, old = (bufs_a, bufs_b) if s_static_parity == 0 else (bufs_b, bufs_a)
        body(stages, *refs, *new, *old)

    edge = n_stages - 1
    for s in list(range(edge)) + list(range(n_tiles, n_tiles + edge)):
        stages = tuple(0 <= s - j < n_tiles for j in range(n_stages))
        pl.when(step == s)(functools.partial(call, s % 2, stages))
    steady = (step >= edge) & (step < n_tiles)
    for parity in range(2):
        pl.when(steady & (lax.rem(step, 2) == parity))(
            functools.partial(call, parity, (True,) * n_stages))


def _prologue_kernel(decf_ref, decb_ref, c_ref, cctx_ref, w_ref, b_ref, invf_ref, wqkv_ref,
                     ws_ref, bs_ref,
                     mod_ref, cos_ref, sin_ref, dsum_ref, af_ref, ab_ref,
                     kdft_ref, kdbt_ref, gf_ref, gb_ref,
                     wkv_ref, wsb_ref, bsf_ref):
    i = pl.program_id(0)
    c = c_ref[...]
    cond = jnp.concatenate([c, jnp.broadcast_to(cctx_ref[...], c.shape)], axis=0)
    mod_ref[...] = _dot(_silu(cond), w_ref[...]) + b_ref[...]

    @pl.when(i == 0)
    def _small_weights():
        wkv_ref[...] = wqkv_ref[:, QK_W:].astype(BF16)
        wsb_ref[...] = ws_ref[...].astype(BF16)
        for g in range(GROUPS):
            bsf_ref[g] = jnp.broadcast_to(bs_ref[g:g + 1, :], (CHUNK, CHUNK)).T

    @pl.when(i == 0)
    def _tables():
        pos = lax.broadcasted_iota(jnp.int32, (GRID_W, LANES), 0).astype(F32)
        lane = lax.broadcasted_iota(jnp.int32, (GRID_W, LANES), 1)
        ang = pos * invf_ref[...]
        cr = jnp.cos(ang)
        sr = jnp.sin(ang)
        sr = jnp.where((lane & 16) == 0, -sr, sr)
        by_row = (lane & 32) == 0
        n_rows = cos_ref.shape[0] // GRID_W
        for r in range(n_rows):
            sl = slice(r * GRID_W, (r + 1) * GRID_W)
            cos_ref[sl, :] = jnp.where(by_row, jnp.broadcast_to(cr[r:r + 1, :], cr.shape), cr)
            sin_ref[sl, :] = jnp.where(by_row, jnp.broadcast_to(sr[r:r + 1, :], sr.shape), sr)

        def log_decay(head_idx, dec_ref):
            d = jnp.full(head_idx.shape, dec_ref[HEADS - 1], F32)
            for h in range(HEADS - 2, -1, -1):
                d = jnp.where(head_idx == h, dec_ref[h], d)
            return -jnp.exp(d)

        n = lax.broadcasted_iota(jnp.int32, (CHUNK, QK_W), 0).astype(F32)
        hl = lax.broadcasted_iota(jnp.int32, (CHUNK, QK_W), 1) // DK
        af_ref[...] = jnp.exp((n + 1.0) * log_decay(hl, decf_ref))
        ab_ref[...] = jnp.exp((CHUNK - n) * log_decay(hl, decb_ref))

        m = lax.broadcasted_iota(jnp.int32, (QK_W, CHUNK), 1).astype(F32)
        hr = lax.broadcasted_iota(jnp.int32, (QK_W, CHUNK), 0) // DK
        lgf = log_decay(hr, decf_ref)
        lgb = log_decay(hr, decb_ref)
        kdft_ref[...] = jnp.exp((CHUNK - 1.0 - m) * lgf)
        kdbt_ref[...] = jnp.exp(m * lgb)
        gf_ref[...] = jnp.exp(float(CHUNK) * lgf)
        gb_ref[...] = jnp.exp(float(CHUNK) * lgb)

        nn = lax.broadcasted_iota(jnp.int32, (CHUNK, CHUNK), 0).astype(F32)
        mm = lax.broadcasted_iota(jnp.int32, (CHUNK, CHUNK), 1).astype(F32)
        for h in range(HEADS):
            lf = -jnp.exp(jnp.full((CHUNK, CHUNK), decf_ref[h], F32))
            lb = -jnp.exp(jnp.full((CHUNK, CHUNK), decb_ref[h], F32))
            df = nn - mm
            db = mm - nn
            fwd = jnp.where(df >= 0, jnp.exp(jnp.maximum(df, 0.0) * lf), 0.0)
            bwd = jnp.where(db >= 0, jnp.exp(jnp.maximum(db, 0.0) * lb), 0.0)
            dsum_ref[h] = fwd + bwd


def _ctx_kernel(n_tiles, n_out_blocks, ctx_ref, mod_ref, nw_ref, wkv_ref, kdft_ref, kdbt_ref,
                gf_ref, gb_ref, win_ref, wout_ref,
                scf_ref, scb_ref, wrest_ref, woutb_ref, h_ref, pa_ref, pb_ref):
    s = pl.program_id(0)
    keep = s < n_out_blocks
    refs = (ctx_ref, mod_ref, nw_ref, wkv_ref, kdft_ref, kdbt_ref, gf_ref, gb_ref,
            win_ref, wout_ref, keep, scf_ref, scb_ref, wrest_ref, woutb_ref, h_ref)
    _run_pipeline_step(s, n_tiles, 2, _ctx_step, refs, (pa_ref,), (pb_ref,))


def _ctx_step(stages, ctx_ref, mod_ref, nw_ref, wkv_ref, kdft_ref, kdbt_ref, gf_ref, gb_ref,
              win_ref, wout_ref, keep, scf_ref, scb_ref, wrest_ref, woutb_ref, h_ref,
              p_new_ref, p_old):
    project_stage, state_stage = stages
    wrest_ref[...] = win_ref[...].astype(BF16)
    woutb_ref[...] = jnp.where(keep, wout_ref[...], 0.0).astype(BF16)
    d = ctx_ref.shape[-1]
    n_chunks = ctx_ref.shape[1] // CHUNK
    ctx_row = mod_ref.shape[0] // 2
    mod = mod_ref[ctx_row:ctx_row + 1, :]

    def norm_rows(rows):
        h = _rms(ctx_ref[0, rows, :], nw_ref[...]) * (1.0 + mod[:, d:2 * d]) + mod[:, 0:d]
        h_ref[rows, :] = h.astype(BF16)

    def project(lo, hi):
        p_new_ref[:, lo:hi] = _dot(h_ref[...], wkv_ref[:, lo:hi])

    new_f, new_b = {}, {}

    def chunk(c):
        sl = slice(c * CHUNK, (c + 1) * CHUNK)
        kt = (p_old[sl, 0:QK_W] * (DK ** -0.5)).T
        v = p_old[sl, QK_W:].astype(BF16)
        new_f[c] = _kv_pairs((kt * kdft_ref[...]).astype(BF16), v)
        new_b[c] = _kv_pairs((kt * kdbt_ref[...]).astype(BF16), v)
        yield

    def combine():
        yield
        sf = jnp.zeros((QK_W, DV), F32)
        for c in range(n_chunks):
            sf = gf_ref[...] * sf + new_f[c]
        sb = jnp.zeros((QK_W, DV), F32)
        for c in range(n_chunks - 1, -1, -1):
            sb = gb_ref[...] * sb + new_b[c]
        scf_ref[0] = sf
        scb_ref[0] = sb

    big, tasks = [], []
    if project_stage:
        for c in range(n_chunks):
            norm_rows(slice(c * CHUNK, (c + 1) * CHUNK))
        big = [functools.partial(project, lo, hi) for lo, hi in _col_groups(p_new_ref.shape[1])]
    if state_stage:
        tasks = [(c, chunk(c), 1) for c in range(n_chunks)] + [(n_chunks - 1, combine(), 2)]
    _emit_pipelined(big, [], tasks)


def _kv_kernel(n_tiles, tiles_per_batch, x_hbm, modc_ref, nw_ref, wkv_ref, cos_ref, sin_ref,
               kdbt_ref, gb_ref, scb_ref, h_ref, kt_ref, v_ref, stb_ref, sb_ref, pa_ref, pb_ref,
               xbuf_ref, xsem_ref):
    s = pl.program_id(0)
    tile = xbuf_ref.shape[1]

    def x_copy(t):
        b = t // tiles_per_batch
        j = tiles_per_batch - 1 - lax.rem(t, tiles_per_batch)
        slot = lax.rem(t, X_RING)
        return pltpu.make_async_copy(x_hbm.at[b, pl.ds(j * tile, tile), :],
                                     xbuf_ref.at[slot], xsem_ref.at[slot])

    @pl.when(s == 0)
    def _prime():
        for t in range(min(X_RING - 1, n_tiles)):
            x_copy(t).start()

    @pl.when(s < n_tiles)
    def _arrived():
        x_copy(s).wait()

    @pl.when(s + X_RING - 1 < n_tiles)
    def _prefetch():
        x_copy(s + X_RING - 1).start()

    @pl.when((s >= 1) & (lax.rem(s - 1, tiles_per_batch) == 0))
    def _init():
        sb_ref[...] = scb_ref[0]

    batch = jnp.minimum(s, n_tiles - 1) // tiles_per_batch
    modc = modc_ref[pl.ds(batch, 1), :]
    xc_ref = xbuf_ref.at[lax.rem(s, X_RING)]
    finished = jnp.clip(s - 1, 0, n_tiles - 1)
    first_row = pl.multiple_of(
        (tiles_per_batch - 1 - lax.rem(finished, tiles_per_batch)) * tile, tile)
    cos_ref = cos_ref.at[pl.ds(first_row, tile), :]
    sin_ref = sin_ref.at[pl.ds(first_row, tile), :]
    refs = (xc_ref, modc, nw_ref, wkv_ref, cos_ref, sin_ref, kdbt_ref, gb_ref,
            h_ref, kt_ref, v_ref, stb_ref, sb_ref)
    _run_pipeline_step(s, n_tiles, 2, _kv_step, refs, (pa_ref,), (pb_ref,))


def _kv_step(stages, xc_ref, modc, nw_ref, wkv_ref, cos_ref, sin_ref, kdbt_ref, gb_ref,
             h_ref, kt_ref, v_ref, stb_ref, sb_ref, p_new_ref, p_old):
    project_stage, finish_stage = stages
    d = xc_ref.shape[-1]
    tile = xc_ref.shape[0]
    half = tile // 2
    row_halves = [slice(0, half), slice(half, tile)]
    quarter = [slice(i * half // 2, (i + 1) * half // 2) for i in range(4)]
    upper16 = (lax.broadcasted_iota(jnp.int32, (CHUNK, LANES), 1) & 16) != 0

    def norm_rows(rows):
        h = _rms(xc_ref[rows, :], nw_ref[...]) * (1.0 + modc[:, d:2 * d]) + modc[:, 0:d]
        h_ref[0, rows, :] = h.astype(BF16)

    def project(rows, lo, hi):
        p_new_ref[rows, lo:hi] = _dot(h_ref[0, rows, :], wkv_ref[:, lo:hi])

    def chunk(c):
        sl = slice(c * CHUNK, (c + 1) * CHUNK)
        k = [_rope(p_old[sl, p * LANES:(p + 1) * LANES], cos_ref[sl, :], sin_ref[sl, :], upper16)
             * (DK ** -0.5) for p in range(PAIRS)]
        v = p_old[sl, QK_W:].astype(BF16)
        v_ref[0, sl, :] = v
        kt = jnp.concatenate([kp.T for kp in k], axis=0)
        kt_ref[0, c] = kt.astype(BF16)
        new = _kv_pairs((kt * kdbt_ref[...]).astype(BF16), v)
        yield
        stb_ref[0, c] = sb_ref[...].astype(BF16)
        sb_ref[...] = gb_ref[...] * sb_ref[...] + new

    projections, late_norms, tasks = [], [], []
    if project_stage:
        norm_rows(quarter[0])
        norm_rows(quarter[1])
        projections = [functools.partial(project, row_halves[r], lo, hi)
                       for r in range(2) for lo, hi in _col_groups(p_new_ref.shape[1])]
        late_norms = [(functools.partial(norm_rows, q), None) for q in quarter[2:]]
    if finish_stage:
        tasks = [(tick, chunk(c), 2) for tick, c in enumerate(range(tile // CHUNK - 1, -1, -1))]
    _emit_pipelined(projections, late_norms, tasks)


def _main_kernel(n_tiles, tiles_per_batch, h_ref, x_ref, mod_ref, w_ref, cos_ref,
                 sin_ref, kt_ref, v_ref, stb_ref, scf_ref, dsum_ref, af_ref, ab_ref, kdft_ref,
                 gf_ref, gnw_ref, lnw_ref, ws_ref, bs_ref, wout_ref, fnw_ref, o_ref,
                 sf_ref, pa_ref, pb_ref, ya_ref, yb_ref):
    s = pl.program_id(0)

    @pl.when((s >= 1) & (lax.rem(s - 1, tiles_per_batch) == 0))
    def _init():
        sf_ref[...] = scf_ref[0]

    batch = jnp.clip(s - 2, 0, n_tiles - 1) // tiles_per_batch
    mod = mod_ref[pl.ds(batch, 1), :]
    refs = (h_ref, x_ref, mod, w_ref, cos_ref, sin_ref, kt_ref, v_ref,
            stb_ref, dsum_ref, af_ref, ab_ref, kdft_ref, gf_ref, gnw_ref, lnw_ref, ws_ref,
            bs_ref, wout_ref, fnw_ref, o_ref, sf_ref)
    _run_pipeline_step(s, n_tiles, 3, _main_step, refs, (pa_ref, ya_ref), (pb_ref, yb_ref))


def _main_step(stages, h_ref, x_ref, mod, w_ref, cos_ref, sin_ref, kt_ref, v_ref,
               stb_ref, dsum_ref, af_ref, ab_ref, kdft_ref, gf_ref, gnw_ref, lnw_ref, ws_ref,
               bs_ref, wout_ref, fnw_ref, o_ref, sf_ref, p_new_ref, y_ref, p_all, y_old):
    project_stage, mix_stage, out_stage = stages
    d = x_ref.shape[-1]
    tile = x_ref.shape[1]
    n_chunks = tile // CHUNK
    half = tile // 2
    row_halves = [slice(0, half), slice(half, tile)]
    c_gr = QK_W
    c_u = c_gr + RET_W
    c_vm = c_u + GM_W
    c_gm = c_vm + GM_W
    lane = lax.broadcasted_iota(jnp.int32, (CHUNK, LANES), 1)
    upper16 = (lane & 16) != 0

    operand = {}

    def loaded_once(key, load):
        if key not in operand:
            operand[key] = load()
        return operand[key]

    def project(r, lo, hi):
        rows = row_halves[r]
        lhs = loaded_once(("h", r), lambda: h_ref[0, rows, :])
        p_new_ref[rows, lo:hi] = _dot(lhs, w_ref[:, lo:hi])

    def retention(c, p):
        sl = slice(c * CHUNK, (c + 1) * CHUNK)
        ps = slice(p * LANES, (p + 1) * LANES)
        q2 = _rope(p_all[sl, ps], cos_ref[sl, :], sin_ref[sl, :], upper16)
        kt = kt_ref[0, c, ps, :]
        vp = v_ref[0, sl, p * 2 * DV:(p + 1) * 2 * DV]
        qa = q2 * af_ref[:, ps]
        qb = q2 * ab_ref[:, ps]
        scores = _dot(q2.astype(BF16), _per_head_columns(kt))
        qx = jnp.concatenate([qa, qb], axis=1).astype(BF16)
        kd = (kt.astype(F32) * kdft_ref[ps, :]).astype(BF16)
        r = _dot(kd, vp)
        yield
        state = jnp.concatenate([sf_ref[ps, :].astype(BF16), stb_ref[0, c, ps, :]], axis=0)
        from_state = _dot(qx, _per_head_columns(state))
        decay = jnp.concatenate([dsum_ref[2 * p + hh] for hh in range(2)], axis=1)
        both = _dot((scores * decay).astype(BF16), _block_diagonal(vp[:, 0:DV], vp[:, DV:])) + from_state
        o = [both[:, hh * DV:(hh + 1) * DV] for hh in range(2)]
        yield
        new = jnp.concatenate([r[0:DK, 0:DV], r[DK:2 * DK, DV:2 * DV]], axis=0)
        sf_ref[ps, :] = gf_ref[ps, :] * sf_ref[ps, :] + new
        for hh in range(2):
            head = 2 * p + hh
            mu = jnp.mean(o[hh], axis=-1, keepdims=True)
            var = jnp.mean(jnp.square(o[hh] - mu), axis=-1, keepdims=True)
            hs = slice(head * DV, (head + 1) * DV)
            y = (o[hh] - mu) * lax.rsqrt(var + EPS) * gnw_ref[:, hs]
            gate_cols = slice(c_gr + head * DV, c_gr + (head + 1) * DV)
            y_ref[sl, hs] = (y * _silu(p_all[sl, gate_cols])).astype(BF16)

    def gating(c):
        sl = slice(c * CHUNK, (c + 1) * CHUNK)
        vg = _gelu(p_all[sl, c_vm:c_gm])
        mu = jnp.mean(vg, axis=-1, keepdims=True)
        var = jnp.mean(jnp.square(vg - mu), axis=-1, keepdims=True)
        yield
        vn = ((vg - mu) * lax.rsqrt(var + EPS) * lnw_ref[...]).astype(BF16)
        s = [_dot(ws_ref[g], vn[:, g * DG:(g + 1) * DG]) for g in range(GROUPS)]
        yield
        for g in range(GROUPS):
            if g == GROUPS // 2:
                yield
            u = _gelu(p_all[sl, c_u + g * DG:c_u + (g + 1) * DG])
            gm = _silu(p_all[sl, c_gm + g * DG:c_gm + (g + 1) * DG])
            y_ref[sl, RET_W + g * DG:RET_W + (g + 1) * DG] = (u * (s[g] + bs_ref[g]) * gm).astype(BF16)

    out_groups = _col_groups(d)
    xn = {}

    def out_project(r, g):
        rows = row_halves[r]
        lo, hi = out_groups[g]
        lhs = loaded_once(("y", r), lambda: y_old[rows, :])
        out = _dot(lhs, wout_ref[:, lo:hi])
        xn[r, g] = x_ref[0, rows, lo:hi] + mod[:, 2 * d + lo:2 * d + hi] * out

    def finish(r, part):
        sub = slice(part * half // 2, (part + 1) * half // 2)
        rows = slice(row_halves[r].start + sub.start, row_halves[r].start + sub.stop)
        parts = [xn[r, g][sub, :] for g in range(len(out_groups))]
        ms = sum(jnp.sum(t * t, axis=-1, keepdims=True) for t in parts) * (1.0 / d)
        rs = lax.rsqrt(ms + EPS)
        for (lo, hi), t in zip(out_groups, parts):
            o_ref[0, rows, lo:hi] = t * rs * fnw_ref[:, lo:hi]

    tasks = []
    if mix_stage:
        for c in range(n_chunks):
            base = 3 * c
            tasks += [(base + min(p, 1), retention(c, p), 3) for p in range(PAIRS)]
            tasks.append((base + 1, gating(c), 4))
    big, relaxed = [], {}

    def place(tick, piece):
        while tick in relaxed:
            tick += 1
        relaxed[tick] = piece

    for r in range(2):
        outs = [functools.partial(out_project, r, g) for g in range(len(out_groups))] if out_stage else []
        projs = ([functools.partial(project, r, lo, hi) for lo, hi in _col_groups(p_new_ref.shape[1])]
                 if project_stage else [])
        for i in range(max(len(outs), len(projs))):
            big += outs[i:i + 1] + projs[i:i + 1]
            if i == len(outs) - 1:
                place(len(big) + 1, functools.partial(finish, r, 0))
                place(len(big) + 3, functools.partial(finish, r, 1))
    side = [(None, relaxed.get(tick)) for tick in range(max(relaxed, default=-1) + 1)]
    _emit_pipelined(big, side, tasks)


def _full(shape):
    return pl.BlockSpec(shape, lambda *_: (0,) * len(shape))


def kernel(x, c, ctx, c_ctx, w_ada, b_ada, norm_w, w_in, ret_decay_f, ret_decay_b, ret_gn_w,
           gmlp_ln_w, w_s, b_s, w_out, final_norm_w):
    bsz, seq, d = x.shape
    ctx_len = ctx.shape[1]
    depth = w_ada.shape[0]
    assert depth == 1 and d % LANES == 0
    assert seq % TILE_KV == 0 and seq % TILE_MAIN == 0 and ctx_len % CHUNK == 0
    assert seq % GRID_W == 0 and TILE_MAIN % (4 * CHUNK) == 0 and TILE_KV % (4 * CHUNK) == 0
    n_chunks = seq // CHUNK
    d3 = 3 * d
    n_kv = QK_W + RET_W
    n_rest = w_in.shape[2] - n_kv
    assert QK_W + n_kv == d
    assert _col_groups(n_rest)[0] == (0, QK_W)
    n_rest_blocks = n_rest // QK_W
    n_out_blocks = d // QK_W
    assert n_rest % QK_W == 0 and d % QK_W == 0
    assert max(n_rest_blocks, n_out_blocks + 1) <= bsz + 1

    quarter = DK // 4
    inv_freq = ROPE_BASE ** (-jnp.arange(quarter, dtype=F32) / quarter)
    invf_lane = jnp.tile(inv_freq, LANES // quarter)[None, :]
    nw = norm_w[0][None, :]
    rows2 = 2 * bsz

    n_col_blocks = d3 // d
    smem = pl.BlockSpec(memory_space=pltpu.SMEM)
    n_out_padded = d + QK_W
    (mod, cos_t, sin_t, dsum, af, ab, kdft, kdbt, gf, gb,
     w_kv, w_s_b, b_s_full) = pl.pallas_call(
        _prologue_kernel,
        grid=(n_col_blocks,),
        in_specs=[smem, smem, _full((bsz, d)), _full((1, d)),
                  pl.BlockSpec((d, d), lambda i: (0, i)),
                  pl.BlockSpec((1, d), lambda i: (0, i)),
                  _full((1, LANES)),
                  pl.BlockSpec((d, d), lambda i: (0, 0)),
                  _full((GROUPS, CHUNK, CHUNK)), _full((GROUPS, CHUNK))],
        out_specs=[pl.BlockSpec((rows2, d), lambda i: (0, i)),
                   _full((seq, LANES)), _full((seq, LANES)),
                   _full((HEADS, CHUNK, CHUNK)),
                   _full((CHUNK, QK_W)), _full((CHUNK, QK_W)),
                   _full((QK_W, CHUNK)), _full((QK_W, CHUNK)),
                   _full((QK_W, DV)), _full((QK_W, DV)),
                   _full((d, n_kv)),
                   _full((GROUPS, CHUNK, CHUNK)), _full((GROUPS, CHUNK, DG))],
        out_shape=[jax.ShapeDtypeStruct((rows2, d3), F32),
                   jax.ShapeDtypeStruct((seq, LANES), F32),
                   jax.ShapeDtypeStruct((seq, LANES), F32),
                   jax.ShapeDtypeStruct((HEADS, CHUNK, CHUNK), F32),
                   jax.ShapeDtypeStruct((CHUNK, QK_W), F32),
                   jax.ShapeDtypeStruct((CHUNK, QK_W), F32),
                   jax.ShapeDtypeStruct((QK_W, CHUNK), F32),
                   jax.ShapeDtypeStruct((QK_W, CHUNK), F32),
                   jax.ShapeDtypeStruct((QK_W, DV), F32),
                   jax.ShapeDtypeStruct((QK_W, DV), F32),
                   jax.ShapeDtypeStruct((d, n_kv), BF16),
                   jax.ShapeDtypeStruct((GROUPS, CHUNK, CHUNK), BF16),
                   jax.ShapeDtypeStruct((GROUPS, CHUNK, DG), F32)],
        compiler_params=pltpu.CompilerParams(dimension_semantics=("arbitrary",)),
        name="prologue",
    )(ret_decay_f[0], ret_decay_b[0], c, c_ctx[None, :], w_ada[0], b_ada[0][None, :], invf_lane,
      w_in[0], w_s[0], b_s[0])

    def rest_block(s):
        return jnp.minimum(s, n_rest_blocks - 1)

    def rest_source(s):
        return jnp.where(s == 0, 0, rest_block(s) + n_kv // QK_W)

    s_cf, s_cb, w_rest, w_out_b = pl.pallas_call(
        functools.partial(_ctx_kernel, bsz, n_out_blocks),
        grid=(bsz + 1,),
        in_specs=[pl.BlockSpec((1, ctx_len, d), lambda s: (jnp.minimum(s, bsz - 1), 0, 0)),
                  _full((rows2, d3)),
                  _full((1, d)), _full((d, n_kv)),
                  _full((QK_W, CHUNK)), _full((QK_W, CHUNK)),
                  _full((QK_W, DV)), _full((QK_W, DV)),
                  pl.BlockSpec((d, QK_W), lambda s: (0, rest_source(s))),
                  pl.BlockSpec((RET_W + GM_W, QK_W),
                               lambda s: (0, jnp.minimum(s, n_out_blocks - 1)))],
        out_specs=[pl.BlockSpec((1, QK_W, DV), lambda s: (jnp.maximum(s - 1, 0), 0, 0)),
                   pl.BlockSpec((1, QK_W, DV), lambda s: (jnp.maximum(s - 1, 0), 0, 0)),
                   pl.BlockSpec((d, QK_W), lambda s: (0, rest_block(s))),
                   pl.BlockSpec((RET_W + GM_W, QK_W),
                                lambda s: (0, jnp.minimum(s, n_out_blocks)))],
        out_shape=[jax.ShapeDtypeStruct((bsz, QK_W, DV), F32),
                   jax.ShapeDtypeStruct((bsz, QK_W, DV), F32),
                   jax.ShapeDtypeStruct((d, n_rest), BF16),
                   jax.ShapeDtypeStruct((RET_W + GM_W, n_out_padded), BF16)],
        scratch_shapes=[pltpu.VMEM((ctx_len, d), BF16),
                        pltpu.VMEM((ctx_len, n_kv), F32),
                        pltpu.VMEM((ctx_len, n_kv), F32)],
        compiler_params=pltpu.CompilerParams(dimension_semantics=("arbitrary",)),
        name="ctx_states",
    )(ctx, mod, nw, w_kv, kdft, kdbt, gf, gb, w_in[0], w_out[0])

    def tile_maps(n_tiles, per_batch, reverse):
        def where(t):
            j = t % per_batch
            return t // per_batch, (per_batch - 1 - j) if reverse else j
        return [lambda s, k=k: where(jnp.clip(s - k, 0, n_tiles - 1)) for k in range(3)]

    nt_kv = seq // TILE_KV
    cpt_kv = TILE_KV // CHUNK
    cur, prv, _ = tile_maps(bsz * nt_kv, nt_kv, reverse=True)
    h, kt, v, st_b = pl.pallas_call(
        functools.partial(_kv_kernel, bsz * nt_kv, nt_kv),
        grid=(bsz * nt_kv + 1,),
        in_specs=[pl.BlockSpec(memory_space=pl.ANY),
                  _full((rows2, d3)),
                  _full((1, d)), _full((d, n_kv)),
                  _full((seq, LANES)), _full((seq, LANES)),
                  _full((QK_W, CHUNK)), _full((QK_W, DV)),
                  pl.BlockSpec((1, QK_W, DV), lambda s: (prv(s)[0], 0, 0))],
        out_specs=[pl.BlockSpec((1, TILE_KV, d), lambda s: (*cur(s), 0)),
                   pl.BlockSpec((1, cpt_kv, QK_W, CHUNK), lambda s: (*prv(s), 0, 0)),
                   pl.BlockSpec((1, TILE_KV, RET_W), lambda s: (*prv(s), 0)),
                   pl.BlockSpec((1, cpt_kv, QK_W, DV), lambda s: (*prv(s), 0, 0))],
        out_shape=[jax.ShapeDtypeStruct((bsz, seq, d), BF16),
                   jax.ShapeDtypeStruct((bsz, n_chunks, QK_W, CHUNK), BF16),
                   jax.ShapeDtypeStruct((bsz, seq, RET_W), BF16),
                   jax.ShapeDtypeStruct((bsz, n_chunks, QK_W, DV), BF16)],
        scratch_shapes=[pltpu.VMEM((QK_W, DV), F32),
                        pltpu.VMEM((TILE_KV, n_kv), F32),
                        pltpu.VMEM((TILE_KV, n_kv), F32),
                        pltpu.VMEM((X_RING, TILE_KV, d), F32),
                        pltpu.SemaphoreType.DMA((X_RING,))],
        compiler_params=pltpu.CompilerParams(dimension_semantics=("arbitrary",)),
        name="kv_sweep",
    )(x, mod, nw, w_kv, cos_t, sin_t, kdbt, gb, s_cb)

    nt = seq // TILE_MAIN
    cpt = TILE_MAIN // CHUNK
    cur, prv, prv2 = tile_maps(bsz * nt, nt, reverse=False)
    out = pl.pallas_call(
        functools.partial(_main_kernel, bsz * nt, nt),
        grid=(bsz * nt + 2,),
        in_specs=[pl.BlockSpec((1, TILE_MAIN, d), lambda s: (*cur(s), 0)),
                  pl.BlockSpec((1, TILE_MAIN, d), lambda s: (*prv2(s), 0)),
                  _full((rows2, d3)),
                  _full((d, n_rest)),
                  pl.BlockSpec((TILE_MAIN, LANES), lambda s: (prv(s)[1], 0)),
                  pl.BlockSpec((TILE_MAIN, LANES), lambda s: (prv(s)[1], 0)),
                  pl.BlockSpec((1, cpt, QK_W, CHUNK), lambda s: (*prv(s), 0, 0)),
                  pl.BlockSpec((1, TILE_MAIN, RET_W), lambda s: (*prv(s), 0)),
                  pl.BlockSpec((1, cpt, QK_W, DV), lambda s: (*prv(s), 0, 0)),
                  pl.BlockSpec((1, QK_W, DV), lambda s: (prv(s)[0], 0, 0)),
                  _full((HEADS, CHUNK, CHUNK)),
                  _full((CHUNK, QK_W)), _full((CHUNK, QK_W)),
                  _full((QK_W, CHUNK)), _full((QK_W, DV)),
                  _full((1, RET_W)), _full((1, GM_W)),
                  _full((GROUPS, CHUNK, CHUNK)), _full((GROUPS, CHUNK, DG)),
                  _full((RET_W + GM_W, n_out_padded)), _full((1, d))],
        out_specs=pl.BlockSpec((1, TILE_MAIN, d), lambda s: (*prv2(s), 0)),
        out_shape=jax.ShapeDtypeStruct((bsz, seq, d), x.dtype),
        scratch_shapes=[pltpu.VMEM((QK_W, DV), F32),
                        pltpu.VMEM((TILE_MAIN, n_rest), F32),
                        pltpu.VMEM((TILE_MAIN, n_rest), F32),
                        pltpu.VMEM((TILE_MAIN, RET_W + GM_W), BF16),
                        pltpu.VMEM((TILE_MAIN, RET_W + GM_W), BF16)],
        compiler_params=pltpu.CompilerParams(dimension_semantics=("arbitrary",)),
        name="main_sweep",
    )(h, x, mod, w_rest, cos_t, sin_t, kt, v, st_b, s_cf, dsum, af, ab, kdft, gf,
      ret_gn_w[0][None, :], gmlp_ln_w[0][None, :], w_s_b, b_s_full, w_out_b,
      final_norm_w[None, :])
    return out
```

```python
import functools

import jax
import jax.numpy as jnp
from jax import lax
from jax.experimental import pallas as pl
from jax.experimental.pallas import tpu as pltpu

F32 = jnp.float32
BF16 = jnp.bfloat16

HEADS = 4
DK = 64
DV = 128
CHUNK = 128
GROUPS = 4
DG = 128
GRID_W = 64
ROPE_BASE = 10000.0
EPS = 1e-6
LANES = 128
PAIRS = HEADS * DK // LANES
RET_W = HEADS * DV
GM_W = GROUPS * DG
QK_W = HEADS * DK

DOT_COLS = 512
TILE_KV = 1024
X_RING = 3
TILE_MAIN = 512


def _rms(x, w):
    ms = jnp.mean(x * x, axis=-1, keepdims=True)
    return x * lax.rsqrt(ms + EPS) * w


def _silu(x):
    return x * jax.nn.sigmoid(x)


def _gelu(x):
    return 0.5 * x * (1.0 + lax.erf(x * (0.5 ** 0.5)))


_dot = functools.partial(jnp.dot, preferred_element_type=F32)


def _col_groups(n):
    first = n % DOT_COLS
    edges = ([0] if first == 0 else [0, first]) + list(range(first + DOT_COLS, n + 1, DOT_COLS))
    return list(zip(edges[:-1], edges[1:]))


def _rope(p, cos, sin_signed, upper16):
    partner = jnp.where(upper16, pltpu.roll(p, 16, 1), pltpu.roll(p, LANES - 16, 1))
    return p * cos + partner * sin_signed


def _per_head_columns(m):
    head = (lax.broadcasted_iota(jnp.int32, m.shape, 0) // DK) % 2
    return jnp.concatenate([jnp.where(head == hh, m, jnp.zeros_like(m)) for hh in range(2)], axis=1)


def _block_diagonal(a, b):
    return jnp.concatenate([jnp.concatenate([a, jnp.zeros_like(b)], axis=1),
                            jnp.concatenate([jnp.zeros_like(a), b], axis=1)], axis=0)


def _kv_pairs(kd, v):
    out = []
    for p in range(PAIRS):
        r = _dot(kd[p * LANES:(p + 1) * LANES, :], v[:, p * 2 * DV:(p + 1) * 2 * DV])
        out.append(r[0:DK, 0:DV])
        out.append(r[DK:2 * DK, DV:2 * DV])
    return jnp.concatenate(out, axis=0)


def _emit_pipelined(big_pieces, side_pieces, tasks):
    in_flight = []
    last_tick = max([start + parts for start, _, parts in tasks] +
                    [len(big_pieces), len(side_pieces)])
    for tick in range(last_tick):
        if tick < len(big_pieces):
            big_pieces[tick]()
        urgent, relaxed = side_pieces[tick] if tick < len(side_pieces) else (None, None)
        if urgent is not None:
            urgent()
        in_flight += [[gen, parts] for start, gen, parts in tasks if start == tick]
        for t in [t for t in in_flight if t[1] > 1] + [t for t in in_flight if t[1] == 1]:
            next(t[0], None)
            t[1] -= 1
        in_flight = [t for t in in_flight if t[1] > 0]
        if relaxed is not None:
            relaxed()
    assert not in_flight


def _run_pipeline_step(step, n_tiles, n_stages, body, refs, bufs_a, bufs_b):
    def call(s_static_parity, stages):
        new, old = (bufs_a, bufs_b) if s_static_parity == 0 else (bufs_b, bufs_a)
        body(stages, *refs, *new, *old)

    edge = n_stages - 1
    for s in list(range(edge)) + list(range(n_tiles, n_tiles + edge)):
        stages = tuple(0 <= s - j < n_tiles for j in range(n_stages))
        pl.when(step == s)(functools.partial(call, s % 2, stages))
    steady = (step >= edge) & (step < n_tiles)
    for parity in range(2):
        pl.when(steady & (lax.rem(step, 2) == parity))(
            functools.partial(call, parity, (True,) * n_stages))


def _prologue_kernel(decf_ref, decb_ref, c_ref, cctx_ref, w_ref, b_ref, invf_ref, wqkv_ref,
                     ws_ref, bs_ref,
                     mod_ref, cos_ref, sin_ref, dsum_ref, af_ref, ab_ref,
                     kdft_ref, kdbt_ref, gf_ref, gb_ref,
                     wkv_ref, wsb_ref, bsf_ref):
    i = pl.program_id(0)
    c = c_ref[...]
    cond = jnp.concatenate([c, jnp.broadcast_to(cctx_ref[...], c.shape)], axis=0)
    mod_ref[...] = _dot(_silu(cond), w_ref[...]) + b_ref[...]

    @pl.when(i == 0)
    def _small_weights():
        wkv_ref[...] = wqkv_ref[:, QK_W:].astype(BF16)
        wsb_ref[...] = ws_ref[...].astype(BF16)
        for g in range(GROUPS):
            bsf_ref[g] = jnp.broadcast_to(bs_ref[g:g + 1, :], (CHUNK, CHUNK)).T

    @pl.when(i == 0)
    def _tables():
        pos = lax.broadcasted_iota(jnp.int32, (GRID_W, LANES), 0).astype(F32)
        lane = lax.broadcasted_iota(jnp.int32, (GRID_W, LANES), 1)
        ang = pos * invf_ref[...]
        cr = jnp.cos(ang)
        sr = jnp.sin(ang)
        sr = jnp.where((lane & 16) == 0, -sr, sr)
        by_row = (lane & 32) == 0
        n_rows = cos_ref.shape[0] // GRID_W
        for r in range(n_rows):
            sl = slice(r * GRID_W, (r + 1) * GRID_W)
            cos_ref[sl, :] = jnp.where(by_row, jnp.broadcast_to(cr[r:r + 1, :], cr.shape), cr)
            sin_ref[sl, :] = jnp.where(by_row, jnp.broadcast_to(sr[r:r + 1, :], sr.shape), sr)

        def log_decay(head_idx, dec_ref):
            d = jnp.full(head_idx.shape, dec_ref[HEADS - 1], F32)
            for h in range(HEADS - 2, -1, -1):
                d = jnp.where(head_idx == h, dec_ref[h], d)
            return -jnp.exp(d)

        n = lax.broadcasted_iota(jnp.int32, (CHUNK, QK_W), 0).astype(F32)
        hl = lax.broadcasted_iota(jnp.int32, (CHUNK, QK_W), 1) // DK
        af_ref[...] = jnp.exp((n + 1.0) * log_decay(hl, decf_ref))
        ab_ref[...] = jnp.exp((CHUNK - n) * log_decay(hl, decb_ref))

        m = lax.broadcasted_iota(jnp.int32, (QK_W, CHUNK), 1).astype(F32)
        hr = lax.broadcasted_iota(jnp.int32, (QK_W, CHUNK), 0) // DK
        lgf = log_decay(hr, decf_ref)
        lgb = log_decay(hr, decb_ref)
        kdft_ref[...] = jnp.exp((CHUNK - 1.0 - m) * lgf)
        kdbt_ref[...] = jnp.exp(m * lgb)
        gf_ref[...] = jnp.exp(float(CHUNK) * lgf)
        gb_ref[...] = jnp.exp(float(CHUNK) * lgb)

        nn = lax.broadcasted_iota(jnp.int32, (CHUNK, CHUNK), 0).astype(F32)
        mm = lax.broadcasted_iota(jnp.int32, (CHUNK, CHUNK), 1).astype(F32)
        for h in range(HEADS):
            lf = -jnp.exp(jnp.full((CHUNK, CHUNK), decf_ref[h], F32))
            lb = -jnp.exp(jnp.full((CHUNK, CHUNK), decb_ref[h], F32))
            df = nn - mm
            db = mm - nn
            fwd = jnp.where(df >= 0, jnp.exp(jnp.maximum(df, 0.0) * lf), 0.0)
            bwd = jnp.where(db >= 0, jnp.exp(jnp.maximum(db, 0.0) * lb), 0.0)
            dsum_ref[h] = fwd + bwd


def _ctx_kernel(n_tiles, n_out_blocks, ctx_ref, mod_ref, nw_ref, wkv_ref, kdft_ref, kdbt_ref,
                gf_ref, gb_ref, win_ref, wout_ref,
                scf_ref, scb_ref, wrest_ref, woutb_ref, h_ref, pa_ref, pb_ref):
    s = pl.program_id(0)
    keep = s < n_out_blocks
    refs = (ctx_ref, mod_ref, nw_ref, wkv_ref, kdft_ref, kdbt_ref, gf_ref, gb_ref,
            win_ref, wout_ref, keep, scf_ref, scb_ref, wrest_ref, woutb_ref, h_ref)
    _run_pipeline_step(s, n_tiles, 2, _ctx_step, refs, (pa_ref,), (pb_ref,))


def _ctx_step(stages, ctx_ref, mod_ref, nw_ref, wkv_ref, kdft_ref, kdbt_ref, gf_ref, gb_ref,
              win_ref, wout_ref, keep, scf_ref, scb_ref, wrest_ref, woutb_ref, h_ref,
              p_new_ref, p_old):
    project_stage, state_stage = stages
    wrest_ref[...] = win_ref[...].astype(BF16)
    woutb_ref[...] = jnp.where(keep, wout_ref[...], 0.0).astype(BF16)
    d = ctx_ref.shape[-1]
    n_chunks = ctx_ref.shape[1] // CHUNK
    ctx_row = mod_ref.shape[0] // 2
    mod = mod_ref[ctx_row:ctx_row + 1, :]

    def norm_rows(rows):
        h = _rms(ctx_ref[0, rows, :], nw_ref[...]) * (1.0 + mod[:, d:2 * d]) + mod[:, 0:d]
        h_ref[rows, :] = h.astype(BF16)

    def project(lo, hi):
        p_new_ref[:, lo:hi] = _dot(h_ref[...], wkv_ref[:, lo:hi])

    new_f, new_b = {}, {}

    def chunk(c):
        sl = slice(c * CHUNK, (c + 1) * CHUNK)
        kt = (p_old[sl, 0:QK_W] * (DK ** -0.5)).T
        v = p_old[sl, QK_W:].astype(BF16)
        new_f[c] = _kv_pairs((kt * kdft_ref[...]).astype(BF16), v)
        new_b[c] = _kv_pairs((kt * kdbt_ref[...]).astype(BF16), v)
        yield

    def combine():
        yield
        sf = jnp.zeros((QK_W, DV), F32)
        for c in range(n_chunks):
            sf = gf_ref[...] * sf + new_f[c]
        sb = jnp.zeros((QK_W, DV), F32)
        for c in range(n_chunks - 1, -1, -1):
            sb = gb_ref[...] * sb + new_b[c]
        scf_ref[0] = sf
        scb_ref[0] = sb

    big, tasks = [], []
    if project_stage:
        for c in range(n_chunks):
            norm_rows(slice(c * CHUNK, (c + 1) * CHUNK))
        big = [functools.partial(project, lo, hi) for lo, hi in _col_groups(p_new_ref.shape[1])]
    if state_stage:
        tasks = [(c, chunk(c), 1) for c in range(n_chunks)] + [(n_chunks - 1, combine(), 2)]
    _emit_pipelined(big, [], tasks)


def _kv_kernel(n_tiles, tiles_per_batch, x_hbm, modc_ref, nw_ref, wkv_ref, cos_ref, sin_ref,
               kdbt_ref, gb_ref, scb_ref, h_ref, kt_ref, v_ref, stb_ref, sb_ref, pa_ref, pb_ref,
               xbuf_ref, xsem_ref):
    s = pl.program_id(0)
    tile = xbuf_ref.shape[1]

    def x_copy(t):
        b = t // tiles_per_batch
        j = tiles_per_batch - 1 - lax.rem(t, tiles_per_batch)
        slot = lax.rem(t, X_RING)
        return pltpu.make_async_copy(x_hbm.at[b, pl.ds(j * tile, tile), :],
                                     xbuf_ref.at[slot], xsem_ref.at[slot])

    @pl.when(s == 0)
    def _prime():
        for t in range(min(X_RING - 1, n_tiles)):
            x_copy(t).start()

    @pl.when(s < n_tiles)
    def _arrived():
        x_copy(s).wait()

    @pl.when(s + X_RING - 1 < n_tiles)
    def _prefetch():
        x_copy(s + X_RING - 1).start()

    @pl.when((s >= 1) & (lax.rem(s - 1, tiles_per_batch) == 0))
    def _init():
        sb_ref[...] = scb_ref[0]

    batch = jnp.minimum(s, n_tiles - 1) // tiles_per_batch
    modc = modc_ref[pl.ds(batch, 1), :]
    xc_ref = xbuf_ref.at[lax.rem(s, X_RING)]
    finished = jnp.clip(s - 1, 0, n_tiles - 1)
    first_row = pl.multiple_of(
        (tiles_per_batch - 1 - lax.rem(finished, tiles_per_batch)) * tile, tile)
    cos_ref = cos_ref.at[pl.ds(first_row, tile), :]
    sin_ref = sin_ref.at[pl.ds(first_row, tile), :]
    refs = (xc_ref, modc, nw_ref, wkv_ref, cos_ref, sin_ref, kdbt_ref, gb_ref,
            h_ref, kt_ref, v_ref, stb_ref, sb_ref)
    _run_pipeline_step(s, n_tiles, 2, _kv_step, refs, (pa_ref,), (pb_ref,))


def _kv_step(stages, xc_ref, modc, nw_ref, wkv_ref, cos_ref, sin_ref, kdbt_ref, gb_ref,
             h_ref, kt_ref, v_ref, stb_ref, sb_ref, p_new_ref, p_old):
    project_stage, finish_stage = stages
    d = xc_ref.shape[-1]
    tile = xc_ref.shape[0]
    half = tile // 2
    row_halves = [slice(0, half), slice(half, tile)]
    quarter = [slice(i * half // 2, (i + 1) * half // 2) for i in range(4)]
    upper16 = (lax.broadcasted_iota(jnp.int32, (CHUNK, LANES), 1) & 16) != 0

    def norm_rows(rows):
        h = _rms(xc_ref[rows, :], nw_ref[...]) * (1.0 + modc[:, d:2 * d]) + modc[:, 0:d]
        h_ref[0, rows, :] = h.astype(BF16)

    def project(rows, lo, hi):
        p_new_ref[rows, lo:hi] = _dot(h_ref[0, rows, :], wkv_ref[:, lo:hi])

    def chunk(c):
        sl = slice(c * CHUNK, (c + 1) * CHUNK)
        k = [_rope(p_old[sl, p * LANES:(p + 1) * LANES], cos_ref[sl, :], sin_ref[sl, :], upper16)
             * (DK ** -0.5) for p in range(PAIRS)]
        v = p_old[sl, QK_W:].astype(BF16)
        v_ref[0, sl, :] = v
        kt = jnp.concatenate([kp.T for kp in k], axis=0)
        kt_ref[0, c] = kt.astype(BF16)
        new = _kv_pairs((kt * kdbt_ref[...]).astype(BF16), v)
        yield
        stb_ref[0, c] = sb_ref[...].astype(BF16)
        sb_ref[...] = gb_ref[...] * sb_ref[...] + new

    projections, late_norms, tasks = [], [], []
    if project_stage:
        norm_rows(quarter[0])
        norm_rows(quarter[1])
        projections = [functools.partial(project, row_halves[r], lo, hi)
                       for r in range(2) for lo, hi in _col_groups(p_new_ref.shape[1])]
        late_norms = [(functools.partial(norm_rows, q), None) for q in quarter[2:]]
    if finish_stage:
        tasks = [(tick, chunk(c), 2) for tick, c in enumerate(range(tile // CHUNK - 1, -1, -1))]
    _emit_pipelined(projections, late_norms, tasks)


def _main_kernel(n_tiles, tiles_per_batch, h_ref, x_ref, mod_ref, w_ref, cos_ref,
                 sin_ref, kt_ref, v_ref, stb_ref, scf_ref, dsum_ref, af_ref, ab_ref, kdft_ref,
                 gf_ref, gnw_ref, lnw_ref, ws_ref, bs_ref, wout_ref, fnw_ref, o_ref,
                 sf_ref, pa_ref, pb_ref, ya_ref, yb_ref):
    s = pl.program_id(0)

    @pl.when((s >= 1) & (lax.rem(s - 1, tiles_per_batch) == 0))
    def _init():
        sf_ref[...] = scf_ref[0]

    batch = jnp.clip(s - 2, 0, n_tiles - 1) // tiles_per_batch
    mod = mod_ref[pl.ds(batch, 1), :]
    refs = (h_ref, x_ref, mod, w_ref, cos_ref, sin_ref, kt_ref, v_ref,
            stb_ref, dsum_ref, af_ref, ab_ref, kdft_ref, gf_ref, gnw_ref, lnw_ref, ws_ref,
            bs_ref, wout_ref, fnw_ref, o_ref, sf_ref)
    _run_pipeline_step(s, n_tiles, 3, _main_step, refs, (pa_ref, ya_ref), (pb_ref, yb_ref))


def _main_step(stages, h_ref, x_ref, mod, w_ref, cos_ref, sin_ref, kt_ref, v_ref,
               stb_ref, dsum_ref, af_ref, ab_ref, kdft_ref, gf_ref, gnw_ref, lnw_ref, ws_ref,
               bs_ref, wout_ref, fnw_ref, o_ref, sf_ref, p_new_ref, y_ref, p_all, y_old):
    project_stage, mix_stage, out_stage = stages
    d = x_ref.shape[-1]
    tile = x_ref.shape[1]
    n_chunks = tile // CHUNK
    half = tile // 2
    row_halves = [slice(0, half), slice(half, tile)]
    c_gr = QK_W
    c_u = c_gr + RET_W
    c_vm = c_u + GM_W
    c_gm = c_vm + GM_W
    lane = lax.broadcasted_iota(jnp.int32, (CHUNK, LANES), 1)
    upper16 = (lane & 16) != 0

    operand = {}

    def loaded_once(key, load):
        if key not in operand:
            operand[key] = load()
        return operand[key]

    def project(r, lo, hi):
        rows = row_halves[r]
        lhs = loaded_once(("h", r), lambda: h_ref[0, rows, :])
        p_new_ref[rows, lo:hi] = _dot(lhs, w_ref[:, lo:hi])

    def retention(c, p):
        sl = slice(c * CHUNK, (c + 1) * CHUNK)
        ps = slice(p * LANES, (p + 1) * LANES)
        q2 = _rope(p_all[sl, ps], cos_ref[sl, :], sin_ref[sl, :], upper16)
        kt = kt_ref[0, c, ps, :]
        vp = v_ref[0, sl, p * 2 * DV:(p + 1) * 2 * DV]
        qa = q2 * af_ref[:, ps]
        qb = q2 * ab_ref[:, ps]
        scores = _dot(q2.astype(BF16), _per_head_columns(kt))
        qx = jnp.concatenate([qa, qb], axis=1).astype(BF16)
        kd = (kt.astype(F32) * kdft_ref[ps, :]).astype(BF16)
        r = _dot(kd, vp)
        state = jnp.concatenate([sf_ref[ps, :].astype(BF16), stb_ref[0, c, ps, :]], axis=0)
        from_state = _dot(qx, _per_head_columns(state))
        yield
        decay = jnp.concatenate([dsum_ref[2 * p + hh] for hh in range(2)], axis=1)
        both = _dot((scores * decay).astype(BF16), _block_diagonal(vp[:, 0:DV], vp[:, DV:])) + from_state
        o = [both[:, hh * DV:(hh + 1) * DV] for hh in range(2)]
        yield
        new = jnp.concatenate([r[0:DK, 0:DV], r[DK:2 * DK, DV:2 * DV]], axis=0)
        sf_ref[ps, :] = gf_ref[ps, :] * sf_ref[ps, :] + new
        for hh in range(2):
            head = 2 * p + hh
            mu = jnp.mean(o[hh], axis=-1, keepdims=True)
            var = jnp.mean(jnp.square(o[hh] - mu), axis=-1, keepdims=True)
            hs = slice(head * DV, (head + 1) * DV)
            y = (o[hh] - mu) * lax.rsqrt(var + EPS) * gnw_ref[:, hs]
            gate_cols = slice(c_gr + head * DV, c_gr + (head + 1) * DV)
            y_ref[sl, hs] = (y * _silu(p_all[sl, gate_cols])).astype(BF16)

    def gating(c):
        sl = slice(c * CHUNK, (c + 1) * CHUNK)
        vg = _gelu(p_all[sl, c_vm:c_gm])
        mu = jnp.mean(vg, axis=-1, keepdims=True)
        var = jnp.mean(jnp.square(vg - mu), axis=-1, keepdims=True)
        yield
        vn = ((vg - mu) * lax.rsqrt(var + EPS) * lnw_ref[...]).astype(BF16)
        s = [_dot(ws_ref[g], vn[:, g * DG:(g + 1) * DG]) for g in range(GROUPS)]
        yield
        for g in range(GROUPS):
            if g == GROUPS // 2:
                yield
            u = _gelu(p_all[sl, c_u + g * DG:c_u + (g + 1) * DG])
            gm = _silu(p_all[sl, c_gm + g * DG:c_gm + (g + 1) * DG])
            y_ref[sl, RET_W + g * DG:RET_W + (g + 1) * DG] = (u * (s[g] + bs_ref[g]) * gm).astype(BF16)

    out_groups = _col_groups(d)
    xn = {}

    def out_project(r, g):
        rows = row_halves[r]
        lo, hi = out_groups[g]
        lhs = loaded_once(("y", r), lambda: y_old[rows, :])
        out = _dot(lhs, wout_ref[:, lo:hi])
        xn[r, g] = x_ref[0, rows, lo:hi] + mod[:, 2 * d + lo:2 * d + hi] * out

    def finish(r, part):
        sub = slice(part * half // 2, (part + 1) * half // 2)
        rows = slice(row_halves[r].start + sub.start, row_halves[r].start + sub.stop)
        parts = [xn[r, g][sub, :] for g in range(len(out_groups))]
        ms = sum(jnp.sum(t * t, axis=-1, keepdims=True) for t in parts) * (1.0 / d)
        rs = lax.rsqrt(ms + EPS)
        for (lo, hi), t in zip(out_groups, parts):
            o_ref[0, rows, lo:hi] = t * rs * fnw_ref[:, lo:hi]

    tasks = []
    if mix_stage:
        for c in range(n_chunks):
            base = 3 * c
            tasks += [(base + min(p, 1), retention(c, p), 3) for p in range(PAIRS)]
            tasks.append((base + 1, gating(c), 4))
    big, relaxed = [], {}

    def place(tick, piece):
        while tick in relaxed:
            tick += 1
        relaxed[tick] = piece

    for r in range(2):
        outs = [functools.partial(out_project, r, g) for g in range(len(out_groups))] if out_stage else []
        projs = ([functools.partial(project, r, lo, hi) for lo, hi in _col_groups(p_new_ref.shape[1])]
                 if project_stage else [])
        for i in range(max(len(outs), len(projs))):
            big += outs[i:i + 1] + projs[i:i + 1]
            if i == len(outs) - 1:
                place(len(big) + 1, functools.partial(finish, r, 0))
                place(len(big) + 3, functools.partial(finish, r, 1))
    side = [(None, relaxed.get(tick)) for tick in range(max(relaxed, default=-1) + 1)]
    _emit_pipelined(big, side, tasks)


def _full(shape):
    return pl.BlockSpec(shape, lambda *_: (0,) * len(shape))


def kernel(x, c, ctx, c_ctx, w_ada, b_ada, norm_w, w_in, ret_decay_f, ret_decay_b, ret_gn_w,
           gmlp_ln_w, w_s, b_s, w_out, final_norm_w):
    bsz, seq, d = x.shape
    ctx_len = ctx.shape[1]
    depth = w_ada.shape[0]
    assert depth == 1 and d % LANES == 0
    assert seq % TILE_KV == 0 and seq % TILE_MAIN == 0 and ctx_len % CHUNK == 0
    assert seq % GRID_W == 0 and TILE_MAIN % (4 * CHUNK) == 0 and TILE_KV % (4 * CHUNK) == 0
    n_chunks = seq // CHUNK
    d3 = 3 * d
    n_kv = QK_W + RET_W
    n_rest = w_in.shape[2] - n_kv
    assert QK_W + n_kv == d
    assert _col_groups(n_rest)[0] == (0, QK_W)
    n_rest_blocks = n_rest // QK_W
    n_out_blocks = d // QK_W
    assert n_rest % QK_W == 0 and d % QK_W == 0
    assert max(n_rest_blocks, n_out_blocks + 1) <= bsz + 1

    quarter = DK // 4
    inv_freq = ROPE_BASE ** (-jnp.arange(quarter, dtype=F32) / quarter)
    invf_lane = jnp.tile(inv_freq, LANES // quarter)[None, :]
    nw = norm_w[0][None, :]
    rows2 = 2 * bsz

    n_col_blocks = d3 // d
    smem = pl.BlockSpec(memory_space=pltpu.SMEM)
    n_out_padded = d + QK_W
    (mod, cos_t, sin_t, dsum, af, ab, kdft, kdbt, gf, gb,
     w_kv, w_s_b, b_s_full) = pl.pallas_call(
        _prologue_kernel,
        grid=(n_col_blocks,),
        in_specs=[smem, smem, _full((bsz, d)), _full((1, d)),
                  pl.BlockSpec((d, d), lambda i: (0, i)),
                  pl.BlockSpec((1, d), lambda i: (0, i)),
                  _full((1, LANES)),
                  pl.BlockSpec((d, d), lambda i: (0, 0)),
                  _full((GROUPS, CHUNK, CHUNK)), _full((GROUPS, CHUNK))],
        out_specs=[pl.BlockSpec((rows2, d), lambda i: (0, i)),
                   _full((seq, LANES)), _full((seq, LANES)),
                   _full((HEADS, CHUNK, CHUNK)),
                   _full((CHUNK, QK_W)), _full((CHUNK, QK_W)),
                   _full((QK_W, CHUNK)), _full((QK_W, CHUNK)),
                   _full((QK_W, DV)), _full((QK_W, DV)),
                   _full((d, n_kv)),
                   _full((GROUPS, CHUNK, CHUNK)), _full((GROUPS, CHUNK, DG))],
        out_shape=[jax.ShapeDtypeStruct((rows2, d3), F32),
                   jax.ShapeDtypeStruct((seq, LANES), F32),
                   jax.ShapeDtypeStruct((seq, LANES), F32),
                   jax.ShapeDtypeStruct((HEADS, CHUNK, CHUNK), F32),
                   jax.ShapeDtypeStruct((CHUNK, QK_W), F32),
                   jax.ShapeDtypeStruct((CHUNK, QK_W), F32),
                   jax.ShapeDtypeStruct((QK_W, CHUNK), F32),
                   jax.ShapeDtypeStruct((QK_W, CHUNK), F32),
                   jax.ShapeDtypeStruct((QK_W, DV), F32),
                   jax.ShapeDtypeStruct((QK_W, DV), F32),
                   jax.ShapeDtypeStruct((d, n_kv), BF16),
                   jax.ShapeDtypeStruct((GROUPS, CHUNK, CHUNK), BF16),
                   jax.ShapeDtypeStruct((GROUPS, CHUNK, DG), F32)],
        compiler_params=pltpu.CompilerParams(dimension_semantics=("arbitrary",)),
        name="prologue",
    )(ret_decay_f[0], ret_decay_b[0], c, c_ctx[None, :], w_ada[0], b_ada[0][None, :], invf_lane,
      w_in[0], w_s[0], b_s[0])

    def rest_block(s):
        return jnp.minimum(s, n_rest_blocks - 1)

    def rest_source(s):
        return jnp.where(s == 0, 0, rest_block(s) + n_kv // QK_W)

    s_cf, s_cb, w_rest, w_out_b = pl.pallas_call(
        functools.partial(_ctx_kernel, bsz, n_out_blocks),
        grid=(bsz + 1,),
        in_specs=[pl.BlockSpec((1, ctx_len, d), lambda s: (jnp.minimum(s, bsz - 1), 0, 0)),
                  _full((rows2, d3)),
                  _full((1, d)), _full((d, n_kv)),
                  _full((QK_W, CHUNK)), _full((QK_W, CHUNK)),
                  _full((QK_W, DV)), _full((QK_W, DV)),
                  pl.BlockSpec((d, QK_W), lambda s: (0, rest_source(s))),
                  pl.BlockSpec((RET_W + GM_W, QK_W),
                               lambda s: (0, jnp.minimum(s, n_out_blocks - 1)))],
        out_specs=[pl.BlockSpec((1, QK_W, DV), lambda s: (jnp.maximum(s - 1, 0), 0, 0)),
                   pl.BlockSpec((1, QK_W, DV), lambda s: (jnp.maximum(s - 1, 0), 0, 0)),
                   pl.BlockSpec((d, QK_W), lambda s: (0, rest_block(s))),
                   pl.BlockSpec((RET_W + GM_W, QK_W),
                                lambda s: (0, jnp.minimum(s, n_out_blocks)))],
        out_shape=[jax.ShapeDtypeStruct((bsz, QK_W, DV), F32),
                   jax.ShapeDtypeStruct((bsz, QK_W, DV), F32),
                   jax.ShapeDtypeStruct((d, n_rest), BF16),
                   jax.ShapeDtypeStruct((RET_W + GM_W, n_out_padded), BF16)],
        scratch_shapes=[pltpu.VMEM((ctx_len, d), BF16),
                        pltpu.VMEM((ctx_len, n_kv), F32),
                        pltpu.VMEM((ctx_len, n_kv), F32)],
        compiler_params=pltpu.CompilerParams(dimension_semantics=("arbitrary",)),
        name="ctx_states",
    )(ctx, mod, nw, w_kv, kdft, kdbt, gf, gb, w_in[0], w_out[0])

    def tile_maps(n_tiles, per_batch, reverse):
        def where(t):
            j = t % per_batch
            return t // per_batch, (per_batch - 1 - j) if reverse else j
        return [lambda s, k=k: where(jnp.clip(s - k, 0, n_tiles - 1)) for k in range(3)]

    nt_kv = seq // TILE_KV
    cpt_kv = TILE_KV // CHUNK
    cur, prv, _ = tile_maps(bsz * nt_kv, nt_kv, reverse=True)
    h, kt, v, st_b = pl.pallas_call(
        functools.partial(_kv_kernel, bsz * nt_kv, nt_kv),
        grid=(bsz * nt_kv + 1,),
        in_specs=[pl.BlockSpec(memory_space=pl.ANY),
                  _full((rows2, d3)),
                  _full((1, d)), _full((d, n_kv)),
                  _full((seq, LANES)), _full((seq, LANES)),
                  _full((QK_W, CHUNK)), _full((QK_W, DV)),
                  pl.BlockSpec((1, QK_W, DV), lambda s: (prv(s)[0], 0, 0))],
        out_specs=[pl.BlockSpec((1, TILE_KV, d), lambda s: (*cur(s), 0)),
                   pl.BlockSpec((1, cpt_kv, QK_W, CHUNK), lambda s: (*prv(s), 0, 0)),
                   pl.BlockSpec((1, TILE_KV, RET_W), lambda s: (*prv(s), 0)),
                   pl.BlockSpec((1, cpt_kv, QK_W, DV), lambda s: (*prv(s), 0, 0))],
        out_shape=[jax.ShapeDtypeStruct((bsz, seq, d), BF16),
                   jax.ShapeDtypeStruct((bsz, n_chunks, QK_W, CHUNK), BF16),
                   jax.ShapeDtypeStruct((bsz, seq, RET_W), BF16),
                   jax.ShapeDtypeStruct((bsz, n_chunks, QK_W, DV), BF16)],
        scratch_shapes=[pltpu.VMEM((QK_W, DV), F32),
                        pltpu.VMEM((TILE_KV, n_kv), F32),
                        pltpu.VMEM((TILE_KV, n_kv), F32),
                        pltpu.VMEM((X_RING, TILE_KV, d), F32),
                        pltpu.SemaphoreType.DMA((X_RING,))],
        compiler_params=pltpu.CompilerParams(dimension_semantics=("arbitrary",)),
        name="kv_sweep",
    )(x, mod, nw, w_kv, cos_t, sin_t, kdbt, gb, s_cb)

    nt = seq // TILE_MAIN
    cpt = TILE_MAIN // CHUNK
    cur, prv, prv2 = tile_maps(bsz * nt, nt, reverse=False)
    out = pl.pallas_call(
        functools.partial(_main_kernel, bsz * nt, nt),
        grid=(bsz * nt + 2,),
        in_specs=[pl.BlockSpec((1, TILE_MAIN, d), lambda s: (*cur(s), 0)),
                  pl.BlockSpec((1, TILE_MAIN, d), lambda s: (*prv2(s), 0)),
                  _full((rows2, d3)),
                  _full((d, n_rest)),
                  pl.BlockSpec((TILE_MAIN, LANES), lambda s: (prv(s)[1], 0)),
                  pl.BlockSpec((TILE_MAIN, LANES), lambda s: (prv(s)[1], 0)),
                  pl.BlockSpec((1, cpt, QK_W, CHUNK), lambda s: (*prv(s), 0, 0)),
                  pl.BlockSpec((1, TILE_MAIN, RET_W), lambda s: (*prv(s), 0)),
                  pl.BlockSpec((1, cpt, QK_W, DV), lambda s: (*prv(s), 0, 0)),
                  pl.BlockSpec((1, QK_W, DV), lambda s: (prv(s)[0], 0, 0)),
                  _full((HEADS, CHUNK, CHUNK)),
                  _full((CHUNK, QK_W)), _full((CHUNK, QK_W)),
                  _full((QK_W, CHUNK)), _full((QK_W, DV)),
                  _full((1, RET_W)), _full((1, GM_W)),
                  _full((GROUPS, CHUNK, CHUNK)), _full((GROUPS, CHUNK, DG)),
                  _full((RET_W + GM_W, n_out_padded)), _full((1, d))],
        out_specs=pl.BlockSpec((1, TILE_MAIN, d), lambda s: (*prv2(s), 0)),
        out_shape=jax.ShapeDtypeStruct((bsz, seq, d), x.dtype),
        scratch_shapes=[pltpu.VMEM((QK_W, DV), F32),
                        pltpu.VMEM((TILE_MAIN, n_rest), F32),
                        pltpu.VMEM((TILE_MAIN, n_rest), F32),
                        pltpu.VMEM((TILE_MAIN, RET_W + GM_W), BF16),
                        pltpu.VMEM((TILE_MAIN, RET_W + GM_W), BF16)],
        compiler_params=pltpu.CompilerParams(dimension_semantics=("arbitrary",)),
        name="main_sweep",
    )(h, x, mod, w_rest, cos_t, sin_t, kt, v, st_b, s_cf, dsum, af, ab, kdft, gf,
      ret_gn_w[0][None, :], gmlp_ln_w[0][None, :], w_s_b, b_s_full, w_out_b,
      final_norm_w[None, :])
    return out
```

```python
import functools

import jax
import jax.numpy as jnp
from jax import lax
from jax.experimental import pallas as pl
from jax.experimental.pallas import tpu as pltpu

F32 = jnp.float32
BF16 = jnp.bfloat16

HEADS = 4
DK = 64
DV = 128
CHUNK = 128
GROUPS = 4
DG = 128
GRID_W = 64
ROPE_BASE = 10000.0
EPS = 1e-6
LANES = 128
PAIRS = HEADS * DK // LANES
RET_W = HEADS * DV
GM_W = GROUPS * DG
QK_W = HEADS * DK

DOT_COLS = 512
TILE_KV = 1024
X_RING = 3
TILE_MAIN = 512


def _rms(x, w):
    ms = jnp.mean(x * x, axis=-1, keepdims=True)
    return x * lax.rsqrt(ms + EPS) * w


def _silu(x):
    return x * jax.nn.sigmoid(x)


def _gelu(x):
    return 0.5 * x * (1.0 + lax.erf(x * (0.5 ** 0.5)))


_dot = functools.partial(jnp.dot, preferred_element_type=F32)


def _col_groups(n):
    first = n % DOT_COLS
    edges = ([0] if first == 0 else [0, first]) + list(range(first + DOT_COLS, n + 1, DOT_COLS))
    return list(zip(edges[:-1], edges[1:]))


def _rope(p, cos, sin_signed, upper16):
    partner = jnp.where(upper16, pltpu.roll(p, 16, 1), pltpu.roll(p, LANES - 16, 1))
    return p * cos + partner * sin_signed


def _per_head_columns(m):
    head = (lax.broadcasted_iota(jnp.int32, m.shape, 0) // DK) % 2
    return jnp.concatenate([jnp.where(head == hh, m, jnp.zeros_like(m)) for hh in range(2)], axis=1)


def _block_diagonal(a, b):
    return jnp.concatenate([jnp.concatenate([a, jnp.zeros_like(b)], axis=1),
                            jnp.concatenate([jnp.zeros_like(a), b], axis=1)], axis=0)


def _kv_pairs(kd, v):
    out = []
    for p in range(PAIRS):
        r = _dot(kd[p * LANES:(p + 1) * LANES, :], v[:, p * 2 * DV:(p + 1) * 2 * DV])
        out.append(r[0:DK, 0:DV])
        out.append(r[DK:2 * DK, DV:2 * DV])
    return jnp.concatenate(out, axis=0)


def _emit_pipelined(big_pieces, side_pieces, tasks):
    in_flight = []
    last_tick = max([start + parts for start, _, parts in tasks] +
                    [len(big_pieces), len(side_pieces)])
    for tick in range(last_tick):
        if tick < len(big_pieces):
            big_pieces[tick]()
        urgent, relaxed = side_pieces[tick] if tick < len(side_pieces) else (None, None)
        if urgent is not None:
            urgent()
        in_flight += [[gen, parts] for start, gen, parts in tasks if start == tick]
        for t in [t for t in in_flight if t[1] > 1] + [t for t in in_flight if t[1] == 1]:
            next(t[0], None)
            t[1] -= 1
        in_flight = [t for t in in_flight if t[1] > 0]
        if relaxed is not None:
            relaxed()
    assert not in_flight


def _run_pipeline_step(step, n_tiles, n_stages, body, refs, bufs_a, bufs_b):
    def call(s_static_parity, stages):
        new, old = (bufs_a, bufs_b) if s_static_parity == 0 else (bufs_b, bufs_a)
        body(stages, *refs, *new, *old)

    edge = n_stages - 1
    for s in list(range(edge)) + list(range(n_tiles, n_tiles + edge)):
        stages = tuple(0 <= s - j < n_tiles for j in range(n_stages))
        pl.when(step == s)(functools.partial(call, s % 2, stages))
    steady = (step >= edge) & (step < n_tiles)
    for parity in range(2):
        pl.when(steady & (lax.rem(step, 2) == parity))(
            functools.partial(call, parity, (True,) * n_stages))


def _prologue_kernel(decf_ref, decb_ref, c_ref, cctx_ref, w_ref, b_ref, invf_ref, wqkv_ref,
                     ws_ref, bs_ref,
                     mod_ref, cos_ref, sin_ref, dsum_ref, af_ref, ab_ref,
                     kdft_ref, kdbt_ref, gf_ref, gb_ref,
                     wkv_ref, wsb_ref, bsf_ref):
    i = pl.program_id(0)
    c = c_ref[...]
    cond = jnp.concatenate([c, jnp.broadcast_to(cctx_ref[...], c.shape)], axis=0)
    mod_ref[...] = _dot(_silu(cond), w_ref[...]) + b_ref[...]
    wkv_ref[...] = wqkv_ref[...].astype(BF16)

    @pl.when(i == 0)
    def _small_weights():
        wsb_ref[...] = ws_ref[...].astype(BF16)
        for g in range(GROUPS):
            bsf_ref[g] = jnp.broadcast_to(bs_ref[g:g + 1, :], (CHUNK, CHUNK)).T

    @pl.when(i == 0)
    def _tables():
        pos = lax.broadcasted_iota(jnp.int32, (GRID_W, LANES), 0).astype(F32)
        lane = lax.broadcasted_iota(jnp.int32, (GRID_W, LANES), 1)
        ang = pos * invf_ref[...]
        cr = jnp.cos(ang)
        sr = jnp.sin(ang)
        sr = jnp.where((lane & 16) == 0, -sr, sr)
        by_row = (lane & 32) == 0
        n_rows = cos_ref.shape[0] // GRID_W
        for r in range(n_rows):
            sl = slice(r * GRID_W, (r + 1) * GRID_W)
            cos_ref[sl, :] = jnp.where(by_row, jnp.broadcast_to(cr[r:r + 1, :], cr.shape), cr)
            sin_ref[sl, :] = jnp.where(by_row, jnp.broadcast_to(sr[r:r + 1, :], sr.shape), sr)

        def log_decay(head_idx, dec_ref):
            d = jnp.full(head_idx.shape, dec_ref[HEADS - 1], F32)
            for h in range(HEADS - 2, -1, -1):
                d = jnp.where(head_idx == h, dec_ref[h], d)
            return -jnp.exp(d)

        n = lax.broadcasted_iota(jnp.int32, (CHUNK, QK_W), 0).astype(F32)
        hl = lax.broadcasted_iota(jnp.int32, (CHUNK, QK_W), 1) // DK
        af_ref[...] = jnp.exp((n + 1.0) * log_decay(hl, decf_ref))
        ab_ref[...] = jnp.exp((CHUNK - n) * log_decay(hl, decb_ref))

        m = lax.broadcasted_iota(jnp.int32, (QK_W, CHUNK), 1).astype(F32)
        hr = lax.broadcasted_iota(jnp.int32, (QK_W, CHUNK), 0) // DK
        lgf = log_decay(hr, decf_ref)
        lgb = log_decay(hr, decb_ref)
        kdft_ref[...] = jnp.exp((CHUNK - 1.0 - m) * lgf)
        kdbt_ref[...] = jnp.exp(m * lgb)
        gf_ref[...] = jnp.exp(float(CHUNK) * lgf)
        gb_ref[...] = jnp.exp(float(CHUNK) * lgb)

        nn = lax.broadcasted_iota(jnp.int32, (CHUNK, CHUNK), 0).astype(F32)
        mm = lax.broadcasted_iota(jnp.int32, (CHUNK, CHUNK), 1).astype(F32)
        for h in range(HEADS):
            lf = -jnp.exp(jnp.full((CHUNK, CHUNK), decf_ref[h], F32))
            lb = -jnp.exp(jnp.full((CHUNK, CHUNK), decb_ref[h], F32))
            df = nn - mm
            db = mm - nn
            fwd = jnp.where(df >= 0, jnp.exp(jnp.maximum(df, 0.0) * lf), 0.0)
            bwd = jnp.where(db >= 0, jnp.exp(jnp.maximum(db, 0.0) * lb), 0.0)
            dsum_ref[h] = fwd + bwd


def _ctx_kernel(n_tiles, n_out_blocks, ctx_ref, mod_ref, nw_ref, wkv_ref, kdft_ref, kdbt_ref,
                gf_ref, gb_ref, win_ref, wout_ref,
                scf_ref, scb_ref, wrest_ref, woutb_ref, h_ref, pa_ref, pb_ref):
    s = pl.program_id(0)
    keep = s < n_out_blocks
    refs = (ctx_ref, mod_ref, nw_ref, wkv_ref, kdft_ref, kdbt_ref, gf_ref, gb_ref,
            win_ref, wout_ref, keep, scf_ref, scb_ref, wrest_ref, woutb_ref, h_ref)
    _run_pipeline_step(s, n_tiles, 2, _ctx_step, refs, (pa_ref,), (pb_ref,))


def _ctx_step(stages, ctx_ref, mod_ref, nw_ref, wkv_ref, kdft_ref, kdbt_ref, gf_ref, gb_ref,
              win_ref, wout_ref, keep, scf_ref, scb_ref, wrest_ref, woutb_ref, h_ref,
              p_new_ref, p_old):
    project_stage, state_stage = stages
    wrest_ref[...] = win_ref[...].astype(BF16)
    woutb_ref[...] = jnp.where(keep, wout_ref[...], 0.0).astype(BF16)
    d = ctx_ref.shape[-1]
    n_chunks = ctx_ref.shape[1] // CHUNK
    ctx_row = mod_ref.shape[0] // 2
    mod = mod_ref[ctx_row:ctx_row + 1, :]

    def norm_rows(rows):
        h = _rms(ctx_ref[0, rows, :], nw_ref[...]) * (1.0 + mod[:, d:2 * d]) + mod[:, 0:d]
        h_ref[rows, :] = h.astype(BF16)

    def project(lo, hi):
        p_new_ref[:, lo:hi] = _dot(h_ref[...], wkv_ref[:, lo:hi])

    new_f, new_b = {}, {}

    def chunk(c):
        sl = slice(c * CHUNK, (c + 1) * CHUNK)
        kt = (p_old[sl, 0:QK_W] * (DK ** -0.5)).T
        v = p_old[sl, QK_W:].astype(BF16)
        new_f[c] = _kv_pairs((kt * kdft_ref[...]).astype(BF16), v)
        new_b[c] = _kv_pairs((kt * kdbt_ref[...]).astype(BF16), v)
        yield

    def combine():
        yield
        sf = jnp.zeros((QK_W, DV), F32)
        for c in range(n_chunks):
            sf = gf_ref[...] * sf + new_f[c]
        sb = jnp.zeros((QK_W, DV), F32)
        for c in range(n_chunks - 1, -1, -1):
            sb = gb_ref[...] * sb + new_b[c]
        scf_ref[0] = sf
        scb_ref[0] = sb

    big, tasks = [], []
    if project_stage:
        for c in range(n_chunks):
            norm_rows(slice(c * CHUNK, (c + 1) * CHUNK))
        big = [functools.partial(project, lo, hi) for lo, hi in _col_groups(p_new_ref.shape[1])]
    if state_stage:
        tasks = [(c, chunk(c), 1) for c in range(n_chunks)] + [(n_chunks - 1, combine(), 2)]
    _emit_pipelined(big, [], tasks)


def _kv_kernel(n_tiles, tiles_per_batch, x_hbm, modc_ref, nw_ref, wkv_ref, cos_ref, sin_ref,
               kdbt_ref, gb_ref, scb_ref, h_ref, kt_ref, v_ref, stb_ref, sb_ref, pa_ref, pb_ref,
               xbuf_ref, xsem_ref):
    s = pl.program_id(0)
    tile = xbuf_ref.shape[1]

    def x_copy(t):
        b = t // tiles_per_batch
        j = tiles_per_batch - 1 - lax.rem(t, tiles_per_batch)
        slot = lax.rem(t, X_RING)
        return pltpu.make_async_copy(x_hbm.at[b, pl.ds(j * tile, tile), :],
                                     xbuf_ref.at[slot], xsem_ref.at[slot])

    @pl.when(s == 0)
    def _prime():
        for t in range(min(X_RING - 1, n_tiles)):
            x_copy(t).start()

    @pl.when(s < n_tiles)
    def _arrived():
        x_copy(s).wait()

    @pl.when(s + X_RING - 1 < n_tiles)
    def _prefetch():
        x_copy(s + X_RING - 1).start()

    @pl.when((s >= 1) & (lax.rem(s - 1, tiles_per_batch) == 0))
    def _init():
        sb_ref[...] = scb_ref[0]

    batch = jnp.minimum(s, n_tiles - 1) // tiles_per_batch
    modc = modc_ref[pl.ds(batch, 1), :]
    xc_ref = xbuf_ref.at[lax.rem(s, X_RING)]
    finished = jnp.clip(s - 1, 0, n_tiles - 1)
    first_row = pl.multiple_of(
        (tiles_per_batch - 1 - lax.rem(finished, tiles_per_batch)) * tile, tile)
    cos_ref = cos_ref.at[pl.ds(first_row, tile), :]
    sin_ref = sin_ref.at[pl.ds(first_row, tile), :]
    refs = (xc_ref, modc, nw_ref, wkv_ref, cos_ref, sin_ref, kdbt_ref, gb_ref,
            h_ref, kt_ref, v_ref, stb_ref, sb_ref)
    _run_pipeline_step(s, n_tiles, 2, _kv_step, refs, (pa_ref,), (pb_ref,))


def _kv_step(stages, xc_ref, modc, nw_ref, wkv_ref, cos_ref, sin_ref, kdbt_ref, gb_ref,
             h_ref, kt_ref, v_ref, stb_ref, sb_ref, p_new_ref, p_old):
    project_stage, finish_stage = stages
    d = xc_ref.shape[-1]
    tile = xc_ref.shape[0]
    half = tile // 2
    row_halves = [slice(0, half), slice(half, tile)]
    quarter = [slice(i * half // 2, (i + 1) * half // 2) for i in range(4)]
    upper16 = (lax.broadcasted_iota(jnp.int32, (CHUNK, LANES), 1) & 16) != 0

    def norm_rows(rows):
        h = _rms(xc_ref[rows, :], nw_ref[...]) * (1.0 + modc[:, d:2 * d]) + modc[:, 0:d]
        h_ref[0, rows, :] = h.astype(BF16)

    def project(rows, lo, hi):
        p_new_ref[rows, lo:hi] = _dot(h_ref[0, rows, :], wkv_ref[:, lo:hi])

    def chunk(c):
        sl = slice(c * CHUNK, (c + 1) * CHUNK)
        k = [_rope(p_old[sl, p * LANES:(p + 1) * LANES], cos_ref[sl, :], sin_ref[sl, :], upper16)
             * (DK ** -0.5) for p in range(PAIRS)]
        v = p_old[sl, QK_W:].astype(BF16)
        v_ref[0, sl, :] = v
        kt = jnp.concatenate([kp.T for kp in k], axis=0)
        kt_ref[0, c] = kt.astype(BF16)
        new = _kv_pairs((kt * kdbt_ref[...]).astype(BF16), v)
        yield
        stb_ref[0, c] = sb_ref[...].astype(BF16)
        sb_ref[...] = gb_ref[...] * sb_ref[...] + new

    projections, late_norms, tasks = [], [], []
    if project_stage:
        norm_rows(quarter[0])
        norm_rows(quarter[1])
        projections = [functools.partial(project, row_halves[r], lo, hi)
                       for r in range(2) for lo, hi in _col_groups(p_new_ref.shape[1])]
        late_norms = [(functools.partial(norm_rows, q), None) for q in quarter[2:]]
    if finish_stage:
        tasks = [(tick, chunk(c), 2) for tick, c in enumerate(range(tile // CHUNK - 1, -1, -1))]
    _emit_pipelined(projections, late_norms, tasks)


def _main_kernel(n_tiles, tiles_per_batch, h_ref, x_ref, mod_ref, w_ref, cos_ref,
                 sin_ref, kt_ref, v_ref, stb_ref, scf_ref, dsum_ref, af_ref, ab_ref, kdft_ref,
                 gf_ref, gnw_ref, lnw_ref, ws_ref, bs_ref, wout_ref, fnw_ref, o_ref,
                 sf_ref, pa_ref, pb_ref, ya_ref, yb_ref):
    s = pl.program_id(0)

    @pl.when((s >= 1) & (lax.rem(s - 1, tiles_per_batch) == 0))
    def _init():
        sf_ref[...] = scf_ref[0]

    batch = jnp.clip(s - 2, 0, n_tiles - 1) // tiles_per_batch
    mod = mod_ref[pl.ds(batch, 1), :]
    refs = (h_ref, x_ref, mod, w_ref, cos_ref, sin_ref, kt_ref, v_ref,
            stb_ref, dsum_ref, af_ref, ab_ref, kdft_ref, gf_ref, gnw_ref, lnw_ref, ws_ref,
            bs_ref, wout_ref, fnw_ref, o_ref, sf_ref)
    _run_pipeline_step(s, n_tiles, 3, _main_step, refs, (pa_ref, ya_ref), (pb_ref, yb_ref))


def _main_step(stages, h_ref, x_ref, mod, w_ref, cos_ref, sin_ref, kt_ref, v_ref,
               stb_ref, dsum_ref, af_ref, ab_ref, kdft_ref, gf_ref, gnw_ref, lnw_ref, ws_ref,
               bs_ref, wout_ref, fnw_ref, o_ref, sf_ref, p_new_ref, y_ref, p_all, y_old):
    project_stage, mix_stage, out_stage = stages
    d = x_ref.shape[-1]
    tile = x_ref.shape[1]
    n_chunks = tile // CHUNK
    half = tile // 2
    row_halves = [slice(0, half), slice(half, tile)]
    c_gr = QK_W
    c_u = c_gr + RET_W
    c_vm = c_u + GM_W
    c_gm = c_vm + GM_W
    lane = lax.broadcasted_iota(jnp.int32, (CHUNK, LANES), 1)
    upper16 = (lane & 16) != 0

    operand = {}

    def loaded_once(key, load):
        if key not in operand:
            operand[key] = load()
        return operand[key]

    def project(r, lo, hi):
        rows = row_halves[r]
        lhs = loaded_once(("h", r), lambda: h_ref[0, rows, :])
        p_new_ref[rows, lo:hi] = _dot(lhs, w_ref[:, lo:hi])

    def retention(c, p):
        sl = slice(c * CHUNK, (c + 1) * CHUNK)
        ps = slice(p * LANES, (p + 1) * LANES)
        q2 = _rope(p_all[sl, ps], cos_ref[sl, :], sin_ref[sl, :], upper16)
        kt = kt_ref[0, c, ps, :]
        vp = v_ref[0, sl, p * 2 * DV:(p + 1) * 2 * DV]
        qa = q2 * af_ref[:, ps]
        qb = q2 * ab_ref[:, ps]
        scores = _dot(q2.astype(BF16), _per_head_columns(kt))
        qx = jnp.concatenate([qa, qb], axis=1).astype(BF16)
        kd = (kt.astype(F32) * kdft_ref[ps, :]).astype(BF16)
        yield
        state = jnp.concatenate([sf_ref[ps, :].astype(BF16), stb_ref[0, c, ps, :]], axis=0)
        from_state = _dot(qx, _per_head_columns(state))
        decay = jnp.concatenate([dsum_ref[2 * p + hh] for hh in range(2)], axis=1)
        both = _dot((scores * decay).astype(BF16), _block_diagonal(vp[:, 0:DV], vp[:, DV:])) + from_state
        o = [both[:, hh * DV:(hh + 1) * DV] for hh in range(2)]
        r = _dot(kd, vp)
        yield
        new = jnp.concatenate([r[0:DK, 0:DV], r[DK:2 * DK, DV:2 * DV]], axis=0)
        sf_ref[ps, :] = gf_ref[ps, :] * sf_ref[ps, :] + new
        for hh in range(2):
            head = 2 * p + hh
            mu = jnp.mean(o[hh], axis=-1, keepdims=True)
            var = jnp.mean(jnp.square(o[hh] - mu), axis=-1, keepdims=True)
            hs = slice(head * DV, (head + 1) * DV)
            y = (o[hh] - mu) * lax.rsqrt(var + EPS) * gnw_ref[:, hs]
            gate_cols = slice(c_gr + head * DV, c_gr + (head + 1) * DV)
            y_ref[sl, hs] = (y * _silu(p_all[sl, gate_cols])).astype(BF16)

    def gating(c):
        sl = slice(c * CHUNK, (c + 1) * CHUNK)
        vg = _gelu(p_all[sl, c_vm:c_gm])
        mu = jnp.mean(vg, axis=-1, keepdims=True)
        var = jnp.mean(jnp.square(vg - mu), axis=-1, keepdims=True)
        yield
        vn = ((vg - mu) * lax.rsqrt(var + EPS) * lnw_ref[...]).astype(BF16)
        s = [_dot(ws_ref[g], vn[:, g * DG:(g + 1) * DG]) for g in range(GROUPS)]
        yield
        for g in range(GROUPS):
            if g == GROUPS // 2:
                yield
            u = _gelu(p_all[sl, c_u + g * DG:c_u + (g + 1) * DG])
            gm = _silu(p_all[sl, c_gm + g * DG:c_gm + (g + 1) * DG])
            y_ref[sl, RET_W + g * DG:RET_W + (g + 1) * DG] = (u * (s[g] + bs_ref[g]) * gm).astype(BF16)

    out_groups = _col_groups(d)
    xn = {}

    def out_project(r, g):
        rows = row_halves[r]
        lo, hi = out_groups[g]
        lhs = loaded_once(("y", r), lambda: y_old[rows, :])
        out = _dot(lhs, wout_ref[:, lo:hi])
        xn[r, g] = x_ref[0, rows, lo:hi] + mod[:, 2 * d + lo:2 * d + hi] * out

    def finish(r, part):
        sub = slice(part * half // 2, (part + 1) * half // 2)
        rows = slice(row_halves[r].start + sub.start, row_halves[r].start + sub.stop)
        parts = [xn[r, g][sub, :] for g in range(len(out_groups))]
        ms = sum(jnp.sum(t * t, axis=-1, keepdims=True) for t in parts) * (1.0 / d)
        rs = lax.rsqrt(ms + EPS)
        for (lo, hi), t in zip(out_groups, parts):
            o_ref[0, rows, lo:hi] = t * rs * fnw_ref[:, lo:hi]

    tasks = []
    if mix_stage:
        for c in range(n_chunks):
            base = 3 * c
            tasks += [(base + min(p, 1), retention(c, p), 3) for p in range(PAIRS)]
            tasks.append((base + 1, gating(c), 4))
    big, relaxed = [], {}

    def place(tick, piece):
        while tick in relaxed:
            tick += 1
        relaxed[tick] = piece

    for r in range(2):
        outs = [functools.partial(out_project, r, g) for g in range(len(out_groups))] if out_stage else []
        projs = ([functools.partial(project, r, lo, hi) for lo, hi in _col_groups(p_new_ref.shape[1])]
                 if project_stage else [])
        for i in range(max(len(outs), len(projs))):
            big += outs[i:i + 1] + projs[i:i + 1]
            if i == len(outs) - 1:
                place(len(big) + 1, functools.partial(finish, r, 0))
                place(len(big) + 3, functools.partial(finish, r, 1))
    side = [(None, relaxed.get(tick)) for tick in range(max(relaxed, default=-1) + 1)]
    _emit_pipelined(big, side, tasks)


def _full(shape):
    return pl.BlockSpec(shape, lambda *_: (0,) * len(shape))


def kernel(x, c, ctx, c_ctx, w_ada, b_ada, norm_w, w_in, ret_decay_f, ret_decay_b, ret_gn_w,
           gmlp_ln_w, w_s, b_s, w_out, final_norm_w):
    bsz, seq, d = x.shape
    ctx_len = ctx.shape[1]
    depth = w_ada.shape[0]
    assert depth == 1 and d % LANES == 0
    assert seq % TILE_KV == 0 and seq % TILE_MAIN == 0 and ctx_len % CHUNK == 0
    assert seq % GRID_W == 0 and TILE_MAIN % (4 * CHUNK) == 0 and TILE_KV % (4 * CHUNK) == 0
    n_chunks = seq // CHUNK
    d3 = 3 * d
    n_kv = QK_W + RET_W
    n_rest = w_in.shape[2] - n_kv
    assert QK_W + n_kv == d
    assert _col_groups(n_rest)[0] == (0, QK_W)
    n_rest_blocks = n_rest // QK_W
    n_out_blocks = d // QK_W
    assert n_rest % QK_W == 0 and d % QK_W == 0
    assert max(n_rest_blocks, n_out_blocks + 1) <= bsz + 1

    quarter = DK // 4
    inv_freq = ROPE_BASE ** (-jnp.arange(quarter, dtype=F32) / quarter)
    invf_lane = jnp.tile(inv_freq, LANES // quarter)[None, :]
    nw = norm_w[0][None, :]
    rows2 = 2 * bsz

    n_col_blocks = d3 // QK_W
    n_kv_blocks = n_kv // QK_W
    assert d3 % QK_W == 0 and n_kv_blocks <= n_col_blocks

    def kv_block(i):
        return jnp.minimum(i, n_kv_blocks - 1)

    smem = pl.BlockSpec(memory_space=pltpu.SMEM)
    n_out_padded = d + QK_W
    (mod, cos_t, sin_t, dsum, af, ab, kdft, kdbt, gf, gb,
     w_kv, w_s_b, b_s_full) = pl.pallas_call(
        _prologue_kernel,
        grid=(n_col_blocks,),
        in_specs=[smem, smem, _full((bsz, d)), _full((1, d)),
                  pl.BlockSpec((d, QK_W), lambda i: (0, i)),
                  pl.BlockSpec((1, QK_W), lambda i: (0, i)),
                  _full((1, LANES)),
                  pl.BlockSpec((d, QK_W), lambda i: (0, kv_block(i) + 1)),
                  _full((GROUPS, CHUNK, CHUNK)), _full((GROUPS, CHUNK))],
        out_specs=[pl.BlockSpec((rows2, QK_W), lambda i: (0, i)),
                   _full((seq, LANES)), _full((seq, LANES)),
                   _full((HEADS, CHUNK, CHUNK)),
                   _full((CHUNK, QK_W)), _full((CHUNK, QK_W)),
                   _full((QK_W, CHUNK)), _full((QK_W, CHUNK)),
                   _full((QK_W, DV)), _full((QK_W, DV)),
                   pl.BlockSpec((d, QK_W), lambda i: (0, kv_block(i))),
                   _full((GROUPS, CHUNK, CHUNK)), _full((GROUPS, CHUNK, DG))],
        out_shape=[jax.ShapeDtypeStruct((rows2, d3), F32),
                   jax.ShapeDtypeStruct((seq, LANES), F32),
                   jax.ShapeDtypeStruct((seq, LANES), F32),
                   jax.ShapeDtypeStruct((HEADS, CHUNK, CHUNK), F32),
                   jax.ShapeDtypeStruct((CHUNK, QK_W), F32),
                   jax.ShapeDtypeStruct((CHUNK, QK_W), F32),
                   jax.ShapeDtypeStruct((QK_W, CHUNK), F32),
                   jax.ShapeDtypeStruct((QK_W, CHUNK), F32),
                   jax.ShapeDtypeStruct((QK_W, DV), F32),
                   jax.ShapeDtypeStruct((QK_W, DV), F32),
                   jax.ShapeDtypeStruct((d, n_kv), BF16),
                   jax.ShapeDtypeStruct((GROUPS, CHUNK, CHUNK), BF16),
                   jax.ShapeDtypeStruct((GROUPS, CHUNK, DG), F32)],
        compiler_params=pltpu.CompilerParams(dimension_semantics=("arbitrary",)),
        name="prologue",
    )(ret_decay_f[0], ret_decay_b[0], c, c_ctx[None, :], w_ada[0], b_ada[0][None, :], invf_lane,
      w_in[0], w_s[0], b_s[0])

    def rest_block(s):
        return jnp.minimum(s, n_rest_blocks - 1)

    def rest_source(s):
        return jnp.where(s == 0, 0, rest_block(s) + n_kv // QK_W)

    s_cf, s_cb, w_rest, w_out_b = pl.pallas_call(
        functools.partial(_ctx_kernel, bsz, n_out_blocks),
        grid=(bsz + 1,),
        in_specs=[pl.BlockSpec((1, ctx_len, d), lambda s: (jnp.minimum(s, bsz - 1), 0, 0)),
                  _full((rows2, d3)),
                  _full((1, d)), _full((d, n_kv)),
                  _full((QK_W, CHUNK)), _full((QK_W, CHUNK)),
                  _full((QK_W, DV)), _full((QK_W, DV)),
                  pl.BlockSpec((d, QK_W), lambda s: (0, rest_source(s))),
                  pl.BlockSpec((RET_W + GM_W, QK_W),
                               lambda s: (0, jnp.minimum(s, n_out_blocks - 1)))],
        out_specs=[pl.BlockSpec((1, QK_W, DV), lambda s: (jnp.maximum(s - 1, 0), 0, 0)),
                   pl.BlockSpec((1, QK_W, DV), lambda s: (jnp.maximum(s - 1, 0), 0, 0)),
                   pl.BlockSpec((d, QK_W), lambda s: (0, rest_block(s))),
                   pl.BlockSpec((RET_W + GM_W, QK_W),
                                lambda s: (0, jnp.minimum(s, n_out_blocks)))],
        out_shape=[jax.ShapeDtypeStruct((bsz, QK_W, DV), F32),
                   jax.ShapeDtypeStruct((bsz, QK_W, DV), F32),
                   jax.ShapeDtypeStruct((d, n_rest), BF16),
                   jax.ShapeDtypeStruct((RET_W + GM_W, n_out_padded), BF16)],
        scratch_shapes=[pltpu.VMEM((ctx_len, d), BF16),
                        pltpu.VMEM((ctx_len, n_kv), F32),
                        pltpu.VMEM((ctx_len, n_kv), F32)],
        compiler_params=pltpu.CompilerParams(dimension_semantics=("arbitrary",)),
        name="ctx_states",
    )(ctx, mod, nw, w_kv, kdft, kdbt, gf, gb, w_in[0], w_out[0])

    def tile_maps(n_tiles, per_batch, reverse):
        def where(t):
            j = t % per_batch
            return t // per_batch, (per_batch - 1 - j) if reverse else j
        return [lambda s, k=k: where(jnp.clip(s - k, 0, n_tiles - 1)) for k in range(3)]

    nt_kv = seq // TILE_KV
    cpt_kv = TILE_KV // CHUNK
    cur, prv, _ = tile_maps(bsz * nt_kv, nt_kv, reverse=True)
    h, kt, v, st_b = pl.pallas_call(
        functools.partial(_kv_kernel, bsz * nt_kv, nt_kv),
        grid=(bsz * nt_kv + 1,),
        in_specs=[pl.BlockSpec(memory_space=pl.ANY),
                  _full((rows2, d3)),
                  _full((1, d)), _full((d, n_kv)),
                  _full((seq, LANES)), _full((seq, LANES)),
                  _full((QK_W, CHUNK)), _full((QK_W, DV)),
                  pl.BlockSpec((1, QK_W, DV), lambda s: (prv(s)[0], 0, 0))],
        out_specs=[pl.BlockSpec((1, TILE_KV, d), lambda s: (*cur(s), 0)),
                   pl.BlockSpec((1, cpt_kv, QK_W, CHUNK), lambda s: (*prv(s), 0, 0)),
                   pl.BlockSpec((1, TILE_KV, RET_W), lambda s: (*prv(s), 0)),
                   pl.BlockSpec((1, cpt_kv, QK_W, DV), lambda s: (*prv(s), 0, 0))],
        out_shape=[jax.ShapeDtypeStruct((bsz, seq, d), BF16),
                   jax.ShapeDtypeStruct((bsz, n_chunks, QK_W, CHUNK), BF16),
                   jax.ShapeDtypeStruct((bsz, seq, RET_W), BF16),
                   jax.ShapeDtypeStruct((bsz, n_chunks, QK_W, DV), BF16)],
        scratch_shapes=[pltpu.VMEM((QK_W, DV), F32),
                        pltpu.VMEM((TILE_KV, n_kv), F32),
                        pltpu.VMEM((TILE_KV, n_kv), F32),
                        pltpu.VMEM((X_RING, TILE_KV, d), F32),
                        pltpu.SemaphoreType.DMA((X_RING,))],
        compiler_params=pltpu.CompilerParams(dimension_semantics=("arbitrary",)),
        name="kv_sweep",
    )(x, mod, nw, w_kv, cos_t, sin_t, kdbt, gb, s_cb)

    nt = seq // TILE_MAIN
    cpt = TILE_MAIN // CHUNK
    cur, prv, prv2 = tile_maps(bsz * nt, nt, reverse=False)
    out = pl.pallas_call(
        functools.partial(_main_kernel, bsz * nt, nt),
        grid=(bsz * nt + 2,),
        in_specs=[pl.BlockSpec((1, TILE_MAIN, d), lambda s: (*cur(s), 0)),
                  pl.BlockSpec((1, TILE_MAIN, d), lambda s: (*prv2(s), 0)),
                  _full((rows2, d3)),
                  _full((d, n_rest)),
                  pl.BlockSpec((TILE_MAIN, LANES), lambda s: (prv(s)[1], 0)),
                  pl.BlockSpec((TILE_MAIN, LANES), lambda s: (prv(s)[1], 0)),
                  pl.BlockSpec((1, cpt, QK_W, CHUNK), lambda s: (*prv(s), 0, 0)),
                  pl.BlockSpec((1, TILE_MAIN, RET_W), lambda s: (*prv(s), 0)),
                  pl.BlockSpec((1, cpt, QK_W, DV), lambda s: (*prv(s), 0, 0)),
                  pl.BlockSpec((1, QK_W, DV), lambda s: (prv(s)[0], 0, 0)),
                  _full((HEADS, CHUNK, CHUNK)),
                  _full((CHUNK, QK_W)), _full((CHUNK, QK_W)),
                  _full((QK_W, CHUNK)), _full((QK_W, DV)),
                  _full((1, RET_W)), _full((1, GM_W)),
                  _full((GROUPS, CHUNK, CHUNK)), _full((GROUPS, CHUNK, DG)),
                  _full((RET_W + GM_W, n_out_padded)), _full((1, d))],
        out_specs=pl.BlockSpec((1, TILE_MAIN, d), lambda s: (*prv2(s), 0)),
        out_shape=jax.ShapeDtypeStruct((bsz, seq, d), x.dtype),
        scratch_shapes=[pltpu.VMEM((QK_W, DV), F32),
                        pltpu.VMEM((TILE_MAIN, n_rest), F32),
                        pltpu.VMEM((TILE_MAIN, n_rest), F32),
                        pltpu.VMEM((TILE_MAIN, RET_W + GM_W), BF16),
                        pltpu.VMEM((TILE_MAIN, RET_W + GM_W), BF16)],
        compiler_params=pltpu.CompilerParams(dimension_semantics=("arbitrary",)),
        name="main_sweep",
    )(h, x, mod, w_rest, cos_t, sin_t, kt, v, st_b, s_cf, dsum, af, ab, kdft, gf,
      ret_gn_w[0][None, :], gmlp_ln_w[0][None, :], w_s_b, b_s_full, w_out_b,
      final_norm_w[None, :])
    return out
```
